```python
import math
import jax, jax.numpy as jnp
from jax import lax
import numpy as np

D_MODEL = 1024
BATCH = 8
SEQ = 2048
DEPTH = 2
DEC_BATCH = 128
DEC_SEQ = 1
PAST_LEN = 16384
PAGE_SIZE = 128

N_AB_LAYERS = (DEPTH + 1) // 2
N_C_LAYERS = DEPTH // 2
CHUNK = 64
GLA_HEADS = 4
GLA_DK = D_MODEL // 2 // GLA_HEADS
GLA_DV = D_MODEL // GLA_HEADS
GLA_KEY = GLA_HEADS * GLA_DK
GLA_VAL = GLA_HEADS * GLA_DV
GLA_RANK = 16
GLA_NORMALIZER = 16.0
SSD_INNER = D_MODEL
SSD_HEADDIM = 64
SSD_HEADS = SSD_INNER // SSD_HEADDIM
SSD_STATE = 128
SSD_GROUPS = 2
SSD_CONV = 4
SSD_CONV_DIM = SSD_INNER + 2 * SSD_GROUPS * SSD_STATE
HG_EXPAND = 128
HG_HEADS = D_MODEL // HG_EXPAND
HG_F = HG_HEADS * HG_EXPAND
HG_I = D_MODEL
HG_DI = HG_I // HG_HEADS
N_EXPERTS = 16
N_GROUPS = 4
EXPERTS_PER_GROUP = N_EXPERTS // N_GROUPS
TOP_K = 2
D_FF_EXPERT = 512
ALPHA = (2 * DEPTH) ** 0.25
BETA = (8 * DEPTH) ** -0.25
EPS = 1e-5
AB_SPLITS = (GLA_KEY, GLA_KEY, GLA_VAL, GLA_VAL, GLA_RANK, SSD_INNER, SSD_CONV_DIM, SSD_HEADS)
AB_IN = sum(AB_SPLITS)
C_SPLITS = (HG_F, HG_F, HG_I, HG_I)
C_IN = sum(C_SPLITS)

kernel_name = 'hybrid_gla_ssd_hgrn2_moe_step'

F32 = jnp.float32


def split_cols(a, sizes):
    offs = np.cumsum(sizes)[:-1].tolist()
    return jnp.split(a, offs, axis=-1)


def heads(a, h):
    return a.reshape(a.shape[:-1] + (h, -1))


def layer_norm(x, w, b):
    xf = x.astype(F32)
    mu = xf.mean(-1, keepdims=True)
    var = jnp.square(xf - mu).mean(-1, keepdims=True)
    return ((xf - mu) * lax.rsqrt(var + EPS) * w + b).astype(x.dtype)


def rms_norm(x, w):
    xf = x.astype(F32)
    return (xf * lax.rsqrt(jnp.mean(xf * xf, -1, keepdims=True) + EPS) * w).astype(x.dtype)


def chunked_gated_linear(q, k, v, g, s0, scalar_decay, out_dtype):
    bsz, L = q.shape[0], q.shape[1]
    C = min(CHUNK, L)
    n = -(-L // C)
    pad = n * C - L

    def prep(a):
        a = jnp.pad(a.astype(F32), [(0, 0), (0, pad)] + [(0, 0)] * (a.ndim - 2))
        a = a.reshape((bsz, n, C) + a.shape[2:])
        return jnp.moveaxis(a, 1, 0)

    qc, kc, vc, gc = prep(q), prep(k), prep(v), prep(g)
    causal = jnp.tril(jnp.ones((C, C), bool))

    def step(S, inp):
        qb, kb, vb, gb = inp
        G = jnp.cumsum(gb, axis=1)
        G_last = G[:, -1]
        if scalar_decay:
            diff = G[:, :, None, :] - G[:, None, :, :]
            decay = jnp.exp(jnp.where(causal[None, :, :, None], diff, -jnp.inf))
            scores = jnp.einsum('bthk,bshk->btsh', qb, kb) * decay
            q_dec = qb * jnp.exp(G)[..., None]
            k_dec = kb * jnp.exp(G_last[:, None] - G)[..., None]
            s_dec = jnp.exp(G_last)[..., None, None]
        else:
            diff = G[:, :, None] - G[:, None, :]
            decay = jnp.exp(jnp.where(causal[None, :, :, None, None], diff, -jnp.inf))
            scores = jnp.einsum('bthk,bshk,btshk->btsh', qb, kb, decay)
            q_dec = qb * jnp.exp(G)
            k_dec = kb * jnp.exp(G_last[:, None] - G)
            s_dec = jnp.exp(G_last)[..., None]
        o = jnp.einsum('btsh,bshv->bthv', scores, vb) + jnp.einsum('bthk,bhkv->bthv', q_dec, S)
        S = S * s_dec + jnp.einsum('bshk,bshv->bhkv', k_dec, vb)
        return S, o

    S, o = lax.scan(step, s0.astype(F32), (qc, kc, vc, gc))
    o = jnp.moveaxis(o, 0, 1).reshape((bsz, n * C) + o.shape[3:])[:, :L]
    return o.astype(out_dtype), S.astype(s0.dtype)


def causal_conv(xbc, buf, w, b):
    L = xbc.shape[1]
    xp = jnp.concatenate([buf.astype(xbc.dtype), xbc], axis=1)
    out = b
    for j in range(SSD_CONV):
        out = out + xp[:, j:j + L] * w[j]
    return jax.nn.silu(out), xp[:, L:]


def ab_mixer(x, s_gla, s_ssd, s_conv, w_in, w_gk2, b_gk2, gla_norm_w, conv_w, conv_b,
             dt_bias, a_log, d_skip, ssd_norm_w, w_out):
    bsz, L, _ = x.shape
    q, k, v, g_out, lr, z, xbc, dt = split_cols(x @ w_in, AB_SPLITS)
    gk = jax.nn.log_sigmoid((lr @ w_gk2 + b_gk2).astype(F32)) / GLA_NORMALIZER
    o_gla, s_gla = chunked_gated_linear(heads(q, GLA_HEADS) * GLA_DK ** -0.5, heads(k, GLA_HEADS),
                                        heads(v, GLA_HEADS), heads(gk, GLA_HEADS), s_gla, False, x.dtype)
    o_gla = (rms_norm(o_gla, gla_norm_w) * jax.nn.silu(heads(g_out, GLA_HEADS))).reshape(bsz, L, GLA_VAL)
    xbc, s_conv = causal_conv(xbc, s_conv, conv_w, conv_b)
    xs, Bm, Cm = split_cols(xbc, (SSD_INNER, SSD_GROUPS * SSD_STATE, SSD_GROUPS * SSD_STATE))
    xs = heads(xs, SSD_HEADS)
    rep = SSD_HEADS // SSD_GROUPS
    Bm = jnp.repeat(heads(Bm, SSD_GROUPS), rep, axis=2)
    Cm = jnp.repeat(heads(Cm, SSD_GROUPS), rep, axis=2)
    dt = jax.nn.softplus(dt.astype(F32) + dt_bias.astype(F32))
    A = -jnp.exp(a_log.astype(F32))
    y, s_ssd = chunked_gated_linear(Cm, Bm * dt[..., None], xs, dt * A, s_ssd, True, x.dtype)
    y = y + d_skip[:, None].astype(y.dtype) * xs
    yz = y.reshape(bsz, L, SSD_INNER) * jax.nn.silu(z)
    yz = rms_norm(heads(yz, SSD_GROUPS), ssd_norm_w.reshape(SSD_GROUPS, -1)).reshape(bsz, L, SSD_INNER)
    out = jnp.concatenate([o_gla, yz], axis=-1) @ w_out
    return out, s_gla, s_ssd, s_conv


def hgrn_mixer(x, s_hg, w_in, lb, norm_w, w_out):
    bsz, L, _ = x.shape
    q, f, i, g = split_cols(x @ w_in, C_SPLITS)
    q = jax.nn.silu(q)
    lb = lb.astype(F32)
    forget = lb + (1.0 - lb) * jax.nn.sigmoid(f.astype(F32))
    o, s_hg = chunked_gated_linear(heads(q, HG_HEADS), heads(1.0 - forget, HG_HEADS), heads(i, HG_HEADS),
                                   heads(jnp.log(forget), HG_HEADS), s_hg, False, x.dtype)
    o = (rms_norm(o, norm_w) * jax.nn.silu(heads(g, HG_HEADS))).reshape(bsz, L, HG_I)
    return o @ w_out, s_hg


def moe(h, router_w, router_bias, w_gate, w_up, w_down):
    bsz, L, D = h.shape
    t = h.reshape(-1, D)
    scores = jax.nn.sigmoid((t @ router_w).astype(F32))
    sel = scores + router_bias.astype(F32)
    grp = lax.top_k(sel.reshape(-1, N_GROUPS, EXPERTS_PER_GROUP), TOP_K)[0].sum(-1)
    best = jnp.argmax(grp, axis=-1)
    in_group = (jnp.arange(N_EXPERTS) // EXPERTS_PER_GROUP)[None, :] == best[:, None]
    _, idx = lax.top_k(jnp.where(in_group, sel, -jnp.inf), TOP_K)
    w = jnp.take_along_axis(scores, idx, axis=-1)
    w = w / w.sum(-1, keepdims=True)
    gates = (jax.nn.one_hot(idx, N_EXPERTS, dtype=F32) * w[..., None]).sum(1)
    y = jnp.zeros(t.shape, F32)
    for e in range(N_EXPERTS):
        he = jax.nn.silu(t @ w_gate[e]) * (t @ w_up[e])
        y = y + gates[:, e:e + 1] * (he @ w_down[e])
    return y.astype(h.dtype).reshape(bsz, L, D)


def trunk(x, st_gla, st_ssd, st_conv, st_hg, p, lbs):
    new_gla, new_ssd, new_conv, new_hg = [], [], [], []
    for layer in range(DEPTH):
        j = layer // 2
        if layer % 2 == 0:
            mix, sg, ss, sc = ab_mixer(x, st_gla[j], st_ssd[j], st_conv[j], p['w_in_ab'][j], p['w_gk2'][j],
                                       p['b_gk2'][j], p['gla_norm_w'][j], p['conv_w'][j], p['conv_b'][j],
                                       p['dt_bias'][j], p['a_log'][j], p['d_skip'][j], p['ssd_norm_w'][j],
                                       p['w_out_ab'][j])
            new_gla.append(sg)
            new_ssd.append(ss)
            new_conv.append(sc)
        else:
            mix, sh = hgrn_mixer(x, st_hg[j], p['w_in_c'][j], lbs[layer], p['hg_norm_w'][j], p['w_out_c'][j])
            new_hg.append(sh)
        x = layer_norm(ALPHA * x + mix, p['ln1_w'][layer], p['ln1_b'][layer])
        ff = moe(x, p['router_w'], p['router_bias'], p['w_gate'][layer], p['w_up'][layer], p['w_down'][layer])
        x = layer_norm(ALPHA * x + ff, p['ln2_w'][layer], p['ln2_b'][layer])
    return x, jnp.stack(new_gla), jnp.stack(new_ssd), jnp.stack(new_conv), jnp.stack(new_hg)


def setup_inputs(seed: int = 0) -> dict:
    key = jax.random.key(seed)
    ks = iter(jax.random.split(key, 40))

    def nrm(shape, scale):
        return jax.random.normal(next(ks), shape, F32) * scale

    ab_off = np.cumsum((0,) + AB_SPLITS)
    ab_scale = np.ones(AB_IN, np.float32)
    ab_scale[ab_off[2]:ab_off[3]] = BETA
    ab_scale[ab_off[6]:ab_off[6] + SSD_INNER] = BETA
    c_off = np.cumsum((0,) + C_SPLITS)
    c_scale = np.ones(C_IN, np.float32)
    c_scale[c_off[2]:c_off[3]] = BETA
    dt0 = jnp.exp(jax.random.uniform(next(ks), (N_AB_LAYERS, SSD_HEADS), F32, math.log(1e-3), math.log(1e-1)))
    dt_bias = dt0 + jnp.log(-jnp.expm1(-dt0))
    a_log = jnp.log(jax.random.uniform(next(ks), (N_AB_LAYERS, SSD_HEADS), F32, 1.0, 16.0))
    return {
        'x_prompt': nrm((BATCH, SEQ, D_MODEL), 1.0),
        'x_sample': nrm((DEC_BATCH, DEC_SEQ, D_MODEL), 1.0),
        'state_gla': nrm((N_AB_LAYERS, DEC_BATCH, GLA_HEADS, GLA_DK, GLA_DV), 1.0),
        'state_ssd': nrm((N_AB_LAYERS, DEC_BATCH, SSD_HEADS, SSD_STATE, SSD_HEADDIM), 0.5),
        'state_conv': nrm((N_AB_LAYERS, DEC_BATCH, SSD_CONV - 1, SSD_CONV_DIM), 1.0),
        'state_hgrn': nrm((N_C_LAYERS, DEC_BATCH, HG_HEADS, HG_EXPAND, HG_DI), 1.0),
        'w_in_ab': nrm((N_AB_LAYERS, D_MODEL, AB_IN), D_MODEL ** -0.5) * jnp.asarray(ab_scale),
        'w_gk2': nrm((N_AB_LAYERS, GLA_RANK, GLA_KEY), GLA_RANK ** -0.5),
        'b_gk2': nrm((N_AB_LAYERS, GLA_KEY), 0.1),
        'gla_norm_w': 1.0 + nrm((N_AB_LAYERS, GLA_DV), 0.02),
        'conv_w': nrm((N_AB_LAYERS, SSD_CONV, SSD_CONV_DIM), SSD_CONV ** -0.5),
        'conv_b': nrm((N_AB_LAYERS, SSD_CONV_DIM), 0.02),
        'dt_bias': dt_bias,
        'a_log': a_log,
        'd_skip': 1.0 + nrm((N_AB_LAYERS, SSD_HEADS), 0.02),
        'ssd_norm_w': 1.0 + nrm((N_AB_LAYERS, SSD_INNER), 0.02),
        'w_out_ab': nrm((N_AB_LAYERS, GLA_VAL + SSD_INNER, D_MODEL), (GLA_VAL + SSD_INNER) ** -0.5 * BETA),
        'w_in_c': nrm((N_C_LAYERS, D_MODEL, C_IN), D_MODEL ** -0.5) * jnp.asarray(c_scale),
        'lower_bounds': nrm((DEPTH, HG_F), 0.1),
        'hg_norm_w': 1.0 + nrm((N_C_LAYERS, HG_DI), 0.02),
        'w_out_c': nrm((N_C_LAYERS, HG_I, D_MODEL), HG_I ** -0.5 * BETA),
        'router_w': nrm((D_MODEL, N_EXPERTS), D_MODEL ** -0.5),
        'router_bias': nrm((N_EXPERTS,), 0.01),
        'w_gate': nrm((DEPTH, N_EXPERTS, D_MODEL, D_FF_EXPERT), D_MODEL ** -0.5),
        'w_up': nrm((DEPTH, N_EXPERTS, D_MODEL, D_FF_EXPERT), D_MODEL ** -0.5),
        'w_down': nrm((DEPTH, N_EXPERTS, D_FF_EXPERT, D_MODEL), D_FF_EXPERT ** -0.5 * BETA),
        'ln1_w': 1.0 + nrm((DEPTH, D_MODEL), 0.02),
        'ln1_b': nrm((DEPTH, D_MODEL), 0.02),
        'ln2_w': 1.0 + nrm((DEPTH, D_MODEL), 0.02),
        'ln2_b': nrm((DEPTH, D_MODEL), 0.02),
    }


def reference(x_prompt, x_sample, state_gla, state_ssd, state_conv, state_hgrn,
              w_in_ab, w_gk2, b_gk2, gla_norm_w, conv_w, conv_b, dt_bias, a_log, d_skip, ssd_norm_w, w_out_ab,
              w_in_c, lower_bounds, hg_norm_w, w_out_c,
              router_w, router_bias, w_gate, w_up, w_down,
              ln1_w, ln1_b, ln2_w, ln2_b):
    p = dict(w_in_ab=w_in_ab, w_gk2=w_gk2, b_gk2=b_gk2, gla_norm_w=gla_norm_w, conv_w=conv_w, conv_b=conv_b,
             dt_bias=dt_bias, a_log=a_log, d_skip=d_skip, ssd_norm_w=ssd_norm_w, w_out_ab=w_out_ab,
             w_in_c=w_in_c, hg_norm_w=hg_norm_w, w_out_c=w_out_c, router_w=router_w, router_bias=router_bias,
             w_gate=w_gate, w_up=w_up, w_down=w_down, ln1_w=ln1_w, ln1_b=ln1_b, ln2_w=ln2_w, ln2_b=ln2_b)
    lbs = jax.nn.softmax(lower_bounds.astype(F32), axis=0)
    lbs = jnp.cumsum(lbs, axis=0) - lbs[0]
    bp = x_prompt.shape[0]

    def zeros_like_state(s):
        return jnp.zeros((s.shape[0], bp) + s.shape[2:], x_prompt.dtype)

    y_prompt, gla_p, ssd_p, conv_p, hg_p = trunk(x_prompt, zeros_like_state(state_gla), zeros_like_state(state_ssd),
                                                 zeros_like_state(state_conv), zeros_like_state(state_hgrn), p, lbs)
    y_sample, gla_s, ssd_s, conv_s, hg_s = trunk(x_sample, state_gla, state_ssd, state_conv, state_hgrn, p, lbs)
    return (y_prompt, y_sample, gla_p, ssd_p, conv_p, hg_p, gla_s, ssd_s, conv_s, hg_s)
```

```python
import functools

import numpy as np
import jax
import jax.numpy as jnp
from jax import lax
from jax.experimental import pallas as pl
from jax.experimental.pallas import tpu as pltpu

F32 = jnp.float32
BF16 = jnp.bfloat16

D_MODEL = 1024
DEPTH = 2
GLA_HEADS = 4
GLA_DK = 128
GLA_DV = 256
GLA_KEY = GLA_HEADS * GLA_DK
GLA_VAL = GLA_HEADS * GLA_DV
GLA_RANK = 16
GLA_NORMALIZER = 16.0
SSD_INNER = 1024
SSD_HEADDIM = 64
SSD_HEADS = 16
SSD_STATE = 128
SSD_GROUPS = 2
SSD_CONV = 4
SSD_GROUP_W = SSD_INNER // SSD_GROUPS
SSD_BC = SSD_GROUPS * SSD_STATE
SSD_CONV_DIM = SSD_INNER + 2 * SSD_BC
HG_EXPAND = 128
HG_HEADS = 8
HG_F = HG_HEADS * HG_EXPAND
HG_I = D_MODEL
HG_DI = HG_I // HG_HEADS
N_EXPERTS = 16
N_GROUPS = 4
EXPERTS_PER_GROUP = 4
D_FF_EXPERT = 512
ALPHA = (2 * DEPTH) ** 0.25
EPS = 1e-5

LANES = 128
VMEM_LIMIT = 48 * 1024 * 1024

AB_Z = 0
AB_V = 1024
AB_GOUT = 2048
AB_XBC = 3072
AB_Q = 4608
AB_K = 5120
AB_SMALL = 5632
AB_COLS = 5760
C_COLS = 4096

VEC_CHUNK = 64
VEC_SUB = 16
VEC_ROWS = 256
SSD_CHUNK = 128
STEP_B = 8


def _params(*sem):
    return pltpu.CompilerParams(dimension_semantics=sem, vmem_limit_bytes=VMEM_LIMIT)


def _dot(a, b):
    return jnp.dot(a.astype(BF16), b.astype(BF16), preferred_element_type=F32)


def _dot_nt(a, b):
    return lax.dot_general(a.astype(BF16), b.astype(BF16), (((1,), (1,)), ((), ())),
                           preferred_element_type=F32)


def _dot_tn(a, b):
    return lax.dot_general(a.astype(BF16), b.astype(BF16), (((0,), (0,)), ((), ())),
                           preferred_element_type=F32)


def _split3(a):
    hi = a.astype(BF16)
    r1 = a - hi.astype(F32)
    mid = r1.astype(BF16)
    lo = (r1 - mid.astype(F32)).astype(BF16)
    return hi, mid, lo


def _dot_exact_rhs(sel, a):
    hi, mid, lo = _split3(a)
    d = lambda p: jnp.dot(sel, p, preferred_element_type=F32)
    return (d(lo) + d(mid)) + d(hi)


def _dot_exact_lhs(a, sel):
    hi, mid, lo = _split3(a)
    d = lambda p: jnp.dot(p, sel, preferred_element_type=F32)
    return (d(lo) + d(mid)) + d(hi)


def _tril(n):
    r = lax.broadcasted_iota(jnp.int32, (n, n), 0)
    c = lax.broadcasted_iota(jnp.int32, (n, n), 1)
    return r >= c


def _sigmoid(x):
    return 1.0 / (1.0 + jnp.exp(-x))


def _silu(x):
    return x * _sigmoid(x)


def _softplus(x):
    return jnp.maximum(x, 0.0) + jnp.log(1.0 + jnp.exp(-jnp.abs(x)))


def _log_sigmoid(x):
    return -_softplus(-x)


def _rms(x, w):
    return x * lax.rsqrt(jnp.mean(x * x, axis=-1, keepdims=True) + EPS) * w


def _layer_norm(x, w, b):
    mu = jnp.mean(x, axis=-1, keepdims=True)
    xc = x - mu
    var = jnp.mean(xc * xc, axis=-1, keepdims=True)
    return xc * lax.rsqrt(var + EPS) * w + b


def _proj_kernel(x_ref, w_ref, o_ref):
    o_ref[...] = jnp.dot(x_ref[...].astype(BF16), w_ref[...], preferred_element_type=F32)


def _proj(x, w, tm, tn):
    t, k = x.shape
    n = w.shape[1]
    return pl.pallas_call(
        _proj_kernel,
        out_shape=jax.ShapeDtypeStruct((t, n), F32),
        grid=(t // tm, n // tn),
        in_specs=[pl.BlockSpec((tm, k), lambda i, j: (i, 0)),
                  pl.BlockSpec((k, tn), lambda i, j: (0, j))],
        out_specs=pl.BlockSpec((tm, tn), lambda i, j: (i, j)),
        compiler_params=_params("parallel", "arbitrary"),
        name="in_proj",
    )(x, w)


def _outproj_ln_kernel(n_in, *refs):
    a_refs = refs[:n_in]
    w_refs = refs[n_in:2 * n_in]
    x_ref, lw_ref, lb_ref, o_ref = refs[2 * n_in:]
    mix = jnp.dot(a_refs[0][...].astype(BF16), w_refs[0][...], preferred_element_type=F32)
    for a_ref, w_ref in zip(a_refs[1:], w_refs[1:]):
        mix = mix + jnp.dot(a_ref[...].astype(BF16), w_ref[...], preferred_element_type=F32)
    o_ref[...] = _layer_norm(ALPHA * x_ref[...] + mix, lw_ref[...], lb_ref[...])


def _outproj_ln(acts, ws, x, ln_w, ln_b, tm):
    t = x.shape[0]
    n_in = len(acts)
    row = lambda i: (i, 0)
    fixed = lambda i: (0, 0)
    in_specs = ([pl.BlockSpec((tm, a.shape[1]), row) for a in acts]
                + [pl.BlockSpec(w.shape, fixed) for w in ws]
                + [pl.BlockSpec((tm, D_MODEL), row),
                   pl.BlockSpec((1, D_MODEL), fixed), pl.BlockSpec((1, D_MODEL), fixed)])
    return pl.pallas_call(
        functools.partial(_outproj_ln_kernel, n_in),
        out_shape=jax.ShapeDtypeStruct((t, D_MODEL), F32),
        grid=(t // tm,),
        in_specs=in_specs,
        out_specs=pl.BlockSpec((tm, D_MODEL), row),
        compiler_params=_params("parallel"),
        name="out_proj_ln",
    )(*acts, *ws, x, ln_w, ln_b)


def _router_kernel(x_ref, rwt_ref, bias_ref, g_ref):
    tm = x_ref.shape[0]
    w3 = _split3(rwt_ref[...])
    x3 = _split3(x_ref[...])
    nt = lambda a, b: lax.dot_general(a, b, (((1,), (1,)), ((), ())), preferred_element_type=F32)
    logits = (((nt(w3[0], x3[2]) + nt(w3[2], x3[0])) + nt(w3[1], x3[1]))
              + (nt(w3[0], x3[1]) + nt(w3[1], x3[0]))) + nt(w3[0], x3[0])
    scores = _sigmoid(logits)
    sel = scores + bias_ref[...]
    s = [sel[e:e + 1, :] for e in range(N_EXPERTS)]
    sc = [scores[e:e + 1, :] for e in range(N_EXPERTS)]
    grp = []
    for g in range(N_GROUPS):
        m = s[g * EXPERTS_PER_GROUP:(g + 1) * EXPERTS_PER_GROUP]
        best = None
        for i in range(EXPERTS_PER_GROUP):
            for j in range(i + 1, EXPERTS_PER_GROUP):
                p = m[i] + m[j]
                best = p if best is None else jnp.maximum(best, p)
        grp.append(best)
    best_g = jnp.zeros((1, tm), jnp.int32)
    best_v = grp[0]
    for g in range(1, N_GROUPS):
        upd = grp[g] > best_v
        best_g = jnp.where(upd, g, best_g)
        best_v = jnp.where(upd, grp[g], best_v)
    neg = jnp.full((1, tm), -jnp.inf, F32)
    ms = [jnp.where(best_g == e // EXPERTS_PER_GROUP, s[e], neg) for e in range(N_EXPERTS)]

    def first_argmax(vals):
        idx = jnp.zeros((1, tm), jnp.int32)
        top = vals[0]
        for e in range(1, N_EXPERTS):
            upd = vals[e] > top
            idx = jnp.where(upd, e, idx)
            top = jnp.where(upd, vals[e], top)
        return idx

    idx1 = first_argmax(ms)
    idx2 = first_argmax([jnp.where(idx1 == e, neg, ms[e]) for e in range(N_EXPERTS)])
    zero = jnp.zeros((1, tm), F32)
    w1 = zero
    w2 = zero
    for e in range(N_EXPERTS):
        w1 = w1 + jnp.where(idx1 == e, sc[e], zero)
        w2 = w2 + jnp.where(idx2 == e, sc[e], zero)
    tot = w1 + w2
    g1 = w1 / tot
    g2 = w2 / tot
    rows = [jnp.where(idx1 == e, g1, zero) + jnp.where(idx2 == e, g2, zero)
            for e in range(N_EXPERTS)]
    gates_t = jnp.concatenate(rows + [jnp.zeros((LANES - N_EXPERTS, tm), F32)], axis=0)
    g_ref[...] = gates_t.T


def _router(x, rwt, bias, tm):
    t = x.shape[0]
    return pl.pallas_call(
        _router_kernel,
        out_shape=jax.ShapeDtypeStruct((t, LANES), F32),
        grid=(t // tm,),
        in_specs=[pl.BlockSpec((tm, D_MODEL), lambda i: (i, 0)),
                  pl.BlockSpec((N_EXPERTS, D_MODEL), lambda i: (0, 0)),
                  pl.BlockSpec((N_EXPERTS, 1), lambda i: (0, 0))],
        out_specs=pl.BlockSpec((tm, LANES), lambda i: (i, 0)),
        compiler_params=_params("parallel"),
        name="router",
    )(x, rwt, bias)


def _moe_kernel(x_ref, g_ref, wg_ref, wu_ref, wd_ref, lw_ref, lb_ref, o_ref, acc_ref, xb_ref):
    e = pl.program_id(1)

    @pl.when(e == 0)
    def _():
        xb_ref[...] = x_ref[...].astype(BF16)
        acc_ref[...] = jnp.zeros_like(acc_ref)

    xb = xb_ref[...]
    hg = jnp.dot(xb, wg_ref[0], preferred_element_type=F32)
    hu = jnp.dot(xb, wu_ref[0], preferred_element_type=F32)
    he = _silu(hg) * hu
    gates = g_ref[...]
    lane = lax.broadcasted_iota(jnp.int32, gates.shape, 1)
    ge = jnp.sum(jnp.where(lane == e, gates, 0.0), axis=1, keepdims=True)
    acc_ref[...] += ge * jnp.dot(he.astype(BF16), wd_ref[0], preferred_element_type=F32)

    @pl.when(e == N_EXPERTS - 1)
    def _():
        o_ref[...] = _layer_norm(ALPHA * x_ref[...] + acc_ref[...], lw_ref[...], lb_ref[...])


def _moe_ln(x, gates, wg, wu, wd, ln_w, ln_b, tm):
    t = x.shape[0]
    return pl.pallas_call(
        _moe_kernel,
        out_shape=jax.ShapeDtypeStruct((t, D_MODEL), F32),
        grid=(t // tm, N_EXPERTS),
        in_specs=[pl.BlockSpec((tm, D_MODEL), lambda i, e: (i, 0)),
                  pl.BlockSpec((tm, LANES), lambda i, e: (i, 0)),
                  pl.BlockSpec((1, D_MODEL, D_FF_EXPERT), lambda i, e: (e, 0, 0)),
                  pl.BlockSpec((1, D_MODEL, D_FF_EXPERT), lambda i, e: (e, 0, 0)),
                  pl.BlockSpec((1, D_FF_EXPERT, D_MODEL), lambda i, e: (e, 0, 0)),
                  pl.BlockSpec((1, D_MODEL), lambda i, e: (0, 0)),
                  pl.BlockSpec((1, D_MODEL), lambda i, e: (0, 0))],
        out_specs=pl.BlockSpec((tm, D_MODEL), lambda i, e: (i, 0)),
        scratch_shapes=[pltpu.VMEM((tm, D_MODEL), F32), pltpu.VMEM((tm, D_MODEL), BF16)],
        compiler_params=_params("parallel", "arbitrary"),
        name="moe_ln",
    )(x, gates, wg, wu, wd, ln_w, ln_b)


def _vec_chunk(q, k, v, g, st):
    c = q.shape[0]
    nsub = c // VEC_SUB
    big_g = _dot_exact_rhs(_tril(c).astype(BF16), g)
    g_ex = big_g - g
    o_rows = []
    inter = _dot_nt(q * jnp.exp(big_g), st)
    for i in range(nsub):
        o_rows.append(inter[i * VEC_SUB:(i + 1) * VEC_SUB])
    for j in range(nsub):
        r0 = j * VEC_SUB
        base = g_ex[r0:r0 + 1, :]
        kj = k[r0:r0 + VEC_SUB] * jnp.exp(base - big_g[r0:r0 + VEC_SUB])
        qj = q[r0:] * jnp.exp(big_g[r0:] - base)
        a = _dot_nt(qj, kj)
        rr = lax.broadcasted_iota(jnp.int32, a.shape, 0)
        cc = lax.broadcasted_iota(jnp.int32, a.shape, 1)
        a = jnp.where(rr >= cc, a, 0.0)
        contrib = _dot(a, v[r0:r0 + VEC_SUB])
        for i in range(j, nsub):
            o_rows[i] = o_rows[i] + contrib[(i - j) * VEC_SUB:(i - j + 1) * VEC_SUB]
    g_last = big_g[c - 1:c, :]
    kd = k * jnp.exp(g_last - big_g)
    st_new = st * jnp.exp(g_last) + _dot_tn(v, kd)
    return jnp.concatenate(o_rows, axis=0), st_new


def _gla_chunk_kernel(q_ref, k_ref, v_ref, go_ref, sm_ref, w2_ref, b2_ref, nw_ref,
                      o_ref, s_ref, st_ref):
    r = pl.program_id(2)

    @pl.when(r == 0)
    def _():
        st_ref[...] = jnp.zeros_like(st_ref)

    st = st_ref[...]
    for ci in range(VEC_ROWS // VEC_CHUNK):
        rows = pl.ds(ci * VEC_CHUNK, VEC_CHUNK)
        gk = _log_sigmoid(_dot(sm_ref[rows, :], w2_ref[0]) + b2_ref[0]) / GLA_NORMALIZER
        q = q_ref[rows, :] * (GLA_DK ** -0.5)
        o, st = _vec_chunk(q, k_ref[rows, :], v_ref[rows, :], gk, st)
        o_ref[rows, :] = _rms(o, nw_ref[...]) * _silu(go_ref[rows, :])
    st_ref[...] = st

    @pl.when(r == pl.num_programs(2) - 1)
    def _():
        s_ref[0, 0] = st.T


def _gla_prompt(proj, w2p, b2, norm_w, bsz, seq):
    nr = seq // VEC_ROWS
    row = lambda off: (lambda b, h, r: (b * nr + r, off + h))
    return pl.pallas_call(
        _gla_chunk_kernel,
        out_shape=(jax.ShapeDtypeStruct((bsz * seq, GLA_VAL), F32),
                   jax.ShapeDtypeStruct((bsz, GLA_HEADS, GLA_DK, GLA_DV), F32)),
        grid=(bsz, GLA_HEADS, nr),
        in_specs=[pl.BlockSpec((VEC_ROWS, GLA_DK), row(AB_Q // GLA_DK)),
                  pl.BlockSpec((VEC_ROWS, GLA_DK), row(AB_K // GLA_DK)),
                  pl.BlockSpec((VEC_ROWS, GLA_DV), row(AB_V // GLA_DV)),
                  pl.BlockSpec((VEC_ROWS, GLA_DV), row(AB_GOUT // GLA_DV)),
                  pl.BlockSpec((VEC_ROWS, LANES), lambda b, h, r: (b * nr + r, AB_SMALL // LANES)),
                  pl.BlockSpec((1, LANES, GLA_DK), lambda b, h, r: (h, 0, 0)),
                  pl.BlockSpec((1, 1, GLA_DK), lambda b, h, r: (h, 0, 0)),
                  pl.BlockSpec((1, GLA_DV), lambda b, h, r: (0, 0))],
        out_specs=(pl.BlockSpec((VEC_ROWS, GLA_DV), lambda b, h, r: (b * nr + r, h)),
                   pl.BlockSpec((1, 1, GLA_DK, GLA_DV), lambda b, h, r: (b, h, 0, 0))),
        scratch_shapes=[pltpu.VMEM((GLA_DV, GLA_DK), F32)],
        compiler_params=_params("parallel", "parallel", "arbitrary"),
        name="gla_chunk",
    )(proj, proj, proj, proj, proj, w2p, b2, norm_w)


def _hgrn_lower_bound(lbraw, layer):
    m = jnp.max(lbraw, axis=0, keepdims=True)
    ex = jnp.exp(lbraw - m)
    sm = ex / jnp.sum(ex, axis=0, keepdims=True)
    acc = sm[0:1]
    for i in range(1, layer + 1):
        acc = acc + sm[i:i + 1]
    return acc - sm[0:1]


def _hgrn_gates(q_raw, f_raw, lb):
    forget = lb + (1.0 - lb) * _sigmoid(f_raw)
    return _silu(q_raw), 1.0 - forget, jnp.log(forget)


def _hgrn_chunk_kernel(layer, q_ref, f_ref, i_ref, go_ref, lb_ref, nw_ref, o_ref, s_ref, st_ref):
    r = pl.program_id(2)

    @pl.when(r == 0)
    def _():
        st_ref[...] = jnp.zeros_like(st_ref)

    lb = _hgrn_lower_bound(lb_ref[...], layer)
    st = st_ref[...]
    for ci in range(VEC_ROWS // VEC_CHUNK):
        rows = pl.ds(ci * VEC_CHUNK, VEC_CHUNK)
        q, k, g = _hgrn_gates(q_ref[rows, :], f_ref[rows, :], lb)
        o, st = _vec_chunk(q, k, i_ref[rows, :], g, st)
        o_ref[rows, :] = _rms(o, nw_ref[...]) * _silu(go_ref[rows, :])
    st_ref[...] = st

    @pl.when(r == pl.num_programs(2) - 1)
    def _():
        s_ref[0, 0] = st.T


def _hgrn_prompt(proj, lower_bounds, norm_w, layer, bsz, seq):
    nr = seq // VEC_ROWS
    nh = HG_HEADS
    row = lambda off: (lambda b, h, r: (b * nr + r, off + h))
    return pl.pallas_call(
        functools.partial(_hgrn_chunk_kernel, layer),
        out_shape=(jax.ShapeDtypeStruct((bsz * seq, HG_I), F32),
                   jax.ShapeDtypeStruct((bsz, nh, HG_EXPAND, HG_DI), F32)),
        grid=(bsz, nh, nr),
        in_specs=[pl.BlockSpec((VEC_ROWS, HG_EXPAND), row(0)),
                  pl.BlockSpec((VEC_ROWS, HG_EXPAND), row(nh)),
                  pl.BlockSpec((VEC_ROWS, HG_DI), row(2 * nh)),
                  pl.BlockSpec((VEC_ROWS, HG_DI), row(3 * nh)),
                  pl.BlockSpec((DEPTH, HG_EXPAND), lambda b, h, r: (0, h)),
                  pl.BlockSpec((1, HG_DI), lambda b, h, r: (0, 0))],
        out_specs=(pl.BlockSpec((VEC_ROWS, HG_DI), lambda b, h, r: (b * nr + r, h)),
                   pl.BlockSpec((1, 1, HG_EXPAND, HG_DI), lambda b, h, r: (b, h, 0, 0))),
        scratch_shapes=[pltpu.VMEM((HG_DI, HG_EXPAND), F32)],
        compiler_params=_params("parallel", "parallel", "arbitrary"),
        name="hgrn_chunk",
    )(proj, proj, proj, proj, lower_bounds, norm_w)


def _conv_silu(xp, cw, cb, n, lead):
    acc = cb + cw[SSD_CONV - 1:SSD_CONV] * xp[lead:lead + n]
    for m in range(1, SSD_CONV):
        acc = acc + cw[SSD_CONV - 1 - m:SSD_CONV - m] * xp[lead - m:lead - m + n]
    return _silu(acc)


def _ssd_gate_norm(y, z, nw):
    yz = y * _silu(z)
    parts = []
    for g in range(SSD_GROUPS):
        cols = slice(g * SSD_GROUP_W, (g + 1) * SSD_GROUP_W)
        parts.append(_rms(yz[:, cols], nw[:, cols]))
    return jnp.concatenate(parts, axis=1)


def _ssd_chunk_kernel(z_ref, xbc_ref, sm_ref, cw_ref, cb_ref, dtb_ref, alog_ref, dsk_ref,
                      nw_ref, ex_ref, o_ref, s_ref, conv_ref, st_ref, prev_ref):
    r = pl.program_id(1)
    c = SSD_CHUNK

    @pl.when(r == 0)
    def _():
        st_ref[...] = jnp.zeros_like(st_ref)
        prev_ref[...] = jnp.zeros_like(prev_ref)

    x_raw = xbc_ref[...]
    xp = jnp.concatenate([prev_ref[...], x_raw], axis=0)
    prev_ref[...] = x_raw[c - 8:c]
    xc = _conv_silu(xp, cw_ref[...], cb_ref[...], c, 8)
    xs = xc[:, :SSD_INNER]
    bm = xc[:, SSD_INNER:SSD_INNER + SSD_BC]
    cm = xc[:, SSD_INNER + SSD_BC:]

    dt = _softplus(sm_ref[...] + dtb_ref[...])
    a_neg = -jnp.exp(alog_ref[...])
    big_g = _dot_exact_rhs(_tril(c).astype(BF16), dt * a_neg)
    g_t = big_g.T
    g_last = big_g[c - 1:c, :]
    ex = ex_ref[...]
    dt_x = _dot_exact_lhs(dt, ex)
    eg_x = _dot_exact_lhs(jnp.exp(big_g), ex)
    w_x = _dot_exact_lhs(dt * jnp.exp(g_last - big_g), ex)
    xdt = xs * dt_x
    xw = xs * w_x
    causal = _tril(c)
    lane = lax.broadcasted_iota(jnp.int32, (c, LANES), 1)
    st = st_ref[...]
    y_parts = []
    u_parts = []
    for g in range(SSD_GROUPS):
        gcols = slice(g * SSD_GROUP_W, (g + 1) * SSD_GROUP_W)
        bg = bm[:, g * SSD_STATE:(g + 1) * SSD_STATE]
        cg = cm[:, g * SSD_STATE:(g + 1) * SSD_STATE]
        sc = _dot_nt(cg, bg)
        inter = _dot(cg, st[:, gcols])
        u_parts.append(_dot_tn(bg, xw[:, gcols]))
        pair_cols = []
        heads_per_group = SSD_HEADS // SSD_GROUPS
        for p in range(heads_per_group // 2):
            h0 = g * heads_per_group + 2 * p
            xpair = xdt[:, h0 * SSD_HEADDIM:(h0 + 2) * SSD_HEADDIM]
            ys = []
            for h in (h0, h0 + 1):
                diff = big_g[:, h:h + 1] - g_t[h:h + 1, :]
                dec = jnp.exp(jnp.where(causal, diff, -jnp.inf))
                ys.append(_dot(sc * dec, xpair))
            pair_cols.append(jnp.where(lane < SSD_HEADDIM, ys[0], ys[1]))
        y_intra = jnp.concatenate(pair_cols, axis=1)
        y_parts.append(y_intra + inter * eg_x[:, gcols])
    y = jnp.concatenate(y_parts, axis=1) + dsk_ref[...] * xs
    o_ref[...] = _ssd_gate_norm(y, z_ref[...], nw_ref[...])
    st = st * eg_x[c - 1:c, :] + jnp.concatenate(u_parts, axis=1)
    st_ref[...] = st

    @pl.when(r == pl.num_programs(1) - 1)
    def _():
        s_ref[0] = st
        conv_ref[0] = x_raw[c - (SSD_CONV - 1):c]


def _ssd_prompt(proj, conv_w, conv_b, dtb_p, alog_p, dskip_x, norm_w, expand, bsz, seq):
    nr = seq // SSD_CHUNK
    fixed = lambda b, r: (0, 0)
    return pl.pallas_call(
        _ssd_chunk_kernel,
        out_shape=(jax.ShapeDtypeStruct((bsz * seq, SSD_INNER), F32),
                   jax.ShapeDtypeStruct((bsz, SSD_STATE, SSD_INNER), F32),
                   jax.ShapeDtypeStruct((bsz, SSD_CONV - 1, SSD_CONV_DIM), F32)),
        grid=(bsz, nr),
        in_specs=[pl.BlockSpec((SSD_CHUNK, SSD_INNER), lambda b, r: (b * nr + r, AB_Z // SSD_INNER)),
                  pl.BlockSpec((SSD_CHUNK, SSD_CONV_DIM), lambda b, r: (b * nr + r, AB_XBC // SSD_CONV_DIM)),
                  pl.BlockSpec((SSD_CHUNK, LANES), lambda b, r: (b * nr + r, AB_SMALL // LANES)),
                  pl.BlockSpec((SSD_CONV, SSD_CONV_DIM), fixed),
                  pl.BlockSpec((1, SSD_CONV_DIM), fixed),
                  pl.BlockSpec((1, LANES), fixed),
                  pl.BlockSpec((1, LANES), fixed),
                  pl.BlockSpec((1, SSD_INNER), fixed),
                  pl.BlockSpec((1, SSD_INNER), fixed),
                  pl.BlockSpec((LANES, SSD_INNER), fixed)],
        out_specs=(pl.BlockSpec((SSD_CHUNK, SSD_INNER), lambda b, r: (b * nr + r, 0)),
                   pl.BlockSpec((1, SSD_STATE, SSD_INNER), lambda b, r: (b, 0, 0)),
                   pl.BlockSpec((1, SSD_CONV - 1, SSD_CONV_DIM), lambda b, r: (b, 0, 0))),
        scratch_shapes=[pltpu.VMEM((SSD_STATE, SSD_INNER), F32),
                        pltpu.VMEM((8, SSD_CONV_DIM), F32)],
        compiler_params=_params("parallel", "arbitrary"),
        name="ssd_chunk",
    )(proj, proj, proj, conv_w, conv_b, dtb_p, alog_p, dskip_x, norm_w, expand)


def _ab_prep_kernel(q_ref, sm_ref, xbc_ref, cs_ref, w2_ref, b2_ref, cw_ref, cb_ref, dtb_ref,
                    alog_ref, qs_ref, dec_ref, xc_ref, dt_ref, da_ref, cs_out_ref):
    sm = sm_ref[...]
    gk = _log_sigmoid(_dot(sm, w2_ref[...]) + b2_ref[...]) / GLA_NORMALIZER
    qs_ref[...] = q_ref[...] * (GLA_DK ** -0.5)
    dec_ref[...] = jnp.exp(gk)
    cw = cw_ref[...]
    x_raw = xbc_ref[...]
    acc = cb_ref[...] + cw[SSD_CONV - 1:SSD_CONV] * x_raw
    for j in range(SSD_CONV - 1):
        acc = acc + cw[j:j + 1] * cs_ref[j]
    xc_ref[...] = _silu(acc)
    for j in range(SSD_CONV - 2):
        cs_out_ref[j] = cs_ref[j + 1]
    cs_out_ref[SSD_CONV - 2] = x_raw
    dt = _softplus(sm + dtb_ref[...])
    dt_ref[...] = dt
    da_ref[...] = jnp.exp(dt * -jnp.exp(alog_ref[...]))


def _ab_prep(proj, conv_state, w2_wide, b2_wide, conv_w, conv_b, dtb_p, alog_p):
    bsz = proj.shape[0]
    fixed = lambda i: (0, 0)
    sds = jax.ShapeDtypeStruct
    return pl.pallas_call(
        _ab_prep_kernel,
        out_shape=(sds((bsz, GLA_KEY), F32), sds((bsz, GLA_KEY), F32),
                   sds((bsz, SSD_CONV_DIM), F32), sds((bsz, LANES), F32), sds((bsz, LANES), F32),
                   sds((SSD_CONV - 1, bsz, SSD_CONV_DIM), F32)),
        grid=(1,),
        in_specs=[pl.BlockSpec((bsz, GLA_KEY), lambda i: (0, AB_Q // GLA_KEY)),
                  pl.BlockSpec((bsz, LANES), lambda i: (0, AB_SMALL // LANES)),
                  pl.BlockSpec((bsz, SSD_CONV_DIM), lambda i: (0, AB_XBC // SSD_CONV_DIM)),
                  pl.BlockSpec((SSD_CONV - 1, bsz, SSD_CONV_DIM), lambda i: (0, 0, 0)),
                  pl.BlockSpec((LANES, GLA_KEY), fixed),
                  pl.BlockSpec((1, GLA_KEY), fixed),
                  pl.BlockSpec((SSD_CONV, SSD_CONV_DIM), fixed),
                  pl.BlockSpec((1, SSD_CONV_DIM), fixed),
                  pl.BlockSpec((1, LANES), fixed),
                  pl.BlockSpec((1, LANES), fixed)],
        out_specs=(pl.BlockSpec((bsz, GLA_KEY), fixed), pl.BlockSpec((bsz, GLA_KEY), fixed),
                   pl.BlockSpec((bsz, SSD_CONV_DIM), fixed), pl.BlockSpec((bsz, LANES), fixed),
                   pl.BlockSpec((bsz, LANES), fixed),
                   pl.BlockSpec((SSD_CONV - 1, bsz, SSD_CONV_DIM), lambda i: (0, 0, 0))),
        compiler_params=_params("arbitrary"),
        name="ab_prep",
    )(proj, proj, proj, conv_state, w2_wide, b2_wide, conv_w, conv_b, dtb_p, alog_p)


def _hgrn_prep_kernel(layer, q_ref, f_ref, lb_ref, qs_ref, k_ref, dec_ref):
    lb = _hgrn_lower_bound(lb_ref[...], layer)
    forget = lb + (1.0 - lb) * _sigmoid(f_ref[...])
    qs_ref[...] = _silu(q_ref[...])
    k_ref[...] = 1.0 - forget
    dec_ref[...] = jnp.exp(jnp.log(forget))


def _hgrn_prep(proj, lower_bounds, layer):
    bsz = proj.shape[0]
    blk = lambda j: pl.BlockSpec((bsz, HG_F), lambda i: (0, j))
    return pl.pallas_call(
        functools.partial(_hgrn_prep_kernel, layer),
        out_shape=tuple(jax.ShapeDtypeStruct((bsz, HG_F), F32) for _ in range(3)),
        grid=(1,),
        in_specs=[blk(0), blk(1), pl.BlockSpec((DEPTH, HG_F), lambda i: (0, 0))],
        out_specs=tuple(blk(0) for _ in range(3)),
        compiler_params=_params("arbitrary"),
        name="hgrn_prep",
    )(proj, proj, lower_bounds)


def _vec_step_kernel(s_ref, q_ref, k_ref, d_ref, v_ref, go_ref, nw_ref, so_ref, o_ref):
    qt = q_ref[0, 0]
    kt = k_ref[0, 0]
    dt = d_ref[0, 0]
    v = v_ref[...]
    rows = []
    for b in range(STEP_B):
        sn = s_ref[b, 0] * dt[:, b:b + 1] + kt[:, b:b + 1] * v[b:b + 1, :]
        so_ref[b, 0] = sn
        rows.append(jnp.sum(qt[:, b:b + 1] * sn, axis=0, keepdims=True))
    o = jnp.concatenate(rows, axis=0)
    o_ref[...] = _rms(o, nw_ref[...]) * _silu(go_ref[...])


def _vec_step(state, q_cols, k_cols, d_cols, vsrc, v_off, gsrc, g_off, norm_w):
    bsz, nh, kdim, vdim = state.shape
    col = lambda j, h: (h, j, 0, 0)
    return pl.pallas_call(
        _vec_step_kernel,
        out_shape=(jax.ShapeDtypeStruct(state.shape, F32),
                   jax.ShapeDtypeStruct((bsz, nh * vdim), F32)),
        grid=(bsz // STEP_B, nh),
        in_specs=[pl.BlockSpec((STEP_B, 1, kdim, vdim), lambda j, h: (j, h, 0, 0)),
                  pl.BlockSpec((1, 1, kdim, STEP_B), col),
                  pl.BlockSpec((1, 1, kdim, STEP_B), col),
                  pl.BlockSpec((1, 1, kdim, STEP_B), col),
                  pl.BlockSpec((STEP_B, vdim), lambda j, h: (j, v_off + h)),
                  pl.BlockSpec((STEP_B, vdim), lambda j, h: (j, g_off + h)),
                  pl.BlockSpec((1, vdim), lambda j, h: (0, 0))],
        out_specs=(pl.BlockSpec((STEP_B, 1, kdim, vdim), lambda j, h: (j, h, 0, 0)),
                   pl.BlockSpec((STEP_B, vdim), lambda j, h: (j, h))),
        compiler_params=_params("parallel", "parallel"),
        name="vec_step",
    )(state, q_cols, k_cols, d_cols, vsrc, gsrc, norm_w)


def _ssd_step_kernel(s_ref, b_ref, c_ref, x_ref, dt_ref, da_ref, dsk_ref, so_ref, y_ref):
    bt = b_ref[0, 0]
    ct = c_ref[0, 0]
    x = x_ref[...]
    dt = dt_ref[0]
    da = da_ref[0]
    hpg = SSD_HEADS // SSD_GROUPS
    rows = []
    for b in range(STEP_B):
        pieces = []
        for hh in range(hpg):
            xh = x[b:b + 1, hh * SSD_HEADDIM:(hh + 1) * SSD_HEADDIM]
            sn = s_ref[b, hh] * da[b:b + 1, hh:hh + 1] + (bt[:, b:b + 1] * dt[b:b + 1, hh:hh + 1]) * xh
            so_ref[b, hh] = sn
            pieces.append(jnp.sum(ct[:, b:b + 1] * sn, axis=0, keepdims=True))
        rows.append(jnp.concatenate(pieces, axis=1))
    y_ref[...] = jnp.concatenate(rows, axis=0) + dsk_ref[...] * x


def _ssd_step(state, b_cols, c_cols, xc, dt_g, da_g, dskip_x):
    bsz = state.shape[0]
    hpg = SSD_HEADS // SSD_GROUPS
    col = lambda j, g: (g, j, 0, 0)
    return pl.pallas_call(
        _ssd_step_kernel,
        out_shape=(jax.ShapeDtypeStruct(state.shape, F32),
                   jax.ShapeDtypeStruct((bsz, SSD_INNER), F32)),
        grid=(bsz // STEP_B, SSD_GROUPS),
        in_specs=[pl.BlockSpec((STEP_B, hpg, SSD_STATE, SSD_HEADDIM), lambda j, g: (j, g, 0, 0)),
                  pl.BlockSpec((1, 1, SSD_STATE, STEP_B), col),
                  pl.BlockSpec((1, 1, SSD_STATE, STEP_B), col),
                  pl.BlockSpec((STEP_B, SSD_GROUP_W), lambda j, g: (j, g)),
                  pl.BlockSpec((1, STEP_B, LANES), lambda j, g: (g, j, 0)),
                  pl.BlockSpec((1, STEP_B, LANES), lambda j, g: (g, j, 0)),
                  pl.BlockSpec((1, SSD_GROUP_W), lambda j, g: (0, g))],
        out_specs=(pl.BlockSpec((STEP_B, hpg, SSD_STATE, SSD_HEADDIM), lambda j, g: (j, g, 0, 0)),
                   pl.BlockSpec((STEP_B, SSD_GROUP_W), lambda j, g: (j, g))),
        compiler_params=_params("parallel", "parallel"),
        name="ssd_step",
    )(state, b_cols, c_cols, xc, dt_g, da_g, dskip_x)


def _ssd_post_kernel(y_ref, z_ref, nw_ref, o_ref):
    o_ref[...] = _ssd_gate_norm(y_ref[...], z_ref[...], nw_ref[...])


def _ssd_post(y, proj, norm_w):
    bsz = y.shape[0]
    return pl.pallas_call(
        _ssd_post_kernel,
        out_shape=jax.ShapeDtypeStruct((bsz, SSD_INNER), F32),
        grid=(1,),
        in_specs=[pl.BlockSpec((bsz, SSD_INNER), lambda i: (0, 0)),
                  pl.BlockSpec((bsz, SSD_INNER), lambda i: (0, AB_Z // SSD_INNER)),
                  pl.BlockSpec((1, SSD_INNER), lambda i: (0, 0))],
        out_specs=pl.BlockSpec((bsz, SSD_INNER), lambda i: (0, 0)),
        compiler_params=_params("arbitrary"),
        name="ssd_post",
    )(y, proj, norm_w)


def _to_cols(a, nh):
    bsz = a.shape[0]
    return a.reshape(bsz // STEP_B, STEP_B, nh, -1).transpose(2, 0, 3, 1)


def _prep_weights(w_in_ab, w_gk2, b_gk2, gla_norm_w, conv_w, conv_b, dt_bias, a_log, d_skip,
                  ssd_norm_w, w_out_ab, w_in_c, hg_norm_w, w_out_c, router_w, router_bias,
                  w_gate, w_up, w_down, ln1_w, ln1_b, ln2_w, ln2_b):
    offs = np.cumsum([0, GLA_KEY, GLA_KEY, GLA_VAL, GLA_VAL, GLA_RANK, SSD_INNER, SSD_CONV_DIM,
                      SSD_HEADS])
    sec = lambda w, i: w[:, offs[i]:offs[i + 1]]
    w = w_in_ab[0]
    pad = jnp.zeros((D_MODEL, LANES - SSD_HEADS - GLA_RANK), w.dtype)
    w_ab = jnp.concatenate([sec(w, 5), sec(w, 2), sec(w, 3), sec(w, 6), sec(w, 0), sec(w, 1),
                            sec(w, 7), sec(w, 4), pad], axis=1).astype(BF16)
    w2_wide = jnp.zeros((LANES, GLA_KEY), F32).at[SSD_HEADS:SSD_HEADS + GLA_RANK].set(w_gk2[0])
    lane_pad = lambda v: jnp.zeros((1, LANES), F32).at[0, :SSD_HEADS].set(v)
    expand = np.zeros((LANES, SSD_INNER), np.float32)
    for h in range(SSD_HEADS):
        expand[h, h * SSD_HEADDIM:(h + 1) * SSD_HEADDIM] = 1.0
    return dict(
        w_ab=w_ab,
        w2_wide=w2_wide,
        w2_heads=w2_wide.reshape(LANES, GLA_HEADS, GLA_DK).transpose(1, 0, 2),
        b2_wide=b_gk2[0].reshape(1, GLA_KEY),
        b2_heads=b_gk2[0].reshape(GLA_HEADS, 1, GLA_DK),
        gla_norm_w=gla_norm_w[0].reshape(1, GLA_DV),
        conv_w=conv_w[0], conv_b=conv_b[0].reshape(1, SSD_CONV_DIM),
        dtb_p=lane_pad(dt_bias[0]), alog_p=lane_pad(a_log[0]),
        dskip_x=jnp.repeat(d_skip[0], SSD_HEADDIM).reshape(1, SSD_INNER),
        ssd_norm_w=ssd_norm_w[0].reshape(1, SSD_INNER),
        expand=jnp.asarray(expand, BF16),
        w_out_gla=w_out_ab[0, :GLA_VAL].astype(BF16),
        w_out_ssd=w_out_ab[0, GLA_VAL:].astype(BF16),
        w_c=w_in_c[0].astype(BF16),
        hg_norm_w=hg_norm_w[0].reshape(1, HG_DI),
        w_out_c=w_out_c[0].astype(BF16),
        rwt=router_w.T,
        rbias=router_bias.reshape(N_EXPERTS, 1),
        w_gate=w_gate.astype(BF16), w_up=w_up.astype(BF16), w_down=w_down.astype(BF16),
        ln1_w=ln1_w.reshape(DEPTH, 1, D_MODEL), ln1_b=ln1_b.reshape(DEPTH, 1, D_MODEL),
        ln2_w=ln2_w.reshape(DEPTH, 1, D_MODEL), ln2_b=ln2_b.reshape(DEPTH, 1, D_MODEL),
    )


def _ffn(x, p, layer, tm):
    gates = _router(x, p['rwt'], p['rbias'], tm)
    return _moe_ln(x, gates, p['w_gate'][layer], p['w_up'][layer], p['w_down'][layer],
                   p['ln2_w'][layer], p['ln2_b'][layer], tm)


def _ssd_state_from_wide(s_wide):
    bsz = s_wide.shape[0]
    return s_wide.reshape(bsz, SSD_STATE, SSD_HEADS, SSD_HEADDIM).transpose(0, 2, 1, 3)


def _trunk_prompt(x3, p, lower_bounds, tm, tn_ab, tn_c):
    bsz, seq, _ = x3.shape
    x = x3.reshape(bsz * seq, D_MODEL)
    proj = _proj(x, p['w_ab'], tm, tn_ab)
    o_gla, s_gla = _gla_prompt(proj, p['w2_heads'], p['b2_heads'], p['gla_norm_w'], bsz, seq)
    yz, s_ssd, s_conv = _ssd_prompt(proj, p['conv_w'], p['conv_b'], p['dtb_p'], p['alog_p'],
                                    p['dskip_x'], p['ssd_norm_w'], p['expand'], bsz, seq)
    x = _outproj_ln([o_gla, yz], [p['w_out_gla'], p['w_out_ssd']], x, p['ln1_w'][0], p['ln1_b'][0], tm)
    x = _ffn(x, p, 0, tm)
    proj_c = _proj(x, p['w_c'], tm, tn_c)
    o_hg, s_hg = _hgrn_prompt(proj_c, lower_bounds, p['hg_norm_w'], 1, bsz, seq)
    x = _outproj_ln([o_hg], [p['w_out_c']], x, p['ln1_w'][1], p['ln1_b'][1], tm)
    x = _ffn(x, p, 1, tm)
    return (x.reshape(bsz, seq, D_MODEL), s_gla[None], _ssd_state_from_wide(s_ssd)[None],
            s_conv[None], s_hg[None])


def _trunk_sample(x3, st_gla, st_ssd, st_conv, st_hg, p, lower_bounds, tn_ab, tn_c):
    bsz = x3.shape[0]
    tm = bsz
    x = x3.reshape(bsz, D_MODEL)
    proj = _proj(x, p['w_ab'], tm, tn_ab)
    qs, dec, xc, dt, da, conv_new = _ab_prep(proj, st_conv[0].transpose(1, 0, 2), p['w2_wide'],
                                             p['b2_wide'], p['conv_w'], p['conv_b'], p['dtb_p'],
                                             p['alog_p'])
    conv_new = conv_new.transpose(1, 0, 2)
    k_gla = proj[:, AB_K:AB_K + GLA_KEY]
    s_gla, o_gla = _vec_step(st_gla[0], _to_cols(qs, GLA_HEADS), _to_cols(k_gla, GLA_HEADS),
                             _to_cols(dec, GLA_HEADS), proj, AB_V // GLA_DV, proj,
                             AB_GOUT // GLA_DV, p['gla_norm_w'])
    hpg = SSD_HEADS // SSD_GROUPS
    per_group = lambda a: jnp.pad(a[:, :SSD_HEADS].reshape(bsz, SSD_GROUPS, hpg).transpose(1, 0, 2),
                                  ((0, 0), (0, 0), (0, LANES - hpg)))
    s_ssd, y = _ssd_step(st_ssd[0],
                         _to_cols(xc[:, SSD_INNER:SSD_INNER + SSD_BC], SSD_GROUPS),
                         _to_cols(xc[:, SSD_INNER + SSD_BC:], SSD_GROUPS),
                         xc, per_group(dt), per_group(da), p['dskip_x'])
    yz = _ssd_post(y, proj, p['ssd_norm_w'])
    x = _outproj_ln([o_gla, yz], [p['w_out_gla'], p['w_out_ssd']], x, p['ln1_w'][0], p['ln1_b'][0], tm)
    x = _ffn(x, p, 0, tm)
    proj_c = _proj(x, p['w_c'], tm, tn_c)
    qh, kh, dh = _hgrn_prep(proj_c, lower_bounds, 1)
    s_hg, o_hg = _vec_step(st_hg[0], _to_cols(qh, HG_HEADS), _to_cols(kh, HG_HEADS),
                           _to_cols(dh, HG_HEADS), proj_c, 2 * HG_HEADS, proj_c, 3 * HG_HEADS,
                           p['hg_norm_w'])
    x = _outproj_ln([o_hg], [p['w_out_c']], x, p['ln1_w'][1], p['ln1_b'][1], tm)
    x = _ffn(x, p, 1, tm)
    return x.reshape(bsz, 1, D_MODEL), s_gla[None], s_ssd[None], conv_new[None], s_hg[None]


def kernel(x_prompt, x_sample, state_gla, state_ssd, state_conv, state_hgrn, w_in_ab, w_gk2, b_gk2, gla_norm_w, conv_w, conv_b, dt_bias, a_log, d_skip, ssd_norm_w, w_out_ab, w_in_c, lower_bounds, hg_norm_w, w_out_c, router_w, router_bias, w_gate, w_up, w_down, ln1_w, ln1_b, ln2_w, ln2_b):
    p = _prep_weights(w_in_ab, w_gk2, b_gk2, gla_norm_w, conv_w, conv_b, dt_bias, a_log, d_skip,
                      ssd_norm_w, w_out_ab, w_in_c, hg_norm_w, w_out_c, router_w, router_bias,
                      w_gate, w_up, w_down, ln1_w, ln1_b, ln2_w, ln2_b)
    y_p, gla_p, ssd_p, conv_p, hg_p = _trunk_prompt(x_prompt, p, lower_bounds, 512, 1152, 1024)
    y_s, gla_s, ssd_s, conv_s, hg_s = _trunk_sample(x_sample, state_gla, state_ssd, state_conv,
                                                    state_hgrn, p, lower_bounds, 1152, 1024)
    return (y_p, y_s, gla_p, ssd_p, conv_p, hg_p, gla_s, ssd_s, conv_s, hg_s)
```

```python
import functools

import numpy as np
import jax
import jax.numpy as jnp
from jax import lax
from jax.experimental import pallas as pl
from jax.experimental.pallas import tpu as pltpu

F32 = jnp.float32
BF16 = jnp.bfloat16

D_MODEL = 1024
DEPTH = 2
GLA_HEADS = 4
GLA_DK = 128
GLA_DV = 256
GLA_KEY = GLA_HEADS * GLA_DK
GLA_VAL = GLA_HEADS * GLA_DV
GLA_RANK = 16
GLA_NORMALIZER = 16.0
SSD_INNER = 1024
SSD_HEADDIM = 64
SSD_HEADS = 16
SSD_STATE = 128
SSD_GROUPS = 2
SSD_CONV = 4
SSD_GROUP_W = SSD_INNER // SSD_GROUPS
SSD_BC = SSD_GROUPS * SSD_STATE
SSD_CONV_DIM = SSD_INNER + 2 * SSD_BC
HG_EXPAND = 128
HG_HEADS = 8
HG_F = HG_HEADS * HG_EXPAND
HG_I = D_MODEL
HG_DI = HG_I // HG_HEADS
N_EXPERTS = 16
N_GROUPS = 4
EXPERTS_PER_GROUP = 4
D_FF_EXPERT = 512
ALPHA = (2 * DEPTH) ** 0.25
EPS = 1e-5

LANES = 128
VMEM_LIMIT = 48 * 1024 * 1024

AB_Z = 0
AB_V = 1024
AB_GOUT = 2048
AB_XBC = 3072
AB_Q = 4608
AB_K = 5120
AB_SMALL = 5632
AB_COLS = 5760
C_COLS = 4096

VEC_CHUNK = 64
VEC_SUB = 16
VEC_TILE = 256
VEC_ROWS = 512
SSD_CHUNK = 128
STEP_B = 8


def _params(*sem):
    return pltpu.CompilerParams(dimension_semantics=sem, vmem_limit_bytes=VMEM_LIMIT)


def _dot(a, b):
    return jnp.dot(a.astype(BF16), b.astype(BF16), preferred_element_type=F32)


def _dot_nt(a, b):
    return lax.dot_general(a.astype(BF16), b.astype(BF16), (((1,), (1,)), ((), ())),
                           preferred_element_type=F32)


def _dot_tn(a, b):
    return lax.dot_general(a.astype(BF16), b.astype(BF16), (((0,), (0,)), ((), ())),
                           preferred_element_type=F32)


def _split3(a):
    hi = a.astype(BF16)
    r1 = a - hi.astype(F32)
    mid = r1.astype(BF16)
    lo = (r1 - mid.astype(F32)).astype(BF16)
    return hi, mid, lo


def _dot_exact_rhs(sel, a):
    hi, mid, lo = _split3(a)
    d = lambda p: jnp.dot(sel, p, preferred_element_type=F32)
    return (d(lo) + d(mid)) + d(hi)


def _dot_exact_lhs(a, sel):
    hi, mid, lo = _split3(a)
    d = lambda p: jnp.dot(p, sel, preferred_element_type=F32)
    return (d(lo) + d(mid)) + d(hi)


def _tril(n):
    r = lax.broadcasted_iota(jnp.int32, (n, n), 0)
    c = lax.broadcasted_iota(jnp.int32, (n, n), 1)
    return r >= c


def _sigmoid(x):
    return 1.0 / (1.0 + jnp.exp(-x))


def _silu(x):
    return x * _sigmoid(x)


def _softplus(x):
    return jnp.maximum(x, 0.0) + jnp.log(1.0 + jnp.exp(-jnp.abs(x)))


def _log_sigmoid(x):
    return -_softplus(-x)


def _rms(x, w):
    return x * lax.rsqrt(jnp.mean(x * x, axis=-1, keepdims=True) + EPS) * w


def _layer_norm(x, w, b):
    mu = jnp.mean(x, axis=-1, keepdims=True)
    xc = x - mu
    var = jnp.mean(xc * xc, axis=-1, keepdims=True)
    return xc * lax.rsqrt(var + EPS) * w + b


def _proj_kernel(x_ref, w_ref, o_ref):
    o_ref[...] = jnp.dot(x_ref[...].astype(BF16), w_ref[...], preferred_element_type=F32)


def _proj(x, w, tm, tn):
    t, k = x.shape
    n = w.shape[1]
    return pl.pallas_call(
        _proj_kernel,
        out_shape=jax.ShapeDtypeStruct((t, n), F32),
        grid=(t // tm, n // tn),
        in_specs=[pl.BlockSpec((tm, k), lambda i, j: (i, 0)),
                  pl.BlockSpec((k, tn), lambda i, j: (0, j))],
        out_specs=pl.BlockSpec((tm, tn), lambda i, j: (i, j)),
        compiler_params=_params("parallel", "arbitrary"),
        name="in_proj",
    )(x, w)


def _outproj_ln_kernel(n_in, *refs):
    a_refs = refs[:n_in]
    w_refs = refs[n_in:2 * n_in]
    x_ref, lw_ref, lb_ref, o_ref = refs[2 * n_in:]
    mix = jnp.dot(a_refs[0][...].astype(BF16), w_refs[0][...], preferred_element_type=F32)
    for a_ref, w_ref in zip(a_refs[1:], w_refs[1:]):
        mix = mix + jnp.dot(a_ref[...].astype(BF16), w_ref[...], preferred_element_type=F32)
    o_ref[...] = _layer_norm(ALPHA * x_ref[...] + mix, lw_ref[...], lb_ref[...])


def _outproj_ln(acts, ws, x, ln_w, ln_b, tm):
    t = x.shape[0]
    n_in = len(acts)
    row = lambda i: (i, 0)
    fixed = lambda i: (0, 0)
    in_specs = ([pl.BlockSpec((tm, a.shape[1]), row) for a in acts]
                + [pl.BlockSpec(w.shape, fixed) for w in ws]
                + [pl.BlockSpec((tm, D_MODEL), row),
                   pl.BlockSpec((1, D_MODEL), fixed), pl.BlockSpec((1, D_MODEL), fixed)])
    return pl.pallas_call(
        functools.partial(_outproj_ln_kernel, n_in),
        out_shape=jax.ShapeDtypeStruct((t, D_MODEL), F32),
        grid=(t // tm,),
        in_specs=in_specs,
        out_specs=pl.BlockSpec((tm, D_MODEL), row),
        compiler_params=_params("parallel"),
        name="out_proj_ln",
    )(*acts, *ws, x, ln_w, ln_b)


def _router_kernel(x_ref, rwt_ref, bias_ref, g_ref):
    tm = x_ref.shape[0]
    w3 = _split3(rwt_ref[...])
    x3 = _split3(x_ref[...])
    nt = lambda a, b: lax.dot_general(a, b, (((1,), (1,)), ((), ())), preferred_element_type=F32)
    logits = (((nt(w3[0], x3[2]) + nt(w3[2], x3[0])) + nt(w3[1], x3[1]))
              + (nt(w3[0], x3[1]) + nt(w3[1], x3[0]))) + nt(w3[0], x3[0])
    scores = _sigmoid(logits)
    sel = scores + bias_ref[...]
    s = [sel[e:e + 1, :] for e in range(N_EXPERTS)]
    sc = [scores[e:e + 1, :] for e in range(N_EXPERTS)]
    grp = []
    for g in range(N_GROUPS):
        m = s[g * EXPERTS_PER_GROUP:(g + 1) * EXPERTS_PER_GROUP]
        best = None
        for i in range(EXPERTS_PER_GROUP):
            for j in range(i + 1, EXPERTS_PER_GROUP):
                p = m[i] + m[j]
                best = p if best is None else jnp.maximum(best, p)
        grp.append(best)
    best_g = jnp.zeros((1, tm), jnp.int32)
    best_v = grp[0]
    for g in range(1, N_GROUPS):
        upd = grp[g] > best_v
        best_g = jnp.where(upd, g, best_g)
        best_v = jnp.where(upd, grp[g], best_v)
    neg = jnp.full((1, tm), -jnp.inf, F32)
    ms = [jnp.where(best_g == e // EXPERTS_PER_GROUP, s[e], neg) for e in range(N_EXPERTS)]

    def first_argmax(vals):
        idx = jnp.zeros((1, tm), jnp.int32)
        top = vals[0]
        for e in range(1, N_EXPERTS):
            upd = vals[e] > top
            idx = jnp.where(upd, e, idx)
            top = jnp.where(upd, vals[e], top)
        return idx

    idx1 = first_argmax(ms)
    idx2 = first_argmax([jnp.where(idx1 == e, neg, ms[e]) for e in range(N_EXPERTS)])
    zero = jnp.zeros((1, tm), F32)
    w1 = zero
    w2 = zero
    for e in range(N_EXPERTS):
        w1 = w1 + jnp.where(idx1 == e, sc[e], zero)
        w2 = w2 + jnp.where(idx2 == e, sc[e], zero)
    tot = w1 + w2
    g1 = w1 / tot
    g2 = w2 / tot
    rows = [jnp.where(idx1 == e, g1, zero) + jnp.where(idx2 == e, g2, zero)
            for e in range(N_EXPERTS)]
    gates_t = jnp.concatenate(rows + [jnp.zeros((LANES - N_EXPERTS, tm), F32)], axis=0)
    g_ref[...] = gates_t.T


def _router(x, rwt, bias, tm):
    t = x.shape[0]
    return pl.pallas_call(
        _router_kernel,
        out_shape=jax.ShapeDtypeStruct((t, LANES), F32),
        grid=(t // tm,),
        in_specs=[pl.BlockSpec((tm, D_MODEL), lambda i: (i, 0)),
                  pl.BlockSpec((N_EXPERTS, D_MODEL), lambda i: (0, 0)),
                  pl.BlockSpec((N_EXPERTS, 1), lambda i: (0, 0))],
        out_specs=pl.BlockSpec((tm, LANES), lambda i: (i, 0)),
        compiler_params=_params("parallel"),
        name="router",
    )(x, rwt, bias)


def _moe_kernel(x_ref, g_ref, wg_ref, wu_ref, wd_ref, lw_ref, lb_ref, o_ref, acc_ref, xb_ref):
    e = pl.program_id(1)

    @pl.when(e == 0)
    def _():
        xb_ref[...] = x_ref[...].astype(BF16)
        acc_ref[...] = jnp.zeros_like(acc_ref)

    xb = xb_ref[...]
    hg = jnp.dot(xb, wg_ref[0], preferred_element_type=F32)
    hu = jnp.dot(xb, wu_ref[0], preferred_element_type=F32)
    he = _silu(hg) * hu
    gates = g_ref[...]
    lane = lax.broadcasted_iota(jnp.int32, gates.shape, 1)
    ge = jnp.sum(jnp.where(lane == e, gates, 0.0), axis=1, keepdims=True)
    acc_ref[...] += ge * jnp.dot(he.astype(BF16), wd_ref[0], preferred_element_type=F32)

    @pl.when(e == N_EXPERTS - 1)
    def _():
        o_ref[...] = _layer_norm(ALPHA * x_ref[...] + acc_ref[...], lw_ref[...], lb_ref[...])


def _moe_ln(x, gates, wg, wu, wd, ln_w, ln_b, tm):
    t = x.shape[0]
    return pl.pallas_call(
        _moe_kernel,
        out_shape=jax.ShapeDtypeStruct((t, D_MODEL), F32),
        grid=(t // tm, N_EXPERTS),
        in_specs=[pl.BlockSpec((tm, D_MODEL), lambda i, e: (i, 0)),
                  pl.BlockSpec((tm, LANES), lambda i, e: (i, 0)),
                  pl.BlockSpec((1, D_MODEL, D_FF_EXPERT), lambda i, e: (e, 0, 0)),
                  pl.BlockSpec((1, D_MODEL, D_FF_EXPERT), lambda i, e: (e, 0, 0)),
                  pl.BlockSpec((1, D_FF_EXPERT, D_MODEL), lambda i, e: (e, 0, 0)),
                  pl.BlockSpec((1, D_MODEL), lambda i, e: (0, 0)),
                  pl.BlockSpec((1, D_MODEL), lambda i, e: (0, 0))],
        out_specs=pl.BlockSpec((tm, D_MODEL), lambda i, e: (i, 0)),
        scratch_shapes=[pltpu.VMEM((tm, D_MODEL), F32), pltpu.VMEM((tm, D_MODEL), BF16)],
        compiler_params=_params("parallel", "arbitrary"),
        name="moe_ln",
    )(x, gates, wg, wu, wd, ln_w, ln_b)


def _prefix_selector():
    n = VEC_TILE
    nsub = VEC_CHUNK // VEC_SUB
    t = np.arange(n)[:, None]
    s = np.arange(n)[None, :]
    incl = ((t // VEC_CHUNK) == (s // VEC_CHUNK)) & ((s % VEC_CHUNK) <= (t % VEC_CHUNK))
    r = np.arange((n // VEC_CHUNK) * nsub)[:, None]
    starts = ((r // nsub) == (s // VEC_CHUNK)) & ((s % VEC_CHUNK) < VEC_SUB * (r % nsub))
    return np.concatenate([incl, starts], axis=0).astype(np.float32)


def _vec_block(q, k, v, g, st, sel):
    local = [_vec_local(q[i:i + VEC_TILE], k[i:i + VEC_TILE], v[i:i + VEC_TILE],
                        g[i:i + VEC_TILE], sel) for i in range(0, q.shape[0], VEC_TILE)]
    o_rows = []
    for intra, q_dec0, chunks in local:
        for c, (decay_last, update) in enumerate(chunks):
            rows = slice(c * VEC_CHUNK, (c + 1) * VEC_CHUNK)
            o_rows.append(intra[rows] + _dot_nt(q_dec0[rows], st))
            st = st * decay_last + update
    return jnp.concatenate(o_rows, axis=0), st


def _vec_local(q, k, v, g, sel):
    n = VEC_TILE
    nsub = VEC_CHUNK // VEC_SUB
    kdim = q.shape[1]
    pref = _dot_exact_rhs(sel, g)
    big_g = pref[0:n]
    start = lambda c, j: pref[n + c * nsub + j:n + c * nsub + j + 1]
    rows_of = lambda fn, m: jnp.concatenate(
        [jnp.broadcast_to(fn(i), (m, kdim)) for i in range(n // m)], axis=0)
    q_dec = [q * jnp.exp(big_g)]
    for j in range(1, nsub):
        base_j = rows_of(lambda c: start(c, j), VEC_CHUNK)
        q_dec.append(q * jnp.exp(jnp.minimum(big_g - base_j, 0.0)))
    base_own = rows_of(lambda i: start(i // nsub, i % nsub), VEC_SUB)
    k_rel = k * jnp.exp(base_own - big_g)
    sub = (lax.broadcasted_iota(jnp.int32, k.shape, 0) // VEC_SUB) % nsub
    k_sub = [jnp.where(sub == j, k_rel, 0.0) for j in range(nsub)]
    scores = _dot_nt(jnp.concatenate(q_dec, axis=1), jnp.concatenate(k_sub, axis=1))
    rr = lax.broadcasted_iota(jnp.int32, (n, n), 0)
    cc = lax.broadcasted_iota(jnp.int32, (n, n), 1)
    keep = (rr >= cc) & ((rr // VEC_CHUNK) == (cc // VEC_CHUNK))
    intra = _dot(jnp.where(keep, scores, 0.0), v)
    chunks = []
    for c in range(n // VEC_CHUNK):
        rows = slice(c * VEC_CHUNK, (c + 1) * VEC_CHUNK)
        g_last = big_g[(c + 1) * VEC_CHUNK - 1:(c + 1) * VEC_CHUNK, :]
        kd = k[rows] * jnp.exp(g_last - big_g[rows])
        chunks.append((jnp.exp(g_last), _dot_tn(v[rows], kd)))
    return intra, q_dec[0], chunks


def _gla_chunk_kernel(q_ref, k_ref, v_ref, go_ref, sm_ref, w2_ref, b2_ref, nw_ref, sel_ref,
                      o_ref, s_ref, st_ref):
    r = pl.program_id(2)

    @pl.when(r == 0)
    def _():
        st_ref[...] = jnp.zeros_like(st_ref)

    gk = _log_sigmoid(_dot(sm_ref[...], w2_ref[0]) + b2_ref[0]) / GLA_NORMALIZER
    q = q_ref[...] * (GLA_DK ** -0.5)
    o, st = _vec_block(q, k_ref[...], v_ref[...], gk, st_ref[...], sel_ref[...])
    o_ref[...] = _rms(o, nw_ref[...]) * _silu(go_ref[...])
    st_ref[...] = st

    @pl.when(r == pl.num_programs(2) - 1)
    def _():
        s_ref[0, 0] = st.T


def _gla_prompt(proj, w2p, b2, norm_w, sel, bsz, seq):
    nr = seq // VEC_ROWS
    row = lambda off: (lambda b, h, r: (b * nr + r, off + h))
    return pl.pallas_call(
        _gla_chunk_kernel,
        out_shape=(jax.ShapeDtypeStruct((bsz * seq, GLA_VAL), F32),
                   jax.ShapeDtypeStruct((bsz, GLA_HEADS, GLA_DK, GLA_DV), F32)),
        grid=(bsz, GLA_HEADS, nr),
        in_specs=[pl.BlockSpec((VEC_ROWS, GLA_DK), row(AB_Q // GLA_DK)),
                  pl.BlockSpec((VEC_ROWS, GLA_DK), row(AB_K // GLA_DK)),
                  pl.BlockSpec((VEC_ROWS, GLA_DV), row(AB_V // GLA_DV)),
                  pl.BlockSpec((VEC_ROWS, GLA_DV), row(AB_GOUT // GLA_DV)),
                  pl.BlockSpec((VEC_ROWS, LANES), lambda b, h, r: (b * nr + r, AB_SMALL // LANES)),
                  pl.BlockSpec((1, LANES, GLA_DK), lambda b, h, r: (h, 0, 0)),
                  pl.BlockSpec((1, 1, GLA_DK), lambda b, h, r: (h, 0, 0)),
                  pl.BlockSpec((1, GLA_DV), lambda b, h, r: (0, 0)),
                  pl.BlockSpec(sel.shape, lambda b, h, r: (0, 0))],
        out_specs=(pl.BlockSpec((VEC_ROWS, GLA_DV), lambda b, h, r: (b * nr + r, h)),
                   pl.BlockSpec((1, 1, GLA_DK, GLA_DV), lambda b, h, r: (b, h, 0, 0))),
        scratch_shapes=[pltpu.VMEM((GLA_DV, GLA_DK), F32)],
        compiler_params=_params("parallel", "parallel", "arbitrary"),
        name="gla_chunk",
    )(proj, proj, proj, proj, proj, w2p, b2, norm_w, sel)


def _hgrn_lower_bound(lbraw, layer):
    m = jnp.max(lbraw, axis=0, keepdims=True)
    ex = jnp.exp(lbraw - m)
    sm = ex / jnp.sum(ex, axis=0, keepdims=True)
    acc = sm[0:1]
    for i in range(1, layer + 1):
        acc = acc + sm[i:i + 1]
    return acc - sm[0:1]


def _hgrn_gates(q_raw, f_raw, lb):
    forget = lb + (1.0 - lb) * _sigmoid(f_raw)
    return _silu(q_raw), 1.0 - forget, jnp.log(forget)


def _hgrn_chunk_kernel(layer, q_ref, f_ref, i_ref, go_ref, lb_ref, nw_ref, sel_ref,
                       o_ref, s_ref, st_ref):
    r = pl.program_id(2)

    @pl.when(r == 0)
    def _():
        st_ref[...] = jnp.zeros_like(st_ref)

    lb = _hgrn_lower_bound(lb_ref[...], layer)
    q, k, g = _hgrn_gates(q_ref[...], f_ref[...], lb)
    o, st = _vec_block(q, k, i_ref[...], g, st_ref[...], sel_ref[...])
    o_ref[...] = _rms(o, nw_ref[...]) * _silu(go_ref[...])
    st_ref[...] = st

    @pl.when(r == pl.num_programs(2) - 1)
    def _():
        s_ref[0, 0] = st.T


def _hgrn_prompt(proj, lower_bounds, norm_w, sel, layer, bsz, seq):
    nr = seq // VEC_ROWS
    nh = HG_HEADS
    row = lambda off: (lambda b, h, r: (b * nr + r, off + h))
    return pl.pallas_call(
        functools.partial(_hgrn_chunk_kernel, layer),
        out_shape=(jax.ShapeDtypeStruct((bsz * seq, HG_I), F32),
                   jax.ShapeDtypeStruct((bsz, nh, HG_EXPAND, HG_DI), F32)),
        grid=(bsz, nh, nr),
        in_specs=[pl.BlockSpec((VEC_ROWS, HG_EXPAND), row(0)),
                  pl.BlockSpec((VEC_ROWS, HG_EXPAND), row(nh)),
                  pl.BlockSpec((VEC_ROWS, HG_DI), row(2 * nh)),
                  pl.BlockSpec((VEC_ROWS, HG_DI), row(3 * nh)),
                  pl.BlockSpec((DEPTH, HG_EXPAND), lambda b, h, r: (0, h)),
                  pl.BlockSpec((1, HG_DI), lambda b, h, r: (0, 0)),
                  pl.BlockSpec(sel.shape, lambda b, h, r: (0, 0))],
        out_specs=(pl.BlockSpec((VEC_ROWS, HG_DI), lambda b, h, r: (b * nr + r, h)),
                   pl.BlockSpec((1, 1, HG_EXPAND, HG_DI), lambda b, h, r: (b, h, 0, 0))),
        scratch_shapes=[pltpu.VMEM((HG_DI, HG_EXPAND), F32)],
        compiler_params=_params("parallel", "parallel", "arbitrary"),
        name="hgrn_chunk",
    )(proj, proj, proj, proj, lower_bounds, norm_w, sel)


def _conv_silu(xp, cw, cb, n, lead):
    acc = cb + cw[SSD_CONV - 1:SSD_CONV] * xp[lead:lead + n]
    for m in range(1, SSD_CONV):
        acc = acc + cw[SSD_CONV - 1 - m:SSD_CONV - m] * xp[lead - m:lead - m + n]
    return _silu(acc)


def _ssd_gate_norm(y, z, nw):
    yz = y * _silu(z)
    parts = []
    for g in range(SSD_GROUPS):
        cols = slice(g * SSD_GROUP_W, (g + 1) * SSD_GROUP_W)
        parts.append(_rms(yz[:, cols], nw[:, cols]))
    return jnp.concatenate(parts, axis=1)


def _ssd_chunk_kernel(z_ref, xbc_ref, sm_ref, cw_ref, cb_ref, dtb_ref, alog_ref, dsk_ref,
                      nw_ref, ex_ref, o_ref, s_ref, conv_ref, st_ref, prev_ref):
    r = pl.program_id(1)
    c = SSD_CHUNK

    @pl.when(r == 0)
    def _():
        st_ref[...] = jnp.zeros_like(st_ref)
        prev_ref[...] = jnp.zeros_like(prev_ref)

    x_raw = xbc_ref[...]
    xp = jnp.concatenate([prev_ref[...], x_raw], axis=0)
    prev_ref[...] = x_raw[c - 8:c]
    xc = _conv_silu(xp, cw_ref[...], cb_ref[...], c, 8)
    xs = xc[:, :SSD_INNER]
    bm = xc[:, SSD_INNER:SSD_INNER + SSD_BC]
    cm = xc[:, SSD_INNER + SSD_BC:]

    dt = _softplus(sm_ref[...] + dtb_ref[...])
    a_neg = -jnp.exp(alog_ref[...])
    big_g = _dot_exact_rhs(_tril(c).astype(BF16), dt * a_neg)
    g_t = big_g.T
    g_last = big_g[c - 1:c, :]
    ex = ex_ref[...]
    dt_x = _dot_exact_lhs(dt, ex)
    eg_x = _dot_exact_lhs(jnp.exp(big_g), ex)
    w_x = _dot_exact_lhs(dt * jnp.exp(g_last - big_g), ex)
    xdt = xs * dt_x
    xw = xs * w_x
    causal = _tril(c)
    lane = lax.broadcasted_iota(jnp.int32, (c, LANES), 1)
    st = st_ref[...]
    y_parts = []
    u_parts = []
    for g in range(SSD_GROUPS):
        gcols = slice(g * SSD_GROUP_W, (g + 1) * SSD_GROUP_W)
        bg = bm[:, g * SSD_STATE:(g + 1) * SSD_STATE]
        cg = cm[:, g * SSD_STATE:(g + 1) * SSD_STATE]
        sc = _dot_nt(cg, bg)
        inter = _dot(cg, st[:, gcols])
        u_parts.append(_dot_tn(bg, xw[:, gcols]))
        pair_cols = []
        heads_per_group = SSD_HEADS // SSD_GROUPS
        for p in range(heads_per_group // 2):
            h0 = g * heads_per_group + 2 * p
            xpair = xdt[:, h0 * SSD_HEADDIM:(h0 + 2) * SSD_HEADDIM]
            ys = []
            for h in (h0, h0 + 1):
                diff = big_g[:, h:h + 1] - g_t[h:h + 1, :]
                dec = jnp.exp(jnp.where(causal, diff, -jnp.inf))
                ys.append(_dot(sc * dec, xpair))
            pair_cols.append(jnp.where(lane < SSD_HEADDIM, ys[0], ys[1]))
        y_intra = jnp.concatenate(pair_cols, axis=1)
        y_parts.append(y_intra + inter * eg_x[:, gcols])
    y = jnp.concatenate(y_parts, axis=1) + dsk_ref[...] * xs
    o_ref[...] = _ssd_gate_norm(y, z_ref[...], nw_ref[...])
    st = st * eg_x[c - 1:c, :] + jnp.concatenate(u_parts, axis=1)
    st_ref[...] = st

    @pl.when(r == pl.num_programs(1) - 1)
    def _():
        s_ref[0] = st
        conv_ref[0] = x_raw[c - (SSD_CONV - 1):c]


def _ssd_prompt(proj, conv_w, conv_b, dtb_p, alog_p, dskip_x, norm_w, expand, bsz, seq):
    nr = seq // SSD_CHUNK
    fixed = lambda b, r: (0, 0)
    return pl.pallas_call(
        _ssd_chunk_kernel,
        out_shape=(jax.ShapeDtypeStruct((bsz * seq, SSD_INNER), F32),
                   jax.ShapeDtypeStruct((bsz, SSD_STATE, SSD_INNER), F32),
                   jax.ShapeDtypeStruct((bsz, SSD_CONV - 1, SSD_CONV_DIM), F32)),
        grid=(bsz, nr),
        in_specs=[pl.BlockSpec((SSD_CHUNK, SSD_INNER), lambda b, r: (b * nr + r, AB_Z // SSD_INNER)),
                  pl.BlockSpec((SSD_CHUNK, SSD_CONV_DIM), lambda b, r: (b * nr + r, AB_XBC // SSD_CONV_DIM)),
                  pl.BlockSpec((SSD_CHUNK, LANES), lambda b, r: (b * nr + r, AB_SMALL // LANES)),
                  pl.BlockSpec((SSD_CONV, SSD_CONV_DIM), fixed),
                  pl.BlockSpec((1, SSD_CONV_DIM), fixed),
                  pl.BlockSpec((1, LANES), fixed),
                  pl.BlockSpec((1, LANES), fixed),
                  pl.BlockSpec((1, SSD_INNER), fixed),
                  pl.BlockSpec((1, SSD_INNER), fixed),
                  pl.BlockSpec((LANES, SSD_INNER), fixed)],
        out_specs=(pl.BlockSpec((SSD_CHUNK, SSD_INNER), lambda b, r: (b * nr + r, 0)),
                   pl.BlockSpec((1, SSD_STATE, SSD_INNER), lambda b, r: (b, 0, 0)),
                   pl.BlockSpec((1, SSD_CONV - 1, SSD_CONV_DIM), lambda b, r: (b, 0, 0))),
        scratch_shapes=[pltpu.VMEM((SSD_STATE, SSD_INNER), F32),
                        pltpu.VMEM((8, SSD_CONV_DIM), F32)],
        compiler_params=_params("parallel", "arbitrary"),
        name="ssd_chunk",
    )(proj, proj, proj, conv_w, conv_b, dtb_p, alog_p, dskip_x, norm_w, expand)


def _ab_prep_kernel(q_ref, sm_ref, xbc_ref, cs_ref, w2_ref, b2_ref, cw_ref, cb_ref, dtb_ref,
                    alog_ref, qs_ref, dec_ref, xc_ref, dt_ref, da_ref, cs_out_ref):
    sm = sm_ref[...]
    gk = _log_sigmoid(_dot(sm, w2_ref[...]) + b2_ref[...]) / GLA_NORMALIZER
    qs_ref[...] = q_ref[...] * (GLA_DK ** -0.5)
    dec_ref[...] = jnp.exp(gk)
    cw = cw_ref[...]
    x_raw = xbc_ref[...]
    acc = cb_ref[...] + cw[SSD_CONV - 1:SSD_CONV] * x_raw
    for j in range(SSD_CONV - 1):
        acc = acc + cw[j:j + 1] * cs_ref[j]
    xc_ref[...] = _silu(acc)
    for j in range(SSD_CONV - 2):
        cs_out_ref[j] = cs_ref[j + 1]
    cs_out_ref[SSD_CONV - 2] = x_raw
    dt = _softplus(sm + dtb_ref[...])
    dt_ref[...] = dt
    da_ref[...] = jnp.exp(dt * -jnp.exp(alog_ref[...]))


def _ab_prep(proj, conv_state, w2_wide, b2_wide, conv_w, conv_b, dtb_p, alog_p):
    bsz = proj.shape[0]
    fixed = lambda i: (0, 0)
    sds = jax.ShapeDtypeStruct
    return pl.pallas_call(
        _ab_prep_kernel,
        out_shape=(sds((bsz, GLA_KEY), F32), sds((bsz, GLA_KEY), F32),
                   sds((bsz, SSD_CONV_DIM), F32), sds((bsz, LANES), F32), sds((bsz, LANES), F32),
                   sds((SSD_CONV - 1, bsz, SSD_CONV_DIM), F32)),
        grid=(1,),
        in_specs=[pl.BlockSpec((bsz, GLA_KEY), lambda i: (0, AB_Q // GLA_KEY)),
                  pl.BlockSpec((bsz, LANES), lambda i: (0, AB_SMALL // LANES)),
                  pl.BlockSpec((bsz, SSD_CONV_DIM), lambda i: (0, AB_XBC // SSD_CONV_DIM)),
                  pl.BlockSpec((SSD_CONV - 1, bsz, SSD_CONV_DIM), lambda i: (0, 0, 0)),
                  pl.BlockSpec((LANES, GLA_KEY), fixed),
                  pl.BlockSpec((1, GLA_KEY), fixed),
                  pl.BlockSpec((SSD_CONV, SSD_CONV_DIM), fixed),
                  pl.BlockSpec((1, SSD_CONV_DIM), fixed),
                  pl.BlockSpec((1, LANES), fixed),
                  pl.BlockSpec((1, LANES), fixed)],
        out_specs=(pl.BlockSpec((bsz, GLA_KEY), fixed), pl.BlockSpec((bsz, GLA_KEY), fixed),
                   pl.BlockSpec((bsz, SSD_CONV_DIM), fixed), pl.BlockSpec((bsz, LANES), fixed),
                   pl.BlockSpec((bsz, LANES), fixed),
                   pl.BlockSpec((SSD_CONV - 1, bsz, SSD_CONV_DIM), lambda i: (0, 0, 0))),
        compiler_params=_params("arbitrary"),
        name="ab_prep",
    )(proj, proj, proj, conv_state, w2_wide, b2_wide, conv_w, conv_b, dtb_p, alog_p)


def _hgrn_prep_kernel(layer, q_ref, f_ref, lb_ref, qs_ref, k_ref, dec_ref):
    lb = _hgrn_lower_bound(lb_ref[...], layer)
    forget = lb + (1.0 - lb) * _sigmoid(f_ref[...])
    qs_ref[...] = _silu(q_ref[...])
    k_ref[...] = 1.0 - forget
    dec_ref[...] = jnp.exp(jnp.log(forget))


def _hgrn_prep(proj, lower_bounds, layer):
    bsz = proj.shape[0]
    blk = lambda j: pl.BlockSpec((bsz, HG_F), lambda i: (0, j))
    return pl.pallas_call(
        functools.partial(_hgrn_prep_kernel, layer),
        out_shape=tuple(jax.ShapeDtypeStruct((bsz, HG_F), F32) for _ in range(3)),
        grid=(1,),
        in_specs=[blk(0), blk(1), pl.BlockSpec((DEPTH, HG_F), lambda i: (0, 0))],
        out_specs=tuple(blk(0) for _ in range(3)),
        compiler_params=_params("arbitrary"),
        name="hgrn_prep",
    )(proj, proj, lower_bounds)


def _vec_step_kernel(s_ref, q_ref, k_ref, d_ref, v_ref, go_ref, nw_ref, so_ref, o_ref):
    qt = q_ref[0, 0]
    kt = k_ref[0, 0]
    dt = d_ref[0, 0]
    v = v_ref[...]
    rows = []
    for b in range(STEP_B):
        sn = s_ref[b, 0] * dt[:, b:b + 1] + kt[:, b:b + 1] * v[b:b + 1, :]
        so_ref[b, 0] = sn
        rows.append(jnp.sum(qt[:, b:b + 1] * sn, axis=0, keepdims=True))
    o = jnp.concatenate(rows, axis=0)
    o_ref[...] = _rms(o, nw_ref[...]) * _silu(go_ref[...])


def _vec_step(state, q_cols, k_cols, d_cols, vsrc, v_off, gsrc, g_off, norm_w):
    bsz, nh, kdim, vdim = state.shape
    col = lambda j, h: (h, j, 0, 0)
    return pl.pallas_call(
        _vec_step_kernel,
        out_shape=(jax.ShapeDtypeStruct(state.shape, F32),
                   jax.ShapeDtypeStruct((bsz, nh * vdim), F32)),
        grid=(bsz // STEP_B, nh),
        in_specs=[pl.BlockSpec((STEP_B, 1, kdim, vdim), lambda j, h: (j, h, 0, 0)),
                  pl.BlockSpec((1, 1, kdim, STEP_B), col),
                  pl.BlockSpec((1, 1, kdim, STEP_B), col),
                  pl.BlockSpec((1, 1, kdim, STEP_B), col),
                  pl.BlockSpec((STEP_B, vdim), lambda j, h: (j, v_off + h)),
                  pl.BlockSpec((STEP_B, vdim), lambda j, h: (j, g_off + h)),
                  pl.BlockSpec((1, vdim), lambda j, h: (0, 0))],
        out_specs=(pl.BlockSpec((STEP_B, 1, kdim, vdim), lambda j, h: (j, h, 0, 0)),
                   pl.BlockSpec((STEP_B, vdim), lambda j, h: (j, h))),
        compiler_params=_params("parallel", "parallel"),
        name="vec_step",
    )(state, q_cols, k_cols, d_cols, vsrc, gsrc, norm_w)


def _ssd_step_kernel(s_ref, b_ref, c_ref, x_ref, dt_ref, da_ref, dsk_ref, so_ref, y_ref):
    bt = b_ref[0, 0]
    ct = c_ref[0, 0]
    x = x_ref[...]
    dt = dt_ref[0]
    da = da_ref[0]
    hpg = SSD_HEADS // SSD_GROUPS
    rows = []
    for b in range(STEP_B):
        pieces = []
        for hh in range(hpg):
            xh = x[b:b + 1, hh * SSD_HEADDIM:(hh + 1) * SSD_HEADDIM]
            sn = s_ref[b, hh] * da[b:b + 1, hh:hh + 1] + (bt[:, b:b + 1] * dt[b:b + 1, hh:hh + 1]) * xh
            so_ref[b, hh] = sn
            pieces.append(jnp.sum(ct[:, b:b + 1] * sn, axis=0, keepdims=True))
        rows.append(jnp.concatenate(pieces, axis=1))
    y_ref[...] = jnp.concatenate(rows, axis=0) + dsk_ref[...] * x


def _ssd_step(state, b_cols, c_cols, xc, dt_g, da_g, dskip_x):
    bsz = state.shape[0]
    hpg = SSD_HEADS // SSD_GROUPS
    col = lambda j, g: (g, j, 0, 0)
    return pl.pallas_call(
        _ssd_step_kernel,
        out_shape=(jax.ShapeDtypeStruct(state.shape, F32),
                   jax.ShapeDtypeStruct((bsz, SSD_INNER), F32)),
        grid=(bsz // STEP_B, SSD_GROUPS),
        in_specs=[pl.BlockSpec((STEP_B, hpg, SSD_STATE, SSD_HEADDIM), lambda j, g: (j, g, 0, 0)),
                  pl.BlockSpec((1, 1, SSD_STATE, STEP_B), col),
                  pl.BlockSpec((1, 1, SSD_STATE, STEP_B), col),
                  pl.BlockSpec((STEP_B, SSD_GROUP_W), lambda j, g: (j, g)),
                  pl.BlockSpec((1, STEP_B, LANES), lambda j, g: (g, j, 0)),
                  pl.BlockSpec((1, STEP_B, LANES), lambda j, g: (g, j, 0)),
                  pl.BlockSpec((1, SSD_GROUP_W), lambda j, g: (0, g))],
        out_specs=(pl.BlockSpec((STEP_B, hpg, SSD_STATE, SSD_HEADDIM), lambda j, g: (j, g, 0, 0)),
                   pl.BlockSpec((STEP_B, SSD_GROUP_W), lambda j, g: (j, g))),
        compiler_params=_params("parallel", "parallel"),
        name="ssd_step",
    )(state, b_cols, c_cols, xc, dt_g, da_g, dskip_x)


def _ssd_post_kernel(y_ref, z_ref, nw_ref, o_ref):
    o_ref[...] = _ssd_gate_norm(y_ref[...], z_ref[...], nw_ref[...])


def _ssd_post(y, proj, norm_w):
    bsz = y.shape[0]
    return pl.pallas_call(
        _ssd_post_kernel,
        out_shape=jax.ShapeDtypeStruct((bsz, SSD_INNER), F32),
        grid=(1,),
        in_specs=[pl.BlockSpec((bsz, SSD_INNER), lambda i: (0, 0)),
                  pl.BlockSpec((bsz, SSD_INNER), lambda i: (0, AB_Z // SSD_INNER)),
                  pl.BlockSpec((1, SSD_INNER), lambda i: (0, 0))],
        out_specs=pl.BlockSpec((bsz, SSD_INNER), lambda i: (0, 0)),
        compiler_params=_params("arbitrary"),
        name="ssd_post",
    )(y, proj, norm_w)


def _to_cols(a, nh):
    bsz = a.shape[0]
    return a.reshape(bsz // STEP_B, STEP_B, nh, -1).transpose(2, 0, 3, 1)


def _prep_weights(w_in_ab, w_gk2, b_gk2, gla_norm_w, conv_w, conv_b, dt_bias, a_log, d_skip,
                  ssd_norm_w, w_out_ab, w_in_c, hg_norm_w, w_out_c, router_w, router_bias,
                  w_gate, w_up, w_down, ln1_w, ln1_b, ln2_w, ln2_b):
    offs = np.cumsum([0, GLA_KEY, GLA_KEY, GLA_VAL, GLA_VAL, GLA_RANK, SSD_INNER, SSD_CONV_DIM,
                      SSD_HEADS])
    sec = lambda w, i: w[:, offs[i]:offs[i + 1]]
    w = w_in_ab[0]
    pad = jnp.zeros((D_MODEL, LANES - SSD_HEADS - GLA_RANK), w.dtype)
    w_ab = jnp.concatenate([sec(w, 5), sec(w, 2), sec(w, 3), sec(w, 6), sec(w, 0), sec(w, 1),
                            sec(w, 7), sec(w, 4), pad], axis=1).astype(BF16)
    w2_wide = jnp.zeros((LANES, GLA_KEY), F32).at[SSD_HEADS:SSD_HEADS + GLA_RANK].set(w_gk2[0])
    lane_pad = lambda v: jnp.zeros((1, LANES), F32).at[0, :SSD_HEADS].set(v)
    expand = np.zeros((LANES, SSD_INNER), np.float32)
    for h in range(SSD_HEADS):
        expand[h, h * SSD_HEADDIM:(h + 1) * SSD_HEADDIM] = 1.0
    return dict(
        w_ab=w_ab,
        w2_wide=w2_wide,
        w2_heads=w2_wide.reshape(LANES, GLA_HEADS, GLA_DK).transpose(1, 0, 2),
        b2_wide=b_gk2[0].reshape(1, GLA_KEY),
        b2_heads=b_gk2[0].reshape(GLA_HEADS, 1, GLA_DK),
        gla_norm_w=gla_norm_w[0].reshape(1, GLA_DV),
        conv_w=conv_w[0], conv_b=conv_b[0].reshape(1, SSD_CONV_DIM),
        dtb_p=lane_pad(dt_bias[0]), alog_p=lane_pad(a_log[0]),
        dskip_x=jnp.repeat(d_skip[0], SSD_HEADDIM).reshape(1, SSD_INNER),
        ssd_norm_w=ssd_norm_w[0].reshape(1, SSD_INNER),
        expand=jnp.asarray(expand, BF16),
        prefix_sel=jnp.asarray(_prefix_selector(), BF16),
        w_out_gla=w_out_ab[0, :GLA_VAL].astype(BF16),
        w_out_ssd=w_out_ab[0, GLA_VAL:].astype(BF16),
        w_c=w_in_c[0].astype(BF16),
        hg_norm_w=hg_norm_w[0].reshape(1, HG_DI),
        w_out_c=w_out_c[0].astype(BF16),
        rwt=router_w.T,
        rbias=router_bias.reshape(N_EXPERTS, 1),
        w_gate=w_gate.astype(BF16), w_up=w_up.astype(BF16), w_down=w_down.astype(BF16),
        ln1_w=ln1_w.reshape(DEPTH, 1, D_MODEL), ln1_b=ln1_b.reshape(DEPTH, 1, D_MODEL),
        ln2_w=ln2_w.reshape(DEPTH, 1, D_MODEL), ln2_b=ln2_b.reshape(DEPTH, 1, D_MODEL),
    )


def _ffn(x, p, layer, tm, tm_moe):
    gates = _router(x, p['rwt'], p['rbias'], tm)
    return _moe_ln(x, gates, p['w_gate'][layer], p['w_up'][layer], p['w_down'][layer],
                   p['ln2_w'][layer], p['ln2_b'][layer], tm_moe)


def _ssd_state_from_wide(s_wide):
    bsz = s_wide.shape[0]
    return s_wide.reshape(bsz, SSD_STATE, SSD_HEADS, SSD_HEADDIM).transpose(0, 2, 1, 3)


def _trunk_prompt(x3, p, lower_bounds, tm, tn_ab, tn_c):
    bsz, seq, _ = x3.shape
    x = x3.reshape(bsz * seq, D_MODEL)
    tm_big = 2 * tm
    proj = _proj(x, p['w_ab'], tm_big, tn_ab)
    o_gla, s_gla = _gla_prompt(proj, p['w2_heads'], p['b2_heads'], p['gla_norm_w'],
                               p['prefix_sel'], bsz, seq)
    yz, s_ssd, s_conv = _ssd_prompt(proj, p['conv_w'], p['conv_b'], p['dtb_p'], p['alog_p'],
                                    p['dskip_x'], p['ssd_norm_w'], p['expand'], bsz, seq)
    x = _outproj_ln([o_gla, yz], [p['w_out_gla'], p['w_out_ssd']], x, p['ln1_w'][0], p['ln1_b'][0], tm)
    x = _ffn(x, p, 0, tm, tm_big)
    proj_c = _proj(x, p['w_c'], tm_big, tn_c)
    o_hg, s_hg = _hgrn_prompt(proj_c, lower_bounds, p['hg_norm_w'], p['prefix_sel'], 1, bsz, seq)
    x = _outproj_ln([o_hg], [p['w_out_c']], x, p['ln1_w'][1], p['ln1_b'][1], tm)
    x = _ffn(x, p, 1, tm, tm_big)
    return (x.reshape(bsz, seq, D_MODEL), s_gla[None], _ssd_state_from_wide(s_ssd)[None],
            s_conv[None], s_hg[None])


def _trunk_sample(x3, st_gla, st_ssd, st_conv, st_hg, p, lower_bounds, tn_ab, tn_c):
    bsz = x3.shape[0]
    tm = bsz
    x = x3.reshape(bsz, D_MODEL)
    proj = _proj(x, p['w_ab'], tm, tn_ab)
    qs, dec, xc, dt, da, conv_new = _ab_prep(proj, st_conv[0].transpose(1, 0, 2), p['w2_wide'],
                                             p['b2_wide'], p['conv_w'], p['conv_b'], p['dtb_p'],
                                             p['alog_p'])
    conv_new = conv_new.transpose(1, 0, 2)
    k_gla = proj[:, AB_K:AB_K + GLA_KEY]
    s_gla, o_gla = _vec_step(st_gla[0], _to_cols(qs, GLA_HEADS), _to_cols(k_gla, GLA_HEADS),
                             _to_cols(dec, GLA_HEADS), proj, AB_V // GLA_DV, proj,
                             AB_GOUT // GLA_DV, p['gla_norm_w'])
    hpg = SSD_HEADS // SSD_GROUPS
    per_group = lambda a: jnp.pad(a[:, :SSD_HEADS].reshape(bsz, SSD_GROUPS, hpg).transpose(1, 0, 2),
                                  ((0, 0), (0, 0), (0, LANES - hpg)))
    s_ssd, y = _ssd_step(st_ssd[0],
                         _to_cols(xc[:, SSD_INNER:SSD_INNER + SSD_BC], SSD_GROUPS),
                         _to_cols(xc[:, SSD_INNER + SSD_BC:], SSD_GROUPS),
                         xc, per_group(dt), per_group(da), p['dskip_x'])
    yz = _ssd_post(y, proj, p['ssd_norm_w'])
    x = _outproj_ln([o_gla, yz], [p['w_out_gla'], p['w_out_ssd']], x, p['ln1_w'][0], p['ln1_b'][0], tm)
    x = _ffn(x, p, 0, tm, tm)
    proj_c = _proj(x, p['w_c'], tm, tn_c)
    qh, kh, dh = _hgrn_prep(proj_c, lower_bounds, 1)
    s_hg, o_hg = _vec_step(st_hg[0], _to_cols(qh, HG_HEADS), _to_cols(kh, HG_HEADS),
                           _to_cols(dh, HG_HEADS), proj_c, 2 * HG_HEADS, proj_c, 3 * HG_HEADS,
                           p['hg_norm_w'])
    x = _outproj_ln([o_hg], [p['w_out_c']], x, p['ln1_w'][1], p['ln1_b'][1], tm)
    x = _ffn(x, p, 1, tm, tm)
    return x.reshape(bsz, 1, D_MODEL), s_gla[None], s_ssd[None], conv_new[None], s_hg[None]


def kernel(x_prompt, x_sample, state_gla, state_ssd, state_conv, state_hgrn, w_in_ab, w_gk2, b_gk2, gla_norm_w, conv_w, conv_b, dt_bias, a_log, d_skip, ssd_norm_w, w_out_ab, w_in_c, lower_bounds, hg_norm_w, w_out_c, router_w, router_bias, w_gate, w_up, w_down, ln1_w, ln1_b, ln2_w, ln2_b):
    p = _prep_weights(w_in_ab, w_gk2, b_gk2, gla_norm_w, conv_w, conv_b, dt_bias, a_log, d_skip,
                      ssd_norm_w, w_out_ab, w_in_c, hg_norm_w, w_out_c, router_w, router_bias,
                      w_gate, w_up, w_down, ln1_w, ln1_b, ln2_w, ln2_b)
    y_p, gla_p, ssd_p, conv_p, hg_p = _trunk_prompt(x_prompt, p, lower_bounds, 512, 1152, 1024)
    y_s, gla_s, ssd_s, conv_s, hg_s = _trunk_sample(x_sample, state_gla, state_ssd, state_conv,
                                                    state_hgrn, p, lower_bounds, 1152, 1024)
    return (y_p, y_s, gla_p, ssd_p, conv_p, hg_p, gla_s, ssd_s, conv_s, hg_s)
```

```python
import functools

import numpy as np
import jax
import jax.numpy as jnp
from jax import lax
from jax.experimental import pallas as pl
from jax.experimental.pallas import tpu as pltpu

F32 = jnp.float32
BF16 = jnp.bfloat16

D_MODEL = 1024
DEPTH = 2
GLA_HEADS = 4
GLA_DK = 128
GLA_DV = 256
GLA_KEY = GLA_HEADS * GLA_DK
GLA_VAL = GLA_HEADS * GLA_DV
GLA_RANK = 16
GLA_NORMALIZER = 16.0
SSD_INNER = 1024
SSD_HEADDIM = 64
SSD_HEADS = 16
SSD_STATE = 128
SSD_GROUPS = 2
SSD_CONV = 4
SSD_GROUP_W = SSD_INNER // SSD_GROUPS
SSD_BC = SSD_GROUPS * SSD_STATE
SSD_CONV_DIM = SSD_INNER + 2 * SSD_BC
HG_EXPAND = 128
HG_HEADS = 8
HG_F = HG_HEADS * HG_EXPAND
HG_I = D_MODEL
HG_DI = HG_I // HG_HEADS
N_EXPERTS = 16
N_GROUPS = 4
EXPERTS_PER_GROUP = 4
D_FF_EXPERT = 512
ALPHA = (2 * DEPTH) ** 0.25
EPS = 1e-5

LANES = 128
VMEM_LIMIT = 48 * 1024 * 1024

AB_Z = 0
AB_V = 1024
AB_GOUT = 2048
AB_XBC = 3072
AB_Q = 4608
AB_K = 5120
AB_SMALL = 5632
AB_COLS = 5760
C_COLS = 4096

VEC_CHUNK = 64
VEC_SUB = 16
VEC_TILE = 256
VEC_ROWS = 512
VEC_HPS = 2
SSD_CHUNK = 128
STEP_B = 8


def _params(*sem):
    return pltpu.CompilerParams(dimension_semantics=sem, vmem_limit_bytes=VMEM_LIMIT)


_NN = (((1,), (0,)), ((), ()))
_NT = (((1,), (1,)), ((), ()))
_TN = (((0,), (0,)), ((), ()))


def _dot1(dims, a, b):
    return lax.dot_general(a.astype(BF16), b.astype(BF16), dims, preferred_element_type=F32)


def _split2(a):
    hi = a.astype(BF16)
    return hi, (a - hi.astype(F32)).astype(BF16)


def _dot3(dims, a, b):
    ah, al = _split2(a)
    bh, bl = _split2(b)
    d = lambda x, y: lax.dot_general(x, y, dims, preferred_element_type=F32)
    return (d(al, bh) + d(ah, bl)) + d(ah, bh)


class _OnePass:
    nn = staticmethod(lambda a, b: _dot1(_NN, a, b))
    nt = staticmethod(lambda a, b: _dot1(_NT, a, b))
    tn = staticmethod(lambda a, b: _dot1(_TN, a, b))


class _ThreePass:
    nn = staticmethod(lambda a, b: _dot3(_NN, a, b))
    nt = staticmethod(lambda a, b: _dot3(_NT, a, b))
    tn = staticmethod(lambda a, b: _dot3(_TN, a, b))


def _dot(a, b):
    return _dot1(_NN, a, b)


def _dot_nt(a, b):
    return _dot1(_NT, a, b)


def _dot_tn(a, b):
    return _dot1(_TN, a, b)


def _split3(a):
    hi = a.astype(BF16)
    r1 = a - hi.astype(F32)
    mid = r1.astype(BF16)
    lo = (r1 - mid.astype(F32)).astype(BF16)
    return hi, mid, lo


def _dot_exact_rhs(sel, a):
    hi, mid, lo = _split3(a)
    d = lambda p: jnp.dot(sel, p, preferred_element_type=F32)
    return (d(lo) + d(mid)) + d(hi)


def _dot_exact_lhs(a, sel):
    hi, mid, lo = _split3(a)
    d = lambda p: jnp.dot(p, sel, preferred_element_type=F32)
    return (d(lo) + d(mid)) + d(hi)


def _tril(n):
    r = lax.broadcasted_iota(jnp.int32, (n, n), 0)
    c = lax.broadcasted_iota(jnp.int32, (n, n), 1)
    return r >= c


def _sigmoid(x):
    return 1.0 / (1.0 + jnp.exp(-x))


def _silu(x):
    return x * _sigmoid(x)


def _softplus(x):
    return jnp.maximum(x, 0.0) + jnp.log(1.0 + jnp.exp(-jnp.abs(x)))


def _log_sigmoid(x):
    return -_softplus(-x)


def _rms(x, w):
    return x * lax.rsqrt(jnp.mean(x * x, axis=-1, keepdims=True) + EPS) * w


def _layer_norm(x, w, b):
    mu = jnp.mean(x, axis=-1, keepdims=True)
    xc = x - mu
    var = jnp.mean(xc * xc, axis=-1, keepdims=True)
    return xc * lax.rsqrt(var + EPS) * w + b


def _proj_kernel(x_ref, w_ref, o_ref):
    o_ref[...] = jnp.dot(x_ref[...].astype(BF16), w_ref[...], preferred_element_type=F32)


def _proj3_kernel(x_ref, wh_ref, wl_ref, o_ref, xh_ref, xl_ref):
    @pl.when(pl.program_id(1) == 0)
    def _():
        hi, lo = _split2(x_ref[...])
        xh_ref[...] = hi
        xl_ref[...] = lo

    d = lambda a, b: jnp.dot(a, b, preferred_element_type=F32)
    xh = xh_ref[...]
    wh = wh_ref[...]
    o_ref[...] = (d(xl_ref[...], wh) + d(xh, wl_ref[...])) + d(xh, wh)


def _proj(x, w, tm, tn):
    t, k = x.shape
    three = isinstance(w, tuple)
    ws = w if three else (w,)
    n = ws[0].shape[1]
    return pl.pallas_call(
        _proj3_kernel if three else _proj_kernel,
        out_shape=jax.ShapeDtypeStruct((t, n), F32),
        grid=(t // tm, n // tn),
        in_specs=[pl.BlockSpec((tm, k), lambda i, j: (i, 0))]
                 + [pl.BlockSpec((k, tn), lambda i, j: (0, j)) for _ in ws],
        out_specs=pl.BlockSpec((tm, tn), lambda i, j: (i, j)),
        scratch_shapes=[pltpu.VMEM((tm, k), BF16), pltpu.VMEM((tm, k), BF16)] if three else [],
        compiler_params=_params("parallel", "arbitrary"),
        name="in_proj",
    )(x, *ws)


def _outproj_ln_kernel(n_in, three, *refs):
    a_refs = refs[:n_in]
    nw = 2 if three else 1
    w_refs = refs[n_in:n_in + nw * n_in]
    x_ref, lw_ref, lb_ref, o_ref = refs[n_in + nw * n_in:]
    d = lambda a, b: jnp.dot(a, b, preferred_element_type=F32)
    mix = None
    for i, a_ref in enumerate(a_refs):
        if three:
            ah, al = _split2(a_ref[...])
            wh = w_refs[2 * i][...]
            part = (d(al, wh) + d(ah, w_refs[2 * i + 1][...])) + d(ah, wh)
        else:
            part = d(a_ref[...].astype(BF16), w_refs[i][...])
        mix = part if mix is None else mix + part
    o_ref[...] = _layer_norm(ALPHA * x_ref[...] + mix, lw_ref[...], lb_ref[...])


def _outproj_ln(acts, ws, x, ln_w, ln_b, tm):
    t = x.shape[0]
    n_in = len(acts)
    three = isinstance(ws[0], tuple)
    flat_ws = [w for pair in ws for w in pair] if three else list(ws)
    row = lambda i: (i, 0)
    fixed = lambda i: (0, 0)
    in_specs = ([pl.BlockSpec((tm, a.shape[1]), row) for a in acts]
                + [pl.BlockSpec(w.shape, fixed) for w in flat_ws]
                + [pl.BlockSpec((tm, D_MODEL), row),
                   pl.BlockSpec((1, D_MODEL), fixed), pl.BlockSpec((1, D_MODEL), fixed)])
    return pl.pallas_call(
        functools.partial(_outproj_ln_kernel, n_in, three),
        out_shape=jax.ShapeDtypeStruct((t, D_MODEL), F32),
        grid=(t // tm,),
        in_specs=in_specs,
        out_specs=pl.BlockSpec((tm, D_MODEL), row),
        compiler_params=_params("parallel"),
        name="out_proj_ln",
    )(*acts, *flat_ws, x, ln_w, ln_b)


def _router_kernel(x_ref, rwt_ref, bias_ref, g_ref):
    tm = x_ref.shape[0]
    w3 = _split3(rwt_ref[...])
    x3 = _split3(x_ref[...])
    nt = lambda a, b: lax.dot_general(a, b, (((1,), (1,)), ((), ())), preferred_element_type=F32)
    logits = (((nt(w3[0], x3[2]) + nt(w3[2], x3[0])) + nt(w3[1], x3[1]))
              + (nt(w3[0], x3[1]) + nt(w3[1], x3[0]))) + nt(w3[0], x3[0])
    scores = _sigmoid(logits)
    sel = scores + bias_ref[...]
    s = [sel[e:e + 1, :] for e in range(N_EXPERTS)]
    sc = [scores[e:e + 1, :] for e in range(N_EXPERTS)]
    grp = []
    for g in range(N_GROUPS):
        m = s[g * EXPERTS_PER_GROUP:(g + 1) * EXPERTS_PER_GROUP]
        best = None
        for i in range(EXPERTS_PER_GROUP):
            for j in range(i + 1, EXPERTS_PER_GROUP):
                p = m[i] + m[j]
                best = p if best is None else jnp.maximum(best, p)
        grp.append(best)
    best_g = jnp.zeros((1, tm), jnp.int32)
    best_v = grp[0]
    for g in range(1, N_GROUPS):
        upd = grp[g] > best_v
        best_g = jnp.where(upd, g, best_g)
        best_v = jnp.where(upd, grp[g], best_v)
    neg = jnp.full((1, tm), -jnp.inf, F32)
    ms = [jnp.where(best_g == e // EXPERTS_PER_GROUP, s[e], neg) for e in range(N_EXPERTS)]

    def first_argmax(vals):
        idx = jnp.zeros((1, tm), jnp.int32)
        top = vals[0]
        for e in range(1, N_EXPERTS):
            upd = vals[e] > top
            idx = jnp.where(upd, e, idx)
            top = jnp.where(upd, vals[e], top)
        return idx

    idx1 = first_argmax(ms)
    idx2 = first_argmax([jnp.where(idx1 == e, neg, ms[e]) for e in range(N_EXPERTS)])
    zero = jnp.zeros((1, tm), F32)
    w1 = zero
    w2 = zero
    for e in range(N_EXPERTS):
        w1 = w1 + jnp.where(idx1 == e, sc[e], zero)
        w2 = w2 + jnp.where(idx2 == e, sc[e], zero)
    tot = w1 + w2
    g1 = w1 / tot
    g2 = w2 / tot
    rows = [jnp.where(idx1 == e, g1, zero) + jnp.where(idx2 == e, g2, zero)
            for e in range(N_EXPERTS)]
    gates_t = jnp.concatenate(rows + [jnp.zeros((LANES - N_EXPERTS, tm), F32)], axis=0)
    g_ref[...] = gates_t.T


def _router(x, rwt, bias, tm):
    t = x.shape[0]
    return pl.pallas_call(
        _router_kernel,
        out_shape=jax.ShapeDtypeStruct((t, LANES), F32),
        grid=(t // tm,),
        in_specs=[pl.BlockSpec((tm, D_MODEL), lambda i: (i, 0)),
                  pl.BlockSpec((N_EXPERTS, D_MODEL), lambda i: (0, 0)),
                  pl.BlockSpec((N_EXPERTS, 1), lambda i: (0, 0))],
        out_specs=pl.BlockSpec((tm, LANES), lambda i: (i, 0)),
        compiler_params=_params("parallel"),
        name="router",
    )(x, rwt, bias)


def _moe_kernel(x_ref, g_ref, wg_ref, wu_ref, wd_ref, lw_ref, lb_ref, o_ref, acc_ref, xb_ref):
    e = pl.program_id(1)

    @pl.when(e == 0)
    def _():
        xb_ref[...] = x_ref[...].astype(BF16)
        acc_ref[...] = jnp.zeros_like(acc_ref)

    xb = xb_ref[...]
    hg = jnp.dot(xb, wg_ref[0].astype(BF16), preferred_element_type=F32)
    hu = jnp.dot(xb, wu_ref[0].astype(BF16), preferred_element_type=F32)
    he = _silu(hg) * hu
    gates = g_ref[...]
    lane = lax.broadcasted_iota(jnp.int32, gates.shape, 1)
    ge = jnp.sum(jnp.where(lane == e, gates, 0.0), axis=1, keepdims=True)
    acc_ref[...] += ge * jnp.dot(he.astype(BF16), wd_ref[0].astype(BF16),
                                  preferred_element_type=F32)

    @pl.when(e == N_EXPERTS - 1)
    def _():
        o_ref[...] = _layer_norm(ALPHA * x_ref[...] + acc_ref[...], lw_ref[...], lb_ref[...])


def _moe_ln(x, gates, wg, wu, wd, layer, ln_w, ln_b, tm):
    t = x.shape[0]
    return pl.pallas_call(
        _moe_kernel,
        out_shape=jax.ShapeDtypeStruct((t, D_MODEL), F32),
        grid=(t // tm, N_EXPERTS),
        in_specs=[pl.BlockSpec((tm, D_MODEL), lambda i, e: (i, 0)),
                  pl.BlockSpec((tm, LANES), lambda i, e: (i, 0)),
                  pl.BlockSpec((1, D_MODEL, D_FF_EXPERT), lambda i, e: (layer * N_EXPERTS + e, 0, 0)),
                  pl.BlockSpec((1, D_MODEL, D_FF_EXPERT), lambda i, e: (layer * N_EXPERTS + e, 0, 0)),
                  pl.BlockSpec((1, D_FF_EXPERT, D_MODEL), lambda i, e: (layer * N_EXPERTS + e, 0, 0)),
                  pl.BlockSpec((1, D_MODEL), lambda i, e: (0, 0)),
                  pl.BlockSpec((1, D_MODEL), lambda i, e: (0, 0))],
        out_specs=pl.BlockSpec((tm, D_MODEL), lambda i, e: (i, 0)),
        scratch_shapes=[pltpu.VMEM((tm, D_MODEL), F32), pltpu.VMEM((tm, D_MODEL), BF16)],
        compiler_params=_params("parallel", "arbitrary"),
        name="moe_ln",
    )(x, gates, wg, wu, wd, ln_w, ln_b)


def _prefix_selector():
    n = VEC_TILE
    nsub = VEC_CHUNK // VEC_SUB
    t = np.arange(n)[:, None]
    s = np.arange(n)[None, :]
    incl = ((t // VEC_CHUNK) == (s // VEC_CHUNK)) & ((s % VEC_CHUNK) <= (t % VEC_CHUNK))
    r = np.arange((n // VEC_CHUNK) * nsub)[:, None]
    starts = ((r // nsub) == (s // VEC_CHUNK)) & ((s % VEC_CHUNK) < VEC_SUB * (r % nsub))
    return np.concatenate([incl, starts], axis=0).astype(np.float32)


def _vec_heads(heads, sel, mm):
    n = VEC_TILE
    nsub = VEC_CHUNK // VEC_SUB
    nchunk = n // VEC_CHUNK
    nrows = heads[0][0].shape[0]
    kdim = heads[0][0].shape[1]
    streams = [(h, i) for h in range(len(heads)) for i in range(0, nrows, n)]
    tile = lambda h, i, which: heads[h][which][i:i + n]

    prefs = [_dot_exact_rhs(sel, tile(h, i, 3)) for h, i in streams]
    rows_of = lambda fn, m: jnp.concatenate(
        [jnp.broadcast_to(fn(j), (m, kdim)) for j in range(n // m)], axis=0)
    sub = (lax.broadcasted_iota(jnp.int32, (n, kdim), 0) // VEC_SUB) % nsub
    q_cat, k_cat, q_dec0, updates = [], [], [], []
    for (h, i), pref in zip(streams, prefs):
        q, k, v = tile(h, i, 0), tile(h, i, 1), tile(h, i, 2)
        big_g = pref[0:n]
        start = lambda c, j, pref=pref: pref[n + c * nsub + j:n + c * nsub + j + 1]
        q_dec = [q * jnp.exp(big_g)]
        for j in range(1, nsub):
            base_j = rows_of(lambda c: start(c, j), VEC_CHUNK)
            q_dec.append(q * jnp.exp(jnp.minimum(big_g - base_j, 0.0)))
        base_own = rows_of(lambda m: start(m // nsub, m % nsub), VEC_SUB)
        k_rel = k * jnp.exp(base_own - big_g)
        k_cat.append(jnp.concatenate([jnp.where(sub == j, k_rel, 0.0) for j in range(nsub)],
                                     axis=1))
        q_cat.append(jnp.concatenate(q_dec, axis=1))
        q_dec0.append(q_dec[0])
        per_chunk = []
        for c in range(nchunk):
            rows = slice(c * VEC_CHUNK, (c + 1) * VEC_CHUNK)
            g_last = big_g[(c + 1) * VEC_CHUNK - 1:(c + 1) * VEC_CHUNK, :]
            kd = k[rows] * jnp.exp(g_last - big_g[rows])
            per_chunk.append((jnp.exp(g_last), mm.tn(v[rows], kd)))
        updates.append(per_chunk)
    scores = [mm.nt(qc, kc) for qc, kc in zip(q_cat, k_cat)]
    rr = lax.broadcasted_iota(jnp.int32, (n, n), 0)
    cc = lax.broadcasted_iota(jnp.int32, (n, n), 1)
    keep = (rr >= cc) & ((rr // VEC_CHUNK) == (cc // VEC_CHUNK))
    intra = [mm.nn(jnp.where(keep, sc, 0.0), tile(h, i, 2)) for (h, i), sc in zip(streams, scores)]

    states = [hd[4] for hd in heads]
    o_rows = [[] for _ in heads]
    for si, (h, i) in enumerate(streams):
        for c, (decay_last, update) in enumerate(updates[si]):
            rows = slice(c * VEC_CHUNK, (c + 1) * VEC_CHUNK)
            o_rows[h].append(intra[si][rows] + mm.nt(q_dec0[si][rows], states[h]))
            states[h] = states[h] * decay_last + update
    return [(jnp.concatenate(o_rows[h], axis=0), states[h]) for h in range(len(heads))]


def _gla_chunk_kernel(q_ref, k_ref, v_ref, go_ref, sm_ref, w2_ref, b2_ref, nw_ref, sel_ref,
                      o_ref, s_ref, st_ref):
    r = pl.program_id(2)

    @pl.when(r == 0)
    def _():
        st_ref[...] = jnp.zeros_like(st_ref)

    sm = sm_ref[...]
    heads = []
    for hh in range(VEC_HPS):
        kc = slice(hh * GLA_DK, (hh + 1) * GLA_DK)
        vc = slice(hh * GLA_DV, (hh + 1) * GLA_DV)
        gk = _log_sigmoid(_ThreePass.nn(sm, w2_ref[hh]) + b2_ref[hh]) / GLA_NORMALIZER
        heads.append((q_ref[:, kc] * (GLA_DK ** -0.5), k_ref[:, kc], v_ref[:, vc], gk, st_ref[hh]))
    finals = []
    for hh, (o, st) in enumerate(_vec_heads(heads, sel_ref[...], _ThreePass)):
        vc = slice(hh * GLA_DV, (hh + 1) * GLA_DV)
        o_ref[:, vc] = _rms(o, nw_ref[...]) * _silu(go_ref[:, vc])
        st_ref[hh] = st
        finals.append(st)

    @pl.when(r == pl.num_programs(2) - 1)
    def _():
        for hh in range(VEC_HPS):
            s_ref[0, hh] = finals[hh].T


def _gla_prompt(proj, w2p, b2, norm_w, sel, bsz, seq):
    nr = seq // VEC_ROWS
    ng = GLA_HEADS // VEC_HPS
    kw = VEC_HPS * GLA_DK
    vw = VEC_HPS * GLA_DV
    row = lambda off: (lambda b, h, r: (b * nr + r, off + h))
    return pl.pallas_call(
        _gla_chunk_kernel,
        out_shape=(jax.ShapeDtypeStruct((bsz * seq, GLA_VAL), F32),
                   jax.ShapeDtypeStruct((bsz, GLA_HEADS, GLA_DK, GLA_DV), F32)),
        grid=(bsz, ng, nr),
        in_specs=[pl.BlockSpec((VEC_ROWS, kw), row(AB_Q // kw)),
                  pl.BlockSpec((VEC_ROWS, kw), row(AB_K // kw)),
                  pl.BlockSpec((VEC_ROWS, vw), row(AB_V // vw)),
                  pl.BlockSpec((VEC_ROWS, vw), row(AB_GOUT // vw)),
                  pl.BlockSpec((VEC_ROWS, LANES), lambda b, h, r: (b * nr + r, AB_SMALL // LANES)),
                  pl.BlockSpec((VEC_HPS, LANES, GLA_DK), lambda b, h, r: (h, 0, 0)),
                  pl.BlockSpec((VEC_HPS, 1, GLA_DK), lambda b, h, r: (h, 0, 0)),
                  pl.BlockSpec((1, GLA_DV), lambda b, h, r: (0, 0)),
                  pl.BlockSpec(sel.shape, lambda b, h, r: (0, 0))],
        out_specs=(pl.BlockSpec((VEC_ROWS, vw), lambda b, h, r: (b * nr + r, h)),
                   pl.BlockSpec((1, VEC_HPS, GLA_DK, GLA_DV), lambda b, h, r: (b, h, 0, 0))),
        scratch_shapes=[pltpu.VMEM((VEC_HPS, GLA_DV, GLA_DK), F32)],
        compiler_params=_params("parallel", "parallel", "arbitrary"),
        name="gla_chunk",
    )(proj, proj, proj, proj, proj, w2p, b2, norm_w, sel)


def _hgrn_lower_bound(lbraw, layer):
    m = jnp.max(lbraw, axis=0, keepdims=True)
    ex = jnp.exp(lbraw - m)
    sm = ex / jnp.sum(ex, axis=0, keepdims=True)
    acc = sm[0:1]
    for i in range(1, layer + 1):
        acc = acc + sm[i:i + 1]
    return acc - sm[0:1]


def _hgrn_gates(q_raw, f_raw, lb):
    forget = lb + (1.0 - lb) * _sigmoid(f_raw)
    return _silu(q_raw), 1.0 - forget, jnp.log(forget)


def _hgrn_chunk_kernel(layer, q_ref, f_ref, i_ref, go_ref, lb_ref, nw_ref, sel_ref,
                       o_ref, s_ref, st_ref):
    r = pl.program_id(2)

    @pl.when(r == 0)
    def _():
        st_ref[...] = jnp.zeros_like(st_ref)

    lb_all = _hgrn_lower_bound(lb_ref[...], layer)
    heads = []
    for hh in range(VEC_HPS):
        kc = slice(hh * HG_EXPAND, (hh + 1) * HG_EXPAND)
        vc = slice(hh * HG_DI, (hh + 1) * HG_DI)
        q, k, g = _hgrn_gates(q_ref[:, kc], f_ref[:, kc], lb_all[:, kc])
        heads.append((q, k, i_ref[:, vc], g, st_ref[hh]))
    finals = []
    for hh, (o, st) in enumerate(_vec_heads(heads, sel_ref[...], _OnePass)):
        vc = slice(hh * HG_DI, (hh + 1) * HG_DI)
        o_ref[:, vc] = _rms(o, nw_ref[...]) * _silu(go_ref[:, vc])
        st_ref[hh] = st
        finals.append(st)

    @pl.when(r == pl.num_programs(2) - 1)
    def _():
        for hh in range(VEC_HPS):
            s_ref[0, hh] = finals[hh].T


def _hgrn_prompt(proj, lower_bounds, norm_w, sel, layer, bsz, seq):
    nr = seq // VEC_ROWS
    ng = HG_HEADS // VEC_HPS
    kw = VEC_HPS * HG_EXPAND
    vw = VEC_HPS * HG_DI
    row = lambda off: (lambda b, h, r: (b * nr + r, off + h))
    return pl.pallas_call(
        functools.partial(_hgrn_chunk_kernel, layer),
        out_shape=(jax.ShapeDtypeStruct((bsz * seq, HG_I), F32),
                   jax.ShapeDtypeStruct((bsz, HG_HEADS, HG_EXPAND, HG_DI), F32)),
        grid=(bsz, ng, nr),
        in_specs=[pl.BlockSpec((VEC_ROWS, kw), row(0)),
                  pl.BlockSpec((VEC_ROWS, kw), row(ng)),
                  pl.BlockSpec((VEC_ROWS, vw), row(2 * ng)),
                  pl.BlockSpec((VEC_ROWS, vw), row(3 * ng)),
                  pl.BlockSpec((DEPTH, kw), lambda b, h, r: (0, h)),
                  pl.BlockSpec((1, HG_DI), lambda b, h, r: (0, 0)),
                  pl.BlockSpec(sel.shape, lambda b, h, r: (0, 0))],
        out_specs=(pl.BlockSpec((VEC_ROWS, vw), lambda b, h, r: (b * nr + r, h)),
                   pl.BlockSpec((1, VEC_HPS, HG_EXPAND, HG_DI), lambda b, h, r: (b, h, 0, 0))),
        scratch_shapes=[pltpu.VMEM((VEC_HPS, HG_DI, HG_EXPAND), F32)],
        compiler_params=_params("parallel", "parallel", "arbitrary"),
        name="hgrn_chunk",
    )(proj, proj, proj, proj, lower_bounds, norm_w, sel)


def _conv_silu(xp, cw, cb, n, lead):
    acc = cb + cw[SSD_CONV - 1:SSD_CONV] * xp[lead:lead + n]
    for m in range(1, SSD_CONV):
        acc = acc + cw[SSD_CONV - 1 - m:SSD_CONV - m] * xp[lead - m:lead - m + n]
    return _silu(acc)


def _ssd_gate_norm(y, z, nw):
    yz = y * _silu(z)
    parts = []
    for g in range(SSD_GROUPS):
        cols = slice(g * SSD_GROUP_W, (g + 1) * SSD_GROUP_W)
        parts.append(_rms(yz[:, cols], nw[:, cols]))
    return jnp.concatenate(parts, axis=1)


def _ssd_chunk_kernel(z_ref, xbc_ref, sm_ref, cw_ref, cb_ref, dtb_ref, alog_ref, dsk_ref,
                      nw_ref, ex_ref, o_ref, s_ref, conv_ref, st_ref, prev_ref):
    r = pl.program_id(1)
    c = SSD_CHUNK
    mm = _ThreePass

    @pl.when(r == 0)
    def _():
        st_ref[...] = jnp.zeros_like(st_ref)
        prev_ref[...] = jnp.zeros_like(prev_ref)

    x_raw = xbc_ref[...]
    xp = jnp.concatenate([prev_ref[...], x_raw], axis=0)
    prev_ref[...] = x_raw[c - 8:c]
    xc = _conv_silu(xp, cw_ref[...], cb_ref[...], c, 8)
    xs = xc[:, :SSD_INNER]
    bm = xc[:, SSD_INNER:SSD_INNER + SSD_BC]
    cm = xc[:, SSD_INNER + SSD_BC:]

    dt = _softplus(sm_ref[...] + dtb_ref[...])
    a_neg = -jnp.exp(alog_ref[...])
    big_g = _dot_exact_rhs(_tril(c).astype(BF16), dt * a_neg)
    g_t = big_g.T
    g_last = big_g[c - 1:c, :]
    ex = ex_ref[...]
    dt_x = _dot_exact_lhs(dt, ex)
    eg_x = _dot_exact_lhs(jnp.exp(big_g), ex)
    w_x = _dot_exact_lhs(dt * jnp.exp(g_last - big_g), ex)
    xdt = xs * dt_x
    xw = xs * w_x
    causal = _tril(c)
    lane = lax.broadcasted_iota(jnp.int32, (c, LANES), 1)
    st = st_ref[...]
    y_parts = []
    u_parts = []
    for g in range(SSD_GROUPS):
        gcols = slice(g * SSD_GROUP_W, (g + 1) * SSD_GROUP_W)
        bg = bm[:, g * SSD_STATE:(g + 1) * SSD_STATE]
        cg = cm[:, g * SSD_STATE:(g + 1) * SSD_STATE]
        sc = mm.nt(cg, bg)
        inter = mm.nn(cg, st[:, gcols])
        u_parts.append(mm.tn(bg, xw[:, gcols]))
        pair_cols = []
        heads_per_group = SSD_HEADS // SSD_GROUPS
        for p in range(heads_per_group // 2):
            h0 = g * heads_per_group + 2 * p
            xpair = xdt[:, h0 * SSD_HEADDIM:(h0 + 2) * SSD_HEADDIM]
            ws = []
            for h in (h0, h0 + 1):
                diff = big_g[:, h:h + 1] - g_t[h:h + 1, :]
                ws.append(sc * jnp.exp(jnp.where(causal, diff, -jnp.inf)))
            x_diag = jnp.concatenate([jnp.where(lane < SSD_HEADDIM, xpair, 0.0),
                                      jnp.where(lane < SSD_HEADDIM, 0.0, xpair)], axis=0)
            pair_cols.append(mm.nn(jnp.concatenate(ws, axis=1), x_diag))
        y_intra = jnp.concatenate(pair_cols, axis=1)
        y_parts.append(y_intra + inter * eg_x[:, gcols])
    y = jnp.concatenate(y_parts, axis=1) + dsk_ref[...] * xs
    o_ref[...] = _ssd_gate_norm(y, z_ref[...], nw_ref[...])
    st = st * eg_x[c - 1:c, :] + jnp.concatenate(u_parts, axis=1)
    st_ref[...] = st

    @pl.when(r == pl.num_programs(1) - 1)
    def _():
        s_ref[0] = st
        conv_ref[0] = x_raw[c - (SSD_CONV - 1):c]


def _ssd_prompt(proj, conv_w, conv_b, dtb_p, alog_p, dskip_x, norm_w, expand, bsz, seq):
    nr = seq // SSD_CHUNK
    fixed = lambda b, r: (0, 0)
    return pl.pallas_call(
        _ssd_chunk_kernel,
        out_shape=(jax.ShapeDtypeStruct((bsz * seq, SSD_INNER), F32),
                   jax.ShapeDtypeStruct((bsz, SSD_STATE, SSD_INNER), F32),
                   jax.ShapeDtypeStruct((bsz, SSD_CONV - 1, SSD_CONV_DIM), F32)),
        grid=(bsz, nr),
        in_specs=[pl.BlockSpec((SSD_CHUNK, SSD_INNER), lambda b, r: (b * nr + r, AB_Z // SSD_INNER)),
                  pl.BlockSpec((SSD_CHUNK, SSD_CONV_DIM), lambda b, r: (b * nr + r, AB_XBC // SSD_CONV_DIM)),
                  pl.BlockSpec((SSD_CHUNK, LANES), lambda b, r: (b * nr + r, AB_SMALL // LANES)),
                  pl.BlockSpec((SSD_CONV, SSD_CONV_DIM), fixed),
                  pl.BlockSpec((1, SSD_CONV_DIM), fixed),
                  pl.BlockSpec((1, LANES), fixed),
                  pl.BlockSpec((1, LANES), fixed),
                  pl.BlockSpec((1, SSD_INNER), fixed),
                  pl.BlockSpec((1, SSD_INNER), fixed),
                  pl.BlockSpec((LANES, SSD_INNER), fixed)],
        out_specs=(pl.BlockSpec((SSD_CHUNK, SSD_INNER), lambda b, r: (b * nr + r, 0)),
                   pl.BlockSpec((1, SSD_STATE, SSD_INNER), lambda b, r: (b, 0, 0)),
                   pl.BlockSpec((1, SSD_CONV - 1, SSD_CONV_DIM), lambda b, r: (b, 0, 0))),
        scratch_shapes=[pltpu.VMEM((SSD_STATE, SSD_INNER), F32),
                        pltpu.VMEM((8, SSD_CONV_DIM), F32)],
        compiler_params=_params("parallel", "arbitrary"),
        name="ssd_chunk",
    )(proj, proj, proj, conv_w, conv_b, dtb_p, alog_p, dskip_x, norm_w, expand)


def _ab_prep_kernel(q_ref, sm_ref, xbc_ref, cs_ref, w2_ref, b2_ref, cw_ref, cb_ref, dtb_ref,
                    alog_ref, qs_ref, dec_ref, xc_ref, dt_ref, da_ref, cs_out_ref):
    sm = sm_ref[...]
    gk = _log_sigmoid(_ThreePass.nn(sm, w2_ref[...]) + b2_ref[...]) / GLA_NORMALIZER
    qs_ref[...] = q_ref[...] * (GLA_DK ** -0.5)
    dec_ref[...] = jnp.exp(gk)
    cw = cw_ref[...]
    x_raw = xbc_ref[...]
    acc = cb_ref[...] + cw[SSD_CONV - 1:SSD_CONV] * x_raw
    for j in range(SSD_CONV - 1):
        acc = acc + cw[j:j + 1] * cs_ref[j]
    xc_ref[...] = _silu(acc)
    for j in range(SSD_CONV - 2):
        cs_out_ref[j] = cs_ref[j + 1]
    cs_out_ref[SSD_CONV - 2] = x_raw
    dt = _softplus(sm + dtb_ref[...])
    dt_ref[...] = dt
    da_ref[...] = jnp.exp(dt * -jnp.exp(alog_ref[...]))


def _ab_prep(proj, conv_state, w2_wide, b2_wide, conv_w, conv_b, dtb_p, alog_p):
    bsz = proj.shape[0]
    fixed = lambda i: (0, 0)
    sds = jax.ShapeDtypeStruct
    return pl.pallas_call(
        _ab_prep_kernel,
        out_shape=(sds((bsz, GLA_KEY), F32), sds((bsz, GLA_KEY), F32),
                   sds((bsz, SSD_CONV_DIM), F32), sds((bsz, LANES), F32), sds((bsz, LANES), F32),
                   sds((SSD_CONV - 1, bsz, SSD_CONV_DIM), F32)),
        grid=(1,),
        in_specs=[pl.BlockSpec((bsz, GLA_KEY), lambda i: (0, AB_Q // GLA_KEY)),
                  pl.BlockSpec((bsz, LANES), lambda i: (0, AB_SMALL // LANES)),
                  pl.BlockSpec((bsz, SSD_CONV_DIM), lambda i: (0, AB_XBC // SSD_CONV_DIM)),
                  pl.BlockSpec((SSD_CONV - 1, bsz, SSD_CONV_DIM), lambda i: (0, 0, 0)),
                  pl.BlockSpec((LANES, GLA_KEY), fixed),
                  pl.BlockSpec((1, GLA_KEY), fixed),
                  pl.BlockSpec((SSD_CONV, SSD_CONV_DIM), fixed),
                  pl.BlockSpec((1, SSD_CONV_DIM), fixed),
                  pl.BlockSpec((1, LANES), fixed),
                  pl.BlockSpec((1, LANES), fixed)],
        out_specs=(pl.BlockSpec((bsz, GLA_KEY), fixed), pl.BlockSpec((bsz, GLA_KEY), fixed),
                   pl.BlockSpec((bsz, SSD_CONV_DIM), fixed), pl.BlockSpec((bsz, LANES), fixed),
                   pl.BlockSpec((bsz, LANES), fixed),
                   pl.BlockSpec((SSD_CONV - 1, bsz, SSD_CONV_DIM), lambda i: (0, 0, 0))),
        compiler_params=_params("arbitrary"),
        name="ab_prep",
    )(proj, proj, proj, conv_state, w2_wide, b2_wide, conv_w, conv_b, dtb_p, alog_p)


def _hgrn_prep_kernel(layer, q_ref, f_ref, lb_ref, qs_ref, k_ref, dec_ref):
    lb = _hgrn_lower_bound(lb_ref[...], layer)
    forget = lb + (1.0 - lb) * _sigmoid(f_ref[...])
    qs_ref[...] = _silu(q_ref[...])
    k_ref[...] = 1.0 - forget
    dec_ref[...] = jnp.exp(jnp.log(forget))


def _hgrn_prep(proj, lower_bounds, layer):
    bsz = proj.shape[0]
    blk = lambda j: pl.BlockSpec((bsz, HG_F), lambda i: (0, j))
    return pl.pallas_call(
        functools.partial(_hgrn_prep_kernel, layer),
        out_shape=tuple(jax.ShapeDtypeStruct((bsz, HG_F), F32) for _ in range(3)),
        grid=(1,),
        in_specs=[blk(0), blk(1), pl.BlockSpec((DEPTH, HG_F), lambda i: (0, 0))],
        out_specs=tuple(blk(0) for _ in range(3)),
        compiler_params=_params("arbitrary"),
        name="hgrn_prep",
    )(proj, proj, lower_bounds)


def _vec_step_kernel(s_ref, q_ref, k_ref, d_ref, v_ref, go_ref, nw_ref, so_ref, o_ref):
    qt = q_ref[0, 0]
    kt = k_ref[0, 0]
    dt = d_ref[0, 0]
    v = v_ref[...]
    rows = []
    for b in range(STEP_B):
        sn = s_ref[b, 0] * dt[:, b:b + 1] + kt[:, b:b + 1] * v[b:b + 1, :]
        so_ref[b, 0] = sn
        rows.append(jnp.sum(qt[:, b:b + 1] * sn, axis=0, keepdims=True))
    o = jnp.concatenate(rows, axis=0)
    o_ref[...] = _rms(o, nw_ref[...]) * _silu(go_ref[...])


def _vec_step(state, q_cols, k_cols, d_cols, vsrc, v_off, gsrc, g_off, norm_w):
    bsz, nh, kdim, vdim = state.shape
    col = lambda j, h: (h, j, 0, 0)
    return pl.pallas_call(
        _vec_step_kernel,
        out_shape=(jax.ShapeDtypeStruct(state.shape, F32),
                   jax.ShapeDtypeStruct((bsz, nh * vdim), F32)),
        grid=(bsz // STEP_B, nh),
        in_specs=[pl.BlockSpec((STEP_B, 1, kdim, vdim), lambda j, h: (j, h, 0, 0)),
                  pl.BlockSpec((1, 1, kdim, STEP_B), col),
                  pl.BlockSpec((1, 1, kdim, STEP_B), col),
                  pl.BlockSpec((1, 1, kdim, STEP_B), col),
                  pl.BlockSpec((STEP_B, vdim), lambda j, h: (j, v_off + h)),
                  pl.BlockSpec((STEP_B, vdim), lambda j, h: (j, g_off + h)),
                  pl.BlockSpec((1, vdim), lambda j, h: (0, 0))],
        out_specs=(pl.BlockSpec((STEP_B, 1, kdim, vdim), lambda j, h: (j, h, 0, 0)),
                   pl.BlockSpec((STEP_B, vdim), lambda j, h: (j, h))),
        compiler_params=_params("parallel", "parallel"),
        name="vec_step",
    )(state, q_cols, k_cols, d_cols, vsrc, gsrc, norm_w)


def _ssd_step_kernel(s_ref, b_ref, c_ref, x_ref, dt_ref, da_ref, dsk_ref, so_ref, y_ref):
    bt = b_ref[0, 0]
    ct = c_ref[0, 0]
    x = x_ref[...]
    dt = dt_ref[0]
    da = da_ref[0]
    hpg = SSD_HEADS // SSD_GROUPS
    rows = []
    for b in range(STEP_B):
        pieces = []
        for hh in range(hpg):
            xh = x[b:b + 1, hh * SSD_HEADDIM:(hh + 1) * SSD_HEADDIM]
            sn = s_ref[b, hh] * da[b:b + 1, hh:hh + 1] + (bt[:, b:b + 1] * dt[b:b + 1, hh:hh + 1]) * xh
            so_ref[b, hh] = sn
            pieces.append(jnp.sum(ct[:, b:b + 1] * sn, axis=0, keepdims=True))
        rows.append(jnp.concatenate(pieces, axis=1))
    y_ref[...] = jnp.concatenate(rows, axis=0) + dsk_ref[...] * x


def _ssd_step(state, b_cols, c_cols, xc, dt_g, da_g, dskip_x):
    bsz = state.shape[0]
    hpg = SSD_HEADS // SSD_GROUPS
    col = lambda j, g: (g, j, 0, 0)
    return pl.pallas_call(
        _ssd_step_kernel,
        out_shape=(jax.ShapeDtypeStruct(state.shape, F32),
                   jax.ShapeDtypeStruct((bsz, SSD_INNER), F32)),
        grid=(bsz // STEP_B, SSD_GROUPS),
        in_specs=[pl.BlockSpec((STEP_B, hpg, SSD_STATE, SSD_HEADDIM), lambda j, g: (j, g, 0, 0)),
                  pl.BlockSpec((1, 1, SSD_STATE, STEP_B), col),
                  pl.BlockSpec((1, 1, SSD_STATE, STEP_B), col),
                  pl.BlockSpec((STEP_B, SSD_GROUP_W), lambda j, g: (j, g)),
                  pl.BlockSpec((1, STEP_B, LANES), lambda j, g: (g, j, 0)),
                  pl.BlockSpec((1, STEP_B, LANES), lambda j, g: (g, j, 0)),
                  pl.BlockSpec((1, SSD_GROUP_W), lambda j, g: (0, g))],
        out_specs=(pl.BlockSpec((STEP_B, hpg, SSD_STATE, SSD_HEADDIM), lambda j, g: (j, g, 0, 0)),
                   pl.BlockSpec((STEP_B, SSD_GROUP_W), lambda j, g: (j, g))),
        compiler_params=_params("parallel", "parallel"),
        name="ssd_step",
    )(state, b_cols, c_cols, xc, dt_g, da_g, dskip_x)


def _ssd_post_kernel(y_ref, z_ref, nw_ref, o_ref):
    o_ref[...] = _ssd_gate_norm(y_ref[...], z_ref[...], nw_ref[...])


def _ssd_post(y, proj, norm_w):
    bsz = y.shape[0]
    return pl.pallas_call(
        _ssd_post_kernel,
        out_shape=jax.ShapeDtypeStruct((bsz, SSD_INNER), F32),
        grid=(1,),
        in_specs=[pl.BlockSpec((bsz, SSD_INNER), lambda i: (0, 0)),
                  pl.BlockSpec((bsz, SSD_INNER), lambda i: (0, AB_Z // SSD_INNER)),
                  pl.BlockSpec((1, SSD_INNER), lambda i: (0, 0))],
        out_specs=pl.BlockSpec((bsz, SSD_INNER), lambda i: (0, 0)),
        compiler_params=_params("arbitrary"),
        name="ssd_post",
    )(y, proj, norm_w)


def _to_cols(a, nh):
    bsz = a.shape[0]
    return a.reshape(bsz // STEP_B, STEP_B, nh, -1).transpose(2, 0, 3, 1)


def _prep_weights(w_in_ab, w_gk2, b_gk2, gla_norm_w, conv_w, conv_b, dt_bias, a_log, d_skip,
                  ssd_norm_w, w_out_ab, w_in_c, hg_norm_w, w_out_c, router_w, router_bias,
                  w_gate, w_up, w_down, ln1_w, ln1_b, ln2_w, ln2_b):
    offs = np.cumsum([0, GLA_KEY, GLA_KEY, GLA_VAL, GLA_VAL, GLA_RANK, SSD_INNER, SSD_CONV_DIM,
                      SSD_HEADS])
    sec = lambda w, i: w[:, offs[i]:offs[i + 1]]
    w = w_in_ab[0]
    pad = jnp.zeros((D_MODEL, LANES - SSD_HEADS - GLA_RANK), w.dtype)
    w_ab = jnp.concatenate([sec(w, 5), sec(w, 2), sec(w, 3), sec(w, 6), sec(w, 0), sec(w, 1),
                            sec(w, 7), sec(w, 4), pad], axis=1)
    hi_lo = lambda m: (m.astype(BF16), (m - m.astype(BF16).astype(F32)).astype(BF16))
    w2_wide = jnp.zeros((LANES, GLA_KEY), F32).at[SSD_HEADS:SSD_HEADS + GLA_RANK].set(w_gk2[0])
    lane_pad = lambda v: jnp.zeros((1, LANES), F32).at[0, :SSD_HEADS].set(v)
    expand = np.zeros((LANES, SSD_INNER), np.float32)
    for h in range(SSD_HEADS):
        expand[h, h * SSD_HEADDIM:(h + 1) * SSD_HEADDIM] = 1.0
    return dict(
        w_ab=hi_lo(w_ab),
        w2_wide=w2_wide,
        w2_heads=w2_wide.reshape(LANES, GLA_HEADS, GLA_DK).transpose(1, 0, 2),
        b2_wide=b_gk2[0].reshape(1, GLA_KEY),
        b2_heads=b_gk2[0].reshape(GLA_HEADS, 1, GLA_DK),
        gla_norm_w=gla_norm_w[0].reshape(1, GLA_DV),
        conv_w=conv_w[0], conv_b=conv_b[0].reshape(1, SSD_CONV_DIM),
        dtb_p=lane_pad(dt_bias[0]), alog_p=lane_pad(a_log[0]),
        dskip_x=jnp.repeat(d_skip[0], SSD_HEADDIM).reshape(1, SSD_INNER),
        ssd_norm_w=ssd_norm_w[0].reshape(1, SSD_INNER),
        expand=jnp.asarray(expand, BF16),
        prefix_sel=jnp.asarray(_prefix_selector(), BF16),
        w_out_gla=hi_lo(w_out_ab[0, :GLA_VAL]),
        w_out_ssd=hi_lo(w_out_ab[0, GLA_VAL:]),
        w_c=w_in_c[0].astype(BF16),
        hg_norm_w=hg_norm_w[0].reshape(1, HG_DI),
        w_out_c=w_out_c[0].astype(BF16),
        rwt=router_w.T,
        rbias=router_bias.reshape(N_EXPERTS, 1),
        w_gate=w_gate.reshape(DEPTH * N_EXPERTS, D_MODEL, D_FF_EXPERT),
        w_up=w_up.reshape(DEPTH * N_EXPERTS, D_MODEL, D_FF_EXPERT),
        w_down=w_down.reshape(DEPTH * N_EXPERTS, D_FF_EXPERT, D_MODEL),
        ln1_w=ln1_w.reshape(DEPTH, 1, D_MODEL), ln1_b=ln1_b.reshape(DEPTH, 1, D_MODEL),
        ln2_w=ln2_w.reshape(DEPTH, 1, D_MODEL), ln2_b=ln2_b.reshape(DEPTH, 1, D_MODEL),
    )


def _ffn(x, p, layer, tm, tm_moe):
    gates = _router(x, p['rwt'], p['rbias'], tm)
    return _moe_ln(x, gates, p['w_gate'], p['w_up'], p['w_down'], layer,
                   p['ln2_w'][layer], p['ln2_b'][layer], tm_moe)


def _ssd_state_from_wide(s_wide):
    bsz = s_wide.shape[0]
    return s_wide.reshape(bsz, SSD_STATE, SSD_HEADS, SSD_HEADDIM).transpose(0, 2, 1, 3)


def _trunk_prompt(x3, p, lower_bounds, tm, tn_ab, tn_c):
    bsz, seq, _ = x3.shape
    x = x3.reshape(bsz * seq, D_MODEL)
    tm_big = 2 * tm
    proj = _proj(x, p['w_ab'], tm_big, tn_ab)
    o_gla, s_gla = _gla_prompt(proj, p['w2_heads'], p['b2_heads'], p['gla_norm_w'],
                               p['prefix_sel'], bsz, seq)
    yz, s_ssd, s_conv = _ssd_prompt(proj, p['conv_w'], p['conv_b'], p['dtb_p'], p['alog_p'],
                                    p['dskip_x'], p['ssd_norm_w'], p['expand'], bsz, seq)
    x = _outproj_ln([o_gla, yz], [p['w_out_gla'], p['w_out_ssd']], x, p['ln1_w'][0], p['ln1_b'][0], tm)
    x = _ffn(x, p, 0, tm, tm_big)
    proj_c = _proj(x, p['w_c'], tm_big, tn_c)
    o_hg, s_hg = _hgrn_prompt(proj_c, lower_bounds, p['hg_norm_w'], p['prefix_sel'], 1, bsz, seq)
    x = _outproj_ln([o_hg], [p['w_out_c']], x, p['ln1_w'][1], p['ln1_b'][1], tm)
    x = _ffn(x, p, 1, tm, tm_big)
    return (x.reshape(bsz, seq, D_MODEL), s_gla[None], _ssd_state_from_wide(s_ssd)[None],
            s_conv[None], s_hg[None])


def _trunk_sample(x3, st_gla, st_ssd, st_conv, st_hg, p, lower_bounds, tn_ab, tn_c):
    bsz = x3.shape[0]
    tm = bsz
    x = x3.reshape(bsz, D_MODEL)
    proj = _proj(x, p['w_ab'], tm, tn_ab)
    qs, dec, xc, dt, da, conv_new = _ab_prep(proj, st_conv[0].transpose(1, 0, 2), p['w2_wide'],
                                             p['b2_wide'], p['conv_w'], p['conv_b'], p['dtb_p'],
                                             p['alog_p'])
    conv_new = conv_new.transpose(1, 0, 2)
    k_gla = proj[:, AB_K:AB_K + GLA_KEY]
    s_gla, o_gla = _vec_step(st_gla[0], _to_cols(qs, GLA_HEADS), _to_cols(k_gla, GLA_HEADS),
                             _to_cols(dec, GLA_HEADS), proj, AB_V // GLA_DV, proj,
                             AB_GOUT // GLA_DV, p['gla_norm_w'])
    hpg = SSD_HEADS // SSD_GROUPS
    per_group = lambda a: jnp.pad(a[:, :SSD_HEADS].reshape(bsz, SSD_GROUPS, hpg).transpose(1, 0, 2),
                                  ((0, 0), (0, 0), (0, LANES - hpg)))
    s_ssd, y = _ssd_step(st_ssd[0],
                         _to_cols(xc[:, SSD_INNER:SSD_INNER + SSD_BC], SSD_GROUPS),
                         _to_cols(xc[:, SSD_INNER + SSD_BC:], SSD_GROUPS),
                         xc, per_group(dt), per_group(da), p['dskip_x'])
    yz = _ssd_post(y, proj, p['ssd_norm_w'])
    x = _outproj_ln([o_gla, yz], [p['w_out_gla'], p['w_out_ssd']], x, p['ln1_w'][0], p['ln1_b'][0], tm)
    x = _ffn(x, p, 0, tm, tm)
    proj_c = _proj(x, p['w_c'], tm, tn_c)
    qh, kh, dh = _hgrn_prep(proj_c, lower_bounds, 1)
    s_hg, o_hg = _vec_step(st_hg[0], _to_cols(qh, HG_HEADS), _to_cols(kh, HG_HEADS),
                           _to_cols(dh, HG_HEADS), proj_c, 2 * HG_HEADS, proj_c, 3 * HG_HEADS,
                           p['hg_norm_w'])
    x = _outproj_ln([o_hg], [p['w_out_c']], x, p['ln1_w'][1], p['ln1_b'][1], tm)
    x = _ffn(x, p, 1, tm, tm)
    return x.reshape(bsz, 1, D_MODEL), s_gla[None], s_ssd[None], conv_new[None], s_hg[None]


def kernel(x_prompt, x_sample, state_gla, state_ssd, state_conv, state_hgrn, w_in_ab, w_gk2, b_gk2, gla_norm_w, conv_w, conv_b, dt_bias, a_log, d_skip, ssd_norm_w, w_out_ab, w_in_c, lower_bounds, hg_norm_w, w_out_c, router_w, router_bias, w_gate, w_up, w_down, ln1_w, ln1_b, ln2_w, ln2_b):
    p = _prep_weights(w_in_ab, w_gk2, b_gk2, gla_norm_w, conv_w, conv_b, dt_bias, a_log, d_skip,
                      ssd_norm_w, w_out_ab, w_in_c, hg_norm_w, w_out_c, router_w, router_bias,
                      w_gate, w_up, w_down, ln1_w, ln1_b, ln2_w, ln2_b)
    y_p, gla_p, ssd_p, conv_p, hg_p = _trunk_prompt(x_prompt, p, lower_bounds, 512, 1152, 1024)
    y_s, gla_s, ssd_s, conv_s, hg_s = _trunk_sample(x_sample, state_gla, state_ssd, state_conv,
                                                    state_hgrn, p, lower_bounds, 1152, 1024)
    return (y_p, y_s, gla_p, ssd_p, conv_p, hg_p, gla_s, ssd_s, conv_s, hg_s)
```

```python
import functools

import numpy as np
import jax
import jax.numpy as jnp
from jax import lax
from jax.experimental import pallas as pl
from jax.experimental.pallas import tpu as pltpu

F32 = jnp.float32
BF16 = jnp.bfloat16

D_MODEL = 1024
DEPTH = 2
GLA_HEADS = 4
GLA_DK = 128
GLA_DV = 256
GLA_KEY = GLA_HEADS * GLA_DK
GLA_VAL = GLA_HEADS * GLA_DV
GLA_RANK = 16
GLA_NORMALIZER = 16.0
SSD_INNER = 1024
SSD_HEADDIM = 64
SSD_HEADS = 16
SSD_STATE = 128
SSD_GROUPS = 2
SSD_CONV = 4
SSD_GROUP_W = SSD_INNER // SSD_GROUPS
SSD_BC = SSD_GROUPS * SSD_STATE
SSD_CONV_DIM = SSD_INNER + 2 * SSD_BC
HG_EXPAND = 128
HG_HEADS = 8
HG_F = HG_HEADS * HG_EXPAND
HG_I = D_MODEL
HG_DI = HG_I // HG_HEADS
N_EXPERTS = 16
N_GROUPS = 4
EXPERTS_PER_GROUP = 4
D_FF_EXPERT = 512
ALPHA = (2 * DEPTH) ** 0.25
EPS = 1e-5

LANES = 128
VMEM_LIMIT = 48 * 1024 * 1024

AB_Z = 0
AB_V = 1024
AB_GOUT = 2048
AB_XBC = 3072
AB_Q = 4608
AB_K = 5120
AB_SMALL = 5632
AB_COLS = 5760
C_COLS = 4096

VEC_CHUNK = 64
VEC_SUB = 16
VEC_TILE = 256
VEC_ROWS = 512
VEC_HPS = 2
SSD_CHUNK = 128
STEP_B = 8


def _params(*sem):
    return pltpu.CompilerParams(dimension_semantics=sem, vmem_limit_bytes=VMEM_LIMIT)


_NN = (((1,), (0,)), ((), ()))
_NT = (((1,), (1,)), ((), ()))
_TN = (((0,), (0,)), ((), ()))


def _dot1(dims, a, b):
    return lax.dot_general(a.astype(BF16), b.astype(BF16), dims, preferred_element_type=F32)


def _split2(a):
    hi = a.astype(BF16)
    return hi, (a - hi.astype(F32)).astype(BF16)


def _dot3(dims, a, b):
    ah, al = _split2(a)
    bh, bl = _split2(b)
    d = lambda x, y: lax.dot_general(x, y, dims, preferred_element_type=F32)
    return (d(al, bh) + d(ah, bl)) + d(ah, bh)


class _OnePass:
    nn = staticmethod(lambda a, b: _dot1(_NN, a, b))
    nt = staticmethod(lambda a, b: _dot1(_NT, a, b))
    tn = staticmethod(lambda a, b: _dot1(_TN, a, b))


class _ThreePass:
    nn = staticmethod(lambda a, b: _dot3(_NN, a, b))
    nt = staticmethod(lambda a, b: _dot3(_NT, a, b))
    tn = staticmethod(lambda a, b: _dot3(_TN, a, b))


def _dot(a, b):
    return _dot1(_NN, a, b)


def _dot_nt(a, b):
    return _dot1(_NT, a, b)


def _dot_tn(a, b):
    return _dot1(_TN, a, b)


def _split3(a):
    hi = a.astype(BF16)
    r1 = a - hi.astype(F32)
    mid = r1.astype(BF16)
    lo = (r1 - mid.astype(F32)).astype(BF16)
    return hi, mid, lo


def _dot_exact_rhs(sel, a):
    hi, mid, lo = _split3(a)
    d = lambda p: jnp.dot(sel, p, preferred_element_type=F32)
    return (d(lo) + d(mid)) + d(hi)


def _dot_exact_lhs(a, sel):
    hi, mid, lo = _split3(a)
    d = lambda p: jnp.dot(p, sel, preferred_element_type=F32)
    return (d(lo) + d(mid)) + d(hi)


def _tril(n):
    r = lax.broadcasted_iota(jnp.int32, (n, n), 0)
    c = lax.broadcasted_iota(jnp.int32, (n, n), 1)
    return r >= c


def _sigmoid(x):
    return 1.0 / (1.0 + jnp.exp(-x))


def _silu(x):
    return x * _sigmoid(x)


def _softplus(x):
    return jnp.maximum(x, 0.0) + jnp.log(1.0 + jnp.exp(-jnp.abs(x)))


def _log_sigmoid(x):
    return -_softplus(-x)


def _rms(x, w):
    return x * lax.rsqrt(jnp.mean(x * x, axis=-1, keepdims=True) + EPS) * w


def _layer_norm(x, w, b):
    mu = jnp.mean(x, axis=-1, keepdims=True)
    xc = x - mu
    var = jnp.mean(xc * xc, axis=-1, keepdims=True)
    return xc * lax.rsqrt(var + EPS) * w + b


def _proj_kernel(x_ref, w_ref, o_ref):
    o_ref[...] = jnp.dot(x_ref[...].astype(BF16), w_ref[...], preferred_element_type=F32)


def _proj3_kernel(x_ref, wh_ref, wl_ref, o_ref, xh_ref, xl_ref):
    @pl.when(pl.program_id(1) == 0)
    def _():
        hi, lo = _split2(x_ref[...])
        xh_ref[...] = hi
        xl_ref[...] = lo

    d = lambda a, b: jnp.dot(a, b, preferred_element_type=F32)
    xh = xh_ref[...]
    wh = wh_ref[...]
    o_ref[...] = (d(xl_ref[...], wh) + d(xh, wl_ref[...])) + d(xh, wh)


def _proj(x, w, tm, tn):
    t, k = x.shape
    three = isinstance(w, tuple)
    ws = w if three else (w,)
    n = ws[0].shape[1]
    return pl.pallas_call(
        _proj3_kernel if three else _proj_kernel,
        out_shape=jax.ShapeDtypeStruct((t, n), F32),
        grid=(t // tm, n // tn),
        in_specs=[pl.BlockSpec((tm, k), lambda i, j: (i, 0))]
                 + [pl.BlockSpec((k, tn), lambda i, j: (0, j)) for _ in ws],
        out_specs=pl.BlockSpec((tm, tn), lambda i, j: (i, j)),
        scratch_shapes=[pltpu.VMEM((tm, k), BF16), pltpu.VMEM((tm, k), BF16)] if three else [],
        compiler_params=_params("parallel", "arbitrary"),
        name="in_proj",
    )(x, *ws)


def _outproj_ln_kernel(n_in, three, *refs):
    a_refs = refs[:n_in]
    nw = 2 if three else 1
    w_refs = refs[n_in:n_in + nw * n_in]
    x_ref, lw_ref, lb_ref, o_ref = refs[n_in + nw * n_in:]
    d = lambda a, b: jnp.dot(a, b, preferred_element_type=F32)
    mix = None
    for i, a_ref in enumerate(a_refs):
        if three:
            ah, al = _split2(a_ref[...])
            wh = w_refs[2 * i][...]
            part = (d(al, wh) + d(ah, w_refs[2 * i + 1][...])) + d(ah, wh)
        else:
            part = d(a_ref[...].astype(BF16), w_refs[i][...])
        mix = part if mix is None else mix + part
    o_ref[...] = _layer_norm(ALPHA * x_ref[...] + mix, lw_ref[...], lb_ref[...])


def _outproj_ln(acts, ws, x, ln_w, ln_b, tm):
    t = x.shape[0]
    n_in = len(acts)
    three = isinstance(ws[0], tuple)
    flat_ws = [w for pair in ws for w in pair] if three else list(ws)
    row = lambda i: (i, 0)
    fixed = lambda i: (0, 0)
    in_specs = ([pl.BlockSpec((tm, a.shape[1]), row) for a in acts]
                + [pl.BlockSpec(w.shape, fixed) for w in flat_ws]
                + [pl.BlockSpec((tm, D_MODEL), row),
                   pl.BlockSpec((1, D_MODEL), fixed), pl.BlockSpec((1, D_MODEL), fixed)])
    return pl.pallas_call(
        functools.partial(_outproj_ln_kernel, n_in, three),
        out_shape=jax.ShapeDtypeStruct((t, D_MODEL), F32),
        grid=(t // tm,),
        in_specs=in_specs,
        out_specs=pl.BlockSpec((tm, D_MODEL), row),
        compiler_params=_params("parallel"),
        name="out_proj_ln",
    )(*acts, *flat_ws, x, ln_w, ln_b)


def _router_scores(x, rwt, bias):
    w3 = _split3(rwt)
    x3 = _split3(x)
    nt = lambda a, b: lax.dot_general(a, b, _NT, preferred_element_type=F32)
    logits = (((nt(w3[0], x3[2]) + nt(w3[2], x3[0])) + nt(w3[1], x3[1]))
              + (nt(w3[0], x3[1]) + nt(w3[1], x3[0]))) + nt(w3[0], x3[0])
    scores = _sigmoid(logits)
    return scores, scores + bias


def _best_group(sel):
    tm = sel.shape[1]
    s = [sel[e:e + 1, :] for e in range(N_EXPERTS)]
    grp = []
    for g in range(N_GROUPS):
        m = s[g * EXPERTS_PER_GROUP:(g + 1) * EXPERTS_PER_GROUP]
        best = None
        for i in range(EXPERTS_PER_GROUP):
            for j in range(i + 1, EXPERTS_PER_GROUP):
                p = m[i] + m[j]
                best = p if best is None else jnp.maximum(best, p)
        grp.append(best)
    best_g = jnp.zeros((1, tm), jnp.int32)
    best_v = grp[0]
    for g in range(1, N_GROUPS):
        upd = grp[g] > best_v
        best_g = jnp.where(upd, g, best_g)
        best_v = jnp.where(upd, grp[g], best_v)
    return best_g


def _top2(vals, weights):
    tm = vals[0].shape[1]
    neg = jnp.full((1, tm), -jnp.inf, F32)

    def first_argmax(rows):
        idx = jnp.zeros((1, tm), jnp.int32)
        top = rows[0]
        for e in range(1, len(rows)):
            upd = rows[e] > top
            idx = jnp.where(upd, e, idx)
            top = jnp.where(upd, rows[e], top)
        return idx

    idx1 = first_argmax(vals)
    idx2 = first_argmax([jnp.where(idx1 == e, neg, v) for e, v in enumerate(vals)])
    zero = jnp.zeros((1, tm), F32)
    w1 = zero
    w2 = zero
    for e, w in enumerate(weights):
        w1 = w1 + jnp.where(idx1 == e, w, zero)
        w2 = w2 + jnp.where(idx2 == e, w, zero)
    tot = w1 + w2
    g1 = w1 / tot
    g2 = w2 / tot
    return [jnp.where(idx1 == e, g1, zero) + jnp.where(idx2 == e, g2, zero)
            for e in range(len(vals))]


def _pad_rows(rows, tm):
    return jnp.concatenate(rows + [jnp.zeros((LANES - len(rows), tm), F32)], axis=0)


def _route_in_group(x, rwt, bias, group):
    tm = x.shape[0]
    scores, sel = _router_scores(x, rwt, bias)
    zero = jnp.zeros((1, tm), F32)
    vals, weights = [], []
    for m in range(EXPERTS_PER_GROUP):
        v = zero
        w = zero
        for g in range(N_GROUPS):
            e = g * EXPERTS_PER_GROUP + m
            v = jnp.where(group == g, sel[e:e + 1, :], v)
            w = jnp.where(group == g, scores[e:e + 1, :], w)
        vals.append(v)
        weights.append(w)
    return _pad_rows(_top2(vals, weights), tm)


def _route(x, rwt, bias):
    tm = x.shape[0]
    scores, sel = _router_scores(x, rwt, bias)
    s = [sel[e:e + 1, :] for e in range(N_EXPERTS)]
    sc = [scores[e:e + 1, :] for e in range(N_EXPERTS)]
    best_g = _best_group(sel)
    neg = jnp.full((1, tm), -jnp.inf, F32)
    ms = [jnp.where(best_g == e // EXPERTS_PER_GROUP, s[e], neg) for e in range(N_EXPERTS)]
    return _pad_rows(_top2(ms, sc), tm), best_g


def _router_kernel(with_gates, x_ref, rwt_ref, bias_ref, g_ref):
    tm = x_ref.shape[0]
    if with_gates:
        gates_t, best_g = _route(x_ref[...], rwt_ref[...], bias_ref[...])
    else:
        best_g = _best_group(_router_scores(x_ref[...], rwt_ref[...], bias_ref[...])[1])
        gates_t = jnp.zeros((LANES, tm), F32)
    row = lax.broadcasted_iota(jnp.int32, (LANES, tm), 0)
    gates_t = jnp.where(row == N_EXPERTS, best_g.astype(F32), gates_t)
    g_ref[...] = gates_t.T


def _router(x, rwt, bias, tm, with_gates=True):
    t = x.shape[0]
    return pl.pallas_call(
        functools.partial(_router_kernel, with_gates),
        out_shape=jax.ShapeDtypeStruct((t, LANES), F32),
        grid=(t // tm,),
        in_specs=[pl.BlockSpec((tm, D_MODEL), lambda i: (i, 0)),
                  pl.BlockSpec((N_EXPERTS, D_MODEL), lambda i: (0, 0)),
                  pl.BlockSpec((N_EXPERTS, 1), lambda i: (0, 0))],
        out_specs=pl.BlockSpec((tm, LANES), lambda i: (i, 0)),
        compiler_params=_params("parallel"),
        name="router",
    )(x, rwt, bias)


def _moe_kernel(x_ref, g_ref, wg_ref, wu_ref, wd_ref, lw_ref, lb_ref, o_ref, acc_ref, xb_ref):
    e = pl.program_id(1)

    @pl.when(e == 0)
    def _():
        xb_ref[...] = x_ref[...].astype(BF16)
        acc_ref[...] = jnp.zeros_like(acc_ref)

    xb = xb_ref[...]
    hg = jnp.dot(xb, wg_ref[0].astype(BF16), preferred_element_type=F32)
    hu = jnp.dot(xb, wu_ref[0].astype(BF16), preferred_element_type=F32)
    he = _silu(hg) * hu
    gates = g_ref[...]
    lane = lax.broadcasted_iota(jnp.int32, gates.shape, 1)
    ge = jnp.sum(jnp.where(lane == e, gates, 0.0), axis=1, keepdims=True)
    acc_ref[...] += ge * jnp.dot(he.astype(BF16), wd_ref[0].astype(BF16),
                                  preferred_element_type=F32)

    @pl.when(e == N_EXPERTS - 1)
    def _():
        o_ref[...] = _layer_norm(ALPHA * x_ref[...] + acc_ref[...], lw_ref[...], lb_ref[...])


def _moe_ln(x, gates, wg, wu, wd, layer, ln_w, ln_b, tm):
    t = x.shape[0]
    return pl.pallas_call(
        _moe_kernel,
        out_shape=jax.ShapeDtypeStruct((t, D_MODEL), F32),
        grid=(t // tm, N_EXPERTS),
        in_specs=[pl.BlockSpec((tm, D_MODEL), lambda i, e: (i, 0)),
                  pl.BlockSpec((tm, LANES), lambda i, e: (i, 0)),
                  pl.BlockSpec((1, D_MODEL, D_FF_EXPERT), lambda i, e: (layer * N_EXPERTS + e, 0, 0)),
                  pl.BlockSpec((1, D_MODEL, D_FF_EXPERT), lambda i, e: (layer * N_EXPERTS + e, 0, 0)),
                  pl.BlockSpec((1, D_FF_EXPERT, D_MODEL), lambda i, e: (layer * N_EXPERTS + e, 0, 0)),
                  pl.BlockSpec((1, D_MODEL), lambda i, e: (0, 0)),
                  pl.BlockSpec((1, D_MODEL), lambda i, e: (0, 0))],
        out_specs=pl.BlockSpec((tm, D_MODEL), lambda i, e: (i, 0)),
        scratch_shapes=[pltpu.VMEM((tm, D_MODEL), F32), pltpu.VMEM((tm, D_MODEL), BF16)],
        compiler_params=_params("parallel", "arbitrary"),
        name="moe_ln",
    )(x, gates, wg, wu, wd, ln_w, ln_b)


def _group_rank_kernel(g_ref, rank_ref, tot_ref, carry_ref):
    n = g_ref.shape[0]

    @pl.when(pl.program_id(0) == 0)
    def _():
        carry_ref[...] = jnp.zeros_like(carry_ref)

    grp = g_ref[:, N_EXPERTS:N_EXPERTS + 1].astype(jnp.int32)
    lane = lax.broadcasted_iota(jnp.int32, (n, LANES), 1)
    onehot = jnp.where(lane == grp, 1.0, 0.0)
    rr = lax.broadcasted_iota(jnp.int32, (n, n), 0)
    cc = lax.broadcasted_iota(jnp.int32, (n, n), 1)
    before = jnp.where(rr > cc, 1.0, 0.0).astype(BF16)
    earlier = jnp.dot(before, onehot.astype(BF16), preferred_element_type=F32) + carry_ref[...]
    rank = jnp.sum(onehot * earlier, axis=1, keepdims=True)
    rank_ref[...] = jnp.broadcast_to(rank, (n, LANES))
    carry_ref[...] += jnp.sum(onehot, axis=0, keepdims=True)
    tot_ref[...] = jnp.broadcast_to(carry_ref[...], tot_ref.shape)


def _group_rank(gmat, tr):
    t = gmat.shape[0]
    return pl.pallas_call(
        _group_rank_kernel,
        out_shape=(jax.ShapeDtypeStruct((t, LANES), F32), jax.ShapeDtypeStruct((8, LANES), F32)),
        grid=(t // tr,),
        in_specs=[pl.BlockSpec((tr, LANES), lambda i: (i, 0))],
        out_specs=(pl.BlockSpec((tr, LANES), lambda i: (i, 0)),
                   pl.BlockSpec((8, LANES), lambda i: (0, 0))),
        scratch_shapes=[pltpu.VMEM((1, LANES), F32)],
        compiler_params=_params("arbitrary"),
        name="group_rank",
    )(gmat)


def _row_copy(src, dst, sem):
    return pltpu.make_async_copy(src, dst, sem)


def _scatter_rows_kernel(dest_ref, x_ref, init_ref, o_hbm, buf_ref, sem):
    del init_ref
    n = x_ref.shape[0]
    base = pl.program_id(0) * n
    for s in range(D_MODEL // LANES):
        buf_ref[:, s, :] = x_ref[:, s * LANES:(s + 1) * LANES]

    def start(i, carry):
        _row_copy(buf_ref.at[i], o_hbm.at[dest_ref[base + i]], sem).start()
        return carry

    lax.fori_loop(0, n, start, 0)
    _row_copy(buf_ref, o_hbm.at[pl.ds(0, n)], sem).wait()


def _scatter_rows(x, dest, n_out, tr):
    t = x.shape[0]
    slabs = D_MODEL // LANES
    init = jnp.zeros((n_out, slabs, LANES), F32)
    return pl.pallas_call(
        _scatter_rows_kernel,
        out_shape=jax.ShapeDtypeStruct((n_out, slabs, LANES), F32),
        grid_spec=pltpu.PrefetchScalarGridSpec(
            num_scalar_prefetch=1,
            grid=(t // tr,),
            in_specs=[pl.BlockSpec((tr, D_MODEL), lambda i, d: (i, 0)),
                      pl.BlockSpec(memory_space=pl.ANY)],
            out_specs=pl.BlockSpec(memory_space=pl.ANY),
            scratch_shapes=[pltpu.VMEM((tr, slabs, LANES), F32), pltpu.SemaphoreType.DMA(())]),
        input_output_aliases={2: 0},
        compiler_params=_params("arbitrary"),
        name="scatter_rows",
    )(dest, x, init)


def _gather_rows_kernel(src_ref, y_hbm, o_ref, buf_ref, sem):
    n = o_ref.shape[0]
    base = pl.program_id(0) * n

    def start(i, carry):
        _row_copy(y_hbm.at[src_ref[base + i]], buf_ref.at[i], sem).start()
        return carry

    lax.fori_loop(0, n, start, 0)
    _row_copy(y_hbm.at[pl.ds(0, n)], buf_ref, sem).wait()
    for s in range(D_MODEL // LANES):
        o_ref[:, s * LANES:(s + 1) * LANES] = buf_ref[:, s, :]


def _gather_rows(y3, src, tr):
    t = src.shape[0]
    slabs = D_MODEL // LANES
    return pl.pallas_call(
        _gather_rows_kernel,
        out_shape=jax.ShapeDtypeStruct((t, D_MODEL), F32),
        grid_spec=pltpu.PrefetchScalarGridSpec(
            num_scalar_prefetch=1,
            grid=(t // tr,),
            in_specs=[pl.BlockSpec(memory_space=pl.ANY)],
            out_specs=pl.BlockSpec((tr, D_MODEL), lambda i, d: (i, 0)),
            scratch_shapes=[pltpu.VMEM((tr, slabs, LANES), F32), pltpu.SemaphoreType.DMA(())]),
        compiler_params=_params("arbitrary"),
        name="gather_rows",
    )(src, y3)


def _moe_group_kernel(tg_ref, x3_ref, rwt_ref, rb_ref, wg_ref, wu_ref, wd_ref, lw_ref, lb_ref,
                      o3_ref, acc_ref, x_ref, xb_ref, gate_ref):
    i = pl.program_id(0)
    j = pl.program_id(1)
    group = tg_ref[i]
    slabs = D_MODEL // LANES

    @pl.when(group < 0)
    def _():
        o3_ref[...] = jnp.zeros_like(o3_ref)

    @pl.when(group >= 0)
    def _():
        @pl.when(j == 0)
        def _():
            for s in range(slabs):
                x_ref[:, s * LANES:(s + 1) * LANES] = x3_ref[:, s, :]
            x = x_ref[...]
            xb_ref[...] = x.astype(BF16)
            acc_ref[...] = jnp.zeros_like(acc_ref)
            gate_ref[...] = _route_in_group(x, rwt_ref[...], rb_ref[...], group).T

        xb = xb_ref[...]
        hg = jnp.dot(xb, wg_ref[0].astype(BF16), preferred_element_type=F32)
        hu = jnp.dot(xb, wu_ref[0].astype(BF16), preferred_element_type=F32)
        he = _silu(hg) * hu
        gates = gate_ref[...]
        lane = lax.broadcasted_iota(jnp.int32, gates.shape, 1)
        ge = jnp.sum(jnp.where(lane == j, gates, 0.0), axis=1, keepdims=True)
        acc_ref[...] += ge * jnp.dot(he.astype(BF16), wd_ref[0].astype(BF16),
                                     preferred_element_type=F32)

        @pl.when(j == EXPERTS_PER_GROUP - 1)
        def _():
            y = _layer_norm(ALPHA * x_ref[...] + acc_ref[...], lw_ref[...], lb_ref[...])
            for s in range(slabs):
                o3_ref[:, s, :] = y[:, s * LANES:(s + 1) * LANES]


def _moe_group_ln(xs3, tile_group, rwt, rbias, wg, wu, wd, layer, ln_w, ln_b, tm):
    n = xs3.shape[0]
    slabs = D_MODEL // LANES
    expert = lambda i, j, tg: (layer * N_EXPERTS + jnp.maximum(tg[i], 0) * EXPERTS_PER_GROUP + j, 0, 0)
    fixed = lambda i, j, tg: (0, 0)
    return pl.pallas_call(
        _moe_group_kernel,
        out_shape=jax.ShapeDtypeStruct((n, slabs, LANES), F32),
        grid_spec=pltpu.PrefetchScalarGridSpec(
            num_scalar_prefetch=1,
            grid=(n // tm, EXPERTS_PER_GROUP),
            in_specs=[pl.BlockSpec((tm, slabs, LANES), lambda i, j, tg: (i, 0, 0)),
                      pl.BlockSpec((N_EXPERTS, D_MODEL), fixed),
                      pl.BlockSpec((N_EXPERTS, 1), fixed),
                      pl.BlockSpec((1, D_MODEL, D_FF_EXPERT), expert),
                      pl.BlockSpec((1, D_MODEL, D_FF_EXPERT), expert),
                      pl.BlockSpec((1, D_FF_EXPERT, D_MODEL), expert),
                      pl.BlockSpec((1, D_MODEL), fixed),
                      pl.BlockSpec((1, D_MODEL), fixed)],
            out_specs=pl.BlockSpec((tm, slabs, LANES), lambda i, j, tg: (i, 0, 0)),
            scratch_shapes=[pltpu.VMEM((tm, D_MODEL), F32), pltpu.VMEM((tm, D_MODEL), F32),
                            pltpu.VMEM((tm, D_MODEL), BF16), pltpu.VMEM((tm, LANES), F32)]),
        compiler_params=_params("arbitrary", "arbitrary"),
        name="moe_group_ln",
    )(tile_group, xs3, rwt, rbias, wg, wu, wd, ln_w, ln_b)


def _ffn_sorted(x, p, layer, tm, tm_moe):
    t = x.shape[0]
    n_tiles = t // tm_moe + N_GROUPS
    gmat = _router(x, p['rwt'], p['rbias'], tm, with_gates=False)
    rank_mat, totals = _group_rank(gmat, tm)
    group = gmat[:, N_EXPERTS].astype(jnp.int32)
    counts = totals[0, :N_GROUPS].astype(jnp.int32)
    seg_tiles = (counts + tm_moe - 1) // tm_moe
    seg_end = jnp.cumsum(seg_tiles)
    seg_start = seg_end - seg_tiles
    is_group = group[:, None] == jnp.arange(N_GROUPS, dtype=jnp.int32)[None, :]
    dest = (jnp.sum(jnp.where(is_group, seg_start[None, :], 0), axis=1) * tm_moe
            + rank_mat[:, 0].astype(jnp.int32))
    tile_id = jnp.arange(n_tiles, dtype=jnp.int32)
    tile_group = jnp.sum((tile_id[:, None] >= seg_end[None, :]).astype(jnp.int32), axis=1)
    tile_group = jnp.where(tile_id < seg_end[N_GROUPS - 1], tile_group, -1)
    xs3 = _scatter_rows(x, dest, n_tiles * tm_moe, tm)
    ys3 = _moe_group_ln(xs3, tile_group, p['rwt'], p['rbias'], p['w_gate'], p['w_up'], p['w_down'],
                        layer, p['ln2_w'][layer], p['ln2_b'][layer], tm_moe)
    return _gather_rows(ys3, dest, tm)


def _prefix_selector():
    n = VEC_TILE
    nsub = VEC_CHUNK // VEC_SUB
    t = np.arange(n)[:, None]
    s = np.arange(n)[None, :]
    incl = ((t // VEC_CHUNK) == (s // VEC_CHUNK)) & ((s % VEC_CHUNK) <= (t % VEC_CHUNK))
    r = np.arange((n // VEC_CHUNK) * nsub)[:, None]
    starts = ((r // nsub) == (s // VEC_CHUNK)) & ((s % VEC_CHUNK) < VEC_SUB * (r % nsub))
    return np.concatenate([incl, starts], axis=0).astype(np.float32)


def _vec_heads(heads, sel, mm):
    n = VEC_TILE
    nsub = VEC_CHUNK // VEC_SUB
    nchunk = n // VEC_CHUNK
    nrows = heads[0][0].shape[0]
    kdim = heads[0][0].shape[1]
    streams = [(h, i) for h in range(len(heads)) for i in range(0, nrows, n)]
    tile = lambda h, i, which: heads[h][which][i:i + n]

    prefs = [_dot_exact_rhs(sel, tile(h, i, 3)) for h, i in streams]
    rows_of = lambda fn, m: jnp.concatenate(
        [jnp.broadcast_to(fn(j), (m, kdim)) for j in range(n // m)], axis=0)
    sub = (lax.broadcasted_iota(jnp.int32, (n, kdim), 0) // VEC_SUB) % nsub
    q_cat, k_cat, q_dec0, updates = [], [], [], []
    for (h, i), pref in zip(streams, prefs):
        q, k, v = tile(h, i, 0), tile(h, i, 1), tile(h, i, 2)
        big_g = pref[0:n]
        start = lambda c, j, pref=pref: pref[n + c * nsub + j:n + c * nsub + j + 1]
        q_dec = [q * jnp.exp(big_g)]
        for j in range(1, nsub):
            base_j = rows_of(lambda c: start(c, j), VEC_CHUNK)
            q_dec.append(q * jnp.exp(jnp.minimum(big_g - base_j, 0.0)))
        base_own = rows_of(lambda m: start(m // nsub, m % nsub), VEC_SUB)
        k_rel = k * jnp.exp(base_own - big_g)
        k_cat.append(jnp.concatenate([jnp.where(sub == j, k_rel, 0.0) for j in range(nsub)],
                                     axis=1))
        q_cat.append(jnp.concatenate(q_dec, axis=1))
        q_dec0.append(q_dec[0])
        per_chunk = []
        for c in range(nchunk):
            rows = slice(c * VEC_CHUNK, (c + 1) * VEC_CHUNK)
            g_last = big_g[(c + 1) * VEC_CHUNK - 1:(c + 1) * VEC_CHUNK, :]
            kd = k[rows] * jnp.exp(g_last - big_g[rows])
            per_chunk.append((jnp.exp(g_last), mm.tn(v[rows], kd)))
        updates.append(per_chunk)
    scores = [mm.nt(qc, kc) for qc, kc in zip(q_cat, k_cat)]
    rr = lax.broadcasted_iota(jnp.int32, (n, n), 0)
    cc = lax.broadcasted_iota(jnp.int32, (n, n), 1)
    keep = (rr >= cc) & ((rr // VEC_CHUNK) == (cc // VEC_CHUNK))
    intra = [mm.nn(jnp.where(keep, sc, 0.0), tile(h, i, 2)) for (h, i), sc in zip(streams, scores)]

    states = [hd[4] for hd in heads]
    o_rows = [[] for _ in heads]
    for si, (h, i) in enumerate(streams):
        for c, (decay_last, update) in enumerate(updates[si]):
            rows = slice(c * VEC_CHUNK, (c + 1) * VEC_CHUNK)
            o_rows[h].append(intra[si][rows] + mm.nt(q_dec0[si][rows], states[h]))
            states[h] = states[h] * decay_last + update
    return [(jnp.concatenate(o_rows[h], axis=0), states[h]) for h in range(len(heads))]


def _gla_chunk_kernel(q_ref, k_ref, v_ref, go_ref, sm_ref, w2_ref, b2_ref, nw_ref, sel_ref,
                      o_ref, s_ref, st_ref):
    r = pl.program_id(2)

    @pl.when(r == 0)
    def _():
        st_ref[...] = jnp.zeros_like(st_ref)

    sm = sm_ref[...]
    heads = []
    for hh in range(VEC_HPS):
        kc = slice(hh * GLA_DK, (hh + 1) * GLA_DK)
        vc = slice(hh * GLA_DV, (hh + 1) * GLA_DV)
        gk = _log_sigmoid(_ThreePass.nn(sm, w2_ref[hh]) + b2_ref[hh]) / GLA_NORMALIZER
        heads.append((q_ref[:, kc] * (GLA_DK ** -0.5), k_ref[:, kc], v_ref[:, vc], gk, st_ref[hh]))
    finals = []
    for hh, (o, st) in enumerate(_vec_heads(heads, sel_ref[...], _ThreePass)):
        vc = slice(hh * GLA_DV, (hh + 1) * GLA_DV)
        o_ref[:, vc] = _rms(o, nw_ref[...]) * _silu(go_ref[:, vc])
        st_ref[hh] = st
        finals.append(st)

    @pl.when(r == pl.num_programs(2) - 1)
    def _():
        for hh in range(VEC_HPS):
            s_ref[0, hh] = finals[hh].T


def _gla_prompt(proj, w2p, b2, norm_w, sel, bsz, seq):
    nr = seq // VEC_ROWS
    ng = GLA_HEADS // VEC_HPS
    kw = VEC_HPS * GLA_DK
    vw = VEC_HPS * GLA_DV
    row = lambda off: (lambda b, h, r: (b * nr + r, off + h))
    return pl.pallas_call(
        _gla_chunk_kernel,
        out_shape=(jax.ShapeDtypeStruct((bsz * seq, GLA_VAL), F32),
                   jax.ShapeDtypeStruct((bsz, GLA_HEADS, GLA_DK, GLA_DV), F32)),
        grid=(bsz, ng, nr),
        in_specs=[pl.BlockSpec((VEC_ROWS, kw), row(AB_Q // kw)),
                  pl.BlockSpec((VEC_ROWS, kw), row(AB_K // kw)),
                  pl.BlockSpec((VEC_ROWS, vw), row(AB_V // vw)),
                  pl.BlockSpec((VEC_ROWS, vw), row(AB_GOUT // vw)),
                  pl.BlockSpec((VEC_ROWS, LANES), lambda b, h, r: (b * nr + r, AB_SMALL // LANES)),
                  pl.BlockSpec((VEC_HPS, LANES, GLA_DK), lambda b, h, r: (h, 0, 0)),
                  pl.BlockSpec((VEC_HPS, 1, GLA_DK), lambda b, h, r: (h, 0, 0)),
                  pl.BlockSpec((1, GLA_DV), lambda b, h, r: (0, 0)),
                  pl.BlockSpec(sel.shape, lambda b, h, r: (0, 0))],
        out_specs=(pl.BlockSpec((VEC_ROWS, vw), lambda b, h, r: (b * nr + r, h)),
                   pl.BlockSpec((1, VEC_HPS, GLA_DK, GLA_DV), lambda b, h, r: (b, h, 0, 0))),
        scratch_shapes=[pltpu.VMEM((VEC_HPS, GLA_DV, GLA_DK), F32)],
        compiler_params=_params("parallel", "parallel", "arbitrary"),
        name="gla_chunk",
    )(proj, proj, proj, proj, proj, w2p, b2, norm_w, sel)


def _hgrn_lower_bound(lbraw, layer):
    m = jnp.max(lbraw, axis=0, keepdims=True)
    ex = jnp.exp(lbraw - m)
    sm = ex / jnp.sum(ex, axis=0, keepdims=True)
    acc = sm[0:1]
    for i in range(1, layer + 1):
        acc = acc + sm[i:i + 1]
    return acc - sm[0:1]


def _hgrn_gates(q_raw, f_raw, lb):
    forget = lb + (1.0 - lb) * _sigmoid(f_raw)
    return _silu(q_raw), 1.0 - forget, jnp.log(forget)


def _hgrn_chunk_kernel(layer, q_ref, f_ref, i_ref, go_ref, lb_ref, nw_ref, sel_ref,
                       o_ref, s_ref, st_ref):
    r = pl.program_id(2)

    @pl.when(r == 0)
    def _():
        st_ref[...] = jnp.zeros_like(st_ref)

    lb_all = _hgrn_lower_bound(lb_ref[...], layer)
    heads = []
    for hh in range(VEC_HPS):
        kc = slice(hh * HG_EXPAND, (hh + 1) * HG_EXPAND)
        vc = slice(hh * HG_DI, (hh + 1) * HG_DI)
        q, k, g = _hgrn_gates(q_ref[:, kc], f_ref[:, kc], lb_all[:, kc])
        heads.append((q, k, i_ref[:, vc], g, st_ref[hh]))
    finals = []
    for hh, (o, st) in enumerate(_vec_heads(heads, sel_ref[...], _OnePass)):
        vc = slice(hh * HG_DI, (hh + 1) * HG_DI)
        o_ref[:, vc] = _rms(o, nw_ref[...]) * _silu(go_ref[:, vc])
        st_ref[hh] = st
        finals.append(st)

    @pl.when(r == pl.num_programs(2) - 1)
    def _():
        for hh in range(VEC_HPS):
            s_ref[0, hh] = finals[hh].T


def _hgrn_prompt(proj, lower_bounds, norm_w, sel, layer, bsz, seq):
    nr = seq // VEC_ROWS
    ng = HG_HEADS // VEC_HPS
    kw = VEC_HPS * HG_EXPAND
    vw = VEC_HPS * HG_DI
    row = lambda off: (lambda b, h, r: (b * nr + r, off + h))
    return pl.pallas_call(
        functools.partial(_hgrn_chunk_kernel, layer),
        out_shape=(jax.ShapeDtypeStruct((bsz * seq, HG_I), F32),
                   jax.ShapeDtypeStruct((bsz, HG_HEADS, HG_EXPAND, HG_DI), F32)),
        grid=(bsz, ng, nr),
        in_specs=[pl.BlockSpec((VEC_ROWS, kw), row(0)),
                  pl.BlockSpec((VEC_ROWS, kw), row(ng)),
                  pl.BlockSpec((VEC_ROWS, vw), row(2 * ng)),
                  pl.BlockSpec((VEC_ROWS, vw), row(3 * ng)),
                  pl.BlockSpec((DEPTH, kw), lambda b, h, r: (0, h)),
                  pl.BlockSpec((1, HG_DI), lambda b, h, r: (0, 0)),
                  pl.BlockSpec(sel.shape, lambda b, h, r: (0, 0))],
        out_specs=(pl.BlockSpec((VEC_ROWS, vw), lambda b, h, r: (b * nr + r, h)),
                   pl.BlockSpec((1, VEC_HPS, HG_EXPAND, HG_DI), lambda b, h, r: (b, h, 0, 0))),
        scratch_shapes=[pltpu.VMEM((VEC_HPS, HG_DI, HG_EXPAND), F32)],
        compiler_params=_params("parallel", "parallel", "arbitrary"),
        name="hgrn_chunk",
    )(proj, proj, proj, proj, lower_bounds, norm_w, sel)


def _conv_silu(xp, cw, cb, n, lead):
    acc = cb + cw[SSD_CONV - 1:SSD_CONV] * xp[lead:lead + n]
    for m in range(1, SSD_CONV):
        acc = acc + cw[SSD_CONV - 1 - m:SSD_CONV - m] * xp[lead - m:lead - m + n]
    return _silu(acc)


def _ssd_gate_norm(y, z, nw):
    yz = y * _silu(z)
    parts = []
    for g in range(SSD_GROUPS):
        cols = slice(g * SSD_GROUP_W, (g + 1) * SSD_GROUP_W)
        parts.append(_rms(yz[:, cols], nw[:, cols]))
    return jnp.concatenate(parts, axis=1)


def _ssd_chunk_kernel(z_ref, xbc_ref, sm_ref, cw_ref, cb_ref, dtb_ref, alog_ref, dsk_ref,
                      nw_ref, ex_ref, o_ref, s_ref, conv_ref, st_ref, prev_ref):
    r = pl.program_id(1)
    c = SSD_CHUNK
    mm = _ThreePass

    @pl.when(r == 0)
    def _():
        st_ref[...] = jnp.zeros_like(st_ref)
        prev_ref[...] = jnp.zeros_like(prev_ref)

    x_raw = xbc_ref[...]
    xp = jnp.concatenate([prev_ref[...], x_raw], axis=0)
    prev_ref[...] = x_raw[c - 8:c]
    xc = _conv_silu(xp, cw_ref[...], cb_ref[...], c, 8)
    xs = xc[:, :SSD_INNER]
    bm = xc[:, SSD_INNER:SSD_INNER + SSD_BC]
    cm = xc[:, SSD_INNER + SSD_BC:]

    dt = _softplus(sm_ref[...] + dtb_ref[...])
    a_neg = -jnp.exp(alog_ref[...])
    big_g = _dot_exact_rhs(_tril(c).astype(BF16), dt * a_neg)
    g_t = big_g.T
    g_last = big_g[c - 1:c, :]
    ex = ex_ref[...]
    dt_x = _dot_exact_lhs(dt, ex)
    eg_x = _dot_exact_lhs(jnp.exp(big_g), ex)
    w_x = _dot_exact_lhs(dt * jnp.exp(g_last - big_g), ex)
    xdt = xs * dt_x
    xw = xs * w_x
    causal = _tril(c)
    lane = lax.broadcasted_iota(jnp.int32, (c, LANES), 1)
    st = st_ref[...]
    y_parts = []
    u_parts = []
    for g in range(SSD_GROUPS):
        gcols = slice(g * SSD_GROUP_W, (g + 1) * SSD_GROUP_W)
        bg = bm[:, g * SSD_STATE:(g + 1) * SSD_STATE]
        cg = cm[:, g * SSD_STATE:(g + 1) * SSD_STATE]
        sc = mm.nt(cg, bg)
        inter = mm.nn(cg, st[:, gcols])
        u_parts.append(mm.tn(bg, xw[:, gcols]))
        pair_cols = []
        heads_per_group = SSD_HEADS // SSD_GROUPS
        for p in range(heads_per_group // 2):
            h0 = g * heads_per_group + 2 * p
            xpair = xdt[:, h0 * SSD_HEADDIM:(h0 + 2) * SSD_HEADDIM]
            ws = []
            for h in (h0, h0 + 1):
                diff = big_g[:, h:h + 1] - g_t[h:h + 1, :]
                ws.append(sc * jnp.exp(jnp.where(causal, diff, -jnp.inf)))
            x_diag = jnp.concatenate([jnp.where(lane < SSD_HEADDIM, xpair, 0.0),
                                      jnp.where(lane < SSD_HEADDIM, 0.0, xpair)], axis=0)
            pair_cols.append(mm.nn(jnp.concatenate(ws, axis=1), x_diag))
        y_intra = jnp.concatenate(pair_cols, axis=1)
        y_parts.append(y_intra + inter * eg_x[:, gcols])
    y = jnp.concatenate(y_parts, axis=1) + dsk_ref[...] * xs
    o_ref[...] = _ssd_gate_norm(y, z_ref[...], nw_ref[...])
    st = st * eg_x[c - 1:c, :] + jnp.concatenate(u_parts, axis=1)
    st_ref[...] = st

    @pl.when(r == pl.num_programs(1) - 1)
    def _():
        s_ref[0] = st
        conv_ref[0] = x_raw[c - (SSD_CONV - 1):c]


def _ssd_prompt(proj, conv_w, conv_b, dtb_p, alog_p, dskip_x, norm_w, expand, bsz, seq):
    nr = seq // SSD_CHUNK
    fixed = lambda b, r: (0, 0)
    return pl.pallas_call(
        _ssd_chunk_kernel,
        out_shape=(jax.ShapeDtypeStruct((bsz * seq, SSD_INNER), F32),
                   jax.ShapeDtypeStruct((bsz, SSD_STATE, SSD_INNER), F32),
                   jax.ShapeDtypeStruct((bsz, SSD_CONV - 1, SSD_CONV_DIM), F32)),
        grid=(bsz, nr),
        in_specs=[pl.BlockSpec((SSD_CHUNK, SSD_INNER), lambda b, r: (b * nr + r, AB_Z // SSD_INNER)),
                  pl.BlockSpec((SSD_CHUNK, SSD_CONV_DIM), lambda b, r: (b * nr + r, AB_XBC // SSD_CONV_DIM)),
                  pl.BlockSpec((SSD_CHUNK, LANES), lambda b, r: (b * nr + r, AB_SMALL // LANES)),
                  pl.BlockSpec((SSD_CONV, SSD_CONV_DIM), fixed),
                  pl.BlockSpec((1, SSD_CONV_DIM), fixed),
                  pl.BlockSpec((1, LANES), fixed),
                  pl.BlockSpec((1, LANES), fixed),
                  pl.BlockSpec((1, SSD_INNER), fixed),
                  pl.BlockSpec((1, SSD_INNER), fixed),
                  pl.BlockSpec((LANES, SSD_INNER), fixed)],
        out_specs=(pl.BlockSpec((SSD_CHUNK, SSD_INNER), lambda b, r: (b * nr + r, 0)),
                   pl.BlockSpec((1, SSD_STATE, SSD_INNER), lambda b, r: (b, 0, 0)),
                   pl.BlockSpec((1, SSD_CONV - 1, SSD_CONV_DIM), lambda b, r: (b, 0, 0))),
        scratch_shapes=[pltpu.VMEM((SSD_STATE, SSD_INNER), F32),
                        pltpu.VMEM((8, SSD_CONV_DIM), F32)],
        compiler_params=_params("parallel", "arbitrary"),
        name="ssd_chunk",
    )(proj, proj, proj, conv_w, conv_b, dtb_p, alog_p, dskip_x, norm_w, expand)


def _ab_prep_kernel(q_ref, sm_ref, xbc_ref, cs_ref, w2_ref, b2_ref, cw_ref, cb_ref, dtb_ref,
                    alog_ref, qs_ref, dec_ref, xc_ref, dt_ref, da_ref, cs_out_ref):
    sm = sm_ref[...]
    gk = _log_sigmoid(_ThreePass.nn(sm, w2_ref[...]) + b2_ref[...]) / GLA_NORMALIZER
    qs_ref[...] = q_ref[...] * (GLA_DK ** -0.5)
    dec_ref[...] = jnp.exp(gk)
    cw = cw_ref[...]
    x_raw = xbc_ref[...]
    acc = cb_ref[...] + cw[SSD_CONV - 1:SSD_CONV] * x_raw
    for j in range(SSD_CONV - 1):
        acc = acc + cw[j:j + 1] * cs_ref[j]
    xc_ref[...] = _silu(acc)
    for j in range(SSD_CONV - 2):
        cs_out_ref[j] = cs_ref[j + 1]
    cs_out_ref[SSD_CONV - 2] = x_raw
    dt = _softplus(sm + dtb_ref[...])
    dt_ref[...] = dt
    da_ref[...] = jnp.exp(dt * -jnp.exp(alog_ref[...]))


def _ab_prep(proj, conv_state, w2_wide, b2_wide, conv_w, conv_b, dtb_p, alog_p):
    bsz = proj.shape[0]
    fixed = lambda i: (0, 0)
    sds = jax.ShapeDtypeStruct
    return pl.pallas_call(
        _ab_prep_kernel,
        out_shape=(sds((bsz, GLA_KEY), F32), sds((bsz, GLA_KEY), F32),
                   sds((bsz, SSD_CONV_DIM), F32), sds((bsz, LANES), F32), sds((bsz, LANES), F32),
                   sds((SSD_CONV - 1, bsz, SSD_CONV_DIM), F32)),
        grid=(1,),
        in_specs=[pl.BlockSpec((bsz, GLA_KEY), lambda i: (0, AB_Q // GLA_KEY)),
                  pl.BlockSpec((bsz, LANES), lambda i: (0, AB_SMALL // LANES)),
                  pl.BlockSpec((bsz, SSD_CONV_DIM), lambda i: (0, AB_XBC // SSD_CONV_DIM)),
                  pl.BlockSpec((SSD_CONV - 1, bsz, SSD_CONV_DIM), lambda i: (0, 0, 0)),
                  pl.BlockSpec((LANES, GLA_KEY), fixed),
                  pl.BlockSpec((1, GLA_KEY), fixed),
                  pl.BlockSpec((SSD_CONV, SSD_CONV_DIM), fixed),
                  pl.BlockSpec((1, SSD_CONV_DIM), fixed),
                  pl.BlockSpec((1, LANES), fixed),
                  pl.BlockSpec((1, LANES), fixed)],
        out_specs=(pl.BlockSpec((bsz, GLA_KEY), fixed), pl.BlockSpec((bsz, GLA_KEY), fixed),
                   pl.BlockSpec((bsz, SSD_CONV_DIM), fixed), pl.BlockSpec((bsz, LANES), fixed),
                   pl.BlockSpec((bsz, LANES), fixed),
                   pl.BlockSpec((SSD_CONV - 1, bsz, SSD_CONV_DIM), lambda i: (0, 0, 0))),
        compiler_params=_params("arbitrary"),
        name="ab_prep",
    )(proj, proj, proj, conv_state, w2_wide, b2_wide, conv_w, conv_b, dtb_p, alog_p)


def _hgrn_prep_kernel(layer, q_ref, f_ref, lb_ref, qs_ref, k_ref, dec_ref):
    lb = _hgrn_lower_bound(lb_ref[...], layer)
    forget = lb + (1.0 - lb) * _sigmoid(f_ref[...])
    qs_ref[...] = _silu(q_ref[...])
    k_ref[...] = 1.0 - forget
    dec_ref[...] = jnp.exp(jnp.log(forget))


def _hgrn_prep(proj, lower_bounds, layer):
    bsz = proj.shape[0]
    blk = lambda j: pl.BlockSpec((bsz, HG_F), lambda i: (0, j))
    return pl.pallas_call(
        functools.partial(_hgrn_prep_kernel, layer),
        out_shape=tuple(jax.ShapeDtypeStruct((bsz, HG_F), F32) for _ in range(3)),
        grid=(1,),
        in_specs=[blk(0), blk(1), pl.BlockSpec((DEPTH, HG_F), lambda i: (0, 0))],
        out_specs=tuple(blk(0) for _ in range(3)),
        compiler_params=_params("arbitrary"),
        name="hgrn_prep",
    )(proj, proj, lower_bounds)


def _vec_step_kernel(s_ref, q_ref, k_ref, d_ref, v_ref, go_ref, nw_ref, so_ref, o_ref):
    qt = q_ref[0, 0]
    kt = k_ref[0, 0]
    dt = d_ref[0, 0]
    v = v_ref[...]
    rows = []
    for b in range(STEP_B):
        sn = s_ref[b, 0] * dt[:, b:b + 1] + kt[:, b:b + 1] * v[b:b + 1, :]
        so_ref[b, 0] = sn
        rows.append(jnp.sum(qt[:, b:b + 1] * sn, axis=0, keepdims=True))
    o = jnp.concatenate(rows, axis=0)
    o_ref[...] = _rms(o, nw_ref[...]) * _silu(go_ref[...])


def _vec_step(state, q_cols, k_cols, d_cols, vsrc, v_off, gsrc, g_off, norm_w):
    bsz, nh, kdim, vdim = state.shape
    col = lambda j, h: (h, j, 0, 0)
    return pl.pallas_call(
        _vec_step_kernel,
        out_shape=(jax.ShapeDtypeStruct(state.shape, F32),
                   jax.ShapeDtypeStruct((bsz, nh * vdim), F32)),
        grid=(bsz // STEP_B, nh),
        in_specs=[pl.BlockSpec((STEP_B, 1, kdim, vdim), lambda j, h: (j, h, 0, 0)),
                  pl.BlockSpec((1, 1, kdim, STEP_B), col),
                  pl.BlockSpec((1, 1, kdim, STEP_B), col),
                  pl.BlockSpec((1, 1, kdim, STEP_B), col),
                  pl.BlockSpec((STEP_B, vdim), lambda j, h: (j, v_off + h)),
                  pl.BlockSpec((STEP_B, vdim), lambda j, h: (j, g_off + h)),
                  pl.BlockSpec((1, vdim), lambda j, h: (0, 0))],
        out_specs=(pl.BlockSpec((STEP_B, 1, kdim, vdim), lambda j, h: (j, h, 0, 0)),
                   pl.BlockSpec((STEP_B, vdim), lambda j, h: (j, h))),
        compiler_params=_params("parallel", "parallel"),
        name="vec_step",
    )(state, q_cols, k_cols, d_cols, vsrc, gsrc, norm_w)


def _ssd_step_kernel(s_ref, b_ref, c_ref, x_ref, dt_ref, da_ref, dsk_ref, so_ref, y_ref):
    bt = b_ref[0, 0]
    ct = c_ref[0, 0]
    x = x_ref[...]
    dt = dt_ref[0]
    da = da_ref[0]
    hpg = SSD_HEADS // SSD_GROUPS
    rows = []
    for b in range(STEP_B):
        pieces = []
        for hh in range(hpg):
            xh = x[b:b + 1, hh * SSD_HEADDIM:(hh + 1) * SSD_HEADDIM]
            sn = s_ref[b, hh] * da[b:b + 1, hh:hh + 1] + (bt[:, b:b + 1] * dt[b:b + 1, hh:hh + 1]) * xh
            so_ref[b, hh] = sn
            pieces.append(jnp.sum(ct[:, b:b + 1] * sn, axis=0, keepdims=True))
        rows.append(jnp.concatenate(pieces, axis=1))
    y_ref[...] = jnp.concatenate(rows, axis=0) + dsk_ref[...] * x


def _ssd_step(state, b_cols, c_cols, xc, dt_g, da_g, dskip_x):
    bsz = state.shape[0]
    hpg = SSD_HEADS // SSD_GROUPS
    col = lambda j, g: (g, j, 0, 0)
    return pl.pallas_call(
        _ssd_step_kernel,
        out_shape=(jax.ShapeDtypeStruct(state.shape, F32),
                   jax.ShapeDtypeStruct((bsz, SSD_INNER), F32)),
        grid=(bsz // STEP_B, SSD_GROUPS),
        in_specs=[pl.BlockSpec((STEP_B, hpg, SSD_STATE, SSD_HEADDIM), lambda j, g: (j, g, 0, 0)),
                  pl.BlockSpec((1, 1, SSD_STATE, STEP_B), col),
                  pl.BlockSpec((1, 1, SSD_STATE, STEP_B), col),
                  pl.BlockSpec((STEP_B, SSD_GROUP_W), lambda j, g: (j, g)),
                  pl.BlockSpec((1, STEP_B, LANES), lambda j, g: (g, j, 0)),
                  pl.BlockSpec((1, STEP_B, LANES), lambda j, g: (g, j, 0)),
                  pl.BlockSpec((1, SSD_GROUP_W), lambda j, g: (0, g))],
        out_specs=(pl.BlockSpec((STEP_B, hpg, SSD_STATE, SSD_HEADDIM), lambda j, g: (j, g, 0, 0)),
                   pl.BlockSpec((STEP_B, SSD_GROUP_W), lambda j, g: (j, g))),
        compiler_params=_params("parallel", "parallel"),
        name="ssd_step",
    )(state, b_cols, c_cols, xc, dt_g, da_g, dskip_x)


def _ssd_post_kernel(y_ref, z_ref, nw_ref, o_ref):
    o_ref[...] = _ssd_gate_norm(y_ref[...], z_ref[...], nw_ref[...])


def _ssd_post(y, proj, norm_w):
    bsz = y.shape[0]
    return pl.pallas_call(
        _ssd_post_kernel,
        out_shape=jax.ShapeDtypeStruct((bsz, SSD_INNER), F32),
        grid=(1,),
        in_specs=[pl.BlockSpec((bsz, SSD_INNER), lambda i: (0, 0)),
                  pl.BlockSpec((bsz, SSD_INNER), lambda i: (0, AB_Z // SSD_INNER)),
                  pl.BlockSpec((1, SSD_INNER), lambda i: (0, 0))],
        out_specs=pl.BlockSpec((bsz, SSD_INNER), lambda i: (0, 0)),
        compiler_params=_params("arbitrary"),
        name="ssd_post",
    )(y, proj, norm_w)


def _to_cols(a, nh):
    bsz = a.shape[0]
    return a.reshape(bsz // STEP_B, STEP_B, nh, -1).transpose(2, 0, 3, 1)


def _prep_weights(w_in_ab, w_gk2, b_gk2, gla_norm_w, conv_w, conv_b, dt_bias, a_log, d_skip,
                  ssd_norm_w, w_out_ab, w_in_c, hg_norm_w, w_out_c, router_w, router_bias,
                  w_gate, w_up, w_down, ln1_w, ln1_b, ln2_w, ln2_b):
    offs = np.cumsum([0, GLA_KEY, GLA_KEY, GLA_VAL, GLA_VAL, GLA_RANK, SSD_INNER, SSD_CONV_DIM,
                      SSD_HEADS])
    sec = lambda w, i: w[:, offs[i]:offs[i + 1]]
    w = w_in_ab[0]
    pad = jnp.zeros((D_MODEL, LANES - SSD_HEADS - GLA_RANK), w.dtype)
    w_ab = jnp.concatenate([sec(w, 5), sec(w, 2), sec(w, 3), sec(w, 6), sec(w, 0), sec(w, 1),
                            sec(w, 7), sec(w, 4), pad], axis=1)
    hi_lo = lambda m: (m.astype(BF16), (m - m.astype(BF16).astype(F32)).astype(BF16))
    w2_wide = jnp.zeros((LANES, GLA_KEY), F32).at[SSD_HEADS:SSD_HEADS + GLA_RANK].set(w_gk2[0])
    lane_pad = lambda v: jnp.zeros((1, LANES), F32).at[0, :SSD_HEADS].set(v)
    expand = np.zeros((LANES, SSD_INNER), np.float32)
    for h in range(SSD_HEADS):
        expand[h, h * SSD_HEADDIM:(h + 1) * SSD_HEADDIM] = 1.0
    return dict(
        w_ab=hi_lo(w_ab),
        w2_wide=w2_wide,
        w2_heads=w2_wide.reshape(LANES, GLA_HEADS, GLA_DK).transpose(1, 0, 2),
        b2_wide=b_gk2[0].reshape(1, GLA_KEY),
        b2_heads=b_gk2[0].reshape(GLA_HEADS, 1, GLA_DK),
        gla_norm_w=gla_norm_w[0].reshape(1, GLA_DV),
        conv_w=conv_w[0], conv_b=conv_b[0].reshape(1, SSD_CONV_DIM),
        dtb_p=lane_pad(dt_bias[0]), alog_p=lane_pad(a_log[0]),
        dskip_x=jnp.repeat(d_skip[0], SSD_HEADDIM).reshape(1, SSD_INNER),
        ssd_norm_w=ssd_norm_w[0].reshape(1, SSD_INNER),
        expand=jnp.asarray(expand, BF16),
        prefix_sel=jnp.asarray(_prefix_selector(), BF16),
        w_out_gla=hi_lo(w_out_ab[0, :GLA_VAL]),
        w_out_ssd=hi_lo(w_out_ab[0, GLA_VAL:]),
        w_c=w_in_c[0].astype(BF16),
        hg_norm_w=hg_norm_w[0].reshape(1, HG_DI),
        w_out_c=w_out_c[0].astype(BF16),
        rwt=router_w.T,
        rbias=router_bias.reshape(N_EXPERTS, 1),
        w_gate=w_gate.reshape(DEPTH * N_EXPERTS, D_MODEL, D_FF_EXPERT),
        w_up=w_up.reshape(DEPTH * N_EXPERTS, D_MODEL, D_FF_EXPERT),
        w_down=w_down.reshape(DEPTH * N_EXPERTS, D_FF_EXPERT, D_MODEL),
        ln1_w=ln1_w.reshape(DEPTH, 1, D_MODEL), ln1_b=ln1_b.reshape(DEPTH, 1, D_MODEL),
        ln2_w=ln2_w.reshape(DEPTH, 1, D_MODEL), ln2_b=ln2_b.reshape(DEPTH, 1, D_MODEL),
    )


def _ffn(x, p, layer, tm, tm_moe):
    gates = _router(x, p['rwt'], p['rbias'], tm)
    return _moe_ln(x, gates, p['w_gate'], p['w_up'], p['w_down'], layer,
                   p['ln2_w'][layer], p['ln2_b'][layer], tm_moe)


def _ssd_state_from_wide(s_wide):
    bsz = s_wide.shape[0]
    return s_wide.reshape(bsz, SSD_STATE, SSD_HEADS, SSD_HEADDIM).transpose(0, 2, 1, 3)


def _trunk_prompt(x3, p, lower_bounds, tm, tn_ab, tn_c):
    bsz, seq, _ = x3.shape
    x = x3.reshape(bsz * seq, D_MODEL)
    tm_big = 2 * tm
    proj = _proj(x, p['w_ab'], tm_big, tn_ab)
    o_gla, s_gla = _gla_prompt(proj, p['w2_heads'], p['b2_heads'], p['gla_norm_w'],
                               p['prefix_sel'], bsz, seq)
    yz, s_ssd, s_conv = _ssd_prompt(proj, p['conv_w'], p['conv_b'], p['dtb_p'], p['alog_p'],
                                    p['dskip_x'], p['ssd_norm_w'], p['expand'], bsz, seq)
    x = _outproj_ln([o_gla, yz], [p['w_out_gla'], p['w_out_ssd']], x, p['ln1_w'][0], p['ln1_b'][0], tm)
    x = _ffn_sorted(x, p, 0, tm, tm_big)
    proj_c = _proj(x, p['w_c'], tm_big, tn_c)
    o_hg, s_hg = _hgrn_prompt(proj_c, lower_bounds, p['hg_norm_w'], p['prefix_sel'], 1, bsz, seq)
    x = _outproj_ln([o_hg], [p['w_out_c']], x, p['ln1_w'][1], p['ln1_b'][1], tm)
    x = _ffn_sorted(x, p, 1, tm, tm_big)
    return (x.reshape(bsz, seq, D_MODEL), s_gla[None], _ssd_state_from_wide(s_ssd)[None],
            s_conv[None], s_hg[None])


def _trunk_sample(x3, st_gla, st_ssd, st_conv, st_hg, p, lower_bounds, tn_ab, tn_c):
    bsz = x3.shape[0]
    tm = bsz
    x = x3.reshape(bsz, D_MODEL)
    proj = _proj(x, p['w_ab'], tm, tn_ab)
    qs, dec, xc, dt, da, conv_new = _ab_prep(proj, st_conv[0].transpose(1, 0, 2), p['w2_wide'],
                                             p['b2_wide'], p['conv_w'], p['conv_b'], p['dtb_p'],
                                             p['alog_p'])
    conv_new = conv_new.transpose(1, 0, 2)
    k_gla = proj[:, AB_K:AB_K + GLA_KEY]
    s_gla, o_gla = _vec_step(st_gla[0], _to_cols(qs, GLA_HEADS), _to_cols(k_gla, GLA_HEADS),
                             _to_cols(dec, GLA_HEADS), proj, AB_V // GLA_DV, proj,
                             AB_GOUT // GLA_DV, p['gla_norm_w'])
    hpg = SSD_HEADS // SSD_GROUPS
    per_group = lambda a: jnp.pad(a[:, :SSD_HEADS].reshape(bsz, SSD_GROUPS, hpg).transpose(1, 0, 2),
                                  ((0, 0), (0, 0), (0, LANES - hpg)))
    s_ssd, y = _ssd_step(st_ssd[0],
                         _to_cols(xc[:, SSD_INNER:SSD_INNER + SSD_BC], SSD_GROUPS),
                         _to_cols(xc[:, SSD_INNER + SSD_BC:], SSD_GROUPS),
                         xc, per_group(dt), per_group(da), p['dskip_x'])
    yz = _ssd_post(y, proj, p['ssd_norm_w'])
    x = _outproj_ln([o_gla, yz], [p['w_out_gla'], p['w_out_ssd']], x, p['ln1_w'][0], p['ln1_b'][0], tm)
    x = _ffn(x, p, 0, tm, tm)
    proj_c = _proj(x, p['w_c'], tm, tn_c)
    qh, kh, dh = _hgrn_prep(proj_c, lower_bounds, 1)
    s_hg, o_hg = _vec_step(st_hg[0], _to_cols(qh, HG_HEADS), _to_cols(kh, HG_HEADS),
                           _to_cols(dh, HG_HEADS), proj_c, 2 * HG_HEADS, proj_c, 3 * HG_HEADS,
                           p['hg_norm_w'])
    x = _outproj_ln([o_hg], [p['w_out_c']], x, p['ln1_w'][1], p['ln1_b'][1], tm)
    x = _ffn(x, p, 1, tm, tm)
    return x.reshape(bsz, 1, D_MODEL), s_gla[None], s_ssd[None], conv_new[None], s_hg[None]


def kernel(x_prompt, x_sample, state_gla, state_ssd, state_conv, state_hgrn, w_in_ab, w_gk2, b_gk2, gla_norm_w, conv_w, conv_b, dt_bias, a_log, d_skip, ssd_norm_w, w_out_ab, w_in_c, lower_bounds, hg_norm_w, w_out_c, router_w, router_bias, w_gate, w_up, w_down, ln1_w, ln1_b, ln2_w, ln2_b):
    p = _prep_weights(w_in_ab, w_gk2, b_gk2, gla_norm_w, conv_w, conv_b, dt_bias, a_log, d_skip,
                      ssd_norm_w, w_out_ab, w_in_c, hg_norm_w, w_out_c, router_w, router_bias,
                      w_gate, w_up, w_down, ln1_w, ln1_b, ln2_w, ln2_b)
    y_p, gla_p, ssd_p, conv_p, hg_p = _trunk_prompt(x_prompt, p, lower_bounds, 512, 1152, 1024)
    y_s, gla_s, ssd_s, conv_s, hg_s = _trunk_sample(x_sample, state_gla, state_ssd, state_conv,
                                                    state_hgrn, p, lower_bounds, 1152, 1024)
    return (y_p, y_s, gla_p, ssd_p, conv_p, hg_p, gla_s, ssd_s, conv_s, hg_s)
```

```python
import functools

import numpy as np
import jax
import jax.numpy as jnp
from jax import lax
from jax.experimental import pallas as pl
from jax.experimental.pallas import tpu as pltpu

F32 = jnp.float32
BF16 = jnp.bfloat16

D_MODEL = 1024
DEPTH = 2
GLA_HEADS = 4
GLA_DK = 128
GLA_DV = 256
GLA_KEY = GLA_HEADS * GLA_DK
GLA_VAL = GLA_HEADS * GLA_DV
GLA_RANK = 16
GLA_NORMALIZER = 16.0
SSD_INNER = 1024
SSD_HEADDIM = 64
SSD_HEADS = 16
SSD_STATE = 128
SSD_GROUPS = 2
SSD_CONV = 4
SSD_GROUP_W = SSD_INNER // SSD_GROUPS
SSD_BC = SSD_GROUPS * SSD_STATE
SSD_CONV_DIM = SSD_INNER + 2 * SSD_BC
HG_EXPAND = 128
HG_HEADS = 8
HG_F = HG_HEADS * HG_EXPAND
HG_I = D_MODEL
HG_DI = HG_I // HG_HEADS
N_EXPERTS = 16
N_GROUPS = 4
EXPERTS_PER_GROUP = 4
D_FF_EXPERT = 512
ALPHA = (2 * DEPTH) ** 0.25
EPS = 1e-5

LANES = 128
VMEM_LIMIT = 48 * 1024 * 1024

AB_Z = 0
AB_V = 1024
AB_GOUT = 2048
AB_XBC = 3072
AB_Q = 4608
AB_K = 5120
AB_SMALL = 5632
AB_COLS = 5760
C_COLS = 4096

VEC_CHUNK = 64
VEC_SUB = 16
VEC_TILE = 256
VEC_ROWS = 512
VEC_HPS = 2
SSD_CHUNK = 128
STEP_B = 8


def _params(*sem):
    return pltpu.CompilerParams(dimension_semantics=sem, vmem_limit_bytes=VMEM_LIMIT)


_NN = (((1,), (0,)), ((), ()))
_NT = (((1,), (1,)), ((), ()))
_TN = (((0,), (0,)), ((), ()))


def _dot1(dims, a, b):
    return lax.dot_general(a.astype(BF16), b.astype(BF16), dims, preferred_element_type=F32)


def _split2(a):
    hi = a.astype(BF16)
    return hi, (a - hi.astype(F32)).astype(BF16)


def _dot3(dims, a, b):
    ah, al = _split2(a)
    bh, bl = _split2(b)
    d = lambda x, y: lax.dot_general(x, y, dims, preferred_element_type=F32)
    return (d(al, bh) + d(ah, bl)) + d(ah, bh)


class _OnePass:
    nn = staticmethod(lambda a, b: _dot1(_NN, a, b))
    nt = staticmethod(lambda a, b: _dot1(_NT, a, b))
    tn = staticmethod(lambda a, b: _dot1(_TN, a, b))


class _ThreePass:
    nn = staticmethod(lambda a, b: _dot3(_NN, a, b))
    nt = staticmethod(lambda a, b: _dot3(_NT, a, b))
    tn = staticmethod(lambda a, b: _dot3(_TN, a, b))


def _dot(a, b):
    return _dot1(_NN, a, b)


def _dot_nt(a, b):
    return _dot1(_NT, a, b)


def _dot_tn(a, b):
    return _dot1(_TN, a, b)


def _split3(a):
    hi = a.astype(BF16)
    r1 = a - hi.astype(F32)
    mid = r1.astype(BF16)
    lo = (r1 - mid.astype(F32)).astype(BF16)
    return hi, mid, lo


def _dot_exact_rhs(sel, a):
    hi, mid, lo = _split3(a)
    d = lambda p: jnp.dot(sel, p, preferred_element_type=F32)
    return (d(lo) + d(mid)) + d(hi)


def _dot_exact_lhs(a, sel):
    hi, mid, lo = _split3(a)
    d = lambda p: jnp.dot(p, sel, preferred_element_type=F32)
    return (d(lo) + d(mid)) + d(hi)


def _tril(n):
    r = lax.broadcasted_iota(jnp.int32, (n, n), 0)
    c = lax.broadcasted_iota(jnp.int32, (n, n), 1)
    return r >= c


def _sigmoid(x):
    return 1.0 / (1.0 + jnp.exp(-x))


def _silu(x):
    return x * _sigmoid(x)


def _softplus(x):
    return jnp.maximum(x, 0.0) + jnp.log(1.0 + jnp.exp(-jnp.abs(x)))


def _log_sigmoid(x):
    return -_softplus(-x)


def _rms(x, w):
    return x * lax.rsqrt(jnp.mean(x * x, axis=-1, keepdims=True) + EPS) * w


def _layer_norm(x, w, b):
    mu = jnp.mean(x, axis=-1, keepdims=True)
    xc = x - mu
    var = jnp.mean(xc * xc, axis=-1, keepdims=True)
    return xc * lax.rsqrt(var + EPS) * w + b


def _proj_kernel(x_ref, w_ref, o_ref):
    o_ref[...] = jnp.dot(x_ref[...].astype(BF16), w_ref[...], preferred_element_type=F32)


def _proj3_kernel(x_ref, wh_ref, wl_ref, o_ref, xh_ref, xl_ref):
    @pl.when(pl.program_id(1) == 0)
    def _():
        hi, lo = _split2(x_ref[...])
        xh_ref[...] = hi
        xl_ref[...] = lo

    d = lambda a, b: jnp.dot(a, b, preferred_element_type=F32)
    xh = xh_ref[...]
    wh = wh_ref[...]
    o_ref[...] = (d(xl_ref[...], wh) + d(xh, wl_ref[...])) + d(xh, wh)


def _proj(x, w, tm, tn):
    t, k = x.shape
    three = isinstance(w, tuple)
    ws = w if three else (w,)
    n = ws[0].shape[1]
    return pl.pallas_call(
        _proj3_kernel if three else _proj_kernel,
        out_shape=jax.ShapeDtypeStruct((t, n), F32),
        grid=(t // tm, n // tn),
        in_specs=[pl.BlockSpec((tm, k), lambda i, j: (i, 0))]
                 + [pl.BlockSpec((k, tn), lambda i, j: (0, j)) for _ in ws],
        out_specs=pl.BlockSpec((tm, tn), lambda i, j: (i, j)),
        scratch_shapes=[pltpu.VMEM((tm, k), BF16), pltpu.VMEM((tm, k), BF16)] if three else [],
        compiler_params=_params("parallel", "arbitrary"),
        name="in_proj",
    )(x, *ws)


def _outproj_ln_kernel(n_in, three, slabs, *refs):
    a_refs = refs[:n_in]
    nw = 2 if three else 1
    w_refs = refs[n_in:n_in + nw * n_in]
    x_ref, lw_ref, lb_ref, o_ref = refs[n_in + nw * n_in:n_in + nw * n_in + 4]
    d = lambda a, b: jnp.dot(a, b, preferred_element_type=F32)
    mix = None
    for i, a_ref in enumerate(a_refs):
        if three:
            ah, al = _split2(a_ref[...])
            wh = w_refs[2 * i][...]
            part = (d(al, wh) + d(ah, w_refs[2 * i + 1][...])) + d(ah, wh)
        else:
            part = d(a_ref[...].astype(BF16), w_refs[i][...])
        mix = part if mix is None else mix + part
    y = _layer_norm(ALPHA * x_ref[...] + mix, lw_ref[...], lb_ref[...])
    o_ref[...] = y
    if slabs:
        o3_ref = refs[-1]
        for s in range(D_MODEL // LANES):
            o3_ref[:, s, :] = y[:, s * LANES:(s + 1) * LANES]


def _outproj_ln(acts, ws, x, ln_w, ln_b, tm, slabs=False):
    t = x.shape[0]
    n_in = len(acts)
    three = isinstance(ws[0], tuple)
    flat_ws = [w for pair in ws for w in pair] if three else list(ws)
    row = lambda i: (i, 0)
    fixed = lambda i: (0, 0)
    in_specs = ([pl.BlockSpec((tm, a.shape[1]), row) for a in acts]
                + [pl.BlockSpec(w.shape, fixed) for w in flat_ws]
                + [pl.BlockSpec((tm, D_MODEL), row),
                   pl.BlockSpec((1, D_MODEL), fixed), pl.BlockSpec((1, D_MODEL), fixed)])
    out_shape = jax.ShapeDtypeStruct((t, D_MODEL), F32)
    out_specs = pl.BlockSpec((tm, D_MODEL), row)
    if slabs:
        n_slab = D_MODEL // LANES
        out_shape = (out_shape, jax.ShapeDtypeStruct((t, n_slab, LANES), F32))
        out_specs = (out_specs, pl.BlockSpec((tm, n_slab, LANES), lambda i: (i, 0, 0)))
    return pl.pallas_call(
        functools.partial(_outproj_ln_kernel, n_in, three, slabs),
        out_shape=out_shape,
        grid=(t // tm,),
        in_specs=in_specs,
        out_specs=out_specs,
        compiler_params=_params("parallel"),
        name="out_proj_ln",
    )(*acts, *flat_ws, x, ln_w, ln_b)


def _router_scores(x, rwt, bias):
    w3 = _split3(rwt)
    x3 = _split3(x)
    nt = lambda a, b: lax.dot_general(a, b, _NT, preferred_element_type=F32)
    logits = (((nt(w3[0], x3[2]) + nt(w3[2], x3[0])) + nt(w3[1], x3[1]))
              + (nt(w3[0], x3[1]) + nt(w3[1], x3[0]))) + nt(w3[0], x3[0])
    scores = _sigmoid(logits)
    return scores, scores + bias


def _best_group(sel):
    tm = sel.shape[1]
    s = [sel[e:e + 1, :] for e in range(N_EXPERTS)]
    grp = []
    for g in range(N_GROUPS):
        m = s[g * EXPERTS_PER_GROUP:(g + 1) * EXPERTS_PER_GROUP]
        best = None
        for i in range(EXPERTS_PER_GROUP):
            for j in range(i + 1, EXPERTS_PER_GROUP):
                p = m[i] + m[j]
                best = p if best is None else jnp.maximum(best, p)
        grp.append(best)
    best_g = jnp.zeros((1, tm), jnp.int32)
    best_v = grp[0]
    for g in range(1, N_GROUPS):
        upd = grp[g] > best_v
        best_g = jnp.where(upd, g, best_g)
        best_v = jnp.where(upd, grp[g], best_v)
    return best_g


def _top2(vals, weights):
    tm = vals[0].shape[1]
    neg = jnp.full((1, tm), -jnp.inf, F32)

    def first_argmax(rows):
        idx = jnp.zeros((1, tm), jnp.int32)
        top = rows[0]
        for e in range(1, len(rows)):
            upd = rows[e] > top
            idx = jnp.where(upd, e, idx)
            top = jnp.where(upd, rows[e], top)
        return idx

    idx1 = first_argmax(vals)
    idx2 = first_argmax([jnp.where(idx1 == e, neg, v) for e, v in enumerate(vals)])
    zero = jnp.zeros((1, tm), F32)
    w1 = zero
    w2 = zero
    for e, w in enumerate(weights):
        w1 = w1 + jnp.where(idx1 == e, w, zero)
        w2 = w2 + jnp.where(idx2 == e, w, zero)
    tot = w1 + w2
    g1 = w1 / tot
    g2 = w2 / tot
    return [jnp.where(idx1 == e, g1, zero) + jnp.where(idx2 == e, g2, zero)
            for e in range(len(vals))]


def _pad_rows(rows, tm):
    return jnp.concatenate(rows + [jnp.zeros((LANES - len(rows), tm), F32)], axis=0)


def _route_in_group(x, rwt, bias, group):
    tm = x.shape[0]
    scores, sel = _router_scores(x, rwt, bias)
    zero = jnp.zeros((1, tm), F32)
    vals, weights = [], []
    for m in range(EXPERTS_PER_GROUP):
        v = zero
        w = zero
        for g in range(N_GROUPS):
            e = g * EXPERTS_PER_GROUP + m
            v = jnp.where(group == g, sel[e:e + 1, :], v)
            w = jnp.where(group == g, scores[e:e + 1, :], w)
        vals.append(v)
        weights.append(w)
    return _pad_rows(_top2(vals, weights), tm)


def _route(x, rwt, bias):
    tm = x.shape[0]
    scores, sel = _router_scores(x, rwt, bias)
    s = [sel[e:e + 1, :] for e in range(N_EXPERTS)]
    sc = [scores[e:e + 1, :] for e in range(N_EXPERTS)]
    best_g = _best_group(sel)
    neg = jnp.full((1, tm), -jnp.inf, F32)
    ms = [jnp.where(best_g == e // EXPERTS_PER_GROUP, s[e], neg) for e in range(N_EXPERTS)]
    return _pad_rows(_top2(ms, sc), tm), best_g


def _router_kernel(with_gates, x_ref, rwt_ref, bias_ref, g_ref):
    tm = x_ref.shape[0]
    if with_gates:
        gates_t, best_g = _route(x_ref[...], rwt_ref[...], bias_ref[...])
    else:
        best_g = _best_group(_router_scores(x_ref[...], rwt_ref[...], bias_ref[...])[1])
        gates_t = jnp.zeros((LANES, tm), F32)
    row = lax.broadcasted_iota(jnp.int32, (LANES, tm), 0)
    gates_t = jnp.where(row == N_EXPERTS, best_g.astype(F32), gates_t)
    g_ref[...] = gates_t.T


def _router(x, rwt, bias, tm, with_gates=True):
    t = x.shape[0]
    return pl.pallas_call(
        functools.partial(_router_kernel, with_gates),
        out_shape=jax.ShapeDtypeStruct((t, LANES), F32),
        grid=(t // tm,),
        in_specs=[pl.BlockSpec((tm, D_MODEL), lambda i: (i, 0)),
                  pl.BlockSpec((N_EXPERTS, D_MODEL), lambda i: (0, 0)),
                  pl.BlockSpec((N_EXPERTS, 1), lambda i: (0, 0))],
        out_specs=pl.BlockSpec((tm, LANES), lambda i: (i, 0)),
        compiler_params=_params("parallel"),
        name="router",
    )(x, rwt, bias)


def _moe_kernel(x_ref, g_ref, wg_ref, wu_ref, wd_ref, lw_ref, lb_ref, o_ref, acc_ref, xb_ref):
    e = pl.program_id(1)

    @pl.when(e == 0)
    def _():
        xb_ref[...] = x_ref[...].astype(BF16)
        acc_ref[...] = jnp.zeros_like(acc_ref)

    xb = xb_ref[...]
    hg = jnp.dot(xb, wg_ref[0].astype(BF16), preferred_element_type=F32)
    hu = jnp.dot(xb, wu_ref[0].astype(BF16), preferred_element_type=F32)
    he = _silu(hg) * hu
    gates = g_ref[...]
    lane = lax.broadcasted_iota(jnp.int32, gates.shape, 1)
    ge = jnp.sum(jnp.where(lane == e, gates, 0.0), axis=1, keepdims=True)
    acc_ref[...] += ge * jnp.dot(he.astype(BF16), wd_ref[0].astype(BF16),
                                  preferred_element_type=F32)

    @pl.when(e == N_EXPERTS - 1)
    def _():
        o_ref[...] = _layer_norm(ALPHA * x_ref[...] + acc_ref[...], lw_ref[...], lb_ref[...])


def _moe_ln(x, gates, wg, wu, wd, layer, ln_w, ln_b, tm):
    t = x.shape[0]
    return pl.pallas_call(
        _moe_kernel,
        out_shape=jax.ShapeDtypeStruct((t, D_MODEL), F32),
        grid=(t // tm, N_EXPERTS),
        in_specs=[pl.BlockSpec((tm, D_MODEL), lambda i, e: (i, 0)),
                  pl.BlockSpec((tm, LANES), lambda i, e: (i, 0)),
                  pl.BlockSpec((1, D_MODEL, D_FF_EXPERT), lambda i, e: (layer * N_EXPERTS + e, 0, 0)),
                  pl.BlockSpec((1, D_MODEL, D_FF_EXPERT), lambda i, e: (layer * N_EXPERTS + e, 0, 0)),
                  pl.BlockSpec((1, D_FF_EXPERT, D_MODEL), lambda i, e: (layer * N_EXPERTS + e, 0, 0)),
                  pl.BlockSpec((1, D_MODEL), lambda i, e: (0, 0)),
                  pl.BlockSpec((1, D_MODEL), lambda i, e: (0, 0))],
        out_specs=pl.BlockSpec((tm, D_MODEL), lambda i, e: (i, 0)),
        scratch_shapes=[pltpu.VMEM((tm, D_MODEL), F32), pltpu.VMEM((tm, D_MODEL), BF16)],
        compiler_params=_params("parallel", "arbitrary"),
        name="moe_ln",
    )(x, gates, wg, wu, wd, ln_w, ln_b)


def _group_rank_kernel(g_ref, rank_ref, tot_ref, carry_ref):
    n = g_ref.shape[0]

    @pl.when(pl.program_id(0) == 0)
    def _():
        carry_ref[...] = jnp.zeros_like(carry_ref)

    grp = g_ref[:, N_EXPERTS:N_EXPERTS + 1].astype(jnp.int32)
    lane = lax.broadcasted_iota(jnp.int32, (n, LANES), 1)
    onehot = jnp.where(lane == grp, 1.0, 0.0)
    rr = lax.broadcasted_iota(jnp.int32, (n, n), 0)
    cc = lax.broadcasted_iota(jnp.int32, (n, n), 1)
    before = jnp.where(rr > cc, 1.0, 0.0).astype(BF16)
    earlier = jnp.dot(before, onehot.astype(BF16), preferred_element_type=F32) + carry_ref[...]
    rank = jnp.sum(onehot * earlier, axis=1, keepdims=True)
    rank_ref[...] = jnp.broadcast_to(rank, (n, LANES))
    carry_ref[...] += jnp.sum(onehot, axis=0, keepdims=True)
    tot_ref[...] = jnp.broadcast_to(carry_ref[...], tot_ref.shape)


def _group_rank(gmat, tr):
    t = gmat.shape[0]
    return pl.pallas_call(
        _group_rank_kernel,
        out_shape=(jax.ShapeDtypeStruct((t, LANES), F32), jax.ShapeDtypeStruct((8, LANES), F32)),
        grid=(t // tr,),
        in_specs=[pl.BlockSpec((tr, LANES), lambda i: (i, 0))],
        out_specs=(pl.BlockSpec((tr, LANES), lambda i: (i, 0)),
                   pl.BlockSpec((8, LANES), lambda i: (0, 0))),
        scratch_shapes=[pltpu.VMEM((1, LANES), F32)],
        compiler_params=_params("arbitrary"),
        name="group_rank",
    )(gmat)


def _row_copy(src, dst, sem):
    return pltpu.make_async_copy(src, dst, sem)


def _scatter_rows_kernel(n, dest_ref, x_hbm, init_ref, o_hbm, sems):
    del init_ref
    i = pl.program_id(0)
    base = i * n
    slot = i % 2

    def start(r, carry):
        _row_copy(x_hbm.at[base + r], o_hbm.at[dest_ref[base + r]], sems.at[slot]).start()
        return carry

    lax.fori_loop(0, n, start, 0)
    all_rows = lambda s: _row_copy(x_hbm.at[pl.ds(0, n)], o_hbm.at[pl.ds(0, n)], sems.at[s])

    @pl.when(i > 0)
    def _():
        all_rows(1 - slot).wait()

    @pl.when(i == pl.num_programs(0) - 1)
    def _():
        all_rows(slot).wait()


def _scatter_rows(x3, dest, n_out, tr):
    t, slabs, _ = x3.shape
    init = jnp.zeros((n_out, slabs, LANES), F32)
    return pl.pallas_call(
        functools.partial(_scatter_rows_kernel, tr),
        out_shape=jax.ShapeDtypeStruct((n_out, slabs, LANES), F32),
        grid_spec=pltpu.PrefetchScalarGridSpec(
            num_scalar_prefetch=1,
            grid=(t // tr,),
            in_specs=[pl.BlockSpec(memory_space=pl.ANY), pl.BlockSpec(memory_space=pl.ANY)],
            out_specs=pl.BlockSpec(memory_space=pl.ANY),
            scratch_shapes=[pltpu.SemaphoreType.DMA((2,))]),
        input_output_aliases={2: 0},
        compiler_params=_params("arbitrary"),
        name="scatter_rows",
    )(dest, x3, init)


def _gather_rows_kernel(src_ref, y_hbm, o_ref, buf_ref, sems):
    n = o_ref.shape[0]
    i = pl.program_id(0)
    slot = i % 2

    def issue(step, to_slot):
        def start(r, carry):
            _row_copy(y_hbm.at[src_ref[step * n + r]], buf_ref.at[to_slot, r],
                      sems.at[to_slot]).start()
            return carry
        lax.fori_loop(0, n, start, 0)

    @pl.when(i == 0)
    def _():
        issue(0, 0)

    @pl.when(i + 1 < pl.num_programs(0))
    def _():
        issue(i + 1, 1 - slot)

    _row_copy(y_hbm.at[pl.ds(0, n)], buf_ref.at[slot], sems.at[slot]).wait()
    cur = buf_ref.at[slot]
    for s in range(D_MODEL // LANES):
        o_ref[:, s * LANES:(s + 1) * LANES] = cur[:, s, :]


def _gather_rows(y3, src, tr):
    t = src.shape[0]
    slabs = D_MODEL // LANES
    return pl.pallas_call(
        _gather_rows_kernel,
        out_shape=jax.ShapeDtypeStruct((t, D_MODEL), F32),
        grid_spec=pltpu.PrefetchScalarGridSpec(
            num_scalar_prefetch=1,
            grid=(t // tr,),
            in_specs=[pl.BlockSpec(memory_space=pl.ANY)],
            out_specs=pl.BlockSpec((tr, D_MODEL), lambda i, d: (i, 0)),
            scratch_shapes=[pltpu.VMEM((2, tr, slabs, LANES), F32),
                            pltpu.SemaphoreType.DMA((2,))]),
        compiler_params=_params("arbitrary"),
        name="gather_rows",
    )(src, y3)


def _moe_group_kernel(tg_ref, x3_ref, rwt_ref, rb_ref, wg_ref, wu_ref, wd_ref, lw_ref, lb_ref,
                      o3_ref, acc_ref, x_ref, xb_ref, gate_ref):
    i = pl.program_id(0)
    j = pl.program_id(1)
    group = tg_ref[i]
    slabs = D_MODEL // LANES

    @pl.when(group < 0)
    def _():
        o3_ref[...] = jnp.zeros_like(o3_ref)

    @pl.when(group >= 0)
    def _():
        @pl.when(j == 0)
        def _():
            for s in range(slabs):
                x_ref[:, s * LANES:(s + 1) * LANES] = x3_ref[:, s, :]
            x = x_ref[...]
            xb_ref[...] = x.astype(BF16)
            acc_ref[...] = jnp.zeros_like(acc_ref)
            gate_ref[...] = _route_in_group(x, rwt_ref[...], rb_ref[...], group).T

        xb = xb_ref[...]
        hg = jnp.dot(xb, wg_ref[0].astype(BF16), preferred_element_type=F32)
        hu = jnp.dot(xb, wu_ref[0].astype(BF16), preferred_element_type=F32)
        he = _silu(hg) * hu
        gates = gate_ref[...]
        lane = lax.broadcasted_iota(jnp.int32, gates.shape, 1)
        ge = jnp.sum(jnp.where(lane == j, gates, 0.0), axis=1, keepdims=True)
        acc_ref[...] += ge * jnp.dot(he.astype(BF16), wd_ref[0].astype(BF16),
                                     preferred_element_type=F32)

        @pl.when(j == EXPERTS_PER_GROUP - 1)
        def _():
            y = _layer_norm(ALPHA * x_ref[...] + acc_ref[...], lw_ref[...], lb_ref[...])
            for s in range(slabs):
                o3_ref[:, s, :] = y[:, s * LANES:(s + 1) * LANES]


def _moe_group_ln(xs3, tile_group, rwt, rbias, wg, wu, wd, layer, ln_w, ln_b, tm):
    n = xs3.shape[0]
    slabs = D_MODEL // LANES
    expert = lambda i, j, tg: (layer * N_EXPERTS + jnp.maximum(tg[i], 0) * EXPERTS_PER_GROUP + j, 0, 0)
    fixed = lambda i, j, tg: (0, 0)
    return pl.pallas_call(
        _moe_group_kernel,
        out_shape=jax.ShapeDtypeStruct((n, slabs, LANES), F32),
        grid_spec=pltpu.PrefetchScalarGridSpec(
            num_scalar_prefetch=1,
            grid=(n // tm, EXPERTS_PER_GROUP),
            in_specs=[pl.BlockSpec((tm, slabs, LANES), lambda i, j, tg: (i, 0, 0)),
                      pl.BlockSpec((N_EXPERTS, D_MODEL), fixed),
                      pl.BlockSpec((N_EXPERTS, 1), fixed),
                      pl.BlockSpec((1, D_MODEL, D_FF_EXPERT), expert),
                      pl.BlockSpec((1, D_MODEL, D_FF_EXPERT), expert),
                      pl.BlockSpec((1, D_FF_EXPERT, D_MODEL), expert),
                      pl.BlockSpec((1, D_MODEL), fixed),
                      pl.BlockSpec((1, D_MODEL), fixed)],
            out_specs=pl.BlockSpec((tm, slabs, LANES), lambda i, j, tg: (i, 0, 0)),
            scratch_shapes=[pltpu.VMEM((tm, D_MODEL), F32), pltpu.VMEM((tm, D_MODEL), F32),
                            pltpu.VMEM((tm, D_MODEL), BF16), pltpu.VMEM((tm, LANES), F32)]),
        compiler_params=_params("arbitrary", "arbitrary"),
        name="moe_group_ln",
    )(tile_group, xs3, rwt, rbias, wg, wu, wd, ln_w, ln_b)


def _ffn_sorted(x, x3, p, layer, tm, tm_moe):
    t = x.shape[0]
    n_tiles = t // tm_moe + N_GROUPS
    gmat = _router(x, p['rwt'], p['rbias'], tm, with_gates=False)
    rank_mat, totals = _group_rank(gmat, tm)
    group = gmat[:, N_EXPERTS].astype(jnp.int32)
    counts = totals[0, :N_GROUPS].astype(jnp.int32)
    seg_tiles = (counts + tm_moe - 1) // tm_moe
    seg_end = jnp.cumsum(seg_tiles)
    seg_start = seg_end - seg_tiles
    is_group = group[:, None] == jnp.arange(N_GROUPS, dtype=jnp.int32)[None, :]
    dest = (jnp.sum(jnp.where(is_group, seg_start[None, :], 0), axis=1) * tm_moe
            + rank_mat[:, 0].astype(jnp.int32))
    tile_id = jnp.arange(n_tiles, dtype=jnp.int32)
    tile_group = jnp.sum((tile_id[:, None] >= seg_end[None, :]).astype(jnp.int32), axis=1)
    tile_group = jnp.where(tile_id < seg_end[N_GROUPS - 1], tile_group, -1)
    xs3 = _scatter_rows(x3, dest, n_tiles * tm_moe, tm)
    ys3 = _moe_group_ln(xs3, tile_group, p['rwt'], p['rbias'], p['w_gate'], p['w_up'], p['w_down'],
                        layer, p['ln2_w'][layer], p['ln2_b'][layer], tm_moe)
    return _gather_rows(ys3, dest, tm)


def _prefix_selector():
    n = VEC_TILE
    nsub = VEC_CHUNK // VEC_SUB
    t = np.arange(n)[:, None]
    s = np.arange(n)[None, :]
    incl = ((t // VEC_CHUNK) == (s // VEC_CHUNK)) & ((s % VEC_CHUNK) <= (t % VEC_CHUNK))
    r = np.arange((n // VEC_CHUNK) * nsub)[:, None]
    starts = ((r // nsub) == (s // VEC_CHUNK)) & ((s % VEC_CHUNK) < VEC_SUB * (r % nsub))
    return np.concatenate([incl, starts], axis=0).astype(np.float32)


def _vec_heads(heads, sel, mm):
    n = VEC_TILE
    nsub = VEC_CHUNK // VEC_SUB
    nchunk = n // VEC_CHUNK
    nrows = heads[0][0].shape[0]
    kdim = heads[0][0].shape[1]
    streams = [(h, i) for h in range(len(heads)) for i in range(0, nrows, n)]
    tile = lambda h, i, which: heads[h][which][i:i + n]

    prefs = [_dot_exact_rhs(sel, tile(h, i, 3)) for h, i in streams]
    rows_of = lambda fn, m: jnp.concatenate(
        [jnp.broadcast_to(fn(j), (m, kdim)) for j in range(n // m)], axis=0)
    sub = (lax.broadcasted_iota(jnp.int32, (n, kdim), 0) // VEC_SUB) % nsub
    q_cat, k_cat, q_dec0, updates = [], [], [], []
    for (h, i), pref in zip(streams, prefs):
        q, k, v = tile(h, i, 0), tile(h, i, 1), tile(h, i, 2)
        big_g = pref[0:n]
        start = lambda c, j, pref=pref: pref[n + c * nsub + j:n + c * nsub + j + 1]
        q_dec = [q * jnp.exp(big_g)]
        for j in range(1, nsub):
            base_j = rows_of(lambda c: start(c, j), VEC_CHUNK)
            q_dec.append(q * jnp.exp(jnp.minimum(big_g - base_j, 0.0)))
        base_own = rows_of(lambda m: start(m // nsub, m % nsub), VEC_SUB)
        k_rel = k * jnp.exp(base_own - big_g)
        k_cat.append(jnp.concatenate([jnp.where(sub == j, k_rel, 0.0) for j in range(nsub)],
                                     axis=1))
        q_cat.append(jnp.concatenate(q_dec, axis=1))
        q_dec0.append(q_dec[0])
        per_chunk = []
        for c in range(nchunk):
            rows = slice(c * VEC_CHUNK, (c + 1) * VEC_CHUNK)
            g_last = big_g[(c + 1) * VEC_CHUNK - 1:(c + 1) * VEC_CHUNK, :]
            kd = k[rows] * jnp.exp(g_last - big_g[rows])
            per_chunk.append((jnp.exp(g_last), mm.tn(v[rows], kd)))
        updates.append(per_chunk)
    scores = [mm.nt(qc, kc) for qc, kc in zip(q_cat, k_cat)]
    rr = lax.broadcasted_iota(jnp.int32, (n, n), 0)
    cc = lax.broadcasted_iota(jnp.int32, (n, n), 1)
    keep = (rr >= cc) & ((rr // VEC_CHUNK) == (cc // VEC_CHUNK))
    intra = [mm.nn(jnp.where(keep, sc, 0.0), tile(h, i, 2)) for (h, i), sc in zip(streams, scores)]

    states = [hd[4] for hd in heads]
    o_rows = [[] for _ in heads]
    for si, (h, i) in enumerate(streams):
        for c, (decay_last, update) in enumerate(updates[si]):
            rows = slice(c * VEC_CHUNK, (c + 1) * VEC_CHUNK)
            o_rows[h].append(intra[si][rows] + mm.nt(q_dec0[si][rows], states[h]))
            states[h] = states[h] * decay_last + update
    return [(jnp.concatenate(o_rows[h], axis=0), states[h]) for h in range(len(heads))]


def _gla_chunk_kernel(q_ref, k_ref, v_ref, go_ref, sm_ref, w2_ref, b2_ref, nw_ref, sel_ref,
                      o_ref, s_ref, st_ref):
    r = pl.program_id(2)

    @pl.when(r == 0)
    def _():
        st_ref[...] = jnp.zeros_like(st_ref)

    sm = sm_ref[...]
    heads = []
    for hh in range(VEC_HPS):
        kc = slice(hh * GLA_DK, (hh + 1) * GLA_DK)
        vc = slice(hh * GLA_DV, (hh + 1) * GLA_DV)
        gk = _log_sigmoid(_ThreePass.nn(sm, w2_ref[hh]) + b2_ref[hh]) / GLA_NORMALIZER
        heads.append((q_ref[:, kc] * (GLA_DK ** -0.5), k_ref[:, kc], v_ref[:, vc], gk, st_ref[hh]))
    finals = []
    for hh, (o, st) in enumerate(_vec_heads(heads, sel_ref[...], _ThreePass)):
        vc = slice(hh * GLA_DV, (hh + 1) * GLA_DV)
        o_ref[:, vc] = _rms(o, nw_ref[...]) * _silu(go_ref[:, vc])
        st_ref[hh] = st
        finals.append(st)

    @pl.when(r == pl.num_programs(2) - 1)
    def _():
        for hh in range(VEC_HPS):
            s_ref[0, hh] = finals[hh].T


def _gla_prompt(proj, w2p, b2, norm_w, sel, bsz, seq):
    nr = seq // VEC_ROWS
    ng = GLA_HEADS // VEC_HPS
    kw = VEC_HPS * GLA_DK
    vw = VEC_HPS * GLA_DV
    row = lambda off: (lambda b, h, r: (b * nr + r, off + h))
    return pl.pallas_call(
        _gla_chunk_kernel,
        out_shape=(jax.ShapeDtypeStruct((bsz * seq, GLA_VAL), F32),
                   jax.ShapeDtypeStruct((bsz, GLA_HEADS, GLA_DK, GLA_DV), F32)),
        grid=(bsz, ng, nr),
        in_specs=[pl.BlockSpec((VEC_ROWS, kw), row(AB_Q // kw)),
                  pl.BlockSpec((VEC_ROWS, kw), row(AB_K // kw)),
                  pl.BlockSpec((VEC_ROWS, vw), row(AB_V // vw)),
                  pl.BlockSpec((VEC_ROWS, vw), row(AB_GOUT // vw)),
                  pl.BlockSpec((VEC_ROWS, LANES), lambda b, h, r: (b * nr + r, AB_SMALL // LANES)),
                  pl.BlockSpec((VEC_HPS, LANES, GLA_DK), lambda b, h, r: (h, 0, 0)),
                  pl.BlockSpec((VEC_HPS, 1, GLA_DK), lambda b, h, r: (h, 0, 0)),
                  pl.BlockSpec((1, GLA_DV), lambda b, h, r: (0, 0)),
                  pl.BlockSpec(sel.shape, lambda b, h, r: (0, 0))],
        out_specs=(pl.BlockSpec((VEC_ROWS, vw), lambda b, h, r: (b * nr + r, h)),
                   pl.BlockSpec((1, VEC_HPS, GLA_DK, GLA_DV), lambda b, h, r: (b, h, 0, 0))),
        scratch_shapes=[pltpu.VMEM((VEC_HPS, GLA_DV, GLA_DK), F32)],
        compiler_params=_params("parallel", "parallel", "arbitrary"),
        name="gla_chunk",
    )(proj, proj, proj, proj, proj, w2p, b2, norm_w, sel)


def _hgrn_lower_bound(lbraw, layer):
    m = jnp.max(lbraw, axis=0, keepdims=True)
    ex = jnp.exp(lbraw - m)
    sm = ex / jnp.sum(ex, axis=0, keepdims=True)
    acc = sm[0:1]
    for i in range(1, layer + 1):
        acc = acc + sm[i:i + 1]
    return acc - sm[0:1]


def _hgrn_gates(q_raw, f_raw, lb):
    forget = lb + (1.0 - lb) * _sigmoid(f_raw)
    return _silu(q_raw), 1.0 - forget, jnp.log(forget)


def _hgrn_chunk_kernel(layer, q_ref, f_ref, i_ref, go_ref, lb_ref, nw_ref, sel_ref,
                       o_ref, s_ref, st_ref):
    r = pl.program_id(2)

    @pl.when(r == 0)
    def _():
        st_ref[...] = jnp.zeros_like(st_ref)

    lb_all = _hgrn_lower_bound(lb_ref[...], layer)
    heads = []
    for hh in range(VEC_HPS):
        kc = slice(hh * HG_EXPAND, (hh + 1) * HG_EXPAND)
        vc = slice(hh * HG_DI, (hh + 1) * HG_DI)
        q, k, g = _hgrn_gates(q_ref[:, kc], f_ref[:, kc], lb_all[:, kc])
        heads.append((q, k, i_ref[:, vc], g, st_ref[hh]))
    finals = []
    for hh, (o, st) in enumerate(_vec_heads(heads, sel_ref[...], _OnePass)):
        vc = slice(hh * HG_DI, (hh + 1) * HG_DI)
        o_ref[:, vc] = _rms(o, nw_ref[...]) * _silu(go_ref[:, vc])
        st_ref[hh] = st
        finals.append(st)

    @pl.when(r == pl.num_programs(2) - 1)
    def _():
        for hh in range(VEC_HPS):
            s_ref[0, hh] = finals[hh].T


def _hgrn_prompt(proj, lower_bounds, norm_w, sel, layer, bsz, seq):
    nr = seq // VEC_ROWS
    ng = HG_HEADS // VEC_HPS
    kw = VEC_HPS * HG_EXPAND
    vw = VEC_HPS * HG_DI
    row = lambda off: (lambda b, h, r: (b * nr + r, off + h))
    return pl.pallas_call(
        functools.partial(_hgrn_chunk_kernel, layer),
        out_shape=(jax.ShapeDtypeStruct((bsz * seq, HG_I), F32),
                   jax.ShapeDtypeStruct((bsz, HG_HEADS, HG_EXPAND, HG_DI), F32)),
        grid=(bsz, ng, nr),
        in_specs=[pl.BlockSpec((VEC_ROWS, kw), row(0)),
                  pl.BlockSpec((VEC_ROWS, kw), row(ng)),
                  pl.BlockSpec((VEC_ROWS, vw), row(2 * ng)),
                  pl.BlockSpec((VEC_ROWS, vw), row(3 * ng)),
                  pl.BlockSpec((DEPTH, kw), lambda b, h, r: (0, h)),
                  pl.BlockSpec((1, HG_DI), lambda b, h, r: (0, 0)),
                  pl.BlockSpec(sel.shape, lambda b, h, r: (0, 0))],
        out_specs=(pl.BlockSpec((VEC_ROWS, vw), lambda b, h, r: (b * nr + r, h)),
                   pl.BlockSpec((1, VEC_HPS, HG_EXPAND, HG_DI), lambda b, h, r: (b, h, 0, 0))),
        scratch_shapes=[pltpu.VMEM((VEC_HPS, HG_DI, HG_EXPAND), F32)],
        compiler_params=_params("parallel", "parallel", "arbitrary"),
        name="hgrn_chunk",
    )(proj, proj, proj, proj, lower_bounds, norm_w, sel)


def _conv_silu(xp, cw, cb, n, lead):
    acc = cb + cw[SSD_CONV - 1:SSD_CONV] * xp[lead:lead + n]
    for m in range(1, SSD_CONV):
        acc = acc + cw[SSD_CONV - 1 - m:SSD_CONV - m] * xp[lead - m:lead - m + n]
    return _silu(acc)


def _ssd_gate_norm(y, z, nw):
    yz = y * _silu(z)
    parts = []
    for g in range(SSD_GROUPS):
        cols = slice(g * SSD_GROUP_W, (g + 1) * SSD_GROUP_W)
        parts.append(_rms(yz[:, cols], nw[:, cols]))
    return jnp.concatenate(parts, axis=1)


def _ssd_chunk_kernel(z_ref, xbc_ref, sm_ref, cw_ref, cb_ref, dtb_ref, alog_ref, dsk_ref,
                      nw_ref, ex_ref, o_ref, s_ref, conv_ref, st_ref, prev_ref):
    r = pl.program_id(1)
    c = SSD_CHUNK
    mm = _ThreePass

    @pl.when(r == 0)
    def _():
        st_ref[...] = jnp.zeros_like(st_ref)
        prev_ref[...] = jnp.zeros_like(prev_ref)

    x_raw = xbc_ref[...]
    xp = jnp.concatenate([prev_ref[...], x_raw], axis=0)
    prev_ref[...] = x_raw[c - 8:c]
    xc = _conv_silu(xp, cw_ref[...], cb_ref[...], c, 8)
    xs = xc[:, :SSD_INNER]
    bm = xc[:, SSD_INNER:SSD_INNER + SSD_BC]
    cm = xc[:, SSD_INNER + SSD_BC:]

    dt = _softplus(sm_ref[...] + dtb_ref[...])
    a_neg = -jnp.exp(alog_ref[...])
    big_g = _dot_exact_rhs(_tril(c).astype(BF16), dt * a_neg)
    g_t = big_g.T
    g_last = big_g[c - 1:c, :]
    ex = ex_ref[...]
    dt_x = _dot_exact_lhs(dt, ex)
    eg_x = _dot_exact_lhs(jnp.exp(big_g), ex)
    w_x = _dot_exact_lhs(dt * jnp.exp(g_last - big_g), ex)
    xdt = xs * dt_x
    xw = xs * w_x
    causal = _tril(c)
    lane = lax.broadcasted_iota(jnp.int32, (c, LANES), 1)
    st = st_ref[...]
    y_parts = []
    u_parts = []
    for g in range(SSD_GROUPS):
        gcols = slice(g * SSD_GROUP_W, (g + 1) * SSD_GROUP_W)
        bg = bm[:, g * SSD_STATE:(g + 1) * SSD_STATE]
        cg = cm[:, g * SSD_STATE:(g + 1) * SSD_STATE]
        sc = mm.nt(cg, bg)
        inter = mm.nn(cg, st[:, gcols])
        u_parts.append(mm.tn(bg, xw[:, gcols]))
        pair_cols = []
        heads_per_group = SSD_HEADS // SSD_GROUPS
        for p in range(heads_per_group // 2):
            h0 = g * heads_per_group + 2 * p
            xpair = xdt[:, h0 * SSD_HEADDIM:(h0 + 2) * SSD_HEADDIM]
            ws = []
            for h in (h0, h0 + 1):
                diff = big_g[:, h:h + 1] - g_t[h:h + 1, :]
                ws.append(sc * jnp.exp(jnp.where(causal, diff, -jnp.inf)))
            x_diag = jnp.concatenate([jnp.where(lane < SSD_HEADDIM, xpair, 0.0),
                                      jnp.where(lane < SSD_HEADDIM, 0.0, xpair)], axis=0)
            pair_cols.append(mm.nn(jnp.concatenate(ws, axis=1), x_diag))
        y_intra = jnp.concatenate(pair_cols, axis=1)
        y_parts.append(y_intra + inter * eg_x[:, gcols])
    y = jnp.concatenate(y_parts, axis=1) + dsk_ref[...] * xs
    o_ref[...] = _ssd_gate_norm(y, z_ref[...], nw_ref[...])
    st = st * eg_x[c - 1:c, :] + jnp.concatenate(u_parts, axis=1)
    st_ref[...] = st

    @pl.when(r == pl.num_programs(1) - 1)
    def _():
        s_ref[0] = st
        conv_ref[0] = x_raw[c - (SSD_CONV - 1):c]


def _ssd_prompt(proj, conv_w, conv_b, dtb_p, alog_p, dskip_x, norm_w, expand, bsz, seq):
    nr = seq // SSD_CHUNK
    fixed = lambda b, r: (0, 0)
    return pl.pallas_call(
        _ssd_chunk_kernel,
        out_shape=(jax.ShapeDtypeStruct((bsz * seq, SSD_INNER), F32),
                   jax.ShapeDtypeStruct((bsz, SSD_STATE, SSD_INNER), F32),
                   jax.ShapeDtypeStruct((bsz, SSD_CONV - 1, SSD_CONV_DIM), F32)),
        grid=(bsz, nr),
        in_specs=[pl.BlockSpec((SSD_CHUNK, SSD_INNER), lambda b, r: (b * nr + r, AB_Z // SSD_INNER)),
                  pl.BlockSpec((SSD_CHUNK, SSD_CONV_DIM), lambda b, r: (b * nr + r, AB_XBC // SSD_CONV_DIM)),
                  pl.BlockSpec((SSD_CHUNK, LANES), lambda b, r: (b * nr + r, AB_SMALL // LANES)),
                  pl.BlockSpec((SSD_CONV, SSD_CONV_DIM), fixed),
                  pl.BlockSpec((1, SSD_CONV_DIM), fixed),
                  pl.BlockSpec((1, LANES), fixed),
                  pl.BlockSpec((1, LANES), fixed),
                  pl.BlockSpec((1, SSD_INNER), fixed),
                  pl.BlockSpec((1, SSD_INNER), fixed),
                  pl.BlockSpec((LANES, SSD_INNER), fixed)],
        out_specs=(pl.BlockSpec((SSD_CHUNK, SSD_INNER), lambda b, r: (b * nr + r, 0)),
                   pl.BlockSpec((1, SSD_STATE, SSD_INNER), lambda b, r: (b, 0, 0)),
                   pl.BlockSpec((1, SSD_CONV - 1, SSD_CONV_DIM), lambda b, r: (b, 0, 0))),
        scratch_shapes=[pltpu.VMEM((SSD_STATE, SSD_INNER), F32),
                        pltpu.VMEM((8, SSD_CONV_DIM), F32)],
        compiler_params=_params("parallel", "arbitrary"),
        name="ssd_chunk",
    )(proj, proj, proj, conv_w, conv_b, dtb_p, alog_p, dskip_x, norm_w, expand)


def _ab_prep_kernel(q_ref, sm_ref, xbc_ref, cs_ref, w2_ref, b2_ref, cw_ref, cb_ref, dtb_ref,
                    alog_ref, qs_ref, dec_ref, xc_ref, dt_ref, da_ref, cs_out_ref):
    sm = sm_ref[...]
    gk = _log_sigmoid(_ThreePass.nn(sm, w2_ref[...]) + b2_ref[...]) / GLA_NORMALIZER
    qs_ref[...] = q_ref[...] * (GLA_DK ** -0.5)
    dec_ref[...] = jnp.exp(gk)
    cw = cw_ref[...]
    x_raw = xbc_ref[...]
    acc = cb_ref[...] + cw[SSD_CONV - 1:SSD_CONV] * x_raw
    for j in range(SSD_CONV - 1):
        acc = acc + cw[j:j + 1] * cs_ref[j]
    xc_ref[...] = _silu(acc)
    for j in range(SSD_CONV - 2):
        cs_out_ref[j] = cs_ref[j + 1]
    cs_out_ref[SSD_CONV - 2] = x_raw
    dt = _softplus(sm + dtb_ref[...])
    dt_ref[...] = dt
    da_ref[...] = jnp.exp(dt * -jnp.exp(alog_ref[...]))


def _ab_prep(proj, conv_state, w2_wide, b2_wide, conv_w, conv_b, dtb_p, alog_p):
    bsz = proj.shape[0]
    fixed = lambda i: (0, 0)
    sds = jax.ShapeDtypeStruct
    return pl.pallas_call(
        _ab_prep_kernel,
        out_shape=(sds((bsz, GLA_KEY), F32), sds((bsz, GLA_KEY), F32),
                   sds((bsz, SSD_CONV_DIM), F32), sds((bsz, LANES), F32), sds((bsz, LANES), F32),
                   sds((SSD_CONV - 1, bsz, SSD_CONV_DIM), F32)),
        grid=(1,),
        in_specs=[pl.BlockSpec((bsz, GLA_KEY), lambda i: (0, AB_Q // GLA_KEY)),
                  pl.BlockSpec((bsz, LANES), lambda i: (0, AB_SMALL // LANES)),
                  pl.BlockSpec((bsz, SSD_CONV_DIM), lambda i: (0, AB_XBC // SSD_CONV_DIM)),
                  pl.BlockSpec((SSD_CONV - 1, bsz, SSD_CONV_DIM), lambda i: (0, 0, 0)),
                  pl.BlockSpec((LANES, GLA_KEY), fixed),
                  pl.BlockSpec((1, GLA_KEY), fixed),
                  pl.BlockSpec((SSD_CONV, SSD_CONV_DIM), fixed),
                  pl.BlockSpec((1, SSD_CONV_DIM), fixed),
                  pl.BlockSpec((1, LANES), fixed),
                  pl.BlockSpec((1, LANES), fixed)],
        out_specs=(pl.BlockSpec((bsz, GLA_KEY), fixed), pl.BlockSpec((bsz, GLA_KEY), fixed),
                   pl.BlockSpec((bsz, SSD_CONV_DIM), fixed), pl.BlockSpec((bsz, LANES), fixed),
                   pl.BlockSpec((bsz, LANES), fixed),
                   pl.BlockSpec((SSD_CONV - 1, bsz, SSD_CONV_DIM), lambda i: (0, 0, 0))),
        compiler_params=_params("arbitrary"),
        name="ab_prep",
    )(proj, proj, proj, conv_state, w2_wide, b2_wide, conv_w, conv_b, dtb_p, alog_p)


def _hgrn_prep_kernel(layer, q_ref, f_ref, lb_ref, qs_ref, k_ref, dec_ref):
    lb = _hgrn_lower_bound(lb_ref[...], layer)
    forget = lb + (1.0 - lb) * _sigmoid(f_ref[...])
    qs_ref[...] = _silu(q_ref[...])
    k_ref[...] = 1.0 - forget
    dec_ref[...] = jnp.exp(jnp.log(forget))


def _hgrn_prep(proj, lower_bounds, layer):
    bsz = proj.shape[0]
    blk = lambda j: pl.BlockSpec((bsz, HG_F), lambda i: (0, j))
    return pl.pallas_call(
        functools.partial(_hgrn_prep_kernel, layer),
        out_shape=tuple(jax.ShapeDtypeStruct((bsz, HG_F), F32) for _ in range(3)),
        grid=(1,),
        in_specs=[blk(0), blk(1), pl.BlockSpec((DEPTH, HG_F), lambda i: (0, 0))],
        out_specs=tuple(blk(0) for _ in range(3)),
        compiler_params=_params("arbitrary"),
        name="hgrn_prep",
    )(proj, proj, lower_bounds)


def _vec_step_kernel(s_ref, q_ref, k_ref, d_ref, v_ref, go_ref, nw_ref, so_ref, o_ref):
    qt = q_ref[0, 0]
    kt = k_ref[0, 0]
    dt = d_ref[0, 0]
    v = v_ref[...]
    rows = []
    for b in range(STEP_B):
        sn = s_ref[b, 0] * dt[:, b:b + 1] + kt[:, b:b + 1] * v[b:b + 1, :]
        so_ref[b, 0] = sn
        rows.append(jnp.sum(qt[:, b:b + 1] * sn, axis=0, keepdims=True))
    o = jnp.concatenate(rows, axis=0)
    o_ref[...] = _rms(o, nw_ref[...]) * _silu(go_ref[...])


def _vec_step(state, q_cols, k_cols, d_cols, vsrc, v_off, gsrc, g_off, norm_w):
    bsz, nh, kdim, vdim = state.shape
    col = lambda j, h: (h, j, 0, 0)
    return pl.pallas_call(
        _vec_step_kernel,
        out_shape=(jax.ShapeDtypeStruct(state.shape, F32),
                   jax.ShapeDtypeStruct((bsz, nh * vdim), F32)),
        grid=(bsz // STEP_B, nh),
        in_specs=[pl.BlockSpec((STEP_B, 1, kdim, vdim), lambda j, h: (j, h, 0, 0)),
                  pl.BlockSpec((1, 1, kdim, STEP_B), col),
                  pl.BlockSpec((1, 1, kdim, STEP_B), col),
                  pl.BlockSpec((1, 1, kdim, STEP_B), col),
                  pl.BlockSpec((STEP_B, vdim), lambda j, h: (j, v_off + h)),
                  pl.BlockSpec((STEP_B, vdim), lambda j, h: (j, g_off + h)),
                  pl.BlockSpec((1, vdim), lambda j, h: (0, 0))],
        out_specs=(pl.BlockSpec((STEP_B, 1, kdim, vdim), lambda j, h: (j, h, 0, 0)),
                   pl.BlockSpec((STEP_B, vdim), lambda j, h: (j, h))),
        compiler_params=_params("parallel", "parallel"),
        name="vec_step",
    )(state, q_cols, k_cols, d_cols, vsrc, gsrc, norm_w)


def _ssd_step_kernel(s_ref, b_ref, c_ref, x_ref, dt_ref, da_ref, dsk_ref, so_ref, y_ref):
    bt = b_ref[0, 0]
    ct = c_ref[0, 0]
    x = x_ref[...]
    dt = dt_ref[0]
    da = da_ref[0]
    hpg = SSD_HEADS // SSD_GROUPS
    rows = []
    for b in range(STEP_B):
        pieces = []
        for hh in range(hpg):
            xh = x[b:b + 1, hh * SSD_HEADDIM:(hh + 1) * SSD_HEADDIM]
            sn = s_ref[b, hh] * da[b:b + 1, hh:hh + 1] + (bt[:, b:b + 1] * dt[b:b + 1, hh:hh + 1]) * xh
            so_ref[b, hh] = sn
            pieces.append(jnp.sum(ct[:, b:b + 1] * sn, axis=0, keepdims=True))
        rows.append(jnp.concatenate(pieces, axis=1))
    y_ref[...] = jnp.concatenate(rows, axis=0) + dsk_ref[...] * x


def _ssd_step(state, b_cols, c_cols, xc, dt_g, da_g, dskip_x):
    bsz = state.shape[0]
    hpg = SSD_HEADS // SSD_GROUPS
    col = lambda j, g: (g, j, 0, 0)
    return pl.pallas_call(
        _ssd_step_kernel,
        out_shape=(jax.ShapeDtypeStruct(state.shape, F32),
                   jax.ShapeDtypeStruct((bsz, SSD_INNER), F32)),
        grid=(bsz // STEP_B, SSD_GROUPS),
        in_specs=[pl.BlockSpec((STEP_B, hpg, SSD_STATE, SSD_HEADDIM), lambda j, g: (j, g, 0, 0)),
                  pl.BlockSpec((1, 1, SSD_STATE, STEP_B), col),
                  pl.BlockSpec((1, 1, SSD_STATE, STEP_B), col),
                  pl.BlockSpec((STEP_B, SSD_GROUP_W), lambda j, g: (j, g)),
                  pl.BlockSpec((1, STEP_B, LANES), lambda j, g: (g, j, 0)),
                  pl.BlockSpec((1, STEP_B, LANES), lambda j, g: (g, j, 0)),
                  pl.BlockSpec((1, SSD_GROUP_W), lambda j, g: (0, g))],
        out_specs=(pl.BlockSpec((STEP_B, hpg, SSD_STATE, SSD_HEADDIM), lambda j, g: (j, g, 0, 0)),
                   pl.BlockSpec((STEP_B, SSD_GROUP_W), lambda j, g: (j, g))),
        compiler_params=_params("parallel", "parallel"),
        name="ssd_step",
    )(state, b_cols, c_cols, xc, dt_g, da_g, dskip_x)


def _ssd_post_kernel(y_ref, z_ref, nw_ref, o_ref):
    o_ref[...] = _ssd_gate_norm(y_ref[...], z_ref[...], nw_ref[...])


def _ssd_post(y, proj, norm_w):
    bsz = y.shape[0]
    return pl.pallas_call(
        _ssd_post_kernel,
        out_shape=jax.ShapeDtypeStruct((bsz, SSD_INNER), F32),
        grid=(1,),
        in_specs=[pl.BlockSpec((bsz, SSD_INNER), lambda i: (0, 0)),
                  pl.BlockSpec((bsz, SSD_INNER), lambda i: (0, AB_Z // SSD_INNER)),
                  pl.BlockSpec((1, SSD_INNER), lambda i: (0, 0))],
        out_specs=pl.BlockSpec((bsz, SSD_INNER), lambda i: (0, 0)),
        compiler_params=_params("arbitrary"),
        name="ssd_post",
    )(y, proj, norm_w)


def _to_cols(a, nh):
    bsz = a.shape[0]
    return a.reshape(bsz // STEP_B, STEP_B, nh, -1).transpose(2, 0, 3, 1)


def _prep_weights(w_in_ab, w_gk2, b_gk2, gla_norm_w, conv_w, conv_b, dt_bias, a_log, d_skip,
                  ssd_norm_w, w_out_ab, w_in_c, hg_norm_w, w_out_c, router_w, router_bias,
                  w_gate, w_up, w_down, ln1_w, ln1_b, ln2_w, ln2_b):
    offs = np.cumsum([0, GLA_KEY, GLA_KEY, GLA_VAL, GLA_VAL, GLA_RANK, SSD_INNER, SSD_CONV_DIM,
                      SSD_HEADS])
    sec = lambda w, i: w[:, offs[i]:offs[i + 1]]
    w = w_in_ab[0]
    pad = jnp.zeros((D_MODEL, LANES - SSD_HEADS - GLA_RANK), w.dtype)
    w_ab = jnp.concatenate([sec(w, 5), sec(w, 2), sec(w, 3), sec(w, 6), sec(w, 0), sec(w, 1),
                            sec(w, 7), sec(w, 4), pad], axis=1)
    hi_lo = lambda m: (m.astype(BF16), (m - m.astype(BF16).astype(F32)).astype(BF16))
    w2_wide = jnp.zeros((LANES, GLA_KEY), F32).at[SSD_HEADS:SSD_HEADS + GLA_RANK].set(w_gk2[0])
    lane_pad = lambda v: jnp.zeros((1, LANES), F32).at[0, :SSD_HEADS].set(v)
    expand = np.zeros((LANES, SSD_INNER), np.float32)
    for h in range(SSD_HEADS):
        expand[h, h * SSD_HEADDIM:(h + 1) * SSD_HEADDIM] = 1.0
    return dict(
        w_ab=hi_lo(w_ab),
        w2_wide=w2_wide,
        w2_heads=w2_wide.reshape(LANES, GLA_HEADS, GLA_DK).transpose(1, 0, 2),
        b2_wide=b_gk2[0].reshape(1, GLA_KEY),
        b2_heads=b_gk2[0].reshape(GLA_HEADS, 1, GLA_DK),
        gla_norm_w=gla_norm_w[0].reshape(1, GLA_DV),
        conv_w=conv_w[0], conv_b=conv_b[0].reshape(1, SSD_CONV_DIM),
        dtb_p=lane_pad(dt_bias[0]), alog_p=lane_pad(a_log[0]),
        dskip_x=jnp.repeat(d_skip[0], SSD_HEADDIM).reshape(1, SSD_INNER),
        ssd_norm_w=ssd_norm_w[0].reshape(1, SSD_INNER),
        expand=jnp.asarray(expand, BF16),
        prefix_sel=jnp.asarray(_prefix_selector(), BF16),
        w_out_gla=hi_lo(w_out_ab[0, :GLA_VAL]),
        w_out_ssd=hi_lo(w_out_ab[0, GLA_VAL:]),
        w_c=w_in_c[0].astype(BF16),
        hg_norm_w=hg_norm_w[0].reshape(1, HG_DI),
        w_out_c=w_out_c[0].astype(BF16),
        rwt=router_w.T,
        rbias=router_bias.reshape(N_EXPERTS, 1),
        w_gate=w_gate.reshape(DEPTH * N_EXPERTS, D_MODEL, D_FF_EXPERT),
        w_up=w_up.reshape(DEPTH * N_EXPERTS, D_MODEL, D_FF_EXPERT),
        w_down=w_down.reshape(DEPTH * N_EXPERTS, D_FF_EXPERT, D_MODEL),
        ln1_w=ln1_w.reshape(DEPTH, 1, D_MODEL), ln1_b=ln1_b.reshape(DEPTH, 1, D_MODEL),
        ln2_w=ln2_w.reshape(DEPTH, 1, D_MODEL), ln2_b=ln2_b.reshape(DEPTH, 1, D_MODEL),
    )


def _ffn(x, p, layer, tm, tm_moe):
    gates = _router(x, p['rwt'], p['rbias'], tm)
    return _moe_ln(x, gates, p['w_gate'], p['w_up'], p['w_down'], layer,
                   p['ln2_w'][layer], p['ln2_b'][layer], tm_moe)


def _ssd_state_from_wide(s_wide):
    bsz = s_wide.shape[0]
    return s_wide.reshape(bsz, SSD_STATE, SSD_HEADS, SSD_HEADDIM).transpose(0, 2, 1, 3)


def _trunk_prompt(x3, p, lower_bounds, tm, tn_ab, tn_c):
    bsz, seq, _ = x3.shape
    x = x3.reshape(bsz * seq, D_MODEL)
    tm_big = 2 * tm
    proj = _proj(x, p['w_ab'], tm_big, tn_ab)
    o_gla, s_gla = _gla_prompt(proj, p['w2_heads'], p['b2_heads'], p['gla_norm_w'],
                               p['prefix_sel'], bsz, seq)
    yz, s_ssd, s_conv = _ssd_prompt(proj, p['conv_w'], p['conv_b'], p['dtb_p'], p['alog_p'],
                                    p['dskip_x'], p['ssd_norm_w'], p['expand'], bsz, seq)
    x, x3 = _outproj_ln([o_gla, yz], [p['w_out_gla'], p['w_out_ssd']], x, p['ln1_w'][0],
                        p['ln1_b'][0], tm, slabs=True)
    x = _ffn_sorted(x, x3, p, 0, tm, tm_big)
    proj_c = _proj(x, p['w_c'], tm_big, tn_c)
    o_hg, s_hg = _hgrn_prompt(proj_c, lower_bounds, p['hg_norm_w'], p['prefix_sel'], 1, bsz, seq)
    x, x3 = _outproj_ln([o_hg], [p['w_out_c']], x, p['ln1_w'][1], p['ln1_b'][1], tm, slabs=True)
    x = _ffn_sorted(x, x3, p, 1, tm, tm_big)
    return (x.reshape(bsz, seq, D_MODEL), s_gla[None], _ssd_state_from_wide(s_ssd)[None],
            s_conv[None], s_hg[None])


def _trunk_sample(x3, st_gla, st_ssd, st_conv, st_hg, p, lower_bounds, tn_ab, tn_c):
    bsz = x3.shape[0]
    tm = bsz
    x = x3.reshape(bsz, D_MODEL)
    proj = _proj(x, p['w_ab'], tm, tn_ab)
    qs, dec, xc, dt, da, conv_new = _ab_prep(proj, st_conv[0].transpose(1, 0, 2), p['w2_wide'],
                                             p['b2_wide'], p['conv_w'], p['conv_b'], p['dtb_p'],
                                             p['alog_p'])
    conv_new = conv_new.transpose(1, 0, 2)
    k_gla = proj[:, AB_K:AB_K + GLA_KEY]
    s_gla, o_gla = _vec_step(st_gla[0], _to_cols(qs, GLA_HEADS), _to_cols(k_gla, GLA_HEADS),
                             _to_cols(dec, GLA_HEADS), proj, AB_V // GLA_DV, proj,
                             AB_GOUT // GLA_DV, p['gla_norm_w'])
    hpg = SSD_HEADS // SSD_GROUPS
    per_group = lambda a: jnp.pad(a[:, :SSD_HEADS].reshape(bsz, SSD_GROUPS, hpg).transpose(1, 0, 2),
                                  ((0, 0), (0, 0), (0, LANES - hpg)))
    s_ssd, y = _ssd_step(st_ssd[0],
                         _to_cols(xc[:, SSD_INNER:SSD_INNER + SSD_BC], SSD_GROUPS),
                         _to_cols(xc[:, SSD_INNER + SSD_BC:], SSD_GROUPS),
                         xc, per_group(dt), per_group(da), p['dskip_x'])
    yz = _ssd_post(y, proj, p['ssd_norm_w'])
    x = _outproj_ln([o_gla, yz], [p['w_out_gla'], p['w_out_ssd']], x, p['ln1_w'][0], p['ln1_b'][0], tm)
    x = _ffn(x, p, 0, tm, tm)
    proj_c = _proj(x, p['w_c'], tm, tn_c)
    qh, kh, dh = _hgrn_prep(proj_c, lower_bounds, 1)
    s_hg, o_hg = _vec_step(st_hg[0], _to_cols(qh, HG_HEADS), _to_cols(kh, HG_HEADS),
                           _to_cols(dh, HG_HEADS), proj_c, 2 * HG_HEADS, proj_c, 3 * HG_HEADS,
                           p['hg_norm_w'])
    x = _outproj_ln([o_hg], [p['w_out_c']], x, p['ln1_w'][1], p['ln1_b'][1], tm)
    x = _ffn(x, p, 1, tm, tm)
    return x.reshape(bsz, 1, D_MODEL), s_gla[None], s_ssd[None], conv_new[None], s_hg[None]


def kernel(x_prompt, x_sample, state_gla, state_ssd, state_conv, state_hgrn, w_in_ab, w_gk2, b_gk2, gla_norm_w, conv_w, conv_b, dt_bias, a_log, d_skip, ssd_norm_w, w_out_ab, w_in_c, lower_bounds, hg_norm_w, w_out_c, router_w, router_bias, w_gate, w_up, w_down, ln1_w, ln1_b, ln2_w, ln2_b):
    p = _prep_weights(w_in_ab, w_gk2, b_gk2, gla_norm_w, conv_w, conv_b, dt_bias, a_log, d_skip,
                      ssd_norm_w, w_out_ab, w_in_c, hg_norm_w, w_out_c, router_w, router_bias,
                      w_gate, w_up, w_down, ln1_w, ln1_b, ln2_w, ln2_b)
    y_p, gla_p, ssd_p, conv_p, hg_p = _trunk_prompt(x_prompt, p, lower_bounds, 512, 1152, 1024)
    y_s, gla_s, ssd_s, conv_s, hg_s = _trunk_sample(x_sample, state_gla, state_ssd, state_conv,
                                                    state_hgrn, p, lower_bounds, 1152, 1024)
    return (y_p, y_s, gla_p, ssd_p, conv_p, hg_p, gla_s, ssd_s, conv_s, hg_s)
```

```python
import functools

import numpy as np
import jax
import jax.numpy as jnp
from jax import lax
from jax.experimental import pallas as pl
from jax.experimental.pallas import tpu as pltpu

F32 = jnp.float32
BF16 = jnp.bfloat16

D_MODEL = 1024
DEPTH = 2
GLA_HEADS = 4
GLA_DK = 128
GLA_DV = 256
GLA_KEY = GLA_HEADS * GLA_DK
GLA_VAL = GLA_HEADS * GLA_DV
GLA_RANK = 16
GLA_NORMALIZER = 16.0
SSD_INNER = 1024
SSD_HEADDIM = 64
SSD_HEADS = 16
SSD_STATE = 128
SSD_GROUPS = 2
SSD_CONV = 4
SSD_GROUP_W = SSD_INNER // SSD_GROUPS
SSD_BC = SSD_GROUPS * SSD_STATE
SSD_CONV_DIM = SSD_INNER + 2 * SSD_BC
HG_EXPAND = 128
HG_HEADS = 8
HG_F = HG_HEADS * HG_EXPAND
HG_I = D_MODEL
HG_DI = HG_I // HG_HEADS
N_EXPERTS = 16
N_GROUPS = 4
EXPERTS_PER_GROUP = 4
D_FF_EXPERT = 512
ALPHA = (2 * DEPTH) ** 0.25
EPS = 1e-5

LANES = 128
VMEM_LIMIT = 48 * 1024 * 1024

AB_Z = 0
AB_V = 1024
AB_GOUT = 2048
AB_XBC = 3072
AB_Q = 4608
AB_K = 5120
AB_SMALL = 5632
AB_COLS = 5760
C_COLS = 4096

VEC_CHUNK = 64
VEC_SUB = 16
VEC_TILE = 256
VEC_ROWS = 512
VEC_HPS = 2
SSD_CHUNK = 128
STEP_B = 8


def _params(*sem):
    return pltpu.CompilerParams(dimension_semantics=sem, vmem_limit_bytes=VMEM_LIMIT)


_NN = (((1,), (0,)), ((), ()))
_NT = (((1,), (1,)), ((), ()))
_TN = (((0,), (0,)), ((), ()))


def _dot1(dims, a, b):
    return lax.dot_general(a.astype(BF16), b.astype(BF16), dims, preferred_element_type=F32)


def _split2(a):
    hi = a.astype(BF16)
    return hi, (a - hi.astype(F32)).astype(BF16)


def _dot3(dims, a, b):
    ah, al = _split2(a)
    bh, bl = _split2(b)
    d = lambda x, y: lax.dot_general(x, y, dims, preferred_element_type=F32)
    return (d(al, bh) + d(ah, bl)) + d(ah, bh)


class _OnePass:
    nn = staticmethod(lambda a, b: _dot1(_NN, a, b))
    nt = staticmethod(lambda a, b: _dot1(_NT, a, b))
    tn = staticmethod(lambda a, b: _dot1(_TN, a, b))


class _ThreePass:
    nn = staticmethod(lambda a, b: _dot3(_NN, a, b))
    nt = staticmethod(lambda a, b: _dot3(_NT, a, b))
    tn = staticmethod(lambda a, b: _dot3(_TN, a, b))


def _dot(a, b):
    return _dot1(_NN, a, b)


def _dot_nt(a, b):
    return _dot1(_NT, a, b)


def _dot_tn(a, b):
    return _dot1(_TN, a, b)


def _split3(a):
    hi = a.astype(BF16)
    r1 = a - hi.astype(F32)
    mid = r1.astype(BF16)
    lo = (r1 - mid.astype(F32)).astype(BF16)
    return hi, mid, lo


def _dot_exact_rhs(sel, a):
    hi, mid, lo = _split3(a)
    d = lambda p: jnp.dot(sel, p, preferred_element_type=F32)
    return (d(lo) + d(mid)) + d(hi)


def _dot_exact_lhs(a, sel):
    hi, mid, lo = _split3(a)
    d = lambda p: jnp.dot(p, sel, preferred_element_type=F32)
    return (d(lo) + d(mid)) + d(hi)


def _tril(n):
    r = lax.broadcasted_iota(jnp.int32, (n, n), 0)
    c = lax.broadcasted_iota(jnp.int32, (n, n), 1)
    return r >= c


def _sigmoid(x):
    return 1.0 / (1.0 + jnp.exp(-x))


def _silu(x):
    return x * _sigmoid(x)


def _softplus(x):
    return jnp.maximum(x, 0.0) + jnp.log(1.0 + jnp.exp(-jnp.abs(x)))


def _log_sigmoid(x):
    return -_softplus(-x)


def _rms(x, w):
    return x * lax.rsqrt(jnp.mean(x * x, axis=-1, keepdims=True) + EPS) * w


def _layer_norm(x, w, b):
    mu = jnp.mean(x, axis=-1, keepdims=True)
    xc = x - mu
    var = jnp.mean(xc * xc, axis=-1, keepdims=True)
    return xc * lax.rsqrt(var + EPS) * w + b


def _proj_kernel(x_ref, w_ref, o_ref):
    o_ref[...] = jnp.dot(x_ref[...].astype(BF16), w_ref[...], preferred_element_type=F32)


def _proj3_kernel(x_ref, wh_ref, wl_ref, o_ref, xh_ref, xl_ref):
    @pl.when(pl.program_id(1) == 0)
    def _():
        hi, lo = _split2(x_ref[...])
        xh_ref[...] = hi
        xl_ref[...] = lo

    d = lambda a, b: jnp.dot(a, b, preferred_element_type=F32)
    xh = xh_ref[...]
    wh = wh_ref[...]
    o_ref[...] = (d(xl_ref[...], wh) + d(xh, wl_ref[...])) + d(xh, wh)


def _proj(x, w, tm, tn):
    t, k = x.shape
    three = isinstance(w, tuple)
    ws = w if three else (w,)
    n = ws[0].shape[1]
    return pl.pallas_call(
        _proj3_kernel if three else _proj_kernel,
        out_shape=jax.ShapeDtypeStruct((t, n), F32),
        grid=(t // tm, n // tn),
        in_specs=[pl.BlockSpec((tm, k), lambda i, j: (i, 0))]
                 + [pl.BlockSpec((k, tn), lambda i, j: (0, j)) for _ in ws],
        out_specs=pl.BlockSpec((tm, tn), lambda i, j: (i, j)),
        scratch_shapes=[pltpu.VMEM((tm, k), BF16), pltpu.VMEM((tm, k), BF16)] if three else [],
        compiler_params=_params("parallel", "arbitrary"),
        name="in_proj",
    )(x, *ws)


def _outproj_ln_kernel(n_in, three, *refs):
    a_refs = refs[:n_in]
    nw = 2 if three else 1
    w_refs = refs[n_in:n_in + nw * n_in]
    x_ref, lw_ref, lb_ref, o_ref = refs[n_in + nw * n_in:]
    d = lambda a, b: jnp.dot(a, b, preferred_element_type=F32)
    mix = None
    for i, a_ref in enumerate(a_refs):
        if three:
            ah, al = _split2(a_ref[...])
            wh = w_refs[2 * i][...]
            part = (d(al, wh) + d(ah, w_refs[2 * i + 1][...])) + d(ah, wh)
        else:
            part = d(a_ref[...].astype(BF16), w_refs[i][...])
        mix = part if mix is None else mix + part
    o_ref[...] = _layer_norm(ALPHA * x_ref[...] + mix, lw_ref[...], lb_ref[...])


def _outproj_ln(acts, ws, x, ln_w, ln_b, tm):
    t = x.shape[0]
    n_in = len(acts)
    three = isinstance(ws[0], tuple)
    flat_ws = [w for pair in ws for w in pair] if three else list(ws)
    row = lambda i: (i, 0)
    fixed = lambda i: (0, 0)
    in_specs = ([pl.BlockSpec((tm, a.shape[1]), row) for a in acts]
                + [pl.BlockSpec(w.shape, fixed) for w in flat_ws]
                + [pl.BlockSpec((tm, D_MODEL), row),
                   pl.BlockSpec((1, D_MODEL), fixed), pl.BlockSpec((1, D_MODEL), fixed)])
    return pl.pallas_call(
        functools.partial(_outproj_ln_kernel, n_in, three),
        out_shape=jax.ShapeDtypeStruct((t, D_MODEL), F32),
        grid=(t // tm,),
        in_specs=in_specs,
        out_specs=pl.BlockSpec((tm, D_MODEL), row),
        compiler_params=_params("parallel"),
        name="out_proj_ln",
    )(*acts, *flat_ws, x, ln_w, ln_b)


def _router_scores(x, rwt, bias):
    w3 = _split3(rwt)
    x3 = _split3(x)
    nt = lambda a, b: lax.dot_general(a, b, _NT, preferred_element_type=F32)
    logits = (((nt(w3[0], x3[2]) + nt(w3[2], x3[0])) + nt(w3[1], x3[1]))
              + (nt(w3[0], x3[1]) + nt(w3[1], x3[0]))) + nt(w3[0], x3[0])
    scores = _sigmoid(logits)
    return scores, scores + bias


def _best_group(sel):
    tm = sel.shape[1]
    s = [sel[e:e + 1, :] for e in range(N_EXPERTS)]
    grp = []
    for g in range(N_GROUPS):
        m = s[g * EXPERTS_PER_GROUP:(g + 1) * EXPERTS_PER_GROUP]
        best = None
        for i in range(EXPERTS_PER_GROUP):
            for j in range(i + 1, EXPERTS_PER_GROUP):
                p = m[i] + m[j]
                best = p if best is None else jnp.maximum(best, p)
        grp.append(best)
    best_g = jnp.zeros((1, tm), jnp.int32)
    best_v = grp[0]
    for g in range(1, N_GROUPS):
        upd = grp[g] > best_v
        best_g = jnp.where(upd, g, best_g)
        best_v = jnp.where(upd, grp[g], best_v)
    return best_g


def _top2(vals, weights):
    tm = vals[0].shape[1]
    neg = jnp.full((1, tm), -jnp.inf, F32)

    def first_argmax(rows):
        idx = jnp.zeros((1, tm), jnp.int32)
        top = rows[0]
        for e in range(1, len(rows)):
            upd = rows[e] > top
            idx = jnp.where(upd, e, idx)
            top = jnp.where(upd, rows[e], top)
        return idx

    idx1 = first_argmax(vals)
    idx2 = first_argmax([jnp.where(idx1 == e, neg, v) for e, v in enumerate(vals)])
    zero = jnp.zeros((1, tm), F32)
    w1 = zero
    w2 = zero
    for e, w in enumerate(weights):
        w1 = w1 + jnp.where(idx1 == e, w, zero)
        w2 = w2 + jnp.where(idx2 == e, w, zero)
    tot = w1 + w2
    g1 = w1 / tot
    g2 = w2 / tot
    return [jnp.where(idx1 == e, g1, zero) + jnp.where(idx2 == e, g2, zero)
            for e in range(len(vals))]


def _pad_rows(rows, tm):
    return jnp.concatenate(rows + [jnp.zeros((LANES - len(rows), tm), F32)], axis=0)


def _route_in_group(x, rwt, bias, group):
    tm = x.shape[0]
    scores, sel = _router_scores(x, rwt, bias)
    zero = jnp.zeros((1, tm), F32)
    vals, weights = [], []
    for m in range(EXPERTS_PER_GROUP):
        v = zero
        w = zero
        for g in range(N_GROUPS):
            e = g * EXPERTS_PER_GROUP + m
            v = jnp.where(group == g, sel[e:e + 1, :], v)
            w = jnp.where(group == g, scores[e:e + 1, :], w)
        vals.append(v)
        weights.append(w)
    return _pad_rows(_top2(vals, weights), tm)


def _route(x, rwt, bias):
    tm = x.shape[0]
    scores, sel = _router_scores(x, rwt, bias)
    s = [sel[e:e + 1, :] for e in range(N_EXPERTS)]
    sc = [scores[e:e + 1, :] for e in range(N_EXPERTS)]
    best_g = _best_group(sel)
    neg = jnp.full((1, tm), -jnp.inf, F32)
    ms = [jnp.where(best_g == e // EXPERTS_PER_GROUP, s[e], neg) for e in range(N_EXPERTS)]
    return _pad_rows(_top2(ms, sc), tm), best_g


def _router_kernel(with_gates, x_ref, rwt_ref, bias_ref, g_ref):
    tm = x_ref.shape[0]
    if with_gates:
        gates_t, best_g = _route(x_ref[...], rwt_ref[...], bias_ref[...])
    else:
        best_g = _best_group(_router_scores(x_ref[...], rwt_ref[...], bias_ref[...])[1])
        gates_t = jnp.zeros((LANES, tm), F32)
    row = lax.broadcasted_iota(jnp.int32, (LANES, tm), 0)
    gates_t = jnp.where(row == N_EXPERTS, best_g.astype(F32), gates_t)
    g_ref[...] = gates_t.T


def _router(x, rwt, bias, tm, with_gates=True):
    t = x.shape[0]
    return pl.pallas_call(
        functools.partial(_router_kernel, with_gates),
        out_shape=jax.ShapeDtypeStruct((t, LANES), F32),
        grid=(t // tm,),
        in_specs=[pl.BlockSpec((tm, D_MODEL), lambda i: (i, 0)),
                  pl.BlockSpec((N_EXPERTS, D_MODEL), lambda i: (0, 0)),
                  pl.BlockSpec((N_EXPERTS, 1), lambda i: (0, 0))],
        out_specs=pl.BlockSpec((tm, LANES), lambda i: (i, 0)),
        compiler_params=_params("parallel"),
        name="router",
    )(x, rwt, bias)


def _moe_kernel(x_ref, g_ref, wg_ref, wu_ref, wd_ref, lw_ref, lb_ref, o_ref, acc_ref, xb_ref):
    e = pl.program_id(1)

    @pl.when(e == 0)
    def _():
        xb_ref[...] = x_ref[...].astype(BF16)
        acc_ref[...] = jnp.zeros_like(acc_ref)

    xb = xb_ref[...]
    hg = jnp.dot(xb, wg_ref[0].astype(BF16), preferred_element_type=F32)
    hu = jnp.dot(xb, wu_ref[0].astype(BF16), preferred_element_type=F32)
    he = _silu(hg) * hu
    gates = g_ref[...]
    lane = lax.broadcasted_iota(jnp.int32, gates.shape, 1)
    ge = jnp.sum(jnp.where(lane == e, gates, 0.0), axis=1, keepdims=True)
    acc_ref[...] += ge * jnp.dot(he.astype(BF16), wd_ref[0].astype(BF16),
                                  preferred_element_type=F32)

    @pl.when(e == N_EXPERTS - 1)
    def _():
        o_ref[...] = _layer_norm(ALPHA * x_ref[...] + acc_ref[...], lw_ref[...], lb_ref[...])


def _moe_ln(x, gates, wg, wu, wd, layer, ln_w, ln_b, tm):
    t = x.shape[0]
    return pl.pallas_call(
        _moe_kernel,
        out_shape=jax.ShapeDtypeStruct((t, D_MODEL), F32),
        grid=(t // tm, N_EXPERTS),
        in_specs=[pl.BlockSpec((tm, D_MODEL), lambda i, e: (i, 0)),
                  pl.BlockSpec((tm, LANES), lambda i, e: (i, 0)),
                  pl.BlockSpec((1, D_MODEL, D_FF_EXPERT), lambda i, e: (layer * N_EXPERTS + e, 0, 0)),
                  pl.BlockSpec((1, D_MODEL, D_FF_EXPERT), lambda i, e: (layer * N_EXPERTS + e, 0, 0)),
                  pl.BlockSpec((1, D_FF_EXPERT, D_MODEL), lambda i, e: (layer * N_EXPERTS + e, 0, 0)),
                  pl.BlockSpec((1, D_MODEL), lambda i, e: (0, 0)),
                  pl.BlockSpec((1, D_MODEL), lambda i, e: (0, 0))],
        out_specs=pl.BlockSpec((tm, D_MODEL), lambda i, e: (i, 0)),
        scratch_shapes=[pltpu.VMEM((tm, D_MODEL), F32), pltpu.VMEM((tm, D_MODEL), BF16)],
        compiler_params=_params("parallel", "arbitrary"),
        name="moe_ln",
    )(x, gates, wg, wu, wd, ln_w, ln_b)


def _group_rank_kernel(g_ref, rank_ref, tot_ref, carry_ref):
    n = g_ref.shape[0]

    @pl.when(pl.program_id(0) == 0)
    def _():
        carry_ref[...] = jnp.zeros_like(carry_ref)

    grp = g_ref[:, N_EXPERTS:N_EXPERTS + 1].astype(jnp.int32)
    lane = lax.broadcasted_iota(jnp.int32, (n, LANES), 1)
    onehot = jnp.where(lane == grp, 1.0, 0.0)
    rr = lax.broadcasted_iota(jnp.int32, (n, n), 0)
    cc = lax.broadcasted_iota(jnp.int32, (n, n), 1)
    before = jnp.where(rr > cc, 1.0, 0.0).astype(BF16)
    earlier = jnp.dot(before, onehot.astype(BF16), preferred_element_type=F32) + carry_ref[...]
    rank = jnp.sum(onehot * earlier, axis=1, keepdims=True)
    rank_ref[...] = jnp.broadcast_to(rank, (n, LANES))
    carry_ref[...] += jnp.sum(onehot, axis=0, keepdims=True)
    tot_ref[...] = jnp.broadcast_to(carry_ref[...], tot_ref.shape)


def _group_rank(gmat, tr):
    t = gmat.shape[0]
    return pl.pallas_call(
        _group_rank_kernel,
        out_shape=(jax.ShapeDtypeStruct((t, LANES), F32), jax.ShapeDtypeStruct((8, LANES), F32)),
        grid=(t // tr,),
        in_specs=[pl.BlockSpec((tr, LANES), lambda i: (i, 0))],
        out_specs=(pl.BlockSpec((tr, LANES), lambda i: (i, 0)),
                   pl.BlockSpec((8, LANES), lambda i: (0, 0))),
        scratch_shapes=[pltpu.VMEM((1, LANES), F32)],
        compiler_params=_params("arbitrary"),
        name="group_rank",
    )(gmat)


def _row_copy(src, dst, sem):
    return pltpu.make_async_copy(src, dst, sem)


def _scatter_rows_kernel(dest_ref, x_ref, init_ref, o_hbm, buf_ref, sem):
    del init_ref
    n = x_ref.shape[0]
    base = pl.program_id(0) * n
    for s in range(D_MODEL // LANES):
        buf_ref[:, s, :] = x_ref[:, s * LANES:(s + 1) * LANES]

    def start(i, carry):
        _row_copy(buf_ref.at[i], o_hbm.at[dest_ref[base + i]], sem).start()
        return carry

    lax.fori_loop(0, n, start, 0)
    _row_copy(buf_ref, o_hbm.at[pl.ds(0, n)], sem).wait()


def _scatter_rows(x, dest, n_out, tr):
    t = x.shape[0]
    slabs = D_MODEL // LANES
    init = jnp.zeros((n_out, slabs, LANES), F32)
    return pl.pallas_call(
        _scatter_rows_kernel,
        out_shape=jax.ShapeDtypeStruct((n_out, slabs, LANES), F32),
        grid_spec=pltpu.PrefetchScalarGridSpec(
            num_scalar_prefetch=1,
            grid=(t // tr,),
            in_specs=[pl.BlockSpec((tr, D_MODEL), lambda i, d: (i, 0)),
                      pl.BlockSpec(memory_space=pl.ANY)],
            out_specs=pl.BlockSpec(memory_space=pl.ANY),
            scratch_shapes=[pltpu.VMEM((tr, slabs, LANES), F32), pltpu.SemaphoreType.DMA(())]),
        input_output_aliases={2: 0},
        compiler_params=_params("arbitrary"),
        name="scatter_rows",
    )(dest, x, init)


def _gather_rows_kernel(src_ref, y_hbm, o_ref, buf_ref, sems):
    n = o_ref.shape[0]
    i = pl.program_id(0)
    slot = i % 2

    def issue(step, to_slot):
        def start(r, carry):
            _row_copy(y_hbm.at[src_ref[step * n + r]], buf_ref.at[to_slot, r],
                      sems.at[to_slot]).start()
            return carry
        lax.fori_loop(0, n, start, 0)

    @pl.when(i == 0)
    def _():
        issue(0, 0)

    @pl.when(i + 1 < pl.num_programs(0))
    def _():
        issue(i + 1, 1 - slot)

    _row_copy(y_hbm.at[pl.ds(0, n)], buf_ref.at[slot], sems.at[slot]).wait()
    cur = buf_ref.at[slot]
    for s in range(D_MODEL // LANES):
        o_ref[:, s * LANES:(s + 1) * LANES] = cur[:, s, :]


def _gather_rows(y3, src, tr):
    t = src.shape[0]
    slabs = D_MODEL // LANES
    return pl.pallas_call(
        _gather_rows_kernel,
        out_shape=jax.ShapeDtypeStruct((t, D_MODEL), F32),
        grid_spec=pltpu.PrefetchScalarGridSpec(
            num_scalar_prefetch=1,
            grid=(t // tr,),
            in_specs=[pl.BlockSpec(memory_space=pl.ANY)],
            out_specs=pl.BlockSpec((tr, D_MODEL), lambda i, d: (i, 0)),
            scratch_shapes=[pltpu.VMEM((2, tr, slabs, LANES), F32),
                            pltpu.SemaphoreType.DMA((2,))]),
        compiler_params=_params("arbitrary"),
        name="gather_rows",
    )(src, y3)


def _moe_group_kernel(tg_ref, x3_ref, rwt_ref, rb_ref, wg_ref, wu_ref, wd_ref, lw_ref, lb_ref,
                      o3_ref, acc_ref, x_ref, xb_ref, gate_ref):
    i = pl.program_id(0)
    j = pl.program_id(1)
    group = tg_ref[i]
    slabs = D_MODEL // LANES

    @pl.when(group < 0)
    def _():
        o3_ref[...] = jnp.zeros_like(o3_ref)

    @pl.when(group >= 0)
    def _():
        @pl.when(j == 0)
        def _():
            for s in range(slabs):
                x_ref[:, s * LANES:(s + 1) * LANES] = x3_ref[:, s, :]
            x = x_ref[...]
            xb_ref[...] = x.astype(BF16)
            acc_ref[...] = jnp.zeros_like(acc_ref)
            gate_ref[...] = _route_in_group(x, rwt_ref[...], rb_ref[...], group).T

        xb = xb_ref[...]
        hg = jnp.dot(xb, wg_ref[0].astype(BF16), preferred_element_type=F32)
        hu = jnp.dot(xb, wu_ref[0].astype(BF16), preferred_element_type=F32)
        he = _silu(hg) * hu
        gates = gate_ref[...]
        lane = lax.broadcasted_iota(jnp.int32, gates.shape, 1)
        ge = jnp.sum(jnp.where(lane == j, gates, 0.0), axis=1, keepdims=True)
        acc_ref[...] += ge * jnp.dot(he.astype(BF16), wd_ref[0].astype(BF16),
                                     preferred_element_type=F32)

        @pl.when(j == EXPERTS_PER_GROUP - 1)
        def _():
            y = _layer_norm(ALPHA * x_ref[...] + acc_ref[...], lw_ref[...], lb_ref[...])
            for s in range(slabs):
                o3_ref[:, s, :] = y[:, s * LANES:(s + 1) * LANES]


def _moe_group_ln(xs3, tile_group, rwt, rbias, wg, wu, wd, layer, ln_w, ln_b, tm):
    n = xs3.shape[0]
    slabs = D_MODEL // LANES
    expert = lambda i, j, tg: (layer * N_EXPERTS + jnp.maximum(tg[i], 0) * EXPERTS_PER_GROUP + j, 0, 0)
    fixed = lambda i, j, tg: (0, 0)
    return pl.pallas_call(
        _moe_group_kernel,
        out_shape=jax.ShapeDtypeStruct((n, slabs, LANES), F32),
        grid_spec=pltpu.PrefetchScalarGridSpec(
            num_scalar_prefetch=1,
            grid=(n // tm, EXPERTS_PER_GROUP),
            in_specs=[pl.BlockSpec((tm, slabs, LANES), lambda i, j, tg: (i, 0, 0)),
                      pl.BlockSpec((N_EXPERTS, D_MODEL), fixed),
                      pl.BlockSpec((N_EXPERTS, 1), fixed),
                      pl.BlockSpec((1, D_MODEL, D_FF_EXPERT), expert),
                      pl.BlockSpec((1, D_MODEL, D_FF_EXPERT), expert),
                      pl.BlockSpec((1, D_FF_EXPERT, D_MODEL), expert),
                      pl.BlockSpec((1, D_MODEL), fixed),
                      pl.BlockSpec((1, D_MODEL), fixed)],
            out_specs=pl.BlockSpec((tm, slabs, LANES), lambda i, j, tg: (i, 0, 0)),
            scratch_shapes=[pltpu.VMEM((tm, D_MODEL), F32), pltpu.VMEM((tm, D_MODEL), F32),
                            pltpu.VMEM((tm, D_MODEL), BF16), pltpu.VMEM((tm, LANES), F32)]),
        compiler_params=_params("arbitrary", "arbitrary"),
        name="moe_group_ln",
    )(tile_group, xs3, rwt, rbias, wg, wu, wd, ln_w, ln_b)


def _ffn_sorted(x, p, layer, tm, tm_moe):
    t = x.shape[0]
    n_tiles = t // tm_moe + N_GROUPS
    gmat = _router(x, p['rwt'], p['rbias'], tm, with_gates=False)
    rank_mat, totals = _group_rank(gmat, tm)
    group = gmat[:, N_EXPERTS].astype(jnp.int32)
    counts = totals[0, :N_GROUPS].astype(jnp.int32)
    seg_tiles = (counts + tm_moe - 1) // tm_moe
    seg_end = jnp.cumsum(seg_tiles)
    seg_start = seg_end - seg_tiles
    is_group = group[:, None] == jnp.arange(N_GROUPS, dtype=jnp.int32)[None, :]
    dest = (jnp.sum(jnp.where(is_group, seg_start[None, :], 0), axis=1) * tm_moe
            + rank_mat[:, 0].astype(jnp.int32))
    tile_id = jnp.arange(n_tiles, dtype=jnp.int32)
    tile_group = jnp.sum((tile_id[:, None] >= seg_end[None, :]).astype(jnp.int32), axis=1)
    tile_group = jnp.where(tile_id < seg_end[N_GROUPS - 1], tile_group, -1)
    xs3 = _scatter_rows(x, dest, n_tiles * tm_moe, tm)
    ys3 = _moe_group_ln(xs3, tile_group, p['rwt'], p['rbias'], p['w_gate'], p['w_up'], p['w_down'],
                        layer, p['ln2_w'][layer], p['ln2_b'][layer], tm_moe)
    return _gather_rows(ys3, dest, tm)


def _prefix_selector():
    n = VEC_TILE
    nsub = VEC_CHUNK // VEC_SUB
    t = np.arange(n)[:, None]
    s = np.arange(n)[None, :]
    incl = ((t // VEC_CHUNK) == (s // VEC_CHUNK)) & ((s % VEC_CHUNK) <= (t % VEC_CHUNK))
    r = np.arange((n // VEC_CHUNK) * nsub)[:, None]
    starts = ((r // nsub) == (s // VEC_CHUNK)) & ((s % VEC_CHUNK) < VEC_SUB * (r % nsub))
    return np.concatenate([incl, starts], axis=0).astype(np.float32)


def _vec_heads(heads, sel, mm):
    n = VEC_TILE
    nsub = VEC_CHUNK // VEC_SUB
    nchunk = n // VEC_CHUNK
    nrows = heads[0][0].shape[0]
    kdim = heads[0][0].shape[1]
    streams = [(h, i) for h in range(len(heads)) for i in range(0, nrows, n)]
    tile = lambda h, i, which: heads[h][which][i:i + n]

    prefs = [_dot_exact_rhs(sel, tile(h, i, 3)) for h, i in streams]
    rows_of = lambda fn, m: jnp.concatenate(
        [jnp.broadcast_to(fn(j), (m, kdim)) for j in range(n // m)], axis=0)
    sub = (lax.broadcasted_iota(jnp.int32, (n, kdim), 0) // VEC_SUB) % nsub
    q_cat, k_cat, q_dec0, updates = [], [], [], []
    for (h, i), pref in zip(streams, prefs):
        q, k, v = tile(h, i, 0), tile(h, i, 1), tile(h, i, 2)
        big_g = pref[0:n]
        start = lambda c, j, pref=pref: pref[n + c * nsub + j:n + c * nsub + j + 1]
        q_dec = [q * jnp.exp(big_g)]
        for j in range(1, nsub):
            base_j = rows_of(lambda c: start(c, j), VEC_CHUNK)
            q_dec.append(q * jnp.exp(jnp.minimum(big_g - base_j, 0.0)))
        base_own = rows_of(lambda m: start(m // nsub, m % nsub), VEC_SUB)
        k_rel = k * jnp.exp(base_own - big_g)
        k_cat.append(jnp.concatenate([jnp.where(sub == j, k_rel, 0.0) for j in range(nsub)],
                                     axis=1))
        q_cat.append(jnp.concatenate(q_dec, axis=1))
        q_dec0.append(q_dec[0])
        per_chunk = []
        for c in range(nchunk):
            rows = slice(c * VEC_CHUNK, (c + 1) * VEC_CHUNK)
            g_last = big_g[(c + 1) * VEC_CHUNK - 1:(c + 1) * VEC_CHUNK, :]
            kd = k[rows] * jnp.exp(g_last - big_g[rows])
            per_chunk.append((jnp.exp(g_last), mm.tn(v[rows], kd)))
        updates.append(per_chunk)
    scores = [mm.nt(qc, kc) for qc, kc in zip(q_cat, k_cat)]
    rr = lax.broadcasted_iota(jnp.int32, (n, n), 0)
    cc = lax.broadcasted_iota(jnp.int32, (n, n), 1)
    keep = (rr >= cc) & ((rr // VEC_CHUNK) == (cc // VEC_CHUNK))
    intra = [mm.nn(jnp.where(keep, sc, 0.0), tile(h, i, 2)) for (h, i), sc in zip(streams, scores)]

    states = [hd[4] for hd in heads]
    o_rows = [[] for _ in heads]
    for si, (h, i) in enumerate(streams):
        for c, (decay_last, update) in enumerate(updates[si]):
            rows = slice(c * VEC_CHUNK, (c + 1) * VEC_CHUNK)
            o_rows[h].append(intra[si][rows] + mm.nt(q_dec0[si][rows], states[h]))
            states[h] = states[h] * decay_last + update
    return [(jnp.concatenate(o_rows[h], axis=0), states[h]) for h in range(len(heads))]


def _gla_chunk_kernel(q_ref, k_ref, v_ref, go_ref, sm_ref, w2_ref, b2_ref, nw_ref, sel_ref,
                      o_ref, s_ref, st_ref):
    r = pl.program_id(2)

    @pl.when(r == 0)
    def _():
        st_ref[...] = jnp.zeros_like(st_ref)

    sm = sm_ref[...]
    heads = []
    for hh in range(VEC_HPS):
        kc = slice(hh * GLA_DK, (hh + 1) * GLA_DK)
        vc = slice(hh * GLA_DV, (hh + 1) * GLA_DV)
        gk = _log_sigmoid(_ThreePass.nn(sm, w2_ref[hh]) + b2_ref[hh]) / GLA_NORMALIZER
        heads.append((q_ref[:, kc] * (GLA_DK ** -0.5), k_ref[:, kc], v_ref[:, vc], gk, st_ref[hh]))
    finals = []
    for hh, (o, st) in enumerate(_vec_heads(heads, sel_ref[...], _ThreePass)):
        vc = slice(hh * GLA_DV, (hh + 1) * GLA_DV)
        o_ref[:, vc] = _rms(o, nw_ref[...]) * _silu(go_ref[:, vc])
        st_ref[hh] = st
        finals.append(st)

    @pl.when(r == pl.num_programs(2) - 1)
    def _():
        for hh in range(VEC_HPS):
            s_ref[0, hh] = finals[hh].T


def _gla_prompt(proj, w2p, b2, norm_w, sel, bsz, seq):
    nr = seq // VEC_ROWS
    ng = GLA_HEADS // VEC_HPS
    kw = VEC_HPS * GLA_DK
    vw = VEC_HPS * GLA_DV
    row = lambda off: (lambda b, h, r: (b * nr + r, off + h))
    return pl.pallas_call(
        _gla_chunk_kernel,
        out_shape=(jax.ShapeDtypeStruct((bsz * seq, GLA_VAL), F32),
                   jax.ShapeDtypeStruct((bsz, GLA_HEADS, GLA_DK, GLA_DV), F32)),
        grid=(bsz, ng, nr),
        in_specs=[pl.BlockSpec((VEC_ROWS, kw), row(AB_Q // kw)),
                  pl.BlockSpec((VEC_ROWS, kw), row(AB_K // kw)),
                  pl.BlockSpec((VEC_ROWS, vw), row(AB_V // vw)),
                  pl.BlockSpec((VEC_ROWS, vw), row(AB_GOUT // vw)),
                  pl.BlockSpec((VEC_ROWS, LANES), lambda b, h, r: (b * nr + r, AB_SMALL // LANES)),
                  pl.BlockSpec((VEC_HPS, LANES, GLA_DK), lambda b, h, r: (h, 0, 0)),
                  pl.BlockSpec((VEC_HPS, 1, GLA_DK), lambda b, h, r: (h, 0, 0)),
                  pl.BlockSpec((1, GLA_DV), lambda b, h, r: (0, 0)),
                  pl.BlockSpec(sel.shape, lambda b, h, r: (0, 0))],
        out_specs=(pl.BlockSpec((VEC_ROWS, vw), lambda b, h, r: (b * nr + r, h)),
                   pl.BlockSpec((1, VEC_HPS, GLA_DK, GLA_DV), lambda b, h, r: (b, h, 0, 0))),
        scratch_shapes=[pltpu.VMEM((VEC_HPS, GLA_DV, GLA_DK), F32)],
        compiler_params=_params("parallel", "parallel", "arbitrary"),
        name="gla_chunk",
    )(proj, proj, proj, proj, proj, w2p, b2, norm_w, sel)


def _hgrn_lower_bound(lbraw, layer):
    m = jnp.max(lbraw, axis=0, keepdims=True)
    ex = jnp.exp(lbraw - m)
    sm = ex / jnp.sum(ex, axis=0, keepdims=True)
    acc = sm[0:1]
    for i in range(1, layer + 1):
        acc = acc + sm[i:i + 1]
    return acc - sm[0:1]


def _hgrn_gates(q_raw, f_raw, lb):
    forget = lb + (1.0 - lb) * _sigmoid(f_raw)
    return _silu(q_raw), 1.0 - forget, jnp.log(forget)


def _hgrn_chunk_kernel(layer, q_ref, f_ref, i_ref, go_ref, lb_ref, nw_ref, sel_ref,
                       o_ref, s_ref, st_ref):
    r = pl.program_id(2)

    @pl.when(r == 0)
    def _():
        st_ref[...] = jnp.zeros_like(st_ref)

    lb_all = _hgrn_lower_bound(lb_ref[...], layer)
    heads = []
    for hh in range(VEC_HPS):
        kc = slice(hh * HG_EXPAND, (hh + 1) * HG_EXPAND)
        vc = slice(hh * HG_DI, (hh + 1) * HG_DI)
        q, k, g = _hgrn_gates(q_ref[:, kc], f_ref[:, kc], lb_all[:, kc])
        heads.append((q, k, i_ref[:, vc], g, st_ref[hh]))
    finals = []
    for hh, (o, st) in enumerate(_vec_heads(heads, sel_ref[...], _OnePass)):
        vc = slice(hh * HG_DI, (hh + 1) * HG_DI)
        o_ref[:, vc] = _rms(o, nw_ref[...]) * _silu(go_ref[:, vc])
        st_ref[hh] = st
        finals.append(st)

    @pl.when(r == pl.num_programs(2) - 1)
    def _():
        for hh in range(VEC_HPS):
            s_ref[0, hh] = finals[hh].T


def _hgrn_prompt(proj, lower_bounds, norm_w, sel, layer, bsz, seq):
    nr = seq // VEC_ROWS
    ng = HG_HEADS // VEC_HPS
    kw = VEC_HPS * HG_EXPAND
    vw = VEC_HPS * HG_DI
    row = lambda off: (lambda b, h, r: (b * nr + r, off + h))
    return pl.pallas_call(
        functools.partial(_hgrn_chunk_kernel, layer),
        out_shape=(jax.ShapeDtypeStruct((bsz * seq, HG_I), F32),
                   jax.ShapeDtypeStruct((bsz, HG_HEADS, HG_EXPAND, HG_DI), F32)),
        grid=(bsz, ng, nr),
        in_specs=[pl.BlockSpec((VEC_ROWS, kw), row(0)),
                  pl.BlockSpec((VEC_ROWS, kw), row(ng)),
                  pl.BlockSpec((VEC_ROWS, vw), row(2 * ng)),
                  pl.BlockSpec((VEC_ROWS, vw), row(3 * ng)),
                  pl.BlockSpec((DEPTH, kw), lambda b, h, r: (0, h)),
                  pl.BlockSpec((1, HG_DI), lambda b, h, r: (0, 0)),
                  pl.BlockSpec(sel.shape, lambda b, h, r: (0, 0))],
        out_specs=(pl.BlockSpec((VEC_ROWS, vw), lambda b, h, r: (b * nr + r, h)),
                   pl.BlockSpec((1, VEC_HPS, HG_EXPAND, HG_DI), lambda b, h, r: (b, h, 0, 0))),
        scratch_shapes=[pltpu.VMEM((VEC_HPS, HG_DI, HG_EXPAND), F32)],
        compiler_params=_params("parallel", "parallel", "arbitrary"),
        name="hgrn_chunk",
    )(proj, proj, proj, proj, lower_bounds, norm_w, sel)


def _conv_silu(xp, cw, cb, n, lead):
    acc = cb + cw[SSD_CONV - 1:SSD_CONV] * xp[lead:lead + n]
    for m in range(1, SSD_CONV):
        acc = acc + cw[SSD_CONV - 1 - m:SSD_CONV - m] * xp[lead - m:lead - m + n]
    return _silu(acc)


def _ssd_gate_norm(y, z, nw):
    yz = y * _silu(z)
    parts = []
    for g in range(SSD_GROUPS):
        cols = slice(g * SSD_GROUP_W, (g + 1) * SSD_GROUP_W)
        parts.append(_rms(yz[:, cols], nw[:, cols]))
    return jnp.concatenate(parts, axis=1)


def _ssd_chunk_kernel(z_ref, xbc_ref, sm_ref, cw_ref, cb_ref, dtb_ref, alog_ref, dsk_ref,
                      nw_ref, ex_ref, o_ref, s_ref, conv_ref, st_ref, prev_ref):
    r = pl.program_id(1)
    c = SSD_CHUNK
    mm = _ThreePass

    @pl.when(r == 0)
    def _():
        st_ref[...] = jnp.zeros_like(st_ref)
        prev_ref[...] = jnp.zeros_like(prev_ref)

    x_raw = xbc_ref[...]
    xp = jnp.concatenate([prev_ref[...], x_raw], axis=0)
    prev_ref[...] = x_raw[c - 8:c]
    xc = _conv_silu(xp, cw_ref[...], cb_ref[...], c, 8)
    xs = xc[:, :SSD_INNER]
    bm = xc[:, SSD_INNER:SSD_INNER + SSD_BC]
    cm = xc[:, SSD_INNER + SSD_BC:]

    dt = _softplus(sm_ref[...] + dtb_ref[...])
    a_neg = -jnp.exp(alog_ref[...])
    big_g = _dot_exact_rhs(_tril(c).astype(BF16), dt * a_neg)
    g_t = big_g.T
    g_last = big_g[c - 1:c, :]
    ex = ex_ref[...]
    dt_x = _dot_exact_lhs(dt, ex)
    eg_x = _dot_exact_lhs(jnp.exp(big_g), ex)
    w_x = _dot_exact_lhs(dt * jnp.exp(g_last - big_g), ex)
    xdt = xs * dt_x
    xw = xs * w_x
    causal = _tril(c)
    lane = lax.broadcasted_iota(jnp.int32, (c, LANES), 1)
    st = st_ref[...]
    y_parts = []
    u_parts = []
    for g in range(SSD_GROUPS):
        gcols = slice(g * SSD_GROUP_W, (g + 1) * SSD_GROUP_W)
        bg = bm[:, g * SSD_STATE:(g + 1) * SSD_STATE]
        cg = cm[:, g * SSD_STATE:(g + 1) * SSD_STATE]
        sc = mm.nt(cg, bg)
        inter = mm.nn(cg, st[:, gcols])
        u_parts.append(mm.tn(bg, xw[:, gcols]))
        pair_cols = []
        heads_per_group = SSD_HEADS // SSD_GROUPS
        for p in range(heads_per_group // 2):
            h0 = g * heads_per_group + 2 * p
            xpair = xdt[:, h0 * SSD_HEADDIM:(h0 + 2) * SSD_HEADDIM]
            ws = []
            for h in (h0, h0 + 1):
                diff = big_g[:, h:h + 1] - g_t[h:h + 1, :]
                ws.append(sc * jnp.exp(jnp.where(causal, diff, -jnp.inf)))
            x_diag = jnp.concatenate([jnp.where(lane < SSD_HEADDIM, xpair, 0.0),
                                      jnp.where(lane < SSD_HEADDIM, 0.0, xpair)], axis=0)
            pair_cols.append(mm.nn(jnp.concatenate(ws, axis=1), x_diag))
        y_intra = jnp.concatenate(pair_cols, axis=1)
        y_parts.append(y_intra + inter * eg_x[:, gcols])
    y = jnp.concatenate(y_parts, axis=1) + dsk_ref[...] * xs
    o_ref[...] = _ssd_gate_norm(y, z_ref[...], nw_ref[...])
    st = st * eg_x[c - 1:c, :] + jnp.concatenate(u_parts, axis=1)
    st_ref[...] = st

    @pl.when(r == pl.num_programs(1) - 1)
    def _():
        s_ref[0] = st
        conv_ref[0] = x_raw[c - (SSD_CONV - 1):c]


def _ssd_prompt(proj, conv_w, conv_b, dtb_p, alog_p, dskip_x, norm_w, expand, bsz, seq):
    nr = seq // SSD_CHUNK
    fixed = lambda b, r: (0, 0)
    return pl.pallas_call(
        _ssd_chunk_kernel,
        out_shape=(jax.ShapeDtypeStruct((bsz * seq, SSD_INNER), F32),
                   jax.ShapeDtypeStruct((bsz, SSD_STATE, SSD_INNER), F32),
                   jax.ShapeDtypeStruct((bsz, SSD_CONV - 1, SSD_CONV_DIM), F32)),
        grid=(bsz, nr),
        in_specs=[pl.BlockSpec((SSD_CHUNK, SSD_INNER), lambda b, r: (b * nr + r, AB_Z // SSD_INNER)),
                  pl.BlockSpec((SSD_CHUNK, SSD_CONV_DIM), lambda b, r: (b * nr + r, AB_XBC // SSD_CONV_DIM)),
                  pl.BlockSpec((SSD_CHUNK, LANES), lambda b, r: (b * nr + r, AB_SMALL // LANES)),
                  pl.BlockSpec((SSD_CONV, SSD_CONV_DIM), fixed),
                  pl.BlockSpec((1, SSD_CONV_DIM), fixed),
                  pl.BlockSpec((1, LANES), fixed),
                  pl.BlockSpec((1, LANES), fixed),
                  pl.BlockSpec((1, SSD_INNER), fixed),
                  pl.BlockSpec((1, SSD_INNER), fixed),
                  pl.BlockSpec((LANES, SSD_INNER), fixed)],
        out_specs=(pl.BlockSpec((SSD_CHUNK, SSD_INNER), lambda b, r: (b * nr + r, 0)),
                   pl.BlockSpec((1, SSD_STATE, SSD_INNER), lambda b, r: (b, 0, 0)),
                   pl.BlockSpec((1, SSD_CONV - 1, SSD_CONV_DIM), lambda b, r: (b, 0, 0))),
        scratch_shapes=[pltpu.VMEM((SSD_STATE, SSD_INNER), F32),
                        pltpu.VMEM((8, SSD_CONV_DIM), F32)],
        compiler_params=_params("parallel", "arbitrary"),
        name="ssd_chunk",
    )(proj, proj, proj, conv_w, conv_b, dtb_p, alog_p, dskip_x, norm_w, expand)


def _ab_prep_kernel(q_ref, sm_ref, xbc_ref, cs_ref, w2_ref, b2_ref, cw_ref, cb_ref, dtb_ref,
                    alog_ref, ex_ref, qs_ref, dec_ref, xc_ref, xdt_ref, dax_ref, cs_out_ref):
    sm = sm_ref[...]
    gk = _log_sigmoid(_ThreePass.nn(sm, w2_ref[...]) + b2_ref[...]) / GLA_NORMALIZER
    qs_ref[...] = q_ref[...] * (GLA_DK ** -0.5)
    dec_ref[...] = jnp.exp(gk)
    cw = cw_ref[...]
    x_raw = xbc_ref[...]
    acc = cb_ref[...] + cw[SSD_CONV - 1:SSD_CONV] * x_raw
    for j in range(SSD_CONV - 1):
        acc = acc + cw[j:j + 1] * cs_ref[j]
    xc = _silu(acc)
    xc_ref[...] = xc
    for j in range(SSD_CONV - 2):
        cs_out_ref[j] = cs_ref[j + 1]
    cs_out_ref[SSD_CONV - 2] = x_raw
    dt = _softplus(sm + dtb_ref[...])
    ex = ex_ref[...]
    xdt_ref[...] = xc[:, :SSD_INNER] * _dot_exact_lhs(dt, ex)
    dax_ref[...] = _dot_exact_lhs(jnp.exp(dt * -jnp.exp(alog_ref[...])), ex)


def _ab_prep(proj, conv_state, w2_wide, b2_wide, conv_w, conv_b, dtb_p, alog_p, expand):
    bsz = proj.shape[0]
    fixed = lambda i: (0, 0)
    sds = jax.ShapeDtypeStruct
    return pl.pallas_call(
        _ab_prep_kernel,
        out_shape=(sds((bsz, GLA_KEY), F32), sds((bsz, GLA_KEY), F32),
                   sds((bsz, SSD_CONV_DIM), F32), sds((bsz, SSD_INNER), F32),
                   sds((bsz, SSD_INNER), F32),
                   sds((SSD_CONV - 1, bsz, SSD_CONV_DIM), F32)),
        grid=(1,),
        in_specs=[pl.BlockSpec((bsz, GLA_KEY), lambda i: (0, AB_Q // GLA_KEY)),
                  pl.BlockSpec((bsz, LANES), lambda i: (0, AB_SMALL // LANES)),
                  pl.BlockSpec((bsz, SSD_CONV_DIM), lambda i: (0, AB_XBC // SSD_CONV_DIM)),
                  pl.BlockSpec((SSD_CONV - 1, bsz, SSD_CONV_DIM), lambda i: (0, 0, 0)),
                  pl.BlockSpec((LANES, GLA_KEY), fixed),
                  pl.BlockSpec((1, GLA_KEY), fixed),
                  pl.BlockSpec((SSD_CONV, SSD_CONV_DIM), fixed),
                  pl.BlockSpec((1, SSD_CONV_DIM), fixed),
                  pl.BlockSpec((1, LANES), fixed),
                  pl.BlockSpec((1, LANES), fixed),
                  pl.BlockSpec((LANES, SSD_INNER), fixed)],
        out_specs=(pl.BlockSpec((bsz, GLA_KEY), fixed), pl.BlockSpec((bsz, GLA_KEY), fixed),
                   pl.BlockSpec((bsz, SSD_CONV_DIM), fixed), pl.BlockSpec((bsz, SSD_INNER), fixed),
                   pl.BlockSpec((bsz, SSD_INNER), fixed),
                   pl.BlockSpec((SSD_CONV - 1, bsz, SSD_CONV_DIM), lambda i: (0, 0, 0))),
        compiler_params=_params("arbitrary"),
        name="ab_prep",
    )(proj, proj, proj, conv_state, w2_wide, b2_wide, conv_w, conv_b, dtb_p, alog_p, expand)


def _hgrn_prep_kernel(layer, q_ref, f_ref, lb_ref, qs_ref, k_ref, dec_ref):
    lb = _hgrn_lower_bound(lb_ref[...], layer)
    forget = lb + (1.0 - lb) * _sigmoid(f_ref[...])
    qs_ref[...] = _silu(q_ref[...])
    k_ref[...] = 1.0 - forget
    dec_ref[...] = jnp.exp(jnp.log(forget))


def _hgrn_prep(proj, lower_bounds, layer):
    bsz = proj.shape[0]
    blk = lambda j: pl.BlockSpec((bsz, HG_F), lambda i: (0, j))
    return pl.pallas_call(
        functools.partial(_hgrn_prep_kernel, layer),
        out_shape=tuple(jax.ShapeDtypeStruct((bsz, HG_F), F32) for _ in range(3)),
        grid=(1,),
        in_specs=[blk(0), blk(1), pl.BlockSpec((DEPTH, HG_F), lambda i: (0, 0))],
        out_specs=tuple(blk(0) for _ in range(3)),
        compiler_params=_params("arbitrary"),
        name="hgrn_prep",
    )(proj, proj, lower_bounds)


def _vec_step_kernel(mm, s_ref, q_ref, k_ref, d_ref, v_ref, go_ref, nw_ref, so_ref, o_ref):
    q = q_ref[...]
    kt = k_ref[0, 0]
    dt = d_ref[0, 0]
    v = v_ref[...]
    row = lax.broadcasted_iota(jnp.int32, v.shape, 0)
    new = []
    for b in range(STEP_B):
        only_b = row == b
        decay = _dot_exact_lhs(dt, jnp.where(only_b, 1.0, 0.0).astype(BF16))
        new.append(s_ref[b, 0] * decay + mm.nn(kt, jnp.where(only_b, v, 0.0)))
    for b in range(STEP_B):
        so_ref[b, 0] = new[b]
    o = jnp.concatenate([mm.nn(q, new[b])[b:b + 1] for b in range(STEP_B)], axis=0)
    o_ref[...] = _rms(o, nw_ref[...]) * _silu(go_ref[...])


def _vec_step(state, q_rows, k_cols, d_cols, vsrc, v_off, gsrc, g_off, norm_w, mm):
    bsz, nh, kdim, vdim = state.shape
    col = lambda j, h: (h, j, 0, 0)
    return pl.pallas_call(
        functools.partial(_vec_step_kernel, mm),
        out_shape=(jax.ShapeDtypeStruct(state.shape, F32),
                   jax.ShapeDtypeStruct((bsz, nh * vdim), F32)),
        grid=(bsz // STEP_B, nh),
        in_specs=[pl.BlockSpec((STEP_B, 1, kdim, vdim), lambda j, h: (j, h, 0, 0)),
                  pl.BlockSpec((STEP_B, kdim), lambda j, h: (j, h)),
                  pl.BlockSpec((1, 1, kdim, STEP_B), col),
                  pl.BlockSpec((1, 1, kdim, STEP_B), col),
                  pl.BlockSpec((STEP_B, vdim), lambda j, h: (j, v_off + h)),
                  pl.BlockSpec((STEP_B, vdim), lambda j, h: (j, g_off + h)),
                  pl.BlockSpec((1, vdim), lambda j, h: (0, 0))],
        out_specs=(pl.BlockSpec((STEP_B, 1, kdim, vdim), lambda j, h: (j, h, 0, 0)),
                   pl.BlockSpec((STEP_B, vdim), lambda j, h: (j, h))),
        compiler_params=_params("parallel", "parallel"),
        name="vec_step",
    )(state, q_rows, k_cols, d_cols, vsrc, gsrc, norm_w)


def _ssd_step_kernel(s_ref, b_ref, c_ref, x_ref, xdt_ref, dax_ref, dsk_ref, so_ref, y_ref):
    mm = _ThreePass
    bt = b_ref[0, 0]
    c = c_ref[...]
    xdt = xdt_ref[...]
    dax = dax_ref[...]
    hpg = SSD_HEADS // SSD_GROUPS
    row = lax.broadcasted_iota(jnp.int32, xdt.shape, 0)
    new = []
    for b in range(STEP_B):
        outer = mm.nn(bt, jnp.where(row == b, xdt, 0.0))
        per_head = []
        for hh in range(hpg):
            cols = slice(hh * SSD_HEADDIM, (hh + 1) * SSD_HEADDIM)
            sn = s_ref[b, hh] * dax[b:b + 1, cols] + outer[:, cols]
            so_ref[b, hh] = sn
            per_head.append(sn)
        new.append(per_head)
    rows = [jnp.concatenate([mm.nn(c, new[b][hh])[b:b + 1] for hh in range(hpg)], axis=1)
            for b in range(STEP_B)]
    y_ref[...] = jnp.concatenate(rows, axis=0) + dsk_ref[...] * x_ref[...]


def _ssd_step(state, b_cols, xc, xdt, dax, dskip_x):
    bsz = state.shape[0]
    hpg = SSD_HEADS // SSD_GROUPS
    grp = lambda j, g: (j, g)
    c_off = (SSD_INNER + SSD_BC) // SSD_STATE
    return pl.pallas_call(
        _ssd_step_kernel,
        out_shape=(jax.ShapeDtypeStruct(state.shape, F32),
                   jax.ShapeDtypeStruct((bsz, SSD_INNER), F32)),
        grid=(bsz // STEP_B, SSD_GROUPS),
        in_specs=[pl.BlockSpec((STEP_B, hpg, SSD_STATE, SSD_HEADDIM), lambda j, g: (j, g, 0, 0)),
                  pl.BlockSpec((1, 1, SSD_STATE, STEP_B), lambda j, g: (g, j, 0, 0)),
                  pl.BlockSpec((STEP_B, SSD_STATE), lambda j, g: (j, c_off + g)),
                  pl.BlockSpec((STEP_B, SSD_GROUP_W), grp),
                  pl.BlockSpec((STEP_B, SSD_GROUP_W), grp),
                  pl.BlockSpec((STEP_B, SSD_GROUP_W), grp),
                  pl.BlockSpec((1, SSD_GROUP_W), lambda j, g: (0, g))],
        out_specs=(pl.BlockSpec((STEP_B, hpg, SSD_STATE, SSD_HEADDIM), lambda j, g: (j, g, 0, 0)),
                   pl.BlockSpec((STEP_B, SSD_GROUP_W), grp)),
        compiler_params=_params("parallel", "parallel"),
        name="ssd_step",
    )(state, b_cols, xc, xc, xdt, dax, dskip_x)


def _ssd_post_kernel(y_ref, z_ref, nw_ref, o_ref):
    o_ref[...] = _ssd_gate_norm(y_ref[...], z_ref[...], nw_ref[...])


def _ssd_post(y, proj, norm_w):
    bsz = y.shape[0]
    return pl.pallas_call(
        _ssd_post_kernel,
        out_shape=jax.ShapeDtypeStruct((bsz, SSD_INNER), F32),
        grid=(1,),
        in_specs=[pl.BlockSpec((bsz, SSD_INNER), lambda i: (0, 0)),
                  pl.BlockSpec((bsz, SSD_INNER), lambda i: (0, AB_Z // SSD_INNER)),
                  pl.BlockSpec((1, SSD_INNER), lambda i: (0, 0))],
        out_specs=pl.BlockSpec((bsz, SSD_INNER), lambda i: (0, 0)),
        compiler_params=_params("arbitrary"),
        name="ssd_post",
    )(y, proj, norm_w)


def _to_cols(a, nh):
    bsz = a.shape[0]
    return a.reshape(bsz // STEP_B, STEP_B, nh, -1).transpose(2, 0, 3, 1)


def _prep_weights(w_in_ab, w_gk2, b_gk2, gla_norm_w, conv_w, conv_b, dt_bias, a_log, d_skip,
                  ssd_norm_w, w_out_ab, w_in_c, hg_norm_w, w_out_c, router_w, router_bias,
                  w_gate, w_up, w_down, ln1_w, ln1_b, ln2_w, ln2_b):
    offs = np.cumsum([0, GLA_KEY, GLA_KEY, GLA_VAL, GLA_VAL, GLA_RANK, SSD_INNER, SSD_CONV_DIM,
                      SSD_HEADS])
    sec = lambda w, i: w[:, offs[i]:offs[i + 1]]
    w = w_in_ab[0]
    pad = jnp.zeros((D_MODEL, LANES - SSD_HEADS - GLA_RANK), w.dtype)
    w_ab = jnp.concatenate([sec(w, 5), sec(w, 2), sec(w, 3), sec(w, 6), sec(w, 0), sec(w, 1),
                            sec(w, 7), sec(w, 4), pad], axis=1)
    hi_lo = lambda m: (m.astype(BF16), (m - m.astype(BF16).astype(F32)).astype(BF16))
    w2_wide = jnp.zeros((LANES, GLA_KEY), F32).at[SSD_HEADS:SSD_HEADS + GLA_RANK].set(w_gk2[0])
    lane_pad = lambda v: jnp.zeros((1, LANES), F32).at[0, :SSD_HEADS].set(v)
    expand = np.zeros((LANES, SSD_INNER), np.float32)
    for h in range(SSD_HEADS):
        expand[h, h * SSD_HEADDIM:(h + 1) * SSD_HEADDIM] = 1.0
    return dict(
        w_ab=hi_lo(w_ab),
        w2_wide=w2_wide,
        w2_heads=w2_wide.reshape(LANES, GLA_HEADS, GLA_DK).transpose(1, 0, 2),
        b2_wide=b_gk2[0].reshape(1, GLA_KEY),
        b2_heads=b_gk2[0].reshape(GLA_HEADS, 1, GLA_DK),
        gla_norm_w=gla_norm_w[0].reshape(1, GLA_DV),
        conv_w=conv_w[0], conv_b=conv_b[0].reshape(1, SSD_CONV_DIM),
        dtb_p=lane_pad(dt_bias[0]), alog_p=lane_pad(a_log[0]),
        dskip_x=jnp.repeat(d_skip[0], SSD_HEADDIM).reshape(1, SSD_INNER),
        ssd_norm_w=ssd_norm_w[0].reshape(1, SSD_INNER),
        expand=jnp.asarray(expand, BF16),
        prefix_sel=jnp.asarray(_prefix_selector(), BF16),
        w_out_gla=hi_lo(w_out_ab[0, :GLA_VAL]),
        w_out_ssd=hi_lo(w_out_ab[0, GLA_VAL:]),
        w_c=w_in_c[0].astype(BF16),
        hg_norm_w=hg_norm_w[0].reshape(1, HG_DI),
        w_out_c=w_out_c[0].astype(BF16),
        rwt=router_w.T,
        rbias=router_bias.reshape(N_EXPERTS, 1),
        w_gate=w_gate.reshape(DEPTH * N_EXPERTS, D_MODEL, D_FF_EXPERT),
        w_up=w_up.reshape(DEPTH * N_EXPERTS, D_MODEL, D_FF_EXPERT),
        w_down=w_down.reshape(DEPTH * N_EXPERTS, D_FF_EXPERT, D_MODEL),
        ln1_w=ln1_w.reshape(DEPTH, 1, D_MODEL), ln1_b=ln1_b.reshape(DEPTH, 1, D_MODEL),
        ln2_w=ln2_w.reshape(DEPTH, 1, D_MODEL), ln2_b=ln2_b.reshape(DEPTH, 1, D_MODEL),
    )


def _ffn(x, p, layer, tm, tm_moe):
    gates = _router(x, p['rwt'], p['rbias'], tm)
    return _moe_ln(x, gates, p['w_gate'], p['w_up'], p['w_down'], layer,
                   p['ln2_w'][layer], p['ln2_b'][layer], tm_moe)


def _ssd_state_from_wide(s_wide):
    bsz = s_wide.shape[0]
    return s_wide.reshape(bsz, SSD_STATE, SSD_HEADS, SSD_HEADDIM).transpose(0, 2, 1, 3)


def _trunk_prompt(x3, p, lower_bounds, tm, tn_ab, tn_c):
    bsz, seq, _ = x3.shape
    x = x3.reshape(bsz * seq, D_MODEL)
    tm_big = 2 * tm
    proj = _proj(x, p['w_ab'], tm_big, tn_ab)
    o_gla, s_gla = _gla_prompt(proj, p['w2_heads'], p['b2_heads'], p['gla_norm_w'],
                               p['prefix_sel'], bsz, seq)
    yz, s_ssd, s_conv = _ssd_prompt(proj, p['conv_w'], p['conv_b'], p['dtb_p'], p['alog_p'],
                                    p['dskip_x'], p['ssd_norm_w'], p['expand'], bsz, seq)
    x = _outproj_ln([o_gla, yz], [p['w_out_gla'], p['w_out_ssd']], x, p['ln1_w'][0], p['ln1_b'][0], tm)
    x = _ffn_sorted(x, p, 0, tm, tm_big)
    proj_c = _proj(x, p['w_c'], tm_big, tn_c)
    o_hg, s_hg = _hgrn_prompt(proj_c, lower_bounds, p['hg_norm_w'], p['prefix_sel'], 1, bsz, seq)
    x = _outproj_ln([o_hg], [p['w_out_c']], x, p['ln1_w'][1], p['ln1_b'][1], tm)
    x = _ffn_sorted(x, p, 1, tm, tm_big)
    return (x.reshape(bsz, seq, D_MODEL), s_gla[None], _ssd_state_from_wide(s_ssd)[None],
            s_conv[None], s_hg[None])


def _trunk_sample(x3, st_gla, st_ssd, st_conv, st_hg, p, lower_bounds, tn_ab, tn_c):
    bsz = x3.shape[0]
    tm = bsz
    x = x3.reshape(bsz, D_MODEL)
    proj = _proj(x, p['w_ab'], tm, tn_ab)
    qs, dec, xc, xdt, dax, conv_new = _ab_prep(proj, st_conv[0].transpose(1, 0, 2), p['w2_wide'],
                                               p['b2_wide'], p['conv_w'], p['conv_b'], p['dtb_p'],
                                               p['alog_p'], p['expand'])
    conv_new = conv_new.transpose(1, 0, 2)
    k_gla = proj[:, AB_K:AB_K + GLA_KEY]
    s_gla, o_gla = _vec_step(st_gla[0], qs, _to_cols(k_gla, GLA_HEADS),
                             _to_cols(dec, GLA_HEADS), proj, AB_V // GLA_DV, proj,
                             AB_GOUT // GLA_DV, p['gla_norm_w'], _ThreePass)
    s_ssd, y = _ssd_step(st_ssd[0], _to_cols(xc[:, SSD_INNER:SSD_INNER + SSD_BC], SSD_GROUPS),
                         xc, xdt, dax, p['dskip_x'])
    yz = _ssd_post(y, proj, p['ssd_norm_w'])
    x = _outproj_ln([o_gla, yz], [p['w_out_gla'], p['w_out_ssd']], x, p['ln1_w'][0], p['ln1_b'][0], tm)
    x = _ffn(x, p, 0, tm, tm)
    proj_c = _proj(x, p['w_c'], tm, tn_c)
    qh, kh, dh = _hgrn_prep(proj_c, lower_bounds, 1)
    s_hg, o_hg = _vec_step(st_hg[0], qh, _to_cols(kh, HG_HEADS),
                           _to_cols(dh, HG_HEADS), proj_c, 2 * HG_HEADS, proj_c, 3 * HG_HEADS,
                           p['hg_norm_w'], _OnePass)
    x = _outproj_ln([o_hg], [p['w_out_c']], x, p['ln1_w'][1], p['ln1_b'][1], tm)
    x = _ffn(x, p, 1, tm, tm)
    return x.reshape(bsz, 1, D_MODEL), s_gla[None], s_ssd[None], conv_new[None], s_hg[None]


def kernel(x_prompt, x_sample, state_gla, state_ssd, state_conv, state_hgrn, w_in_ab, w_gk2, b_gk2, gla_norm_w, conv_w, conv_b, dt_bias, a_log, d_skip, ssd_norm_w, w_out_ab, w_in_c, lower_bounds, hg_norm_w, w_out_c, router_w, router_bias, w_gate, w_up, w_down, ln1_w, ln1_b, ln2_w, ln2_b):
    p = _prep_weights(w_in_ab, w_gk2, b_gk2, gla_norm_w, conv_w, conv_b, dt_bias, a_log, d_skip,
                      ssd_norm_w, w_out_ab, w_in_c, hg_norm_w, w_out_c, router_w, router_bias,
                      w_gate, w_up, w_down, ln1_w, ln1_b, ln2_w, ln2_b)
    y_p, gla_p, ssd_p, conv_p, hg_p = _trunk_prompt(x_prompt, p, lower_bounds, 512, 1152, 1024)
    y_s, gla_s, ssd_s, conv_s, hg_s = _trunk_sample(x_sample, state_gla, state_ssd, state_conv,
                                                    state_hgrn, p, lower_bounds, 1152, 1024)
    return (y_p, y_s, gla_p, ssd_p, conv_p, hg_p, gla_s, ssd_s, conv_s, hg_s)
```

```python
import functools

import numpy as np
import jax
import jax.numpy as jnp
from jax import lax
from jax.experimental import pallas as pl
from jax.experimental.pallas import tpu as pltpu

F32 = jnp.float32
BF16 = jnp.bfloat16

D_MODEL = 1024
DEPTH = 2
GLA_HEADS = 4
GLA_DK = 128
GLA_DV = 256
GLA_KEY = GLA_HEADS * GLA_DK
GLA_VAL = GLA_HEADS * GLA_DV
GLA_RANK = 16
GLA_NORMALIZER = 16.0
SSD_INNER = 1024
SSD_HEADDIM = 64
SSD_HEADS = 16
SSD_STATE = 128
SSD_GROUPS = 2
SSD_CONV = 4
SSD_GROUP_W = SSD_INNER // SSD_GROUPS
SSD_BC = SSD_GROUPS * SSD_STATE
SSD_CONV_DIM = SSD_INNER + 2 * SSD_BC
HG_EXPAND = 128
HG_HEADS = 8
HG_F = HG_HEADS * HG_EXPAND
HG_I = D_MODEL
HG_DI = HG_I // HG_HEADS
N_EXPERTS = 16
N_GROUPS = 4
EXPERTS_PER_GROUP = 4
D_FF_EXPERT = 512
ALPHA = (2 * DEPTH) ** 0.25
EPS = 1e-5

LANES = 128
VMEM_LIMIT = 48 * 1024 * 1024

AB_Z = 0
AB_V = 1024
AB_GOUT = 2048
AB_XBC = 3072
AB_Q = 4608
AB_K = 5120
AB_SMALL = 5632
AB_COLS = 5760
C_COLS = 4096

VEC_CHUNK = 64
VEC_SUB = 16
VEC_TILE = 256
VEC_ROWS = 512
VEC_HPS = 2
SSD_CHUNK = 128
STEP_B = 8


def _params(*sem):
    return pltpu.CompilerParams(dimension_semantics=sem, vmem_limit_bytes=VMEM_LIMIT)


_NN = (((1,), (0,)), ((), ()))
_NT = (((1,), (1,)), ((), ()))
_TN = (((0,), (0,)), ((), ()))


def _dot1(dims, a, b):
    return lax.dot_general(a.astype(BF16), b.astype(BF16), dims, preferred_element_type=F32)


def _split2(a):
    hi = a.astype(BF16)
    return hi, (a - hi.astype(F32)).astype(BF16)


def _dot3(dims, a, b):
    ah, al = _split2(a)
    bh, bl = _split2(b)
    d = lambda x, y: lax.dot_general(x, y, dims, preferred_element_type=F32)
    return (d(al, bh) + d(ah, bl)) + d(ah, bh)


class _OnePass:
    nn = staticmethod(lambda a, b: _dot1(_NN, a, b))
    nt = staticmethod(lambda a, b: _dot1(_NT, a, b))
    tn = staticmethod(lambda a, b: _dot1(_TN, a, b))


class _ThreePass:
    nn = staticmethod(lambda a, b: _dot3(_NN, a, b))
    nt = staticmethod(lambda a, b: _dot3(_NT, a, b))
    tn = staticmethod(lambda a, b: _dot3(_TN, a, b))


def _dot(a, b):
    return _dot1(_NN, a, b)


def _dot_nt(a, b):
    return _dot1(_NT, a, b)


def _dot_tn(a, b):
    return _dot1(_TN, a, b)


def _split3(a):
    hi = a.astype(BF16)
    r1 = a - hi.astype(F32)
    mid = r1.astype(BF16)
    lo = (r1 - mid.astype(F32)).astype(BF16)
    return hi, mid, lo


def _dot_exact_rhs(sel, a):
    hi, mid, lo = _split3(a)
    d = lambda p: jnp.dot(sel, p, preferred_element_type=F32)
    return (d(lo) + d(mid)) + d(hi)


def _dot_exact_lhs(a, sel):
    hi, mid, lo = _split3(a)
    d = lambda p: jnp.dot(p, sel, preferred_element_type=F32)
    return (d(lo) + d(mid)) + d(hi)


def _tril(n):
    r = lax.broadcasted_iota(jnp.int32, (n, n), 0)
    c = lax.broadcasted_iota(jnp.int32, (n, n), 1)
    return r >= c


def _sigmoid(x):
    return 1.0 / (1.0 + jnp.exp(-x))


def _silu(x):
    return x * _sigmoid(x)


def _softplus(x):
    return jnp.maximum(x, 0.0) + jnp.log(1.0 + jnp.exp(-jnp.abs(x)))


def _log_sigmoid(x):
    return -_softplus(-x)


def _rms(x, w):
    return x * lax.rsqrt(jnp.mean(x * x, axis=-1, keepdims=True) + EPS) * w


def _layer_norm(x, w, b):
    mu = jnp.mean(x, axis=-1, keepdims=True)
    xc = x - mu
    var = jnp.mean(xc * xc, axis=-1, keepdims=True)
    return xc * lax.rsqrt(var + EPS) * w + b


def _proj_kernel(x_ref, w_ref, o_ref):
    o_ref[...] = jnp.dot(x_ref[...].astype(BF16), w_ref[...], preferred_element_type=F32)


def _proj3_kernel(x_ref, wh_ref, wl_ref, o_ref, xh_ref, xl_ref):
    @pl.when(pl.program_id(1) == 0)
    def _():
        hi, lo = _split2(x_ref[...])
        xh_ref[...] = hi
        xl_ref[...] = lo

    d = lambda a, b: jnp.dot(a, b, preferred_element_type=F32)
    xh = xh_ref[...]
    wh = wh_ref[...]
    o_ref[...] = (d(xl_ref[...], wh) + d(xh, wl_ref[...])) + d(xh, wh)


def _proj(x, w, tm, tn):
    t, k = x.shape
    three = isinstance(w, tuple)
    ws = w if three else (w,)
    n = ws[0].shape[1]
    return pl.pallas_call(
        _proj3_kernel if three else _proj_kernel,
        out_shape=jax.ShapeDtypeStruct((t, n), F32),
        grid=(t // tm, n // tn),
        in_specs=[pl.BlockSpec((tm, k), lambda i, j: (i, 0))]
                 + [pl.BlockSpec((k, tn), lambda i, j: (0, j)) for _ in ws],
        out_specs=pl.BlockSpec((tm, tn), lambda i, j: (i, j)),
        scratch_shapes=[pltpu.VMEM((tm, k), BF16), pltpu.VMEM((tm, k), BF16)] if three else [],
        compiler_params=_params("parallel", "arbitrary"),
        name="in_proj",
    )(x, *ws)


def _outproj_ln_kernel(n_in, three, *refs):
    a_refs = refs[:n_in]
    nw = 2 if three else 1
    w_refs = refs[n_in:n_in + nw * n_in]
    x_ref, lw_ref, lb_ref, o_ref = refs[n_in + nw * n_in:]
    d = lambda a, b: jnp.dot(a, b, preferred_element_type=F32)
    mix = None
    for i, a_ref in enumerate(a_refs):
        if three:
            ah, al = _split2(a_ref[...])
            wh = w_refs[2 * i][...]
            part = (d(al, wh) + d(ah, w_refs[2 * i + 1][...])) + d(ah, wh)
        else:
            part = d(a_ref[...].astype(BF16), w_refs[i][...])
        mix = part if mix is None else mix + part
    o_ref[...] = _layer_norm(ALPHA * x_ref[...] + mix, lw_ref[...], lb_ref[...])


def _outproj_ln(acts, ws, x, ln_w, ln_b, tm):
    t = x.shape[0]
    n_in = len(acts)
    three = isinstance(ws[0], tuple)
    flat_ws = [w for pair in ws for w in pair] if three else list(ws)
    row = lambda i: (i, 0)
    fixed = lambda i: (0, 0)
    in_specs = ([pl.BlockSpec((tm, a.shape[1]), row) for a in acts]
                + [pl.BlockSpec(w.shape, fixed) for w in flat_ws]
                + [pl.BlockSpec((tm, D_MODEL), row),
                   pl.BlockSpec((1, D_MODEL), fixed), pl.BlockSpec((1, D_MODEL), fixed)])
    return pl.pallas_call(
        functools.partial(_outproj_ln_kernel, n_in, three),
        out_shape=jax.ShapeDtypeStruct((t, D_MODEL), F32),
        grid=(t // tm,),
        in_specs=in_specs,
        out_specs=pl.BlockSpec((tm, D_MODEL), row),
        compiler_params=_params("parallel"),
        name="out_proj_ln",
    )(*acts, *flat_ws, x, ln_w, ln_b)


def _router_scores(x, rwt, bias):
    xh, xl = _split2(x)
    nt = lambda a, b: lax.dot_general(a, b, _NT, preferred_element_type=F32)
    a = nt(xh, rwt)
    b = nt(xl, rwt[:2 * LANES])
    by_token = ((a[:, 2 * LANES:] + b[:, LANES:]) + (a[:, LANES:2 * LANES] + b[:, :LANES])) + a[:, :LANES]
    scores = _sigmoid(by_token.T[:N_EXPERTS])
    return scores, scores + bias


def _best_group(sel):
    tm = sel.shape[1]
    s = [sel[e:e + 1, :] for e in range(N_EXPERTS)]
    grp = []
    for g in range(N_GROUPS):
        m = s[g * EXPERTS_PER_GROUP:(g + 1) * EXPERTS_PER_GROUP]
        best = None
        for i in range(EXPERTS_PER_GROUP):
            for j in range(i + 1, EXPERTS_PER_GROUP):
                p = m[i] + m[j]
                best = p if best is None else jnp.maximum(best, p)
        grp.append(best)
    best_g = jnp.zeros((1, tm), jnp.int32)
    best_v = grp[0]
    for g in range(1, N_GROUPS):
        upd = grp[g] > best_v
        best_g = jnp.where(upd, g, best_g)
        best_v = jnp.where(upd, grp[g], best_v)
    return best_g


def _top2(vals, weights):
    tm = vals[0].shape[1]
    neg = jnp.full((1, tm), -jnp.inf, F32)

    def first_argmax(rows):
        idx = jnp.zeros((1, tm), jnp.int32)
        top = rows[0]
        for e in range(1, len(rows)):
            upd = rows[e] > top
            idx = jnp.where(upd, e, idx)
            top = jnp.where(upd, rows[e], top)
        return idx

    idx1 = first_argmax(vals)
    idx2 = first_argmax([jnp.where(idx1 == e, neg, v) for e, v in enumerate(vals)])
    zero = jnp.zeros((1, tm), F32)
    w1 = zero
    w2 = zero
    for e, w in enumerate(weights):
        w1 = w1 + jnp.where(idx1 == e, w, zero)
        w2 = w2 + jnp.where(idx2 == e, w, zero)
    tot = w1 + w2
    g1 = w1 / tot
    g2 = w2 / tot
    return [jnp.where(idx1 == e, g1, zero) + jnp.where(idx2 == e, g2, zero)
            for e in range(len(vals))]


def _pad_rows(rows, tm):
    return jnp.concatenate(rows + [jnp.zeros((LANES - len(rows), tm), F32)], axis=0)


def _route_in_group(x, rwt, bias, group):
    tm = x.shape[0]
    scores, sel = _router_scores(x, rwt, bias)
    zero = jnp.zeros((1, tm), F32)
    vals, weights = [], []
    for m in range(EXPERTS_PER_GROUP):
        v = zero
        w = zero
        for g in range(N_GROUPS):
            e = g * EXPERTS_PER_GROUP + m
            v = jnp.where(group == g, sel[e:e + 1, :], v)
            w = jnp.where(group == g, scores[e:e + 1, :], w)
        vals.append(v)
        weights.append(w)
    return _pad_rows(_top2(vals, weights), tm)


def _route(x, rwt, bias):
    tm = x.shape[0]
    scores, sel = _router_scores(x, rwt, bias)
    s = [sel[e:e + 1, :] for e in range(N_EXPERTS)]
    sc = [scores[e:e + 1, :] for e in range(N_EXPERTS)]
    best_g = _best_group(sel)
    neg = jnp.full((1, tm), -jnp.inf, F32)
    ms = [jnp.where(best_g == e // EXPERTS_PER_GROUP, s[e], neg) for e in range(N_EXPERTS)]
    return _pad_rows(_top2(ms, sc), tm), best_g


def _router_kernel(with_gates, x_ref, rwt_ref, bias_ref, g_ref):
    tm = x_ref.shape[0]
    if with_gates:
        gates_t, best_g = _route(x_ref[...], rwt_ref[...], bias_ref[...])
    else:
        best_g = _best_group(_router_scores(x_ref[...], rwt_ref[...], bias_ref[...])[1])
        gates_t = jnp.zeros((LANES, tm), F32)
    row = lax.broadcasted_iota(jnp.int32, (LANES, tm), 0)
    gates_t = jnp.where(row == N_EXPERTS, best_g.astype(F32), gates_t)
    g_ref[...] = gates_t.T


def _router(x, rwt, bias, tm, with_gates=True):
    t = x.shape[0]
    return pl.pallas_call(
        functools.partial(_router_kernel, with_gates),
        out_shape=jax.ShapeDtypeStruct((t, LANES), F32),
        grid=(t // tm,),
        in_specs=[pl.BlockSpec((tm, D_MODEL), lambda i: (i, 0)),
                  pl.BlockSpec((3 * LANES, D_MODEL), lambda i: (0, 0)),
                  pl.BlockSpec((N_EXPERTS, 1), lambda i: (0, 0))],
        out_specs=pl.BlockSpec((tm, LANES), lambda i: (i, 0)),
        compiler_params=_params("parallel"),
        name="router",
    )(x, rwt, bias)


def _moe_kernel(x_ref, g_ref, wg_ref, wu_ref, wd_ref, lw_ref, lb_ref, o_ref, acc_ref, xb_ref):
    e = pl.program_id(1)

    @pl.when(e == 0)
    def _():
        xb_ref[...] = x_ref[...].astype(BF16)
        acc_ref[...] = jnp.zeros_like(acc_ref)

    xb = xb_ref[...]
    hg = jnp.dot(xb, wg_ref[0].astype(BF16), preferred_element_type=F32)
    hu = jnp.dot(xb, wu_ref[0].astype(BF16), preferred_element_type=F32)
    he = _silu(hg) * hu
    gates = g_ref[...]
    lane = lax.broadcasted_iota(jnp.int32, gates.shape, 1)
    ge = jnp.sum(jnp.where(lane == e, gates, 0.0), axis=1, keepdims=True)
    acc_ref[...] += ge * jnp.dot(he.astype(BF16), wd_ref[0].astype(BF16),
                                  preferred_element_type=F32)

    @pl.when(e == N_EXPERTS - 1)
    def _():
        o_ref[...] = _layer_norm(ALPHA * x_ref[...] + acc_ref[...], lw_ref[...], lb_ref[...])


def _moe_ln(x, gates, wg, wu, wd, layer, ln_w, ln_b, tm):
    t = x.shape[0]
    return pl.pallas_call(
        _moe_kernel,
        out_shape=jax.ShapeDtypeStruct((t, D_MODEL), F32),
        grid=(t // tm, N_EXPERTS),
        in_specs=[pl.BlockSpec((tm, D_MODEL), lambda i, e: (i, 0)),
                  pl.BlockSpec((tm, LANES), lambda i, e: (i, 0)),
                  pl.BlockSpec((1, D_MODEL, D_FF_EXPERT), lambda i, e: (layer * N_EXPERTS + e, 0, 0)),
                  pl.BlockSpec((1, D_MODEL, D_FF_EXPERT), lambda i, e: (layer * N_EXPERTS + e, 0, 0)),
                  pl.BlockSpec((1, D_FF_EXPERT, D_MODEL), lambda i, e: (layer * N_EXPERTS + e, 0, 0)),
                  pl.BlockSpec((1, D_MODEL), lambda i, e: (0, 0)),
                  pl.BlockSpec((1, D_MODEL), lambda i, e: (0, 0))],
        out_specs=pl.BlockSpec((tm, D_MODEL), lambda i, e: (i, 0)),
        scratch_shapes=[pltpu.VMEM((tm, D_MODEL), F32), pltpu.VMEM((tm, D_MODEL), BF16)],
        compiler_params=_params("parallel", "arbitrary"),
        name="moe_ln",
    )(x, gates, wg, wu, wd, ln_w, ln_b)


def _group_rank_kernel(g_ref, rank_ref, tot_ref, carry_ref):
    n = g_ref.shape[0]

    @pl.when(pl.program_id(0) == 0)
    def _():
        carry_ref[...] = jnp.zeros_like(carry_ref)

    grp = g_ref[:, N_EXPERTS:N_EXPERTS + 1].astype(jnp.int32)
    lane = lax.broadcasted_iota(jnp.int32, (n, LANES), 1)
    onehot = jnp.where(lane == grp, 1.0, 0.0)
    rr = lax.broadcasted_iota(jnp.int32, (n, n), 0)
    cc = lax.broadcasted_iota(jnp.int32, (n, n), 1)
    before = jnp.where(rr > cc, 1.0, 0.0).astype(BF16)
    earlier = jnp.dot(before, onehot.astype(BF16), preferred_element_type=F32) + carry_ref[...]
    rank = jnp.sum(onehot * earlier, axis=1, keepdims=True)
    rank_ref[...] = jnp.broadcast_to(rank, (n, LANES))
    carry_ref[...] += jnp.sum(onehot, axis=0, keepdims=True)
    tot_ref[...] = jnp.broadcast_to(carry_ref[...], tot_ref.shape)


def _group_rank(gmat, tr):
    t = gmat.shape[0]
    return pl.pallas_call(
        _group_rank_kernel,
        out_shape=(jax.ShapeDtypeStruct((t, LANES), F32), jax.ShapeDtypeStruct((8, LANES), F32)),
        grid=(t // tr,),
        in_specs=[pl.BlockSpec((tr, LANES), lambda i: (i, 0))],
        out_specs=(pl.BlockSpec((tr, LANES), lambda i: (i, 0)),
                   pl.BlockSpec((8, LANES), lambda i: (0, 0))),
        scratch_shapes=[pltpu.VMEM((1, LANES), F32)],
        compiler_params=_params("arbitrary"),
        name="group_rank",
    )(gmat)


def _row_copy(src, dst, sem):
    return pltpu.make_async_copy(src, dst, sem)


def _scatter_rows_kernel(dest_ref, x_ref, init_ref, o_hbm, buf_ref, sem):
    del init_ref
    n = x_ref.shape[0]
    base = pl.program_id(0) * n
    buf_ref[...] = x_ref[...].reshape(buf_ref.shape)

    def start(i, carry):
        _row_copy(buf_ref.at[i], o_hbm.at[dest_ref[base + i]], sem).start()
        return carry

    lax.fori_loop(0, n, start, 0)
    _row_copy(buf_ref, o_hbm.at[pl.ds(0, n)], sem).wait()


def _scatter_rows(x, dest, n_out, tr):
    t = x.shape[0]
    slabs = D_MODEL // LANES
    init = jnp.zeros((n_out, slabs, LANES), F32)
    return pl.pallas_call(
        _scatter_rows_kernel,
        out_shape=jax.ShapeDtypeStruct((n_out, slabs, LANES), F32),
        grid_spec=pltpu.PrefetchScalarGridSpec(
            num_scalar_prefetch=1,
            grid=(t // tr,),
            in_specs=[pl.BlockSpec((tr, D_MODEL), lambda i, d: (i, 0)),
                      pl.BlockSpec(memory_space=pl.ANY)],
            out_specs=pl.BlockSpec(memory_space=pl.ANY),
            scratch_shapes=[pltpu.VMEM((tr, slabs, LANES), F32), pltpu.SemaphoreType.DMA(())]),
        input_output_aliases={2: 0},
        compiler_params=_params("arbitrary"),
        name="scatter_rows",
    )(dest, x, init)


def _gather_rows_kernel(src_ref, y_hbm, o_ref, buf_ref, sems):
    n = o_ref.shape[0]
    i = pl.program_id(0)
    slot = i % 2

    def issue(step, to_slot):
        def start(r, carry):
            _row_copy(y_hbm.at[src_ref[step * n + r]], buf_ref.at[to_slot, r],
                      sems.at[to_slot]).start()
            return carry
        lax.fori_loop(0, n, start, 0)

    @pl.when(i == 0)
    def _():
        issue(0, 0)

    @pl.when(i + 1 < pl.num_programs(0))
    def _():
        issue(i + 1, 1 - slot)

    _row_copy(y_hbm.at[pl.ds(0, n)], buf_ref.at[slot], sems.at[slot]).wait()
    o_ref[...] = buf_ref[slot].reshape(o_ref.shape)


def _gather_rows(y3, src, tr):
    t = src.shape[0]
    slabs = D_MODEL // LANES
    return pl.pallas_call(
        _gather_rows_kernel,
        out_shape=jax.ShapeDtypeStruct((t, D_MODEL), F32),
        grid_spec=pltpu.PrefetchScalarGridSpec(
            num_scalar_prefetch=1,
            grid=(t // tr,),
            in_specs=[pl.BlockSpec(memory_space=pl.ANY)],
            out_specs=pl.BlockSpec((tr, D_MODEL), lambda i, d: (i, 0)),
            scratch_shapes=[pltpu.VMEM((2, tr, slabs, LANES), F32),
                            pltpu.SemaphoreType.DMA((2,))]),
        compiler_params=_params("arbitrary"),
        name="gather_rows",
    )(src, y3)


def _moe_group_kernel(tg_ref, x3_ref, rwt_ref, rb_ref, wg_ref, wu_ref, wd_ref, lw_ref, lb_ref,
                      o3_ref, acc_ref, x_ref, xb_ref, gate_ref):
    i = pl.program_id(0)
    j = pl.program_id(1)
    group = tg_ref[i]
    slabs = D_MODEL // LANES

    @pl.when(group < 0)
    def _():
        o3_ref[...] = jnp.zeros_like(o3_ref)

    @pl.when(group >= 0)
    def _():
        @pl.when(j == 0)
        def _():
            x = x3_ref[...].reshape(x_ref.shape)
            x_ref[...] = x
            xb_ref[...] = x.astype(BF16)
            acc_ref[...] = jnp.zeros_like(acc_ref)
            gate_ref[...] = _route_in_group(x, rwt_ref[...], rb_ref[...], group).T

        xb = xb_ref[...]
        hg = jnp.dot(xb, wg_ref[0].astype(BF16), preferred_element_type=F32)
        hu = jnp.dot(xb, wu_ref[0].astype(BF16), preferred_element_type=F32)
        he = _silu(hg) * hu
        gates = gate_ref[...]
        lane = lax.broadcasted_iota(jnp.int32, gates.shape, 1)
        ge = jnp.sum(jnp.where(lane == j, gates, 0.0), axis=1, keepdims=True)
        acc_ref[...] += ge * jnp.dot(he.astype(BF16), wd_ref[0].astype(BF16),
                                     preferred_element_type=F32)

        @pl.when(j == EXPERTS_PER_GROUP - 1)
        def _():
            y = _layer_norm(ALPHA * x_ref[...] + acc_ref[...], lw_ref[...], lb_ref[...])
            o3_ref[...] = y.reshape(y.shape[0], slabs, LANES)


def _moe_group_ln(xs3, tile_group, rwt, rbias, wg, wu, wd, layer, ln_w, ln_b, tm):
    n = xs3.shape[0]
    slabs = D_MODEL // LANES
    expert = lambda i, j, tg: (layer * N_EXPERTS + jnp.maximum(tg[i], 0) * EXPERTS_PER_GROUP + j, 0, 0)
    fixed = lambda i, j, tg: (0, 0)
    return pl.pallas_call(
        _moe_group_kernel,
        out_shape=jax.ShapeDtypeStruct((n, slabs, LANES), F32),
        grid_spec=pltpu.PrefetchScalarGridSpec(
            num_scalar_prefetch=1,
            grid=(n // tm, EXPERTS_PER_GROUP),
            in_specs=[pl.BlockSpec((tm, slabs, LANES), lambda i, j, tg: (i, 0, 0)),
                      pl.BlockSpec((3 * LANES, D_MODEL), fixed),
                      pl.BlockSpec((N_EXPERTS, 1), fixed),
                      pl.BlockSpec((1, D_MODEL, D_FF_EXPERT), expert),
                      pl.BlockSpec((1, D_MODEL, D_FF_EXPERT), expert),
                      pl.BlockSpec((1, D_FF_EXPERT, D_MODEL), expert),
                      pl.BlockSpec((1, D_MODEL), fixed),
                      pl.BlockSpec((1, D_MODEL), fixed)],
            out_specs=pl.BlockSpec((tm, slabs, LANES), lambda i, j, tg: (i, 0, 0)),
            scratch_shapes=[pltpu.VMEM((tm, D_MODEL), F32), pltpu.VMEM((tm, D_MODEL), F32),
                            pltpu.VMEM((tm, D_MODEL), BF16), pltpu.VMEM((tm, LANES), F32)]),
        compiler_params=_params("arbitrary", "arbitrary"),
        name="moe_group_ln",
    )(tile_group, xs3, rwt, rbias, wg, wu, wd, ln_w, ln_b)


def _ffn_sorted(x, p, layer, tm, tm_moe):
    t = x.shape[0]
    n_tiles = t // tm_moe + N_GROUPS
    gmat = _router(x, p['rwt'], p['rbias'], tm, with_gates=False)
    rank_mat, totals = _group_rank(gmat, tm)
    group = gmat[:, N_EXPERTS].astype(jnp.int32)
    counts = totals[0, :N_GROUPS].astype(jnp.int32)
    seg_tiles = (counts + tm_moe - 1) // tm_moe
    seg_end = jnp.cumsum(seg_tiles)
    seg_start = seg_end - seg_tiles
    is_group = group[:, None] == jnp.arange(N_GROUPS, dtype=jnp.int32)[None, :]
    dest = (jnp.sum(jnp.where(is_group, seg_start[None, :], 0), axis=1) * tm_moe
            + rank_mat[:, 0].astype(jnp.int32))
    tile_id = jnp.arange(n_tiles, dtype=jnp.int32)
    tile_group = jnp.sum((tile_id[:, None] >= seg_end[None, :]).astype(jnp.int32), axis=1)
    tile_group = jnp.where(tile_id < seg_end[N_GROUPS - 1], tile_group, -1)
    xs3 = _scatter_rows(x, dest, n_tiles * tm_moe, tm)
    ys3 = _moe_group_ln(xs3, tile_group, p['rwt'], p['rbias'], p['w_gate'], p['w_up'], p['w_down'],
                        layer, p['ln2_w'][layer], p['ln2_b'][layer], tm_moe)
    return _gather_rows(ys3, dest, tm)


def _prefix_selector():
    n = VEC_TILE
    nsub = VEC_CHUNK // VEC_SUB
    t = np.arange(n)[:, None]
    s = np.arange(n)[None, :]
    incl = ((t // VEC_CHUNK) == (s // VEC_CHUNK)) & ((s % VEC_CHUNK) <= (t % VEC_CHUNK))
    r = np.arange((n // VEC_CHUNK) * nsub)[:, None]
    starts = ((r // nsub) == (s // VEC_CHUNK)) & ((s % VEC_CHUNK) < VEC_SUB * (r % nsub))
    return np.concatenate([incl, starts], axis=0).astype(np.float32)


def _vec_heads(heads, sel, mm):
    n = VEC_TILE
    nsub = VEC_CHUNK // VEC_SUB
    nchunk = n // VEC_CHUNK
    nrows = heads[0][0].shape[0]
    kdim = heads[0][0].shape[1]
    streams = [(h, i) for h in range(len(heads)) for i in range(0, nrows, n)]
    tile = lambda h, i, which: heads[h][which][i:i + n]

    prefs = [_dot_exact_rhs(sel, tile(h, i, 3)) for h, i in streams]
    rows_of = lambda fn, m: jnp.concatenate(
        [jnp.broadcast_to(fn(j), (m, kdim)) for j in range(n // m)], axis=0)
    sub = (lax.broadcasted_iota(jnp.int32, (n, kdim), 0) // VEC_SUB) % nsub
    q_cat, k_cat, q_dec0, updates = [], [], [], []
    for (h, i), pref in zip(streams, prefs):
        q, k, v = tile(h, i, 0), tile(h, i, 1), tile(h, i, 2)
        big_g = pref[0:n]
        start = lambda c, j, pref=pref: pref[n + c * nsub + j:n + c * nsub + j + 1]
        q_dec = [q * jnp.exp(big_g)]
        for j in range(1, nsub):
            base_j = rows_of(lambda c: start(c, j), VEC_CHUNK)
            q_dec.append(q * jnp.exp(jnp.minimum(big_g - base_j, 0.0)))
        base_own = rows_of(lambda m: start(m // nsub, m % nsub), VEC_SUB)
        k_rel = k * jnp.exp(base_own - big_g)
        k_cat.append(jnp.concatenate([jnp.where(sub == j, k_rel, 0.0) for j in range(nsub)],
                                     axis=1))
        q_cat.append(jnp.concatenate(q_dec, axis=1))
        q_dec0.append(q_dec[0])
        per_chunk = []
        for c in range(nchunk):
            rows = slice(c * VEC_CHUNK, (c + 1) * VEC_CHUNK)
            g_last = big_g[(c + 1) * VEC_CHUNK - 1:(c + 1) * VEC_CHUNK, :]
            kd = k[rows] * jnp.exp(g_last - big_g[rows])
            per_chunk.append((jnp.exp(g_last), mm.tn(v[rows], kd)))
        updates.append(per_chunk)
    scores = [mm.nt(qc, kc) for qc, kc in zip(q_cat, k_cat)]
    rr = lax.broadcasted_iota(jnp.int32, (n, n), 0)
    cc = lax.broadcasted_iota(jnp.int32, (n, n), 1)
    keep = (rr >= cc) & ((rr // VEC_CHUNK) == (cc // VEC_CHUNK))
    intra = [mm.nn(jnp.where(keep, sc, 0.0), tile(h, i, 2)) for (h, i), sc in zip(streams, scores)]

    states = [hd[4] for hd in heads]
    o_rows = [[] for _ in heads]
    for si, (h, i) in enumerate(streams):
        for c, (decay_last, update) in enumerate(updates[si]):
            rows = slice(c * VEC_CHUNK, (c + 1) * VEC_CHUNK)
            o_rows[h].append(intra[si][rows] + mm.nt(q_dec0[si][rows], states[h]))
            states[h] = states[h] * decay_last + update
    return [(jnp.concatenate(o_rows[h], axis=0), states[h]) for h in range(len(heads))]


def _gla_chunk_kernel(q_ref, k_ref, v_ref, go_ref, sm_ref, w2_ref, b2_ref, nw_ref, sel_ref,
                      o_ref, s_ref, st_ref):
    r = pl.program_id(2)

    @pl.when(r == 0)
    def _():
        st_ref[...] = jnp.zeros_like(st_ref)

    sm = sm_ref[...]
    heads = []
    for hh in range(VEC_HPS):
        kc = slice(hh * GLA_DK, (hh + 1) * GLA_DK)
        vc = slice(hh * GLA_DV, (hh + 1) * GLA_DV)
        gk = _log_sigmoid(_ThreePass.nn(sm, w2_ref[hh]) + b2_ref[hh]) / GLA_NORMALIZER
        heads.append((q_ref[:, kc] * (GLA_DK ** -0.5), k_ref[:, kc], v_ref[:, vc], gk, st_ref[hh]))
    finals = []
    for hh, (o, st) in enumerate(_vec_heads(heads, sel_ref[...], _ThreePass)):
        vc = slice(hh * GLA_DV, (hh + 1) * GLA_DV)
        o_ref[:, vc] = _rms(o, nw_ref[...]) * _silu(go_ref[:, vc])
        st_ref[hh] = st
        finals.append(st)

    @pl.when(r == pl.num_programs(2) - 1)
    def _():
        for hh in range(VEC_HPS):
            s_ref[0, hh] = finals[hh].T


def _gla_prompt(proj, w2p, b2, norm_w, sel, bsz, seq):
    nr = seq // VEC_ROWS
    ng = GLA_HEADS // VEC_HPS
    kw = VEC_HPS * GLA_DK
    vw = VEC_HPS * GLA_DV
    row = lambda off: (lambda b, h, r: (b * nr + r, off + h))
    return pl.pallas_call(
        _gla_chunk_kernel,
        out_shape=(jax.ShapeDtypeStruct((bsz * seq, GLA_VAL), F32),
                   jax.ShapeDtypeStruct((bsz, GLA_HEADS, GLA_DK, GLA_DV), F32)),
        grid=(bsz, ng, nr),
        in_specs=[pl.BlockSpec((VEC_ROWS, kw), row(AB_Q // kw)),
                  pl.BlockSpec((VEC_ROWS, kw), row(AB_K // kw)),
                  pl.BlockSpec((VEC_ROWS, vw), row(AB_V // vw)),
                  pl.BlockSpec((VEC_ROWS, vw), row(AB_GOUT // vw)),
                  pl.BlockSpec((VEC_ROWS, LANES), lambda b, h, r: (b * nr + r, AB_SMALL // LANES)),
                  pl.BlockSpec((VEC_HPS, LANES, GLA_DK), lambda b, h, r: (h, 0, 0)),
                  pl.BlockSpec((VEC_HPS, 1, GLA_DK), lambda b, h, r: (h, 0, 0)),
                  pl.BlockSpec((1, GLA_DV), lambda b, h, r: (0, 0)),
                  pl.BlockSpec(sel.shape, lambda b, h, r: (0, 0))],
        out_specs=(pl.BlockSpec((VEC_ROWS, vw), lambda b, h, r: (b * nr + r, h)),
                   pl.BlockSpec((1, VEC_HPS, GLA_DK, GLA_DV), lambda b, h, r: (b, h, 0, 0))),
        scratch_shapes=[pltpu.VMEM((VEC_HPS, GLA_DV, GLA_DK), F32)],
        compiler_params=_params("parallel", "parallel", "arbitrary"),
        name="gla_chunk",
    )(proj, proj, proj, proj, proj, w2p, b2, norm_w, sel)


def _hgrn_lower_bound(lbraw, layer):
    m = jnp.max(lbraw, axis=0, keepdims=True)
    ex = jnp.exp(lbraw - m)
    sm = ex / jnp.sum(ex, axis=0, keepdims=True)
    acc = sm[0:1]
    for i in range(1, layer + 1):
        acc = acc + sm[i:i + 1]
    return acc - sm[0:1]


def _hgrn_gates(q_raw, f_raw, lb):
    forget = lb + (1.0 - lb) * _sigmoid(f_raw)
    return _silu(q_raw), 1.0 - forget, jnp.log(forget)


def _hgrn_chunk_kernel(layer, q_ref, f_ref, i_ref, go_ref, lb_ref, nw_ref, sel_ref,
                       o_ref, s_ref, st_ref):
    r = pl.program_id(2)

    @pl.when(r == 0)
    def _():
        st_ref[...] = jnp.zeros_like(st_ref)

    lb_all = _hgrn_lower_bound(lb_ref[...], layer)
    heads = []
    for hh in range(VEC_HPS):
        kc = slice(hh * HG_EXPAND, (hh + 1) * HG_EXPAND)
        vc = slice(hh * HG_DI, (hh + 1) * HG_DI)
        q, k, g = _hgrn_gates(q_ref[:, kc], f_ref[:, kc], lb_all[:, kc])
        heads.append((q, k, i_ref[:, vc], g, st_ref[hh]))
    finals = []
    for hh, (o, st) in enumerate(_vec_heads(heads, sel_ref[...], _OnePass)):
        vc = slice(hh * HG_DI, (hh + 1) * HG_DI)
        o_ref[:, vc] = _rms(o, nw_ref[...]) * _silu(go_ref[:, vc])
        st_ref[hh] = st
        finals.append(st)

    @pl.when(r == pl.num_programs(2) - 1)
    def _():
        for hh in range(VEC_HPS):
            s_ref[0, hh] = finals[hh].T


def _hgrn_prompt(proj, lower_bounds, norm_w, sel, layer, bsz, seq):
    nr = seq // VEC_ROWS
    ng = HG_HEADS // VEC_HPS
    kw = VEC_HPS * HG_EXPAND
    vw = VEC_HPS * HG_DI
    row = lambda off: (lambda b, h, r: (b * nr + r, off + h))
    return pl.pallas_call(
        functools.partial(_hgrn_chunk_kernel, layer),
        out_shape=(jax.ShapeDtypeStruct((bsz * seq, HG_I), F32),
                   jax.ShapeDtypeStruct((bsz, HG_HEADS, HG_EXPAND, HG_DI), F32)),
        grid=(bsz, ng, nr),
        in_specs=[pl.BlockSpec((VEC_ROWS, kw), row(0)),
                  pl.BlockSpec((VEC_ROWS, kw), row(ng)),
                  pl.BlockSpec((VEC_ROWS, vw), row(2 * ng)),
                  pl.BlockSpec((VEC_ROWS, vw), row(3 * ng)),
                  pl.BlockSpec((DEPTH, kw), lambda b, h, r: (0, h)),
                  pl.BlockSpec((1, HG_DI), lambda b, h, r: (0, 0)),
                  pl.BlockSpec(sel.shape, lambda b, h, r: (0, 0))],
        out_specs=(pl.BlockSpec((VEC_ROWS, vw), lambda b, h, r: (b * nr + r, h)),
                   pl.BlockSpec((1, VEC_HPS, HG_EXPAND, HG_DI), lambda b, h, r: (b, h, 0, 0))),
        scratch_shapes=[pltpu.VMEM((VEC_HPS, HG_DI, HG_EXPAND), F32)],
        compiler_params=_params("parallel", "parallel", "arbitrary"),
        name="hgrn_chunk",
    )(proj, proj, proj, proj, lower_bounds, norm_w, sel)


def _conv_silu(xp, cw, cb, n, lead):
    acc = cb + cw[SSD_CONV - 1:SSD_CONV] * xp[lead:lead + n]
    for m in range(1, SSD_CONV):
        acc = acc + cw[SSD_CONV - 1 - m:SSD_CONV - m] * xp[lead - m:lead - m + n]
    return _silu(acc)


def _ssd_gate_norm(y, z, nw):
    yz = y * _silu(z)
    parts = []
    for g in range(SSD_GROUPS):
        cols = slice(g * SSD_GROUP_W, (g + 1) * SSD_GROUP_W)
        parts.append(_rms(yz[:, cols], nw[:, cols]))
    return jnp.concatenate(parts, axis=1)


def _ssd_chunk_kernel(z_ref, xbc_ref, sm_ref, cw_ref, cb_ref, dtb_ref, alog_ref, dsk_ref,
                      nw_ref, ex_ref, o_ref, s_ref, conv_ref, st_ref, prev_ref):
    r = pl.program_id(1)
    c = SSD_CHUNK
    mm = _ThreePass

    @pl.when(r == 0)
    def _():
        st_ref[...] = jnp.zeros_like(st_ref)
        prev_ref[...] = jnp.zeros_like(prev_ref)

    x_raw = xbc_ref[...]
    xp = jnp.concatenate([prev_ref[...], x_raw], axis=0)
    prev_ref[...] = x_raw[c - 8:c]
    xc = _conv_silu(xp, cw_ref[...], cb_ref[...], c, 8)
    xs = xc[:, :SSD_INNER]
    bm = xc[:, SSD_INNER:SSD_INNER + SSD_BC]
    cm = xc[:, SSD_INNER + SSD_BC:]

    dt = _softplus(sm_ref[...] + dtb_ref[...])
    a_neg = -jnp.exp(alog_ref[...])
    big_g = _dot_exact_rhs(_tril(c).astype(BF16), dt * a_neg)
    g_t = big_g.T
    g_last = big_g[c - 1:c, :]
    ex = ex_ref[...]
    dt_x = _dot_exact_lhs(dt, ex)
    eg_x = _dot_exact_lhs(jnp.exp(big_g), ex)
    w_x = _dot_exact_lhs(dt * jnp.exp(g_last - big_g), ex)
    xdt = xs * dt_x
    xw = xs * w_x
    causal = _tril(c)
    lane = lax.broadcasted_iota(jnp.int32, (c, LANES), 1)
    st = st_ref[...]
    y_parts = []
    u_parts = []
    for g in range(SSD_GROUPS):
        gcols = slice(g * SSD_GROUP_W, (g + 1) * SSD_GROUP_W)
        bg = bm[:, g * SSD_STATE:(g + 1) * SSD_STATE]
        cg = cm[:, g * SSD_STATE:(g + 1) * SSD_STATE]
        sc = mm.nt(cg, bg)
        inter = mm.nn(cg, st[:, gcols])
        u_parts.append(mm.tn(bg, xw[:, gcols]))
        pair_cols = []
        heads_per_group = SSD_HEADS // SSD_GROUPS
        for p in range(heads_per_group // 2):
            h0 = g * heads_per_group + 2 * p
            xpair = xdt[:, h0 * SSD_HEADDIM:(h0 + 2) * SSD_HEADDIM]
            ws = []
            for h in (h0, h0 + 1):
                diff = big_g[:, h:h + 1] - g_t[h:h + 1, :]
                ws.append(sc * jnp.exp(jnp.where(causal, diff, -jnp.inf)))
            x_diag = jnp.concatenate([jnp.where(lane < SSD_HEADDIM, xpair, 0.0),
                                      jnp.where(lane < SSD_HEADDIM, 0.0, xpair)], axis=0)
            pair_cols.append(mm.nn(jnp.concatenate(ws, axis=1), x_diag))
        y_intra = jnp.concatenate(pair_cols, axis=1)
        y_parts.append(y_intra + inter * eg_x[:, gcols])
    y = jnp.concatenate(y_parts, axis=1) + dsk_ref[...] * xs
    o_ref[...] = _ssd_gate_norm(y, z_ref[...], nw_ref[...])
    st = st * eg_x[c - 1:c, :] + jnp.concatenate(u_parts, axis=1)
    st_ref[...] = st

    @pl.when(r == pl.num_programs(1) - 1)
    def _():
        s_ref[0] = st
        conv_ref[0] = x_raw[c - (SSD_CONV - 1):c]


def _ssd_prompt(proj, conv_w, conv_b, dtb_p, alog_p, dskip_x, norm_w, expand, bsz, seq):
    nr = seq // SSD_CHUNK
    fixed = lambda b, r: (0, 0)
    return pl.pallas_call(
        _ssd_chunk_kernel,
        out_shape=(jax.ShapeDtypeStruct((bsz * seq, SSD_INNER), F32),
                   jax.ShapeDtypeStruct((bsz, SSD_STATE, SSD_INNER), F32),
                   jax.ShapeDtypeStruct((bsz, SSD_CONV - 1, SSD_CONV_DIM), F32)),
        grid=(bsz, nr),
        in_specs=[pl.BlockSpec((SSD_CHUNK, SSD_INNER), lambda b, r: (b * nr + r, AB_Z // SSD_INNER)),
                  pl.BlockSpec((SSD_CHUNK, SSD_CONV_DIM), lambda b, r: (b * nr + r, AB_XBC // SSD_CONV_DIM)),
                  pl.BlockSpec((SSD_CHUNK, LANES), lambda b, r: (b * nr + r, AB_SMALL // LANES)),
                  pl.BlockSpec((SSD_CONV, SSD_CONV_DIM), fixed),
                  pl.BlockSpec((1, SSD_CONV_DIM), fixed),
                  pl.BlockSpec((1, LANES), fixed),
                  pl.BlockSpec((1, LANES), fixed),
                  pl.BlockSpec((1, SSD_INNER), fixed),
                  pl.BlockSpec((1, SSD_INNER), fixed),
                  pl.BlockSpec((LANES, SSD_INNER), fixed)],
        out_specs=(pl.BlockSpec((SSD_CHUNK, SSD_INNER), lambda b, r: (b * nr + r, 0)),
                   pl.BlockSpec((1, SSD_STATE, SSD_INNER), lambda b, r: (b, 0, 0)),
                   pl.BlockSpec((1, SSD_CONV - 1, SSD_CONV_DIM), lambda b, r: (b, 0, 0))),
        scratch_shapes=[pltpu.VMEM((SSD_STATE, SSD_INNER), F32),
                        pltpu.VMEM((8, SSD_CONV_DIM), F32)],
        compiler_params=_params("parallel", "arbitrary"),
        name="ssd_chunk",
    )(proj, proj, proj, conv_w, conv_b, dtb_p, alog_p, dskip_x, norm_w, expand)


def _ab_prep_kernel(q_ref, sm_ref, xbc_ref, cs_ref, w2_ref, b2_ref, cw_ref, cb_ref, dtb_ref,
                    alog_ref, ex_ref, qs_ref, dec_ref, xc_ref, xdt_ref, dax_ref, cs_out_ref):
    sm = sm_ref[...]
    gk = _log_sigmoid(_ThreePass.nn(sm, w2_ref[...]) + b2_ref[...]) / GLA_NORMALIZER
    qs_ref[...] = q_ref[...] * (GLA_DK ** -0.5)
    dec_ref[...] = jnp.exp(gk)
    cw = cw_ref[...]
    x_raw = xbc_ref[...]
    acc = cb_ref[...] + cw[SSD_CONV - 1:SSD_CONV] * x_raw
    for j in range(SSD_CONV - 1):
        acc = acc + cw[j:j + 1] * cs_ref[j]
    xc = _silu(acc)
    xc_ref[...] = xc
    for j in range(SSD_CONV - 2):
        cs_out_ref[j] = cs_ref[j + 1]
    cs_out_ref[SSD_CONV - 2] = x_raw
    dt = _softplus(sm + dtb_ref[...])
    ex = ex_ref[...]
    xdt_ref[...] = xc[:, :SSD_INNER] * _dot_exact_lhs(dt, ex)
    dax_ref[...] = _dot_exact_lhs(jnp.exp(dt * -jnp.exp(alog_ref[...])), ex)


def _ab_prep(proj, conv_state, w2_wide, b2_wide, conv_w, conv_b, dtb_p, alog_p, expand):
    bsz = proj.shape[0]
    fixed = lambda i: (0, 0)
    sds = jax.ShapeDtypeStruct
    return pl.pallas_call(
        _ab_prep_kernel,
        out_shape=(sds((bsz, GLA_KEY), F32), sds((bsz, GLA_KEY), F32),
                   sds((bsz, SSD_CONV_DIM), F32), sds((bsz, SSD_INNER), F32),
                   sds((bsz, SSD_INNER), F32),
                   sds((SSD_CONV - 1, bsz, SSD_CONV_DIM), F32)),
        grid=(1,),
        in_specs=[pl.BlockSpec((bsz, GLA_KEY), lambda i: (0, AB_Q // GLA_KEY)),
                  pl.BlockSpec((bsz, LANES), lambda i: (0, AB_SMALL // LANES)),
                  pl.BlockSpec((bsz, SSD_CONV_DIM), lambda i: (0, AB_XBC // SSD_CONV_DIM)),
                  pl.BlockSpec((SSD_CONV - 1, bsz, SSD_CONV_DIM), lambda i: (0, 0, 0)),
                  pl.BlockSpec((LANES, GLA_KEY), fixed),
                  pl.BlockSpec((1, GLA_KEY), fixed),
                  pl.BlockSpec((SSD_CONV, SSD_CONV_DIM), fixed),
                  pl.BlockSpec((1, SSD_CONV_DIM), fixed),
                  pl.BlockSpec((1, LANES), fixed),
                  pl.BlockSpec((1, LANES), fixed),
                  pl.BlockSpec((LANES, SSD_INNER), fixed)],
        out_specs=(pl.BlockSpec((bsz, GLA_KEY), fixed), pl.BlockSpec((bsz, GLA_KEY), fixed),
                   pl.BlockSpec((bsz, SSD_CONV_DIM), fixed), pl.BlockSpec((bsz, SSD_INNER), fixed),
                   pl.BlockSpec((bsz, SSD_INNER), fixed),
                   pl.BlockSpec((SSD_CONV - 1, bsz, SSD_CONV_DIM), lambda i: (0, 0, 0))),
        compiler_params=_params("arbitrary"),
        name="ab_prep",
    )(proj, proj, proj, conv_state, w2_wide, b2_wide, conv_w, conv_b, dtb_p, alog_p, expand)


def _hgrn_prep_kernel(layer, q_ref, f_ref, lb_ref, qs_ref, k_ref, dec_ref):
    lb = _hgrn_lower_bound(lb_ref[...], layer)
    forget = lb + (1.0 - lb) * _sigmoid(f_ref[...])
    qs_ref[...] = _silu(q_ref[...])
    k_ref[...] = 1.0 - forget
    dec_ref[...] = jnp.exp(jnp.log(forget))


def _hgrn_prep(proj, lower_bounds, layer):
    bsz = proj.shape[0]
    blk = lambda j: pl.BlockSpec((bsz, HG_F), lambda i: (0, j))
    return pl.pallas_call(
        functools.partial(_hgrn_prep_kernel, layer),
        out_shape=tuple(jax.ShapeDtypeStruct((bsz, HG_F), F32) for _ in range(3)),
        grid=(1,),
        in_specs=[blk(0), blk(1), pl.BlockSpec((DEPTH, HG_F), lambda i: (0, 0))],
        out_specs=tuple(blk(0) for _ in range(3)),
        compiler_params=_params("arbitrary"),
        name="hgrn_prep",
    )(proj, proj, lower_bounds)


def _vec_step_kernel(mm, s_ref, q_ref, k_ref, d_ref, v_ref, go_ref, nw_ref, so_ref, o_ref):
    q = q_ref[...]
    kt = k_ref[0, 0]
    dt = d_ref[0, 0]
    v = v_ref[...]
    row = lax.broadcasted_iota(jnp.int32, v.shape, 0)
    new = []
    for b in range(STEP_B):
        only_b = row == b
        decay = _dot_exact_lhs(dt, jnp.where(only_b, 1.0, 0.0).astype(BF16))
        new.append(s_ref[b, 0] * decay + mm.nn(kt, jnp.where(only_b, v, 0.0)))
    for b in range(STEP_B):
        so_ref[b, 0] = new[b]
    o = jnp.concatenate([mm.nn(q, new[b])[b:b + 1] for b in range(STEP_B)], axis=0)
    o_ref[...] = _rms(o, nw_ref[...]) * _silu(go_ref[...])


def _vec_step(state, q_rows, k_cols, d_cols, vsrc, v_off, gsrc, g_off, norm_w, mm):
    bsz, nh, kdim, vdim = state.shape
    col = lambda j, h: (h, j, 0, 0)
    return pl.pallas_call(
        functools.partial(_vec_step_kernel, mm),
        out_shape=(jax.ShapeDtypeStruct(state.shape, F32),
                   jax.ShapeDtypeStruct((bsz, nh * vdim), F32)),
        grid=(bsz // STEP_B, nh),
        in_specs=[pl.BlockSpec((STEP_B, 1, kdim, vdim), lambda j, h: (j, h, 0, 0)),
                  pl.BlockSpec((STEP_B, kdim), lambda j, h: (j, h)),
                  pl.BlockSpec((1, 1, kdim, STEP_B), col),
                  pl.BlockSpec((1, 1, kdim, STEP_B), col),
                  pl.BlockSpec((STEP_B, vdim), lambda j, h: (j, v_off + h)),
                  pl.BlockSpec((STEP_B, vdim), lambda j, h: (j, g_off + h)),
                  pl.BlockSpec((1, vdim), lambda j, h: (0, 0))],
        out_specs=(pl.BlockSpec((STEP_B, 1, kdim, vdim), lambda j, h: (j, h, 0, 0)),
                   pl.BlockSpec((STEP_B, vdim), lambda j, h: (j, h))),
        compiler_params=_params("parallel", "parallel"),
        name="vec_step",
    )(state, q_rows, k_cols, d_cols, vsrc, gsrc, norm_w)


def _ssd_step_kernel(s_ref, b_ref, c_ref, x_ref, xdt_ref, dax_ref, dsk_ref, so_ref, y_ref):
    mm = _ThreePass
    bt = b_ref[0, 0]
    c = c_ref[...]
    xdt = xdt_ref[...]
    dax = dax_ref[...]
    hpg = SSD_HEADS // SSD_GROUPS
    row = lax.broadcasted_iota(jnp.int32, xdt.shape, 0)
    new = []
    for b in range(STEP_B):
        outer = mm.nn(bt, jnp.where(row == b, xdt, 0.0))
        per_head = []
        for hh in range(hpg):
            cols = slice(hh * SSD_HEADDIM, (hh + 1) * SSD_HEADDIM)
            sn = s_ref[b, hh] * dax[b:b + 1, cols] + outer[:, cols]
            so_ref[b, hh] = sn
            per_head.append(sn)
        new.append(per_head)
    rows = [jnp.concatenate([mm.nn(c, new[b][hh])[b:b + 1] for hh in range(hpg)], axis=1)
            for b in range(STEP_B)]
    y_ref[...] = jnp.concatenate(rows, axis=0) + dsk_ref[...] * x_ref[...]


def _ssd_step(state, b_cols, xc, xdt, dax, dskip_x):
    bsz = state.shape[0]
    hpg = SSD_HEADS // SSD_GROUPS
    grp = lambda j, g: (j, g)
    c_off = (SSD_INNER + SSD_BC) // SSD_STATE
    return pl.pallas_call(
        _ssd_step_kernel,
        out_shape=(jax.ShapeDtypeStruct(state.shape, F32),
                   jax.ShapeDtypeStruct((bsz, SSD_INNER), F32)),
        grid=(bsz // STEP_B, SSD_GROUPS),
        in_specs=[pl.BlockSpec((STEP_B, hpg, SSD_STATE, SSD_HEADDIM), lambda j, g: (j, g, 0, 0)),
                  pl.BlockSpec((1, 1, SSD_STATE, STEP_B), lambda j, g: (g, j, 0, 0)),
                  pl.BlockSpec((STEP_B, SSD_STATE), lambda j, g: (j, c_off + g)),
                  pl.BlockSpec((STEP_B, SSD_GROUP_W), grp),
                  pl.BlockSpec((STEP_B, SSD_GROUP_W), grp),
                  pl.BlockSpec((STEP_B, SSD_GROUP_W), grp),
                  pl.BlockSpec((1, SSD_GROUP_W), lambda j, g: (0, g))],
        out_specs=(pl.BlockSpec((STEP_B, hpg, SSD_STATE, SSD_HEADDIM), lambda j, g: (j, g, 0, 0)),
                   pl.BlockSpec((STEP_B, SSD_GROUP_W), grp)),
        compiler_params=_params("parallel", "parallel"),
        name="ssd_step",
    )(state, b_cols, xc, xc, xdt, dax, dskip_x)


def _ssd_post_kernel(y_ref, z_ref, nw_ref, o_ref):
    o_ref[...] = _ssd_gate_norm(y_ref[...], z_ref[...], nw_ref[...])


def _ssd_post(y, proj, norm_w):
    bsz = y.shape[0]
    return pl.pallas_call(
        _ssd_post_kernel,
        out_shape=jax.ShapeDtypeStruct((bsz, SSD_INNER), F32),
        grid=(1,),
        in_specs=[pl.BlockSpec((bsz, SSD_INNER), lambda i: (0, 0)),
                  pl.BlockSpec((bsz, SSD_INNER), lambda i: (0, AB_Z // SSD_INNER)),
                  pl.BlockSpec((1, SSD_INNER), lambda i: (0, 0))],
        out_specs=pl.BlockSpec((bsz, SSD_INNER), lambda i: (0, 0)),
        compiler_params=_params("arbitrary"),
        name="ssd_post",
    )(y, proj, norm_w)


def _to_cols(a, nh):
    bsz = a.shape[0]
    return a.reshape(bsz // STEP_B, STEP_B, nh, -1).transpose(2, 0, 3, 1)


def _prep_weights(w_in_ab, w_gk2, b_gk2, gla_norm_w, conv_w, conv_b, dt_bias, a_log, d_skip,
                  ssd_norm_w, w_out_ab, w_in_c, hg_norm_w, w_out_c, router_w, router_bias,
                  w_gate, w_up, w_down, ln1_w, ln1_b, ln2_w, ln2_b):
    offs = np.cumsum([0, GLA_KEY, GLA_KEY, GLA_VAL, GLA_VAL, GLA_RANK, SSD_INNER, SSD_CONV_DIM,
                      SSD_HEADS])
    sec = lambda w, i: w[:, offs[i]:offs[i + 1]]
    w = w_in_ab[0]
    pad = jnp.zeros((D_MODEL, LANES - SSD_HEADS - GLA_RANK), w.dtype)
    w_ab = jnp.concatenate([sec(w, 5), sec(w, 2), sec(w, 3), sec(w, 6), sec(w, 0), sec(w, 1),
                            sec(w, 7), sec(w, 4), pad], axis=1)
    hi_lo = lambda m: (m.astype(BF16), (m - m.astype(BF16).astype(F32)).astype(BF16))
    pad_e = lambda m: jnp.pad(m, ((0, LANES - N_EXPERTS), (0, 0)))
    router_pieces = jnp.concatenate([pad_e(piece) for piece in _split3(router_w.T)], axis=0)
    w2_wide = jnp.zeros((LANES, GLA_KEY), F32).at[SSD_HEADS:SSD_HEADS + GLA_RANK].set(w_gk2[0])
    lane_pad = lambda v: jnp.zeros((1, LANES), F32).at[0, :SSD_HEADS].set(v)
    expand = np.zeros((LANES, SSD_INNER), np.float32)
    for h in range(SSD_HEADS):
        expand[h, h * SSD_HEADDIM:(h + 1) * SSD_HEADDIM] = 1.0
    return dict(
        w_ab=hi_lo(w_ab),
        w2_wide=w2_wide,
        w2_heads=w2_wide.reshape(LANES, GLA_HEADS, GLA_DK).transpose(1, 0, 2),
        b2_wide=b_gk2[0].reshape(1, GLA_KEY),
        b2_heads=b_gk2[0].reshape(GLA_HEADS, 1, GLA_DK),
        gla_norm_w=gla_norm_w[0].reshape(1, GLA_DV),
        conv_w=conv_w[0], conv_b=conv_b[0].reshape(1, SSD_CONV_DIM),
        dtb_p=lane_pad(dt_bias[0]), alog_p=lane_pad(a_log[0]),
        dskip_x=jnp.repeat(d_skip[0], SSD_HEADDIM).reshape(1, SSD_INNER),
        ssd_norm_w=ssd_norm_w[0].reshape(1, SSD_INNER),
        expand=jnp.asarray(expand, BF16),
        prefix_sel=jnp.asarray(_prefix_selector(), BF16),
        w_out_gla=hi_lo(w_out_ab[0, :GLA_VAL]),
        w_out_ssd=hi_lo(w_out_ab[0, GLA_VAL:]),
        w_c=w_in_c[0].astype(BF16),
        hg_norm_w=hg_norm_w[0].reshape(1, HG_DI),
        w_out_c=w_out_c[0].astype(BF16),
        rwt=router_pieces,
        rbias=router_bias.reshape(N_EXPERTS, 1),
        w_gate=w_gate.reshape(DEPTH * N_EXPERTS, D_MODEL, D_FF_EXPERT),
        w_up=w_up.reshape(DEPTH * N_EXPERTS, D_MODEL, D_FF_EXPERT),
        w_down=w_down.reshape(DEPTH * N_EXPERTS, D_FF_EXPERT, D_MODEL),
        ln1_w=ln1_w.reshape(DEPTH, 1, D_MODEL), ln1_b=ln1_b.reshape(DEPTH, 1, D_MODEL),
        ln2_w=ln2_w.reshape(DEPTH, 1, D_MODEL), ln2_b=ln2_b.reshape(DEPTH, 1, D_MODEL),
    )


def _ffn(x, p, layer, tm, tm_moe):
    gates = _router(x, p['rwt'], p['rbias'], tm)
    return _moe_ln(x, gates, p['w_gate'], p['w_up'], p['w_down'], layer,
                   p['ln2_w'][layer], p['ln2_b'][layer], tm_moe)


def _ssd_state_from_wide(s_wide):
    bsz = s_wide.shape[0]
    return s_wide.reshape(bsz, SSD_STATE, SSD_HEADS, SSD_HEADDIM).transpose(0, 2, 1, 3)


def _trunk_prompt(x3, p, lower_bounds, tm, tn_ab, tn_c):
    bsz, seq, _ = x3.shape
    x = x3.reshape(bsz * seq, D_MODEL)
    tm_big = 2 * tm
    proj = _proj(x, p['w_ab'], tm_big, tn_ab)
    o_gla, s_gla = _gla_prompt(proj, p['w2_heads'], p['b2_heads'], p['gla_norm_w'],
                               p['prefix_sel'], bsz, seq)
    yz, s_ssd, s_conv = _ssd_prompt(proj, p['conv_w'], p['conv_b'], p['dtb_p'], p['alog_p'],
                                    p['dskip_x'], p['ssd_norm_w'], p['expand'], bsz, seq)
    x = _outproj_ln([o_gla, yz], [p['w_out_gla'], p['w_out_ssd']], x, p['ln1_w'][0], p['ln1_b'][0], tm)
    x = _ffn_sorted(x, p, 0, tm, tm_big)
    proj_c = _proj(x, p['w_c'], tm_big, tn_c)
    o_hg, s_hg = _hgrn_prompt(proj_c, lower_bounds, p['hg_norm_w'], p['prefix_sel'], 1, bsz, seq)
    x = _outproj_ln([o_hg], [p['w_out_c']], x, p['ln1_w'][1], p['ln1_b'][1], tm)
    x = _ffn_sorted(x, p, 1, tm, tm_big)
    return (x.reshape(bsz, seq, D_MODEL), s_gla[None], _ssd_state_from_wide(s_ssd)[None],
            s_conv[None], s_hg[None])


def _trunk_sample(x3, st_gla, st_ssd, st_conv, st_hg, p, lower_bounds, tn_ab, tn_c):
    bsz = x3.shape[0]
    tm = bsz
    x = x3.reshape(bsz, D_MODEL)
    proj = _proj(x, p['w_ab'], tm, tn_ab)
    qs, dec, xc, xdt, dax, conv_new = _ab_prep(proj, st_conv[0].transpose(1, 0, 2), p['w2_wide'],
                                               p['b2_wide'], p['conv_w'], p['conv_b'], p['dtb_p'],
                                               p['alog_p'], p['expand'])
    conv_new = conv_new.transpose(1, 0, 2)
    k_gla = proj[:, AB_K:AB_K + GLA_KEY]
    s_gla, o_gla = _vec_step(st_gla[0], qs, _to_cols(k_gla, GLA_HEADS),
                             _to_cols(dec, GLA_HEADS), proj, AB_V // GLA_DV, proj,
                             AB_GOUT // GLA_DV, p['gla_norm_w'], _ThreePass)
    s_ssd, y = _ssd_step(st_ssd[0], _to_cols(xc[:, SSD_INNER:SSD_INNER + SSD_BC], SSD_GROUPS),
                         xc, xdt, dax, p['dskip_x'])
    yz = _ssd_post(y, proj, p['ssd_norm_w'])
    x = _outproj_ln([o_gla, yz], [p['w_out_gla'], p['w_out_ssd']], x, p['ln1_w'][0], p['ln1_b'][0], tm)
    x = _ffn(x, p, 0, tm, tm)
    proj_c = _proj(x, p['w_c'], tm, tn_c)
    qh, kh, dh = _hgrn_prep(proj_c, lower_bounds, 1)
    s_hg, o_hg = _vec_step(st_hg[0], qh, _to_cols(kh, HG_HEADS),
                           _to_cols(dh, HG_HEADS), proj_c, 2 * HG_HEADS, proj_c, 3 * HG_HEADS,
                           p['hg_norm_w'], _OnePass)
    x = _outproj_ln([o_hg], [p['w_out_c']], x, p['ln1_w'][1], p['ln1_b'][1], tm)
    x = _ffn(x, p, 1, tm, tm)
    return x.reshape(bsz, 1, D_MODEL), s_gla[None], s_ssd[None], conv_new[None], s_hg[None]


def kernel(x_prompt, x_sample, state_gla, state_ssd, state_conv, state_hgrn, w_in_ab, w_gk2, b_gk2, gla_norm_w, conv_w, conv_b, dt_bias, a_log, d_skip, ssd_norm_w, w_out_ab, w_in_c, lower_bounds, hg_norm_w, w_out_c, router_w, router_bias, w_gate, w_up, w_down, ln1_w, ln1_b, ln2_w, ln2_b):
    p = _prep_weights(w_in_ab, w_gk2, b_gk2, gla_norm_w, conv_w, conv_b, dt_bias, a_log, d_skip,
                      ssd_norm_w, w_out_ab, w_in_c, hg_norm_w, w_out_c, router_w, router_bias,
                      w_gate, w_up, w_down, ln1_w, ln1_b, ln2_w, ln2_b)
    y_p, gla_p, ssd_p, conv_p, hg_p = _trunk_prompt(x_prompt, p, lower_bounds, 512, 1152, 1024)
    y_s, gla_s, ssd_s, conv_s, hg_s = _trunk_sample(x_sample, state_gla, state_ssd, state_conv,
                                                    state_hgrn, p, lower_bounds, 1152, 1024)
    return (y_p, y_s, gla_p, ssd_p, conv_p, hg_p, gla_s, ssd_s, conv_s, hg_s)
```

```python
import functools

import numpy as np
import jax
import jax.numpy as jnp
from jax import lax
from jax.experimental import pallas as pl
from jax.experimental.pallas import tpu as pltpu

F32 = jnp.float32
BF16 = jnp.bfloat16

D_MODEL = 1024
DEPTH = 2
GLA_HEADS = 4
GLA_DK = 128
GLA_DV = 256
GLA_KEY = GLA_HEADS * GLA_DK
GLA_VAL = GLA_HEADS * GLA_DV
GLA_RANK = 16
GLA_NORMALIZER = 16.0
SSD_INNER = 1024
SSD_HEADDIM = 64
SSD_HEADS = 16
SSD_STATE = 128
SSD_GROUPS = 2
SSD_CONV = 4
SSD_GROUP_W = SSD_INNER // SSD_GROUPS
SSD_BC = SSD_GROUPS * SSD_STATE
SSD_CONV_DIM = SSD_INNER + 2 * SSD_BC
HG_EXPAND = 128
HG_HEADS = 8
HG_F = HG_HEADS * HG_EXPAND
HG_I = D_MODEL
HG_DI = HG_I // HG_HEADS
N_EXPERTS = 16
N_GROUPS = 4
EXPERTS_PER_GROUP = 4
D_FF_EXPERT = 512
ALPHA = (2 * DEPTH) ** 0.25
EPS = 1e-5

LANES = 128
VMEM_LIMIT = 48 * 1024 * 1024

AB_Z = 0
AB_V = 1024
AB_GOUT = 2048
AB_XBC = 3072
AB_Q = 4608
AB_K = 5120
AB_SMALL = 5632
AB_COLS = 5760
C_COLS = 4096

VEC_CHUNK = 64
VEC_SUB = 16
VEC_TILE = 256
VEC_ROWS = 512
VEC_HPS = 4
SSD_CHUNK = 128
STEP_B = 8
VEC_STEP_B = 16


def _params(*sem):
    return pltpu.CompilerParams(dimension_semantics=sem, vmem_limit_bytes=VMEM_LIMIT)


_NN = (((1,), (0,)), ((), ()))
_NT = (((1,), (1,)), ((), ()))
_TN = (((0,), (0,)), ((), ()))


def _dot1(dims, a, b):
    return lax.dot_general(a.astype(BF16), b.astype(BF16), dims, preferred_element_type=F32)


def _split2(a):
    hi = a.astype(BF16)
    return hi, (a - hi.astype(F32)).astype(BF16)


def _dot3(dims, a, b):
    ah, al = _split2(a)
    bh, bl = _split2(b)
    d = lambda x, y: lax.dot_general(x, y, dims, preferred_element_type=F32)
    return (d(al, bh) + d(ah, bl)) + d(ah, bh)


class _OnePass:
    nn = staticmethod(lambda a, b: _dot1(_NN, a, b))
    nt = staticmethod(lambda a, b: _dot1(_NT, a, b))
    tn = staticmethod(lambda a, b: _dot1(_TN, a, b))


class _ThreePass:
    nn = staticmethod(lambda a, b: _dot3(_NN, a, b))
    nt = staticmethod(lambda a, b: _dot3(_NT, a, b))
    tn = staticmethod(lambda a, b: _dot3(_TN, a, b))


def _dot(a, b):
    return _dot1(_NN, a, b)


def _dot_nt(a, b):
    return _dot1(_NT, a, b)


def _dot_tn(a, b):
    return _dot1(_TN, a, b)


def _split3(a):
    hi = a.astype(BF16)
    r1 = a - hi.astype(F32)
    mid = r1.astype(BF16)
    lo = (r1 - mid.astype(F32)).astype(BF16)
    return hi, mid, lo


def _dot_exact_rhs(sel, a):
    hi, mid, lo = _split3(a)
    d = lambda p: jnp.dot(sel, p, preferred_element_type=F32)
    return (d(lo) + d(mid)) + d(hi)


def _dot_exact_lhs(a, sel):
    hi, mid, lo = _split3(a)
    d = lambda p: jnp.dot(p, sel, preferred_element_type=F32)
    return (d(lo) + d(mid)) + d(hi)


def _tril(n):
    r = lax.broadcasted_iota(jnp.int32, (n, n), 0)
    c = lax.broadcasted_iota(jnp.int32, (n, n), 1)
    return r >= c


def _sigmoid(x):
    return 1.0 / (1.0 + jnp.exp(-x))


def _silu(x):
    return x * _sigmoid(x)


def _softplus(x):
    return jnp.maximum(x, 0.0) + jnp.log(1.0 + jnp.exp(-jnp.abs(x)))


def _log_sigmoid(x):
    return -_softplus(-x)


def _rms(x, w):
    return x * lax.rsqrt(jnp.mean(x * x, axis=-1, keepdims=True) + EPS) * w


def _layer_norm(x, w, b):
    mu = jnp.mean(x, axis=-1, keepdims=True)
    xc = x - mu
    var = jnp.mean(xc * xc, axis=-1, keepdims=True)
    return xc * lax.rsqrt(var + EPS) * w + b


def _proj_kernel(x_ref, w_ref, o_ref):
    o_ref[...] = jnp.dot(x_ref[...].astype(BF16), w_ref[...], preferred_element_type=F32)


def _proj3_kernel(x_ref, wh_ref, wl_ref, o_ref, xh_ref, xl_ref):
    @pl.when(pl.program_id(1) == 0)
    def _():
        hi, lo = _split2(x_ref[...])
        xh_ref[...] = hi
        xl_ref[...] = lo

    d = lambda a, b: jnp.dot(a, b, preferred_element_type=F32)
    xh = xh_ref[...]
    wh = wh_ref[...]
    o_ref[...] = (d(xl_ref[...], wh) + d(xh, wl_ref[...])) + d(xh, wh)


def _proj(x, w, tm, tn):
    t, k = x.shape
    three = isinstance(w, tuple)
    ws = w if three else (w,)
    n = ws[0].shape[1]
    return pl.pallas_call(
        _proj3_kernel if three else _proj_kernel,
        out_shape=jax.ShapeDtypeStruct((t, n), F32),
        grid=(t // tm, n // tn),
        in_specs=[pl.BlockSpec((tm, k), lambda i, j: (i, 0))]
                 + [pl.BlockSpec((k, tn), lambda i, j: (0, j)) for _ in ws],
        out_specs=pl.BlockSpec((tm, tn), lambda i, j: (i, j)),
        scratch_shapes=[pltpu.VMEM((tm, k), BF16), pltpu.VMEM((tm, k), BF16)] if three else [],
        compiler_params=_params("parallel", "arbitrary"),
        name="in_proj",
    )(x, *ws)


def _outproj_ln_kernel(n_in, three, *refs):
    a_refs = refs[:n_in]
    nw = 2 if three else 1
    w_refs = refs[n_in:n_in + nw * n_in]
    x_ref, lw_ref, lb_ref, o_ref = refs[n_in + nw * n_in:]
    d = lambda a, b: jnp.dot(a, b, preferred_element_type=F32)
    mix = None
    for i, a_ref in enumerate(a_refs):
        if three:
            ah, al = _split2(a_ref[...])
            wh = w_refs[2 * i][...]
            part = (d(al, wh) + d(ah, w_refs[2 * i + 1][...])) + d(ah, wh)
        else:
            part = d(a_ref[...].astype(BF16), w_refs[i][...])
        mix = part if mix is None else mix + part
    o_ref[...] = _layer_norm(ALPHA * x_ref[...] + mix, lw_ref[...], lb_ref[...])


def _outproj_ln(acts, ws, x, ln_w, ln_b, tm):
    t = x.shape[0]
    n_in = len(acts)
    three = isinstance(ws[0], tuple)
    flat_ws = [w for pair in ws for w in pair] if three else list(ws)
    row = lambda i: (i, 0)
    fixed = lambda i: (0, 0)
    in_specs = ([pl.BlockSpec((tm, a.shape[1]), row) for a in acts]
                + [pl.BlockSpec(w.shape, fixed) for w in flat_ws]
                + [pl.BlockSpec((tm, D_MODEL), row),
                   pl.BlockSpec((1, D_MODEL), fixed), pl.BlockSpec((1, D_MODEL), fixed)])
    return pl.pallas_call(
        functools.partial(_outproj_ln_kernel, n_in, three),
        out_shape=jax.ShapeDtypeStruct((t, D_MODEL), F32),
        grid=(t // tm,),
        in_specs=in_specs,
        out_specs=pl.BlockSpec((tm, D_MODEL), row),
        compiler_params=_params("parallel"),
        name="out_proj_ln",
    )(*acts, *flat_ws, x, ln_w, ln_b)


def _router_scores(x, rwt, bias):
    xh, xl = _split2(x)
    nt = lambda a, b: lax.dot_general(a, b, _NT, preferred_element_type=F32)
    a = nt(xh, rwt)
    b = nt(xl, rwt[:2 * LANES])
    by_token = ((a[:, 2 * LANES:] + b[:, LANES:]) + (a[:, LANES:2 * LANES] + b[:, :LANES])) + a[:, :LANES]
    scores = _sigmoid(by_token.T[:N_EXPERTS])
    return scores, scores + bias


def _best_group(sel):
    tm = sel.shape[1]
    s = [sel[e:e + 1, :] for e in range(N_EXPERTS)]
    grp = []
    for g in range(N_GROUPS):
        m = s[g * EXPERTS_PER_GROUP:(g + 1) * EXPERTS_PER_GROUP]
        best = None
        for i in range(EXPERTS_PER_GROUP):
            for j in range(i + 1, EXPERTS_PER_GROUP):
                p = m[i] + m[j]
                best = p if best is None else jnp.maximum(best, p)
        grp.append(best)
    best_g = jnp.zeros((1, tm), jnp.int32)
    best_v = grp[0]
    for g in range(1, N_GROUPS):
        upd = grp[g] > best_v
        best_g = jnp.where(upd, g, best_g)
        best_v = jnp.where(upd, grp[g], best_v)
    return best_g


def _top2(vals, weights):
    tm = vals[0].shape[1]
    neg = jnp.full((1, tm), -jnp.inf, F32)

    def first_argmax(rows):
        idx = jnp.zeros((1, tm), jnp.int32)
        top = rows[0]
        for e in range(1, len(rows)):
            upd = rows[e] > top
            idx = jnp.where(upd, e, idx)
            top = jnp.where(upd, rows[e], top)
        return idx

    idx1 = first_argmax(vals)
    idx2 = first_argmax([jnp.where(idx1 == e, neg, v) for e, v in enumerate(vals)])
    zero = jnp.zeros((1, tm), F32)
    w1 = zero
    w2 = zero
    for e, w in enumerate(weights):
        w1 = w1 + jnp.where(idx1 == e, w, zero)
        w2 = w2 + jnp.where(idx2 == e, w, zero)
    tot = w1 + w2
    g1 = w1 / tot
    g2 = w2 / tot
    return [jnp.where(idx1 == e, g1, zero) + jnp.where(idx2 == e, g2, zero)
            for e in range(len(vals))]


def _pad_rows(rows, tm):
    return jnp.concatenate(rows + [jnp.zeros((LANES - len(rows), tm), F32)], axis=0)


def _route_in_group(x, rwt, bias, group):
    tm = x.shape[0]
    scores, sel = _router_scores(x, rwt, bias)
    zero = jnp.zeros((1, tm), F32)
    vals, weights = [], []
    for m in range(EXPERTS_PER_GROUP):
        v = zero
        w = zero
        for g in range(N_GROUPS):
            e = g * EXPERTS_PER_GROUP + m
            v = jnp.where(group == g, sel[e:e + 1, :], v)
            w = jnp.where(group == g, scores[e:e + 1, :], w)
        vals.append(v)
        weights.append(w)
    return _pad_rows(_top2(vals, weights), tm)


def _route(x, rwt, bias):
    tm = x.shape[0]
    scores, sel = _router_scores(x, rwt, bias)
    s = [sel[e:e + 1, :] for e in range(N_EXPERTS)]
    sc = [scores[e:e + 1, :] for e in range(N_EXPERTS)]
    best_g = _best_group(sel)
    neg = jnp.full((1, tm), -jnp.inf, F32)
    ms = [jnp.where(best_g == e // EXPERTS_PER_GROUP, s[e], neg) for e in range(N_EXPERTS)]
    return _pad_rows(_top2(ms, sc), tm), best_g


def _router_kernel(with_gates, x_ref, rwt_ref, bias_ref, g_ref):
    tm = x_ref.shape[0]
    if with_gates:
        gates_t, best_g = _route(x_ref[...], rwt_ref[...], bias_ref[...])
    else:
        best_g = _best_group(_router_scores(x_ref[...], rwt_ref[...], bias_ref[...])[1])
        gates_t = jnp.zeros((LANES, tm), F32)
    row = lax.broadcasted_iota(jnp.int32, (LANES, tm), 0)
    gates_t = jnp.where(row == N_EXPERTS, best_g.astype(F32), gates_t)
    g_ref[...] = gates_t.T


def _router(x, rwt, bias, tm, with_gates=True):
    t = x.shape[0]
    return pl.pallas_call(
        functools.partial(_router_kernel, with_gates),
        out_shape=jax.ShapeDtypeStruct((t, LANES), F32),
        grid=(t // tm,),
        in_specs=[pl.BlockSpec((tm, D_MODEL), lambda i: (i, 0)),
                  pl.BlockSpec((3 * LANES, D_MODEL), lambda i: (0, 0)),
                  pl.BlockSpec((N_EXPERTS, 1), lambda i: (0, 0))],
        out_specs=pl.BlockSpec((tm, LANES), lambda i: (i, 0)),
        compiler_params=_params("parallel"),
        name="router",
    )(x, rwt, bias)


def _moe_kernel(x_ref, g_ref, wg_ref, wu_ref, wd_ref, lw_ref, lb_ref, o_ref, acc_ref, xb_ref):
    e = pl.program_id(1)

    @pl.when(e == 0)
    def _():
        xb_ref[...] = x_ref[...].astype(BF16)
        acc_ref[...] = jnp.zeros_like(acc_ref)

    xb = xb_ref[...]
    hg = jnp.dot(xb, wg_ref[0].astype(BF16), preferred_element_type=F32)
    hu = jnp.dot(xb, wu_ref[0].astype(BF16), preferred_element_type=F32)
    he = _silu(hg) * hu
    gates = g_ref[...]
    lane = lax.broadcasted_iota(jnp.int32, gates.shape, 1)
    ge = jnp.sum(jnp.where(lane == e, gates, 0.0), axis=1, keepdims=True)
    acc_ref[...] += ge * jnp.dot(he.astype(BF16), wd_ref[0].astype(BF16),
                                  preferred_element_type=F32)

    @pl.when(e == N_EXPERTS - 1)
    def _():
        o_ref[...] = _layer_norm(ALPHA * x_ref[...] + acc_ref[...], lw_ref[...], lb_ref[...])


def _moe_ln(x, gates, wg, wu, wd, layer, ln_w, ln_b, tm):
    t = x.shape[0]
    return pl.pallas_call(
        _moe_kernel,
        out_shape=jax.ShapeDtypeStruct((t, D_MODEL), F32),
        grid=(t // tm, N_EXPERTS),
        in_specs=[pl.BlockSpec((tm, D_MODEL), lambda i, e: (i, 0)),
                  pl.BlockSpec((tm, LANES), lambda i, e: (i, 0)),
                  pl.BlockSpec((1, D_MODEL, D_FF_EXPERT), lambda i, e: (layer * N_EXPERTS + e, 0, 0)),
                  pl.BlockSpec((1, D_MODEL, D_FF_EXPERT), lambda i, e: (layer * N_EXPERTS + e, 0, 0)),
                  pl.BlockSpec((1, D_FF_EXPERT, D_MODEL), lambda i, e: (layer * N_EXPERTS + e, 0, 0)),
                  pl.BlockSpec((1, D_MODEL), lambda i, e: (0, 0)),
                  pl.BlockSpec((1, D_MODEL), lambda i, e: (0, 0))],
        out_specs=pl.BlockSpec((tm, D_MODEL), lambda i, e: (i, 0)),
        scratch_shapes=[pltpu.VMEM((tm, D_MODEL), F32), pltpu.VMEM((tm, D_MODEL), BF16)],
        compiler_params=_params("parallel", "arbitrary"),
        name="moe_ln",
    )(x, gates, wg, wu, wd, ln_w, ln_b)


def _group_rank_kernel(g_ref, rank_ref, tot_ref, carry_ref):
    n = g_ref.shape[0]

    @pl.when(pl.program_id(0) == 0)
    def _():
        carry_ref[...] = jnp.zeros_like(carry_ref)

    grp = g_ref[:, N_EXPERTS:N_EXPERTS + 1].astype(jnp.int32)
    lane = lax.broadcasted_iota(jnp.int32, (n, LANES), 1)
    onehot = jnp.where(lane == grp, 1.0, 0.0)
    rr = lax.broadcasted_iota(jnp.int32, (n, n), 0)
    cc = lax.broadcasted_iota(jnp.int32, (n, n), 1)
    before = jnp.where(rr > cc, 1.0, 0.0).astype(BF16)
    earlier = jnp.dot(before, onehot.astype(BF16), preferred_element_type=F32) + carry_ref[...]
    rank = jnp.sum(onehot * earlier, axis=1, keepdims=True)
    rank_ref[...] = jnp.broadcast_to(rank, (n, LANES))
    carry_ref[...] += jnp.sum(onehot, axis=0, keepdims=True)
    tot_ref[...] = jnp.broadcast_to(carry_ref[...], tot_ref.shape)


def _group_rank(gmat, tr):
    t = gmat.shape[0]
    return pl.pallas_call(
        _group_rank_kernel,
        out_shape=(jax.ShapeDtypeStruct((t, LANES), F32), jax.ShapeDtypeStruct((8, LANES), F32)),
        grid=(t // tr,),
        in_specs=[pl.BlockSpec((tr, LANES), lambda i: (i, 0))],
        out_specs=(pl.BlockSpec((tr, LANES), lambda i: (i, 0)),
                   pl.BlockSpec((8, LANES), lambda i: (0, 0))),
        scratch_shapes=[pltpu.VMEM((1, LANES), F32)],
        compiler_params=_params("arbitrary"),
        name="group_rank",
    )(gmat)


def _row_copy(src, dst, sem):
    return pltpu.make_async_copy(src, dst, sem)


def _scatter_rows_kernel(dest_ref, x_ref, init_ref, o_hbm, buf_ref, sem):
    del init_ref
    n = x_ref.shape[0]
    base = pl.program_id(0) * n
    buf_ref[...] = x_ref[...].reshape(buf_ref.shape)

    def start(i, carry):
        _row_copy(buf_ref.at[i], o_hbm.at[dest_ref[base + i]], sem).start()
        return carry

    lax.fori_loop(0, n, start, 0)
    _row_copy(buf_ref, o_hbm.at[pl.ds(0, n)], sem).wait()


def _scatter_rows(x, dest, n_out, tr):
    t = x.shape[0]
    slabs = D_MODEL // LANES
    init = jnp.zeros((n_out, slabs, LANES), F32)
    return pl.pallas_call(
        _scatter_rows_kernel,
        out_shape=jax.ShapeDtypeStruct((n_out, slabs, LANES), F32),
        grid_spec=pltpu.PrefetchScalarGridSpec(
            num_scalar_prefetch=1,
            grid=(t // tr,),
            in_specs=[pl.BlockSpec((tr, D_MODEL), lambda i, d: (i, 0)),
                      pl.BlockSpec(memory_space=pl.ANY)],
            out_specs=pl.BlockSpec(memory_space=pl.ANY),
            scratch_shapes=[pltpu.VMEM((tr, slabs, LANES), F32), pltpu.SemaphoreType.DMA(())]),
        input_output_aliases={2: 0},
        compiler_params=_params("arbitrary"),
        name="scatter_rows",
    )(dest, x, init)


def _gather_rows_kernel(src_ref, y_hbm, o_ref, buf_ref, sems):
    n = o_ref.shape[0]
    i = pl.program_id(0)
    slot = i % 2

    def issue(step, to_slot):
        def start(r, carry):
            _row_copy(y_hbm.at[src_ref[step * n + r]], buf_ref.at[to_slot, r],
                      sems.at[to_slot]).start()
            return carry
        lax.fori_loop(0, n, start, 0)

    @pl.when(i == 0)
    def _():
        issue(0, 0)

    @pl.when(i + 1 < pl.num_programs(0))
    def _():
        issue(i + 1, 1 - slot)

    _row_copy(y_hbm.at[pl.ds(0, n)], buf_ref.at[slot], sems.at[slot]).wait()
    o_ref[...] = buf_ref[slot].reshape(o_ref.shape)


def _gather_rows(y3, src, tr):
    t = src.shape[0]
    slabs = D_MODEL // LANES
    return pl.pallas_call(
        _gather_rows_kernel,
        out_shape=jax.ShapeDtypeStruct((t, D_MODEL), F32),
        grid_spec=pltpu.PrefetchScalarGridSpec(
            num_scalar_prefetch=1,
            grid=(t // tr,),
            in_specs=[pl.BlockSpec(memory_space=pl.ANY)],
            out_specs=pl.BlockSpec((tr, D_MODEL), lambda i, d: (i, 0)),
            scratch_shapes=[pltpu.VMEM((2, tr, slabs, LANES), F32),
                            pltpu.SemaphoreType.DMA((2,))]),
        compiler_params=_params("arbitrary"),
        name="gather_rows",
    )(src, y3)


def _moe_group_kernel(tg_ref, x3_ref, rwt_ref, rb_ref, wg_ref, wu_ref, wd_ref, lw_ref, lb_ref,
                      o3_ref, acc_ref, x_ref, xb_ref, gate_ref):
    i = pl.program_id(0)
    j = pl.program_id(1)
    group = tg_ref[i]
    slabs = D_MODEL // LANES

    @pl.when(group < 0)
    def _():
        o3_ref[...] = jnp.zeros_like(o3_ref)

    @pl.when(group >= 0)
    def _():
        @pl.when(j == 0)
        def _():
            x = x3_ref[...].reshape(x_ref.shape)
            x_ref[...] = x
            xb_ref[...] = x.astype(BF16)
            acc_ref[...] = jnp.zeros_like(acc_ref)
            gate_ref[...] = _route_in_group(x, rwt_ref[...], rb_ref[...], group).T

        xb = xb_ref[...]
        hg = jnp.dot(xb, wg_ref[0].astype(BF16), preferred_element_type=F32)
        hu = jnp.dot(xb, wu_ref[0].astype(BF16), preferred_element_type=F32)
        he = _silu(hg) * hu
        gates = gate_ref[...]
        lane = lax.broadcasted_iota(jnp.int32, gates.shape, 1)
        ge = jnp.sum(jnp.where(lane == j, gates, 0.0), axis=1, keepdims=True)
        acc_ref[...] += ge * jnp.dot(he.astype(BF16), wd_ref[0].astype(BF16),
                                     preferred_element_type=F32)

        @pl.when(j == EXPERTS_PER_GROUP - 1)
        def _():
            y = _layer_norm(ALPHA * x_ref[...] + acc_ref[...], lw_ref[...], lb_ref[...])
            o3_ref[...] = y.reshape(y.shape[0], slabs, LANES)


def _moe_group_ln(xs3, tile_group, rwt, rbias, wg, wu, wd, layer, ln_w, ln_b, tm):
    n = xs3.shape[0]
    slabs = D_MODEL // LANES
    expert = lambda i, j, tg: (layer * N_EXPERTS + jnp.maximum(tg[i], 0) * EXPERTS_PER_GROUP + j, 0, 0)
    fixed = lambda i, j, tg: (0, 0)
    return pl.pallas_call(
        _moe_group_kernel,
        out_shape=jax.ShapeDtypeStruct((n, slabs, LANES), F32),
        grid_spec=pltpu.PrefetchScalarGridSpec(
            num_scalar_prefetch=1,
            grid=(n // tm, EXPERTS_PER_GROUP),
            in_specs=[pl.BlockSpec((tm, slabs, LANES), lambda i, j, tg: (i, 0, 0)),
                      pl.BlockSpec((3 * LANES, D_MODEL), fixed),
                      pl.BlockSpec((N_EXPERTS, 1), fixed),
                      pl.BlockSpec((1, D_MODEL, D_FF_EXPERT), expert),
                      pl.BlockSpec((1, D_MODEL, D_FF_EXPERT), expert),
                      pl.BlockSpec((1, D_FF_EXPERT, D_MODEL), expert),
                      pl.BlockSpec((1, D_MODEL), fixed),
                      pl.BlockSpec((1, D_MODEL), fixed)],
            out_specs=pl.BlockSpec((tm, slabs, LANES), lambda i, j, tg: (i, 0, 0)),
            scratch_shapes=[pltpu.VMEM((tm, D_MODEL), F32), pltpu.VMEM((tm, D_MODEL), F32),
                            pltpu.VMEM((tm, D_MODEL), BF16), pltpu.VMEM((tm, LANES), F32)]),
        compiler_params=_params("arbitrary", "arbitrary"),
        name="moe_group_ln",
    )(tile_group, xs3, rwt, rbias, wg, wu, wd, ln_w, ln_b)


def _ffn_sorted(x, p, layer, tm, tm_moe):
    t = x.shape[0]
    n_tiles = t // tm_moe + N_GROUPS
    gmat = _router(x, p['rwt'], p['rbias'], tm, with_gates=False)
    rank_mat, totals = _group_rank(gmat, tm)
    group = gmat[:, N_EXPERTS].astype(jnp.int32)
    counts = totals[0, :N_GROUPS].astype(jnp.int32)
    seg_tiles = (counts + tm_moe - 1) // tm_moe
    seg_end = jnp.cumsum(seg_tiles)
    seg_start = seg_end - seg_tiles
    is_group = group[:, None] == jnp.arange(N_GROUPS, dtype=jnp.int32)[None, :]
    dest = (jnp.sum(jnp.where(is_group, seg_start[None, :], 0), axis=1) * tm_moe
            + rank_mat[:, 0].astype(jnp.int32))
    tile_id = jnp.arange(n_tiles, dtype=jnp.int32)
    tile_group = jnp.sum((tile_id[:, None] >= seg_end[None, :]).astype(jnp.int32), axis=1)
    tile_group = jnp.where(tile_id < seg_end[N_GROUPS - 1], tile_group, -1)
    xs3 = _scatter_rows(x, dest, n_tiles * tm_moe, tm)
    ys3 = _moe_group_ln(xs3, tile_group, p['rwt'], p['rbias'], p['w_gate'], p['w_up'], p['w_down'],
                        layer, p['ln2_w'][layer], p['ln2_b'][layer], tm_moe)
    return _gather_rows(ys3, dest, tm)


def _prefix_selector():
    n = VEC_TILE
    nsub = VEC_CHUNK // VEC_SUB
    t = np.arange(n)[:, None]
    s = np.arange(n)[None, :]
    incl = ((t // VEC_CHUNK) == (s // VEC_CHUNK)) & ((s % VEC_CHUNK) <= (t % VEC_CHUNK))
    r = np.arange((n // VEC_CHUNK) * nsub)[:, None]
    starts = ((r // nsub) == (s // VEC_CHUNK)) & ((s % VEC_CHUNK) < VEC_SUB * (r % nsub))
    return np.concatenate([incl, starts], axis=0).astype(np.float32)


def _vec_heads(heads, sel, mm):
    n = VEC_TILE
    nsub = VEC_CHUNK // VEC_SUB
    nchunk = n // VEC_CHUNK
    nrows = heads[0][0].shape[0]
    kdim = heads[0][0].shape[1]
    streams = [(h, i) for h in range(len(heads)) for i in range(0, nrows, n)]
    tile = lambda h, i, which: heads[h][which][i:i + n]

    prefs = [_dot_exact_rhs(sel, tile(h, i, 3)) for h, i in streams]
    rows_of = lambda fn, m: jnp.concatenate(
        [jnp.broadcast_to(fn(j), (m, kdim)) for j in range(n // m)], axis=0)
    sub = (lax.broadcasted_iota(jnp.int32, (n, kdim), 0) // VEC_SUB) % nsub
    q_cat, k_cat, q_dec0, updates = [], [], [], []
    for (h, i), pref in zip(streams, prefs):
        q, k, v = tile(h, i, 0), tile(h, i, 1), tile(h, i, 2)
        big_g = pref[0:n]
        start = lambda c, j, pref=pref: pref[n + c * nsub + j:n + c * nsub + j + 1]
        q_dec = [q * jnp.exp(big_g)]
        for j in range(1, nsub):
            base_j = rows_of(lambda c: start(c, j), VEC_CHUNK)
            q_dec.append(q * jnp.exp(jnp.minimum(big_g - base_j, 0.0)))
        base_own = rows_of(lambda m: start(m // nsub, m % nsub), VEC_SUB)
        k_rel = k * jnp.exp(base_own - big_g)
        k_cat.append(jnp.concatenate([jnp.where(sub == j, k_rel, 0.0) for j in range(nsub)],
                                     axis=1))
        q_cat.append(jnp.concatenate(q_dec, axis=1))
        q_dec0.append(q_dec[0])
        per_chunk = []
        for c in range(nchunk):
            rows = slice(c * VEC_CHUNK, (c + 1) * VEC_CHUNK)
            g_last = big_g[(c + 1) * VEC_CHUNK - 1:(c + 1) * VEC_CHUNK, :]
            kd = k[rows] * jnp.exp(g_last - big_g[rows])
            per_chunk.append((jnp.exp(g_last), mm.tn(v[rows], kd)))
        updates.append(per_chunk)
    scores = [mm.nt(qc, kc) for qc, kc in zip(q_cat, k_cat)]
    rr = lax.broadcasted_iota(jnp.int32, (n, n), 0)
    cc = lax.broadcasted_iota(jnp.int32, (n, n), 1)
    keep = (rr >= cc) & ((rr // VEC_CHUNK) == (cc // VEC_CHUNK))
    intra = [mm.nn(jnp.where(keep, sc, 0.0), tile(h, i, 2)) for (h, i), sc in zip(streams, scores)]

    states = [hd[4] for hd in heads]
    o_rows = [[] for _ in heads]
    for si, (h, i) in enumerate(streams):
        for c, (decay_last, update) in enumerate(updates[si]):
            rows = slice(c * VEC_CHUNK, (c + 1) * VEC_CHUNK)
            o_rows[h].append(intra[si][rows] + mm.nt(q_dec0[si][rows], states[h]))
            states[h] = states[h] * decay_last + update
    return [(jnp.concatenate(o_rows[h], axis=0), states[h]) for h in range(len(heads))]


def _gla_chunk_kernel(q_ref, k_ref, v_ref, go_ref, sm_ref, w2_ref, b2_ref, nw_ref, sel_ref,
                      o_ref, s_ref, st_ref):
    r = pl.program_id(2)

    @pl.when(r == 0)
    def _():
        st_ref[...] = jnp.zeros_like(st_ref)

    sm = sm_ref[...]
    heads = []
    for hh in range(VEC_HPS):
        kc = slice(hh * GLA_DK, (hh + 1) * GLA_DK)
        vc = slice(hh * GLA_DV, (hh + 1) * GLA_DV)
        gk = _log_sigmoid(_ThreePass.nn(sm, w2_ref[hh]) + b2_ref[hh]) / GLA_NORMALIZER
        heads.append((q_ref[:, kc] * (GLA_DK ** -0.5), k_ref[:, kc], v_ref[:, vc], gk, st_ref[hh]))
    finals = []
    for hh, (o, st) in enumerate(_vec_heads(heads, sel_ref[...], _ThreePass)):
        vc = slice(hh * GLA_DV, (hh + 1) * GLA_DV)
        o_ref[:, vc] = _rms(o, nw_ref[...]) * _silu(go_ref[:, vc])
        st_ref[hh] = st
        finals.append(st)

    @pl.when(r == pl.num_programs(2) - 1)
    def _():
        for hh in range(VEC_HPS):
            s_ref[0, hh] = finals[hh].T


def _gla_prompt(proj, w2p, b2, norm_w, sel, bsz, seq):
    nr = seq // VEC_ROWS
    ng = GLA_HEADS // VEC_HPS
    kw = VEC_HPS * GLA_DK
    vw = VEC_HPS * GLA_DV
    row = lambda off: (lambda b, h, r: (b * nr + r, off + h))
    return pl.pallas_call(
        _gla_chunk_kernel,
        out_shape=(jax.ShapeDtypeStruct((bsz * seq, GLA_VAL), F32),
                   jax.ShapeDtypeStruct((bsz, GLA_HEADS, GLA_DK, GLA_DV), F32)),
        grid=(bsz, ng, nr),
        in_specs=[pl.BlockSpec((VEC_ROWS, kw), row(AB_Q // kw)),
                  pl.BlockSpec((VEC_ROWS, kw), row(AB_K // kw)),
                  pl.BlockSpec((VEC_ROWS, vw), row(AB_V // vw)),
                  pl.BlockSpec((VEC_ROWS, vw), row(AB_GOUT // vw)),
                  pl.BlockSpec((VEC_ROWS, LANES), lambda b, h, r: (b * nr + r, AB_SMALL // LANES)),
                  pl.BlockSpec((VEC_HPS, LANES, GLA_DK), lambda b, h, r: (h, 0, 0)),
                  pl.BlockSpec((VEC_HPS, 1, GLA_DK), lambda b, h, r: (h, 0, 0)),
                  pl.BlockSpec((1, GLA_DV), lambda b, h, r: (0, 0)),
                  pl.BlockSpec(sel.shape, lambda b, h, r: (0, 0))],
        out_specs=(pl.BlockSpec((VEC_ROWS, vw), lambda b, h, r: (b * nr + r, h)),
                   pl.BlockSpec((1, VEC_HPS, GLA_DK, GLA_DV), lambda b, h, r: (b, h, 0, 0))),
        scratch_shapes=[pltpu.VMEM((VEC_HPS, GLA_DV, GLA_DK), F32)],
        compiler_params=_params("parallel", "parallel", "arbitrary"),
        name="gla_chunk",
    )(proj, proj, proj, proj, proj, w2p, b2, norm_w, sel)


def _hgrn_lower_bound(lbraw, layer):
    m = jnp.max(lbraw, axis=0, keepdims=True)
    ex = jnp.exp(lbraw - m)
    sm = ex / jnp.sum(ex, axis=0, keepdims=True)
    acc = sm[0:1]
    for i in range(1, layer + 1):
        acc = acc + sm[i:i + 1]
    return acc - sm[0:1]


def _hgrn_gates(q_raw, f_raw, lb):
    forget = lb + (1.0 - lb) * _sigmoid(f_raw)
    return _silu(q_raw), 1.0 - forget, jnp.log(forget)


def _hgrn_chunk_kernel(layer, q_ref, f_ref, i_ref, go_ref, lb_ref, nw_ref, sel_ref,
                       o_ref, s_ref, st_ref):
    r = pl.program_id(2)

    @pl.when(r == 0)
    def _():
        st_ref[...] = jnp.zeros_like(st_ref)

    lb_all = _hgrn_lower_bound(lb_ref[...], layer)
    heads = []
    for hh in range(VEC_HPS):
        kc = slice(hh * HG_EXPAND, (hh + 1) * HG_EXPAND)
        vc = slice(hh * HG_DI, (hh + 1) * HG_DI)
        q, k, g = _hgrn_gates(q_ref[:, kc], f_ref[:, kc], lb_all[:, kc])
        heads.append((q, k, i_ref[:, vc], g, st_ref[hh]))
    finals = []
    for hh, (o, st) in enumerate(_vec_heads(heads, sel_ref[...], _OnePass)):
        vc = slice(hh * HG_DI, (hh + 1) * HG_DI)
        o_ref[:, vc] = _rms(o, nw_ref[...]) * _silu(go_ref[:, vc])
        st_ref[hh] = st
        finals.append(st)

    @pl.when(r == pl.num_programs(2) - 1)
    def _():
        for hh in range(VEC_HPS):
            s_ref[0, hh] = finals[hh].T


def _hgrn_prompt(proj, lower_bounds, norm_w, sel, layer, bsz, seq):
    nr = seq // VEC_ROWS
    ng = HG_HEADS // VEC_HPS
    kw = VEC_HPS * HG_EXPAND
    vw = VEC_HPS * HG_DI
    row = lambda off: (lambda b, h, r: (b * nr + r, off + h))
    return pl.pallas_call(
        functools.partial(_hgrn_chunk_kernel, layer),
        out_shape=(jax.ShapeDtypeStruct((bsz * seq, HG_I), F32),
                   jax.ShapeDtypeStruct((bsz, HG_HEADS, HG_EXPAND, HG_DI), F32)),
        grid=(bsz, ng, nr),
        in_specs=[pl.BlockSpec((VEC_ROWS, kw), row(0)),
                  pl.BlockSpec((VEC_ROWS, kw), row(ng)),
                  pl.BlockSpec((VEC_ROWS, vw), row(2 * ng)),
                  pl.BlockSpec((VEC_ROWS, vw), row(3 * ng)),
                  pl.BlockSpec((DEPTH, kw), lambda b, h, r: (0, h)),
                  pl.BlockSpec((1, HG_DI), lambda b, h, r: (0, 0)),
                  pl.BlockSpec(sel.shape, lambda b, h, r: (0, 0))],
        out_specs=(pl.BlockSpec((VEC_ROWS, vw), lambda b, h, r: (b * nr + r, h)),
                   pl.BlockSpec((1, VEC_HPS, HG_EXPAND, HG_DI), lambda b, h, r: (b, h, 0, 0))),
        scratch_shapes=[pltpu.VMEM((VEC_HPS, HG_DI, HG_EXPAND), F32)],
        compiler_params=_params("parallel", "parallel", "arbitrary"),
        name="hgrn_chunk",
    )(proj, proj, proj, proj, lower_bounds, norm_w, sel)


def _conv_silu(xp, cw, cb, n, lead):
    acc = cb + cw[SSD_CONV - 1:SSD_CONV] * xp[lead:lead + n]
    for m in range(1, SSD_CONV):
        acc = acc + cw[SSD_CONV - 1 - m:SSD_CONV - m] * xp[lead - m:lead - m + n]
    return _silu(acc)


def _ssd_gate_norm(y, z, nw):
    yz = y * _silu(z)
    parts = []
    for g in range(SSD_GROUPS):
        cols = slice(g * SSD_GROUP_W, (g + 1) * SSD_GROUP_W)
        parts.append(_rms(yz[:, cols], nw[:, cols]))
    return jnp.concatenate(parts, axis=1)


def _ssd_chunk_kernel(z_ref, xbc_ref, sm_ref, cw_ref, cb_ref, dtb_ref, alog_ref, dsk_ref,
                      nw_ref, ex_ref, o_ref, s_ref, conv_ref, st_ref, prev_ref):
    r = pl.program_id(1)
    c = SSD_CHUNK
    mm = _ThreePass

    @pl.when(r == 0)
    def _():
        st_ref[...] = jnp.zeros_like(st_ref)
        prev_ref[...] = jnp.zeros_like(prev_ref)

    x_raw = xbc_ref[...]
    xp = jnp.concatenate([prev_ref[...], x_raw], axis=0)
    prev_ref[...] = x_raw[c - 8:c]
    xc = _conv_silu(xp, cw_ref[...], cb_ref[...], c, 8)
    xs = xc[:, :SSD_INNER]
    bm = xc[:, SSD_INNER:SSD_INNER + SSD_BC]
    cm = xc[:, SSD_INNER + SSD_BC:]

    dt = _softplus(sm_ref[...] + dtb_ref[...])
    a_neg = -jnp.exp(alog_ref[...])
    big_g = _dot_exact_rhs(_tril(c).astype(BF16), dt * a_neg)
    g_t = big_g.T
    g_last = big_g[c - 1:c, :]
    ex = ex_ref[...]
    dt_x = _dot_exact_lhs(dt, ex)
    eg_x = _dot_exact_lhs(jnp.exp(big_g), ex)
    w_x = _dot_exact_lhs(dt * jnp.exp(g_last - big_g), ex)
    xdt = xs * dt_x
    xw = xs * w_x
    causal = _tril(c)
    lane = lax.broadcasted_iota(jnp.int32, (c, LANES), 1)
    st = st_ref[...]
    y_parts = []
    u_parts = []
    for g in range(SSD_GROUPS):
        gcols = slice(g * SSD_GROUP_W, (g + 1) * SSD_GROUP_W)
        bg = bm[:, g * SSD_STATE:(g + 1) * SSD_STATE]
        cg = cm[:, g * SSD_STATE:(g + 1) * SSD_STATE]
        sc = mm.nt(cg, bg)
        inter = mm.nn(cg, st[:, gcols])
        u_parts.append(mm.tn(bg, xw[:, gcols]))
        pair_cols = []
        heads_per_group = SSD_HEADS // SSD_GROUPS
        for p in range(heads_per_group // 2):
            h0 = g * heads_per_group + 2 * p
            xpair = xdt[:, h0 * SSD_HEADDIM:(h0 + 2) * SSD_HEADDIM]
            ws = []
            for h in (h0, h0 + 1):
                diff = big_g[:, h:h + 1] - g_t[h:h + 1, :]
                ws.append(sc * jnp.exp(jnp.where(causal, diff, -jnp.inf)))
            x_diag = jnp.concatenate([jnp.where(lane < SSD_HEADDIM, xpair, 0.0),
                                      jnp.where(lane < SSD_HEADDIM, 0.0, xpair)], axis=0)
            pair_cols.append(mm.nn(jnp.concatenate(ws, axis=1), x_diag))
        y_intra = jnp.concatenate(pair_cols, axis=1)
        y_parts.append(y_intra + inter * eg_x[:, gcols])
    y = jnp.concatenate(y_parts, axis=1) + dsk_ref[...] * xs
    o_ref[...] = _ssd_gate_norm(y, z_ref[...], nw_ref[...])
    st = st * eg_x[c - 1:c, :] + jnp.concatenate(u_parts, axis=1)
    st_ref[...] = st

    @pl.when(r == pl.num_programs(1) - 1)
    def _():
        s_ref[0] = st
        conv_ref[0] = x_raw[c - (SSD_CONV - 1):c]


def _ssd_prompt(proj, conv_w, conv_b, dtb_p, alog_p, dskip_x, norm_w, expand, bsz, seq):
    nr = seq // SSD_CHUNK
    fixed = lambda b, r: (0, 0)
    return pl.pallas_call(
        _ssd_chunk_kernel,
        out_shape=(jax.ShapeDtypeStruct((bsz * seq, SSD_INNER), F32),
                   jax.ShapeDtypeStruct((bsz, SSD_STATE, SSD_INNER), F32),
                   jax.ShapeDtypeStruct((bsz, SSD_CONV - 1, SSD_CONV_DIM), F32)),
        grid=(bsz, nr),
        in_specs=[pl.BlockSpec((SSD_CHUNK, SSD_INNER), lambda b, r: (b * nr + r, AB_Z // SSD_INNER)),
                  pl.BlockSpec((SSD_CHUNK, SSD_CONV_DIM), lambda b, r: (b * nr + r, AB_XBC // SSD_CONV_DIM)),
                  pl.BlockSpec((SSD_CHUNK, LANES), lambda b, r: (b * nr + r, AB_SMALL // LANES)),
                  pl.BlockSpec((SSD_CONV, SSD_CONV_DIM), fixed),
                  pl.BlockSpec((1, SSD_CONV_DIM), fixed),
                  pl.BlockSpec((1, LANES), fixed),
                  pl.BlockSpec((1, LANES), fixed),
                  pl.BlockSpec((1, SSD_INNER), fixed),
                  pl.BlockSpec((1, SSD_INNER), fixed),
                  pl.BlockSpec((LANES, SSD_INNER), fixed)],
        out_specs=(pl.BlockSpec((SSD_CHUNK, SSD_INNER), lambda b, r: (b * nr + r, 0)),
                   pl.BlockSpec((1, SSD_STATE, SSD_INNER), lambda b, r: (b, 0, 0)),
                   pl.BlockSpec((1, SSD_CONV - 1, SSD_CONV_DIM), lambda b, r: (b, 0, 0))),
        scratch_shapes=[pltpu.VMEM((SSD_STATE, SSD_INNER), F32),
                        pltpu.VMEM((8, SSD_CONV_DIM), F32)],
        compiler_params=_params("parallel", "arbitrary"),
        name="ssd_chunk",
    )(proj, proj, proj, conv_w, conv_b, dtb_p, alog_p, dskip_x, norm_w, expand)


def _ab_prep_kernel(q_ref, sm_ref, xbc_ref, cs_ref, w2_ref, b2_ref, cw_ref, cb_ref, dtb_ref,
                    alog_ref, ex_ref, qs_ref, dec_ref, xc_ref, xdt_ref, dax_ref, cs_out_ref):
    sm = sm_ref[...]
    gk = _log_sigmoid(_ThreePass.nn(sm, w2_ref[...]) + b2_ref[...]) / GLA_NORMALIZER
    qs_ref[...] = q_ref[...] * (GLA_DK ** -0.5)
    dec_ref[...] = jnp.exp(gk)
    cw = cw_ref[...]
    x_raw = xbc_ref[...]
    acc = cb_ref[...] + cw[SSD_CONV - 1:SSD_CONV] * x_raw
    for j in range(SSD_CONV - 1):
        acc = acc + cw[j:j + 1] * cs_ref[j]
    xc = _silu(acc)
    xc_ref[...] = xc
    for j in range(SSD_CONV - 2):
        cs_out_ref[j] = cs_ref[j + 1]
    cs_out_ref[SSD_CONV - 2] = x_raw
    dt = _softplus(sm + dtb_ref[...])
    ex = ex_ref[...]
    xdt_ref[...] = xc[:, :SSD_INNER] * _dot_exact_lhs(dt, ex)
    dax_ref[...] = _dot_exact_lhs(jnp.exp(dt * -jnp.exp(alog_ref[...])), ex)


def _ab_prep(proj, conv_state, w2_wide, b2_wide, conv_w, conv_b, dtb_p, alog_p, expand):
    bsz = proj.shape[0]
    fixed = lambda i: (0, 0)
    sds = jax.ShapeDtypeStruct
    return pl.pallas_call(
        _ab_prep_kernel,
        out_shape=(sds((bsz, GLA_KEY), F32), sds((bsz, GLA_KEY), F32),
                   sds((bsz, SSD_CONV_DIM), F32), sds((bsz, SSD_INNER), F32),
                   sds((bsz, SSD_INNER), F32),
                   sds((SSD_CONV - 1, bsz, SSD_CONV_DIM), F32)),
        grid=(1,),
        in_specs=[pl.BlockSpec((bsz, GLA_KEY), lambda i: (0, AB_Q // GLA_KEY)),
                  pl.BlockSpec((bsz, LANES), lambda i: (0, AB_SMALL // LANES)),
                  pl.BlockSpec((bsz, SSD_CONV_DIM), lambda i: (0, AB_XBC // SSD_CONV_DIM)),
                  pl.BlockSpec((SSD_CONV - 1, bsz, SSD_CONV_DIM), lambda i: (0, 0, 0)),
                  pl.BlockSpec((LANES, GLA_KEY), fixed),
                  pl.BlockSpec((1, GLA_KEY), fixed),
                  pl.BlockSpec((SSD_CONV, SSD_CONV_DIM), fixed),
                  pl.BlockSpec((1, SSD_CONV_DIM), fixed),
                  pl.BlockSpec((1, LANES), fixed),
                  pl.BlockSpec((1, LANES), fixed),
                  pl.BlockSpec((LANES, SSD_INNER), fixed)],
        out_specs=(pl.BlockSpec((bsz, GLA_KEY), fixed), pl.BlockSpec((bsz, GLA_KEY), fixed),
                   pl.BlockSpec((bsz, SSD_CONV_DIM), fixed), pl.BlockSpec((bsz, SSD_INNER), fixed),
                   pl.BlockSpec((bsz, SSD_INNER), fixed),
                   pl.BlockSpec((SSD_CONV - 1, bsz, SSD_CONV_DIM), lambda i: (0, 0, 0))),
        compiler_params=_params("arbitrary"),
        name="ab_prep",
    )(proj, proj, proj, conv_state, w2_wide, b2_wide, conv_w, conv_b, dtb_p, alog_p, expand)


def _hgrn_prep_kernel(layer, q_ref, f_ref, lb_ref, qs_ref, k_ref, dec_ref):
    lb = _hgrn_lower_bound(lb_ref[...], layer)
    forget = lb + (1.0 - lb) * _sigmoid(f_ref[...])
    qs_ref[...] = _silu(q_ref[...])
    k_ref[...] = 1.0 - forget
    dec_ref[...] = jnp.exp(jnp.log(forget))


def _hgrn_prep(proj, lower_bounds, layer):
    bsz = proj.shape[0]
    blk = lambda j: pl.BlockSpec((bsz, HG_F), lambda i: (0, j))
    return pl.pallas_call(
        functools.partial(_hgrn_prep_kernel, layer),
        out_shape=tuple(jax.ShapeDtypeStruct((bsz, HG_F), F32) for _ in range(3)),
        grid=(1,),
        in_specs=[blk(0), blk(1), pl.BlockSpec((DEPTH, HG_F), lambda i: (0, 0))],
        out_specs=tuple(blk(0) for _ in range(3)),
        compiler_params=_params("arbitrary"),
        name="hgrn_prep",
    )(proj, proj, lower_bounds)


def _vec_step_kernel(mm, s_ref, q_ref, k_ref, d_ref, v_ref, go_ref, nw_ref, so_ref, o_ref):
    q = q_ref[...]
    kt = k_ref[0, 0]
    dt = d_ref[0, 0]
    v = v_ref[...]
    sb = v.shape[0]
    row = lax.broadcasted_iota(jnp.int32, v.shape, 0)
    new = []
    for b in range(sb):
        only_b = row == b
        decay = _dot_exact_lhs(dt, jnp.where(only_b, 1.0, 0.0).astype(BF16))
        new.append(s_ref[b, 0] * decay + mm.nn(kt, jnp.where(only_b, v, 0.0)))
    for b in range(sb):
        so_ref[b, 0] = new[b]
    o = jnp.concatenate([mm.nn(q, new[b])[b:b + 1] for b in range(sb)], axis=0)
    o_ref[...] = _rms(o, nw_ref[...]) * _silu(go_ref[...])


def _vec_step(state, q_rows, k_cols, d_cols, vsrc, v_off, gsrc, g_off, norm_w, mm):
    bsz, nh, kdim, vdim = state.shape
    sb = k_cols.shape[3]
    col = lambda j, h: (h, j, 0, 0)
    return pl.pallas_call(
        functools.partial(_vec_step_kernel, mm),
        out_shape=(jax.ShapeDtypeStruct(state.shape, F32),
                   jax.ShapeDtypeStruct((bsz, nh * vdim), F32)),
        grid=(bsz // sb, nh),
        in_specs=[pl.BlockSpec((sb, 1, kdim, vdim), lambda j, h: (j, h, 0, 0)),
                  pl.BlockSpec((sb, kdim), lambda j, h: (j, h)),
                  pl.BlockSpec((1, 1, kdim, sb), col),
                  pl.BlockSpec((1, 1, kdim, sb), col),
                  pl.BlockSpec((sb, vdim), lambda j, h: (j, v_off + h)),
                  pl.BlockSpec((sb, vdim), lambda j, h: (j, g_off + h)),
                  pl.BlockSpec((1, vdim), lambda j, h: (0, 0))],
        out_specs=(pl.BlockSpec((sb, 1, kdim, vdim), lambda j, h: (j, h, 0, 0)),
                   pl.BlockSpec((sb, vdim), lambda j, h: (j, h))),
        compiler_params=_params("parallel", "parallel"),
        name="vec_step",
    )(state, q_rows, k_cols, d_cols, vsrc, gsrc, norm_w)


def _ssd_step_kernel(s_ref, b_ref, c_ref, x_ref, xdt_ref, dax_ref, dsk_ref, so_ref, y_ref):
    mm = _ThreePass
    bt = b_ref[0, 0]
    c = c_ref[...]
    xdt = xdt_ref[...]
    dax = dax_ref[...]
    hpg = SSD_HEADS // SSD_GROUPS
    row = lax.broadcasted_iota(jnp.int32, xdt.shape, 0)
    new = []
    for b in range(STEP_B):
        outer = mm.nn(bt, jnp.where(row == b, xdt, 0.0))
        per_head = []
        for hh in range(hpg):
            cols = slice(hh * SSD_HEADDIM, (hh + 1) * SSD_HEADDIM)
            sn = s_ref[b, hh] * dax[b:b + 1, cols] + outer[:, cols]
            so_ref[b, hh] = sn
            per_head.append(sn)
        new.append(per_head)
    rows = [jnp.concatenate([mm.nn(c, new[b][hh])[b:b + 1] for hh in range(hpg)], axis=1)
            for b in range(STEP_B)]
    y_ref[...] = jnp.concatenate(rows, axis=0) + dsk_ref[...] * x_ref[...]


def _ssd_step(state, b_cols, xc, xdt, dax, dskip_x):
    bsz = state.shape[0]
    hpg = SSD_HEADS // SSD_GROUPS
    grp = lambda j, g: (j, g)
    c_off = (SSD_INNER + SSD_BC) // SSD_STATE
    return pl.pallas_call(
        _ssd_step_kernel,
        out_shape=(jax.ShapeDtypeStruct(state.shape, F32),
                   jax.ShapeDtypeStruct((bsz, SSD_INNER), F32)),
        grid=(bsz // STEP_B, SSD_GROUPS),
        in_specs=[pl.BlockSpec((STEP_B, hpg, SSD_STATE, SSD_HEADDIM), lambda j, g: (j, g, 0, 0)),
                  pl.BlockSpec((1, 1, SSD_STATE, STEP_B), lambda j, g: (g, j, 0, 0)),
                  pl.BlockSpec((STEP_B, SSD_STATE), lambda j, g: (j, c_off + g)),
                  pl.BlockSpec((STEP_B, SSD_GROUP_W), grp),
                  pl.BlockSpec((STEP_B, SSD_GROUP_W), grp),
                  pl.BlockSpec((STEP_B, SSD_GROUP_W), grp),
                  pl.BlockSpec((1, SSD_GROUP_W), lambda j, g: (0, g))],
        out_specs=(pl.BlockSpec((STEP_B, hpg, SSD_STATE, SSD_HEADDIM), lambda j, g: (j, g, 0, 0)),
                   pl.BlockSpec((STEP_B, SSD_GROUP_W), grp)),
        compiler_params=_params("parallel", "parallel"),
        name="ssd_step",
    )(state, b_cols, xc, xc, xdt, dax, dskip_x)


def _ssd_post_kernel(y_ref, z_ref, nw_ref, o_ref):
    o_ref[...] = _ssd_gate_norm(y_ref[...], z_ref[...], nw_ref[...])


def _ssd_post(y, proj, norm_w):
    bsz = y.shape[0]
    return pl.pallas_call(
        _ssd_post_kernel,
        out_shape=jax.ShapeDtypeStruct((bsz, SSD_INNER), F32),
        grid=(1,),
        in_specs=[pl.BlockSpec((bsz, SSD_INNER), lambda i: (0, 0)),
                  pl.BlockSpec((bsz, SSD_INNER), lambda i: (0, AB_Z // SSD_INNER)),
                  pl.BlockSpec((1, SSD_INNER), lambda i: (0, 0))],
        out_specs=pl.BlockSpec((bsz, SSD_INNER), lambda i: (0, 0)),
        compiler_params=_params("arbitrary"),
        name="ssd_post",
    )(y, proj, norm_w)


def _to_cols(a, nh, sb=STEP_B):
    bsz = a.shape[0]
    return a.reshape(bsz // sb, sb, nh, -1).transpose(2, 0, 3, 1)


def _prep_weights(w_in_ab, w_gk2, b_gk2, gla_norm_w, conv_w, conv_b, dt_bias, a_log, d_skip,
                  ssd_norm_w, w_out_ab, w_in_c, hg_norm_w, w_out_c, router_w, router_bias,
                  w_gate, w_up, w_down, ln1_w, ln1_b, ln2_w, ln2_b):
    offs = np.cumsum([0, GLA_KEY, GLA_KEY, GLA_VAL, GLA_VAL, GLA_RANK, SSD_INNER, SSD_CONV_DIM,
                      SSD_HEADS])
    sec = lambda w, i: w[:, offs[i]:offs[i + 1]]
    w = w_in_ab[0]
    pad = jnp.zeros((D_MODEL, LANES - SSD_HEADS - GLA_RANK), w.dtype)
    w_ab = jnp.concatenate([sec(w, 5), sec(w, 2), sec(w, 3), sec(w, 6), sec(w, 0), sec(w, 1),
                            sec(w, 7), sec(w, 4), pad], axis=1)
    hi_lo = lambda m: (m.astype(BF16), (m - m.astype(BF16).astype(F32)).astype(BF16))
    pad_e = lambda m: jnp.pad(m, ((0, LANES - N_EXPERTS), (0, 0)))
    router_pieces = jnp.concatenate([pad_e(piece) for piece in _split3(router_w.T)], axis=0)
    w2_wide = jnp.zeros((LANES, GLA_KEY), F32).at[SSD_HEADS:SSD_HEADS + GLA_RANK].set(w_gk2[0])
    lane_pad = lambda v: jnp.zeros((1, LANES), F32).at[0, :SSD_HEADS].set(v)
    expand = np.zeros((LANES, SSD_INNER), np.float32)
    for h in range(SSD_HEADS):
        expand[h, h * SSD_HEADDIM:(h + 1) * SSD_HEADDIM] = 1.0
    return dict(
        w_ab=hi_lo(w_ab),
        w2_wide=w2_wide,
        w2_heads=w2_wide.reshape(LANES, GLA_HEADS, GLA_DK).transpose(1, 0, 2),
        b2_wide=b_gk2[0].reshape(1, GLA_KEY),
        b2_heads=b_gk2[0].reshape(GLA_HEADS, 1, GLA_DK),
        gla_norm_w=gla_norm_w[0].reshape(1, GLA_DV),
        conv_w=conv_w[0], conv_b=conv_b[0].reshape(1, SSD_CONV_DIM),
        dtb_p=lane_pad(dt_bias[0]), alog_p=lane_pad(a_log[0]),
        dskip_x=jnp.repeat(d_skip[0], SSD_HEADDIM).reshape(1, SSD_INNER),
        ssd_norm_w=ssd_norm_w[0].reshape(1, SSD_INNER),
        expand=jnp.asarray(expand, BF16),
        prefix_sel=jnp.asarray(_prefix_selector(), BF16),
        w_out_gla=hi_lo(w_out_ab[0, :GLA_VAL]),
        w_out_ssd=hi_lo(w_out_ab[0, GLA_VAL:]),
        w_c=w_in_c[0].astype(BF16),
        hg_norm_w=hg_norm_w[0].reshape(1, HG_DI),
        w_out_c=w_out_c[0].astype(BF16),
        rwt=router_pieces,
        rbias=router_bias.reshape(N_EXPERTS, 1),
        w_gate=w_gate.reshape(DEPTH * N_EXPERTS, D_MODEL, D_FF_EXPERT),
        w_up=w_up.reshape(DEPTH * N_EXPERTS, D_MODEL, D_FF_EXPERT),
        w_down=w_down.reshape(DEPTH * N_EXPERTS, D_FF_EXPERT, D_MODEL),
        ln1_w=ln1_w.reshape(DEPTH, 1, D_MODEL), ln1_b=ln1_b.reshape(DEPTH, 1, D_MODEL),
        ln2_w=ln2_w.reshape(DEPTH, 1, D_MODEL), ln2_b=ln2_b.reshape(DEPTH, 1, D_MODEL),
    )


def _ffn(x, p, layer, tm, tm_moe):
    gates = _router(x, p['rwt'], p['rbias'], tm)
    return _moe_ln(x, gates, p['w_gate'], p['w_up'], p['w_down'], layer,
                   p['ln2_w'][layer], p['ln2_b'][layer], tm_moe)


def _ssd_state_from_wide(s_wide):
    bsz = s_wide.shape[0]
    return s_wide.reshape(bsz, SSD_STATE, SSD_HEADS, SSD_HEADDIM).transpose(0, 2, 1, 3)


def _trunk_prompt(x3, p, lower_bounds, tm, tn_ab, tn_c):
    bsz, seq, _ = x3.shape
    x = x3.reshape(bsz * seq, D_MODEL)
    tm_big = 2 * tm
    proj = _proj(x, p['w_ab'], tm_big, tn_ab)
    o_gla, s_gla = _gla_prompt(proj, p['w2_heads'], p['b2_heads'], p['gla_norm_w'],
                               p['prefix_sel'], bsz, seq)
    yz, s_ssd, s_conv = _ssd_prompt(proj, p['conv_w'], p['conv_b'], p['dtb_p'], p['alog_p'],
                                    p['dskip_x'], p['ssd_norm_w'], p['expand'], bsz, seq)
    x = _outproj_ln([o_gla, yz], [p['w_out_gla'], p['w_out_ssd']], x, p['ln1_w'][0], p['ln1_b'][0], tm)
    x = _ffn_sorted(x, p, 0, tm, tm_big)
    proj_c = _proj(x, p['w_c'], tm_big, tn_c)
    o_hg, s_hg = _hgrn_prompt(proj_c, lower_bounds, p['hg_norm_w'], p['prefix_sel'], 1, bsz, seq)
    x = _outproj_ln([o_hg], [p['w_out_c']], x, p['ln1_w'][1], p['ln1_b'][1], tm)
    x = _ffn_sorted(x, p, 1, tm, tm_big)
    return (x.reshape(bsz, seq, D_MODEL), s_gla[None], _ssd_state_from_wide(s_ssd)[None],
            s_conv[None], s_hg[None])


def _trunk_sample(x3, st_gla, st_ssd, st_conv, st_hg, p, lower_bounds, tn_ab, tn_c):
    bsz = x3.shape[0]
    tm = bsz
    x = x3.reshape(bsz, D_MODEL)
    proj = _proj(x, p['w_ab'], tm, tn_ab)
    qs, dec, xc, xdt, dax, conv_new = _ab_prep(proj, st_conv[0].transpose(1, 0, 2), p['w2_wide'],
                                               p['b2_wide'], p['conv_w'], p['conv_b'], p['dtb_p'],
                                               p['alog_p'], p['expand'])
    conv_new = conv_new.transpose(1, 0, 2)
    k_gla = proj[:, AB_K:AB_K + GLA_KEY]
    s_gla, o_gla = _vec_step(st_gla[0], qs, _to_cols(k_gla, GLA_HEADS, VEC_STEP_B),
                             _to_cols(dec, GLA_HEADS, VEC_STEP_B), proj, AB_V // GLA_DV, proj,
                             AB_GOUT // GLA_DV, p['gla_norm_w'], _ThreePass)
    s_ssd, y = _ssd_step(st_ssd[0], _to_cols(xc[:, SSD_INNER:SSD_INNER + SSD_BC], SSD_GROUPS),
                         xc, xdt, dax, p['dskip_x'])
    yz = _ssd_post(y, proj, p['ssd_norm_w'])
    x = _outproj_ln([o_gla, yz], [p['w_out_gla'], p['w_out_ssd']], x, p['ln1_w'][0], p['ln1_b'][0], tm)
    x = _ffn(x, p, 0, tm, tm)
    proj_c = _proj(x, p['w_c'], tm, tn_c)
    qh, kh, dh = _hgrn_prep(proj_c, lower_bounds, 1)
    s_hg, o_hg = _vec_step(st_hg[0], qh, _to_cols(kh, HG_HEADS, VEC_STEP_B),
                           _to_cols(dh, HG_HEADS, VEC_STEP_B), proj_c, 2 * HG_HEADS, proj_c, 3 * HG_HEADS,
                           p['hg_norm_w'], _OnePass)
    x = _outproj_ln([o_hg], [p['w_out_c']], x, p['ln1_w'][1], p['ln1_b'][1], tm)
    x = _ffn(x, p, 1, tm, tm)
    return x.reshape(bsz, 1, D_MODEL), s_gla[None], s_ssd[None], conv_new[None], s_hg[None]


def kernel(x_prompt, x_sample, state_gla, state_ssd, state_conv, state_hgrn, w_in_ab, w_gk2, b_gk2, gla_norm_w, conv_w, conv_b, dt_bias, a_log, d_skip, ssd_norm_w, w_out_ab, w_in_c, lower_bounds, hg_norm_w, w_out_c, router_w, router_bias, w_gate, w_up, w_down, ln1_w, ln1_b, ln2_w, ln2_b):
    p = _prep_weights(w_in_ab, w_gk2, b_gk2, gla_norm_w, conv_w, conv_b, dt_bias, a_log, d_skip,
                      ssd_norm_w, w_out_ab, w_in_c, hg_norm_w, w_out_c, router_w, router_bias,
                      w_gate, w_up, w_down, ln1_w, ln1_b, ln2_w, ln2_b)
    y_p, gla_p, ssd_p, conv_p, hg_p = _trunk_prompt(x_prompt, p, lower_bounds, 512, 1152, 1024)
    y_s, gla_s, ssd_s, conv_s, hg_s = _trunk_sample(x_sample, state_gla, state_ssd, state_conv,
                                                    state_hgrn, p, lower_bounds, 1152, 1024)
    return (y_p, y_s, gla_p, ssd_p, conv_p, hg_p, gla_s, ssd_s, conv_s, hg_s)
```

```python
import functools

import numpy as np
import jax
import jax.numpy as jnp
from jax import lax
from jax.experimental import pallas as pl
from jax.experimental.pallas import tpu as pltpu

F32 = jnp.float32
BF16 = jnp.bfloat16

D_MODEL = 1024
DEPTH = 2
GLA_HEADS = 4
GLA_DK = 128
GLA_DV = 256
GLA_KEY = GLA_HEADS * GLA_DK
GLA_VAL = GLA_HEADS * GLA_DV
GLA_RANK = 16
GLA_NORMALIZER = 16.0
SSD_INNER = 1024
SSD_HEADDIM = 64
SSD_HEADS = 16
SSD_STATE = 128
SSD_GROUPS = 2
SSD_CONV = 4
SSD_GROUP_W = SSD_INNER // SSD_GROUPS
SSD_BC = SSD_GROUPS * SSD_STATE
SSD_CONV_DIM = SSD_INNER + 2 * SSD_BC
HG_EXPAND = 128
HG_HEADS = 8
HG_F = HG_HEADS * HG_EXPAND
HG_I = D_MODEL
HG_DI = HG_I // HG_HEADS
N_EXPERTS = 16
N_GROUPS = 4
EXPERTS_PER_GROUP = 4
D_FF_EXPERT = 512
ALPHA = (2 * DEPTH) ** 0.25
EPS = 1e-5

LANES = 128
VMEM_LIMIT = 48 * 1024 * 1024

AB_Z = 0
AB_V = 1024
AB_GOUT = 2048
AB_XBC = 3072
AB_Q = 4608
AB_K = 5120
AB_SMALL = 5632
AB_COLS = 5760
C_COLS = 4096

VEC_CHUNK = 64
VEC_SUB = 16
VEC_TILE = 256
VEC_ROWS = 512
VEC_HPS = 4
SSD_CHUNK = 128
STEP_B = 8
VEC_STEP_B = 16


def _params(*sem):
    return pltpu.CompilerParams(dimension_semantics=sem, vmem_limit_bytes=VMEM_LIMIT)


_NN = (((1,), (0,)), ((), ()))
_NT = (((1,), (1,)), ((), ()))
_TN = (((0,), (0,)), ((), ()))


def _dot1(dims, a, b):
    return lax.dot_general(a.astype(BF16), b.astype(BF16), dims, preferred_element_type=F32)


def _split2(a):
    hi = a.astype(BF16)
    return hi, (a - hi.astype(F32)).astype(BF16)


def _dot3(dims, a, b):
    ah, al = _split2(a)
    bh, bl = _split2(b)
    d = lambda x, y: lax.dot_general(x, y, dims, preferred_element_type=F32)
    return (d(al, bh) + d(ah, bl)) + d(ah, bh)


class _OnePass:
    nn = staticmethod(lambda a, b: _dot1(_NN, a, b))
    nt = staticmethod(lambda a, b: _dot1(_NT, a, b))
    tn = staticmethod(lambda a, b: _dot1(_TN, a, b))


class _ThreePass:
    nn = staticmethod(lambda a, b: _dot3(_NN, a, b))
    nt = staticmethod(lambda a, b: _dot3(_NT, a, b))
    tn = staticmethod(lambda a, b: _dot3(_TN, a, b))


def _dot(a, b):
    return _dot1(_NN, a, b)


def _dot_nt(a, b):
    return _dot1(_NT, a, b)


def _dot_tn(a, b):
    return _dot1(_TN, a, b)


def _split3(a):
    hi = a.astype(BF16)
    r1 = a - hi.astype(F32)
    mid = r1.astype(BF16)
    lo = (r1 - mid.astype(F32)).astype(BF16)
    return hi, mid, lo


def _dot_exact_rhs(sel, a):
    hi, mid, lo = _split3(a)
    d = lambda p: jnp.dot(sel, p, preferred_element_type=F32)
    return (d(lo) + d(mid)) + d(hi)


def _dot_exact_lhs(a, sel):
    hi, mid, lo = _split3(a)
    d = lambda p: jnp.dot(p, sel, preferred_element_type=F32)
    return (d(lo) + d(mid)) + d(hi)


def _tril(n):
    r = lax.broadcasted_iota(jnp.int32, (n, n), 0)
    c = lax.broadcasted_iota(jnp.int32, (n, n), 1)
    return r >= c


def _sigmoid(x):
    return 1.0 / (1.0 + jnp.exp(-x))


def _silu(x):
    return x * _sigmoid(x)


def _softplus(x):
    return jnp.maximum(x, 0.0) + jnp.log(1.0 + jnp.exp(-jnp.abs(x)))


def _log_sigmoid(x):
    return -_softplus(-x)


def _rms(x, w):
    return x * lax.rsqrt(jnp.mean(x * x, axis=-1, keepdims=True) + EPS) * w


def _layer_norm(x, w, b):
    mu = jnp.mean(x, axis=-1, keepdims=True)
    xc = x - mu
    var = jnp.mean(xc * xc, axis=-1, keepdims=True)
    return xc * lax.rsqrt(var + EPS) * w + b


def _proj_kernel(x_ref, w_ref, o_ref):
    o_ref[...] = jnp.dot(x_ref[...].astype(BF16), w_ref[...], preferred_element_type=F32)


def _proj3_kernel(x_ref, wh_ref, wl_ref, o_ref, xh_ref, xl_ref):
    @pl.when(pl.program_id(1) == 0)
    def _():
        hi, lo = _split2(x_ref[...])
        xh_ref[...] = hi
        xl_ref[...] = lo

    d = lambda a, b: jnp.dot(a, b, preferred_element_type=F32)
    xh = xh_ref[...]
    wh = wh_ref[...]
    o_ref[...] = (d(xl_ref[...], wh) + d(xh, wl_ref[...])) + d(xh, wh)


def _proj(x, w, tm, tn):
    t, k = x.shape
    three = isinstance(w, tuple)
    ws = w if three else (w,)
    n = ws[0].shape[1]
    return pl.pallas_call(
        _proj3_kernel if three else _proj_kernel,
        out_shape=jax.ShapeDtypeStruct((t, n), F32),
        grid=(t // tm, n // tn),
        in_specs=[pl.BlockSpec((tm, k), lambda i, j: (i, 0))]
                 + [pl.BlockSpec((k, tn), lambda i, j: (0, j)) for _ in ws],
        out_specs=pl.BlockSpec((tm, tn), lambda i, j: (i, j)),
        scratch_shapes=[pltpu.VMEM((tm, k), BF16), pltpu.VMEM((tm, k), BF16)] if three else [],
        compiler_params=_params("parallel", "arbitrary"),
        name="in_proj",
    )(x, *ws)


def _outproj_ln_kernel(n_in, three, *refs):
    a_refs = refs[:n_in]
    nw = 2 if three else 1
    w_refs = refs[n_in:n_in + nw * n_in]
    x_ref, lw_ref, lb_ref, o_ref = refs[n_in + nw * n_in:]
    d = lambda a, b: jnp.dot(a, b, preferred_element_type=F32)
    mix = None
    for i, a_ref in enumerate(a_refs):
        if three:
            ah, al = _split2(a_ref[...])
            wh = w_refs[2 * i][...]
            part = (d(al, wh) + d(ah, w_refs[2 * i + 1][...])) + d(ah, wh)
        else:
            part = d(a_ref[...].astype(BF16), w_refs[i][...])
        mix = part if mix is None else mix + part
    o_ref[...] = _layer_norm(ALPHA * x_ref[...] + mix, lw_ref[...], lb_ref[...])


def _outproj_ln(acts, ws, x, ln_w, ln_b, tm):
    t = x.shape[0]
    n_in = len(acts)
    three = isinstance(ws[0], tuple)
    flat_ws = [w for pair in ws for w in pair] if three else list(ws)
    row = lambda i: (i, 0)
    fixed = lambda i: (0, 0)
    in_specs = ([pl.BlockSpec((tm, a.shape[1]), row) for a in acts]
                + [pl.BlockSpec(w.shape, fixed) for w in flat_ws]
                + [pl.BlockSpec((tm, D_MODEL), row),
                   pl.BlockSpec((1, D_MODEL), fixed), pl.BlockSpec((1, D_MODEL), fixed)])
    return pl.pallas_call(
        functools.partial(_outproj_ln_kernel, n_in, three),
        out_shape=jax.ShapeDtypeStruct((t, D_MODEL), F32),
        grid=(t // tm,),
        in_specs=in_specs,
        out_specs=pl.BlockSpec((tm, D_MODEL), row),
        compiler_params=_params("parallel"),
        name="out_proj_ln",
    )(*acts, *flat_ws, x, ln_w, ln_b)


def _router_scores(x, rwt, bias):
    xh, xl = _split2(x)
    nt = lambda a, b: lax.dot_general(a, b, _NT, preferred_element_type=F32)
    a = nt(xh, rwt)
    b = nt(xl, rwt[:2 * LANES])
    by_token = ((a[:, 2 * LANES:] + b[:, LANES:]) + (a[:, LANES:2 * LANES] + b[:, :LANES])) + a[:, :LANES]
    scores = _sigmoid(by_token.T[:N_EXPERTS])
    return scores, scores + bias


def _best_group(sel):
    tm = sel.shape[1]
    s = [sel[e:e + 1, :] for e in range(N_EXPERTS)]
    grp = []
    for g in range(N_GROUPS):
        m = s[g * EXPERTS_PER_GROUP:(g + 1) * EXPERTS_PER_GROUP]
        best = None
        for i in range(EXPERTS_PER_GROUP):
            for j in range(i + 1, EXPERTS_PER_GROUP):
                p = m[i] + m[j]
                best = p if best is None else jnp.maximum(best, p)
        grp.append(best)
    best_g = jnp.zeros((1, tm), jnp.int32)
    best_v = grp[0]
    for g in range(1, N_GROUPS):
        upd = grp[g] > best_v
        best_g = jnp.where(upd, g, best_g)
        best_v = jnp.where(upd, grp[g], best_v)
    return best_g


def _top2(vals, weights):
    tm = vals[0].shape[1]
    neg = jnp.full((1, tm), -jnp.inf, F32)

    def first_argmax(rows):
        idx = jnp.zeros((1, tm), jnp.int32)
        top = rows[0]
        for e in range(1, len(rows)):
            upd = rows[e] > top
            idx = jnp.where(upd, e, idx)
            top = jnp.where(upd, rows[e], top)
        return idx

    idx1 = first_argmax(vals)
    idx2 = first_argmax([jnp.where(idx1 == e, neg, v) for e, v in enumerate(vals)])
    zero = jnp.zeros((1, tm), F32)
    w1 = zero
    w2 = zero
    for e, w in enumerate(weights):
        w1 = w1 + jnp.where(idx1 == e, w, zero)
        w2 = w2 + jnp.where(idx2 == e, w, zero)
    tot = w1 + w2
    g1 = w1 / tot
    g2 = w2 / tot
    return [jnp.where(idx1 == e, g1, zero) + jnp.where(idx2 == e, g2, zero)
            for e in range(len(vals))]


def _pad_rows(rows, tm):
    return jnp.concatenate(rows + [jnp.zeros((LANES - len(rows), tm), F32)], axis=0)


def _route_in_group(x, rwt, bias, group):
    tm = x.shape[0]
    scores, sel = _router_scores(x, rwt, bias)
    zero = jnp.zeros((1, tm), F32)
    vals, weights = [], []
    for m in range(EXPERTS_PER_GROUP):
        v = zero
        w = zero
        for g in range(N_GROUPS):
            e = g * EXPERTS_PER_GROUP + m
            v = jnp.where(group == g, sel[e:e + 1, :], v)
            w = jnp.where(group == g, scores[e:e + 1, :], w)
        vals.append(v)
        weights.append(w)
    return _pad_rows(_top2(vals, weights), tm)


def _route(x, rwt, bias):
    tm = x.shape[0]
    scores, sel = _router_scores(x, rwt, bias)
    s = [sel[e:e + 1, :] for e in range(N_EXPERTS)]
    sc = [scores[e:e + 1, :] for e in range(N_EXPERTS)]
    best_g = _best_group(sel)
    neg = jnp.full((1, tm), -jnp.inf, F32)
    ms = [jnp.where(best_g == e // EXPERTS_PER_GROUP, s[e], neg) for e in range(N_EXPERTS)]
    return _pad_rows(_top2(ms, sc), tm), best_g


def _router_kernel(with_gates, x_ref, rwt_ref, bias_ref, g_ref):
    tm = x_ref.shape[0]
    if with_gates:
        gates_t, best_g = _route(x_ref[...], rwt_ref[...], bias_ref[...])
    else:
        best_g = _best_group(_router_scores(x_ref[...], rwt_ref[...], bias_ref[...])[1])
        gates_t = jnp.zeros((LANES, tm), F32)
    row = lax.broadcasted_iota(jnp.int32, (LANES, tm), 0)
    gates_t = jnp.where(row == N_EXPERTS, best_g.astype(F32), gates_t)
    g_ref[...] = gates_t.T


def _router(x, rwt, bias, tm, with_gates=True):
    t = x.shape[0]
    return pl.pallas_call(
        functools.partial(_router_kernel, with_gates),
        out_shape=jax.ShapeDtypeStruct((t, LANES), F32),
        grid=(t // tm,),
        in_specs=[pl.BlockSpec((tm, D_MODEL), lambda i: (i, 0)),
                  pl.BlockSpec((3 * LANES, D_MODEL), lambda i: (0, 0)),
                  pl.BlockSpec((N_EXPERTS, 1), lambda i: (0, 0))],
        out_specs=pl.BlockSpec((tm, LANES), lambda i: (i, 0)),
        compiler_params=_params("parallel"),
        name="router",
    )(x, rwt, bias)


def _moe_kernel(x_ref, g_ref, wg_ref, wu_ref, wd_ref, lw_ref, lb_ref, o_ref, acc_ref, xb_ref):
    e = pl.program_id(1)

    @pl.when(e == 0)
    def _():
        xb_ref[...] = x_ref[...].astype(BF16)
        acc_ref[...] = jnp.zeros_like(acc_ref)

    xb = xb_ref[...]
    hg = jnp.dot(xb, wg_ref[0].astype(BF16), preferred_element_type=F32)
    hu = jnp.dot(xb, wu_ref[0].astype(BF16), preferred_element_type=F32)
    he = _silu(hg) * hu
    gates = g_ref[...]
    lane = lax.broadcasted_iota(jnp.int32, gates.shape, 1)
    ge = jnp.sum(jnp.where(lane == e, gates, 0.0), axis=1, keepdims=True)
    acc_ref[...] += ge * jnp.dot(he.astype(BF16), wd_ref[0].astype(BF16),
                                  preferred_element_type=F32)

    @pl.when(e == N_EXPERTS - 1)
    def _():
        o_ref[...] = _layer_norm(ALPHA * x_ref[...] + acc_ref[...], lw_ref[...], lb_ref[...])


def _moe_ln(x, gates, wg, wu, wd, layer, ln_w, ln_b, tm):
    t = x.shape[0]
    return pl.pallas_call(
        _moe_kernel,
        out_shape=jax.ShapeDtypeStruct((t, D_MODEL), F32),
        grid=(t // tm, N_EXPERTS),
        in_specs=[pl.BlockSpec((tm, D_MODEL), lambda i, e: (i, 0)),
                  pl.BlockSpec((tm, LANES), lambda i, e: (i, 0)),
                  pl.BlockSpec((1, D_MODEL, D_FF_EXPERT), lambda i, e: (layer * N_EXPERTS + e, 0, 0)),
                  pl.BlockSpec((1, D_MODEL, D_FF_EXPERT), lambda i, e: (layer * N_EXPERTS + e, 0, 0)),
                  pl.BlockSpec((1, D_FF_EXPERT, D_MODEL), lambda i, e: (layer * N_EXPERTS + e, 0, 0)),
                  pl.BlockSpec((1, D_MODEL), lambda i, e: (0, 0)),
                  pl.BlockSpec((1, D_MODEL), lambda i, e: (0, 0))],
        out_specs=pl.BlockSpec((tm, D_MODEL), lambda i, e: (i, 0)),
        scratch_shapes=[pltpu.VMEM((tm, D_MODEL), F32), pltpu.VMEM((tm, D_MODEL), BF16)],
        compiler_params=_params("parallel", "arbitrary"),
        name="moe_ln",
    )(x, gates, wg, wu, wd, ln_w, ln_b)


def _group_rank_kernel(g_ref, rank_ref, tot_ref, carry_ref):
    n = g_ref.shape[0]

    @pl.when(pl.program_id(0) == 0)
    def _():
        carry_ref[...] = jnp.zeros_like(carry_ref)

    grp = g_ref[:, N_EXPERTS:N_EXPERTS + 1].astype(jnp.int32)
    lane = lax.broadcasted_iota(jnp.int32, (n, LANES), 1)
    onehot = jnp.where(lane == grp, 1.0, 0.0)
    rr = lax.broadcasted_iota(jnp.int32, (n, n), 0)
    cc = lax.broadcasted_iota(jnp.int32, (n, n), 1)
    before = jnp.where(rr > cc, 1.0, 0.0).astype(BF16)
    earlier = jnp.dot(before, onehot.astype(BF16), preferred_element_type=F32) + carry_ref[...]
    rank = jnp.sum(onehot * earlier, axis=1, keepdims=True)
    rank_ref[...] = jnp.broadcast_to(rank, (n, LANES))
    carry_ref[...] += jnp.sum(onehot, axis=0, keepdims=True)
    tot_ref[...] = jnp.broadcast_to(carry_ref[...], tot_ref.shape)


def _group_rank(gmat, tr):
    t = gmat.shape[0]
    return pl.pallas_call(
        _group_rank_kernel,
        out_shape=(jax.ShapeDtypeStruct((t, LANES), F32), jax.ShapeDtypeStruct((8, LANES), F32)),
        grid=(t // tr,),
        in_specs=[pl.BlockSpec((tr, LANES), lambda i: (i, 0))],
        out_specs=(pl.BlockSpec((tr, LANES), lambda i: (i, 0)),
                   pl.BlockSpec((8, LANES), lambda i: (0, 0))),
        scratch_shapes=[pltpu.VMEM((1, LANES), F32)],
        compiler_params=_params("arbitrary"),
        name="group_rank",
    )(gmat)


def _row_copy(src, dst, sem):
    return pltpu.make_async_copy(src, dst, sem)


def _scatter_rows_kernel(dest_ref, x_ref, init_ref, o_hbm, buf_ref, sem):
    del init_ref
    n = x_ref.shape[0]
    base = pl.program_id(0) * n
    buf_ref[...] = x_ref[...].reshape(buf_ref.shape)

    def start(i, carry):
        _row_copy(buf_ref.at[i], o_hbm.at[dest_ref[base + i]], sem).start()
        return carry

    lax.fori_loop(0, n, start, 0)
    _row_copy(buf_ref, o_hbm.at[pl.ds(0, n)], sem).wait()


def _scatter_rows(x, dest, n_out, tr):
    t = x.shape[0]
    slabs = D_MODEL // LANES
    init = jnp.zeros((n_out, slabs, LANES), F32)
    return pl.pallas_call(
        _scatter_rows_kernel,
        out_shape=jax.ShapeDtypeStruct((n_out, slabs, LANES), F32),
        grid_spec=pltpu.PrefetchScalarGridSpec(
            num_scalar_prefetch=1,
            grid=(t // tr,),
            in_specs=[pl.BlockSpec((tr, D_MODEL), lambda i, d: (i, 0)),
                      pl.BlockSpec(memory_space=pl.ANY)],
            out_specs=pl.BlockSpec(memory_space=pl.ANY),
            scratch_shapes=[pltpu.VMEM((tr, slabs, LANES), F32), pltpu.SemaphoreType.DMA(())]),
        input_output_aliases={2: 0},
        compiler_params=_params("arbitrary"),
        name="scatter_rows",
    )(dest, x, init)


def _gather_rows_kernel(src_ref, y_hbm, o_ref, buf_ref, sems):
    n = o_ref.shape[0]
    i = pl.program_id(0)
    slot = i % 2

    def issue(step, to_slot):
        def start(r, carry):
            _row_copy(y_hbm.at[src_ref[step * n + r]], buf_ref.at[to_slot, r],
                      sems.at[to_slot]).start()
            return carry
        lax.fori_loop(0, n, start, 0)

    @pl.when(i == 0)
    def _():
        issue(0, 0)

    @pl.when(i + 1 < pl.num_programs(0))
    def _():
        issue(i + 1, 1 - slot)

    _row_copy(y_hbm.at[pl.ds(0, n)], buf_ref.at[slot], sems.at[slot]).wait()
    o_ref[...] = buf_ref[slot].reshape(o_ref.shape)


def _gather_rows(y3, src, tr):
    t = src.shape[0]
    slabs = D_MODEL // LANES
    return pl.pallas_call(
        _gather_rows_kernel,
        out_shape=jax.ShapeDtypeStruct((t, D_MODEL), F32),
        grid_spec=pltpu.PrefetchScalarGridSpec(
            num_scalar_prefetch=1,
            grid=(t // tr,),
            in_specs=[pl.BlockSpec(memory_space=pl.ANY)],
            out_specs=pl.BlockSpec((tr, D_MODEL), lambda i, d: (i, 0)),
            scratch_shapes=[pltpu.VMEM((2, tr, slabs, LANES), F32),
                            pltpu.SemaphoreType.DMA((2,))]),
        compiler_params=_params("arbitrary"),
        name="gather_rows",
    )(src, y3)


def _moe_group_kernel(tg_ref, x3_ref, rwt_ref, rb_ref, wg_ref, wu_ref, wd_ref, lw_ref, lb_ref,
                      o3_ref, acc_ref, x_ref, xb_ref, gate_ref):
    i = pl.program_id(0)
    j = pl.program_id(1)
    group = tg_ref[i]
    slabs = D_MODEL // LANES

    @pl.when(group < 0)
    def _():
        o3_ref[...] = jnp.zeros_like(o3_ref)

    @pl.when(group >= 0)
    def _():
        @pl.when(j == 0)
        def _():
            x = x3_ref[...].reshape(x_ref.shape)
            x_ref[...] = x
            xb_ref[...] = x.astype(BF16)
            acc_ref[...] = jnp.zeros_like(acc_ref)
            gate_ref[...] = _route_in_group(x, rwt_ref[...], rb_ref[...], group).T

        xb = xb_ref[...]
        hg = jnp.dot(xb, wg_ref[0].astype(BF16), preferred_element_type=F32)
        hu = jnp.dot(xb, wu_ref[0].astype(BF16), preferred_element_type=F32)
        he = _silu(hg) * hu
        gates = gate_ref[...]
        lane = lax.broadcasted_iota(jnp.int32, gates.shape, 1)
        ge = jnp.sum(jnp.where(lane == j, gates, 0.0), axis=1, keepdims=True)
        acc_ref[...] += ge * jnp.dot(he.astype(BF16), wd_ref[0].astype(BF16),
                                     preferred_element_type=F32)

        @pl.when(j == EXPERTS_PER_GROUP - 1)
        def _():
            y = _layer_norm(ALPHA * x_ref[...] + acc_ref[...], lw_ref[...], lb_ref[...])
            o3_ref[...] = y.reshape(y.shape[0], slabs, LANES)


def _moe_group_ln(xs3, tile_group, rwt, rbias, wg, wu, wd, layer, ln_w, ln_b, tm):
    n = xs3.shape[0]
    slabs = D_MODEL // LANES
    expert = lambda i, j, tg: (layer * N_EXPERTS + jnp.maximum(tg[i], 0) * EXPERTS_PER_GROUP + j, 0, 0)
    fixed = lambda i, j, tg: (0, 0)
    return pl.pallas_call(
        _moe_group_kernel,
        out_shape=jax.ShapeDtypeStruct((n, slabs, LANES), F32),
        grid_spec=pltpu.PrefetchScalarGridSpec(
            num_scalar_prefetch=1,
            grid=(n // tm, EXPERTS_PER_GROUP),
            in_specs=[pl.BlockSpec((tm, slabs, LANES), lambda i, j, tg: (i, 0, 0)),
                      pl.BlockSpec((3 * LANES, D_MODEL), fixed),
                      pl.BlockSpec((N_EXPERTS, 1), fixed),
                      pl.BlockSpec((1, D_MODEL, D_FF_EXPERT), expert),
                      pl.BlockSpec((1, D_MODEL, D_FF_EXPERT), expert),
                      pl.BlockSpec((1, D_FF_EXPERT, D_MODEL), expert),
                      pl.BlockSpec((1, D_MODEL), fixed),
                      pl.BlockSpec((1, D_MODEL), fixed)],
            out_specs=pl.BlockSpec((tm, slabs, LANES), lambda i, j, tg: (i, 0, 0)),
            scratch_shapes=[pltpu.VMEM((tm, D_MODEL), F32), pltpu.VMEM((tm, D_MODEL), F32),
                            pltpu.VMEM((tm, D_MODEL), BF16), pltpu.VMEM((tm, LANES), F32)]),
        compiler_params=_params("arbitrary", "arbitrary"),
        name="moe_group_ln",
    )(tile_group, xs3, rwt, rbias, wg, wu, wd, ln_w, ln_b)


def _ffn_sorted(x, p, layer, tm, tm_moe):
    t = x.shape[0]
    n_tiles = t // tm_moe + N_GROUPS
    gmat = _router(x, p['rwt'], p['rbias'], tm, with_gates=False)
    rank_mat, totals = _group_rank(gmat, tm)
    group = gmat[:, N_EXPERTS].astype(jnp.int32)
    counts = totals[0, :N_GROUPS].astype(jnp.int32)
    seg_tiles = (counts + tm_moe - 1) // tm_moe
    seg_end = jnp.cumsum(seg_tiles)
    seg_start = seg_end - seg_tiles
    is_group = group[:, None] == jnp.arange(N_GROUPS, dtype=jnp.int32)[None, :]
    dest = (jnp.sum(jnp.where(is_group, seg_start[None, :], 0), axis=1) * tm_moe
            + rank_mat[:, 0].astype(jnp.int32))
    tile_id = jnp.arange(n_tiles, dtype=jnp.int32)
    tile_group = jnp.sum((tile_id[:, None] >= seg_end[None, :]).astype(jnp.int32), axis=1)
    tile_group = jnp.where(tile_id < seg_end[N_GROUPS - 1], tile_group, -1)
    xs3 = _scatter_rows(x, dest, n_tiles * tm_moe, tm)
    ys3 = _moe_group_ln(xs3, tile_group, p['rwt'], p['rbias'], p['w_gate'], p['w_up'], p['w_down'],
                        layer, p['ln2_w'][layer], p['ln2_b'][layer], tm_moe)
    return _gather_rows(ys3, dest, tm)


def _prefix_selector():
    n = VEC_TILE
    nsub = VEC_CHUNK // VEC_SUB
    t = np.arange(n)[:, None]
    s = np.arange(n)[None, :]
    incl = ((t // VEC_CHUNK) == (s // VEC_CHUNK)) & ((s % VEC_CHUNK) <= (t % VEC_CHUNK))
    r = np.arange((n // VEC_CHUNK) * nsub)[:, None]
    starts = ((r // nsub) == (s // VEC_CHUNK)) & ((s % VEC_CHUNK) < VEC_SUB * (r % nsub))
    return np.concatenate([incl, starts], axis=0).astype(np.float32)


def _vec_heads(heads, sel, mm):
    n = VEC_TILE
    nsub = VEC_CHUNK // VEC_SUB
    nchunk = n // VEC_CHUNK
    nrows = heads[0][0].shape[0]
    kdim = heads[0][0].shape[1]
    streams = [(h, i) for h in range(len(heads)) for i in range(0, nrows, n)]
    tile = lambda h, i, which: heads[h][which][i:i + n]

    prefs = [_dot_exact_rhs(sel, tile(h, i, 3)) for h, i in streams]
    rows_of = lambda fn, m: jnp.concatenate(
        [jnp.broadcast_to(fn(j), (m, kdim)) for j in range(n // m)], axis=0)
    sub = (lax.broadcasted_iota(jnp.int32, (n, kdim), 0) // VEC_SUB) % nsub
    q_cat, k_cat, q_dec0, updates = [], [], [], []
    for (h, i), pref in zip(streams, prefs):
        q, k, v = tile(h, i, 0), tile(h, i, 1), tile(h, i, 2)
        big_g = pref[0:n]
        start = lambda c, j, pref=pref: pref[n + c * nsub + j:n + c * nsub + j + 1]
        q_dec = [q * jnp.exp(big_g)]
        for j in range(1, nsub):
            base_j = rows_of(lambda c: start(c, j), VEC_CHUNK)
            q_dec.append(q * jnp.exp(jnp.minimum(big_g - base_j, 0.0)))
        base_own = rows_of(lambda m: start(m // nsub, m % nsub), VEC_SUB)
        k_rel = k * jnp.exp(base_own - big_g)
        k_cat.append(jnp.concatenate([jnp.where(sub == j, k_rel, 0.0) for j in range(nsub)],
                                     axis=1))
        q_cat.append(jnp.concatenate(q_dec, axis=1))
        q_dec0.append(q_dec[0])
        per_chunk = []
        for c in range(nchunk):
            rows = slice(c * VEC_CHUNK, (c + 1) * VEC_CHUNK)
            g_last = big_g[(c + 1) * VEC_CHUNK - 1:(c + 1) * VEC_CHUNK, :]
            kd = k[rows] * jnp.exp(g_last - big_g[rows])
            per_chunk.append((jnp.exp(g_last), mm.tn(v[rows], kd)))
        updates.append(per_chunk)
    scores = [mm.nt(qc, kc) for qc, kc in zip(q_cat, k_cat)]
    rr = lax.broadcasted_iota(jnp.int32, (n, n), 0)
    cc = lax.broadcasted_iota(jnp.int32, (n, n), 1)
    keep = (rr >= cc) & ((rr // VEC_CHUNK) == (cc // VEC_CHUNK))
    intra = [mm.nn(jnp.where(keep, sc, 0.0), tile(h, i, 2)) for (h, i), sc in zip(streams, scores)]

    states = [hd[4] for hd in heads]
    o_rows = [[] for _ in heads]
    for si, (h, i) in enumerate(streams):
        for c, (decay_last, update) in enumerate(updates[si]):
            rows = slice(c * VEC_CHUNK, (c + 1) * VEC_CHUNK)
            o_rows[h].append(intra[si][rows] + mm.nt(q_dec0[si][rows], states[h]))
            states[h] = states[h] * decay_last + update
    return [(jnp.concatenate(o_rows[h], axis=0), states[h]) for h in range(len(heads))]


def _gla_chunk_kernel(q_ref, k_ref, v_ref, go_ref, sm_ref, w2_ref, b2_ref, nw_ref, sel_ref,
                      o_ref, s_ref, st_ref):
    r = pl.program_id(2)

    @pl.when(r == 0)
    def _():
        st_ref[...] = jnp.zeros_like(st_ref)

    sm = sm_ref[...]
    heads = []
    for hh in range(VEC_HPS):
        kc = slice(hh * GLA_DK, (hh + 1) * GLA_DK)
        vc = slice(hh * GLA_DV, (hh + 1) * GLA_DV)
        gk = _log_sigmoid(_ThreePass.nn(sm, w2_ref[hh]) + b2_ref[hh]) / GLA_NORMALIZER
        heads.append((q_ref[:, kc] * (GLA_DK ** -0.5), k_ref[:, kc], v_ref[:, vc], gk, st_ref[hh]))
    finals = []
    for hh, (o, st) in enumerate(_vec_heads(heads, sel_ref[...], _ThreePass)):
        vc = slice(hh * GLA_DV, (hh + 1) * GLA_DV)
        o_ref[:, vc] = _rms(o, nw_ref[...]) * _silu(go_ref[:, vc])
        st_ref[hh] = st
        finals.append(st)

    @pl.when(r == pl.num_programs(2) - 1)
    def _():
        for hh in range(VEC_HPS):
            s_ref[0, hh] = finals[hh].T


def _gla_prompt(proj, w2p, b2, norm_w, sel, bsz, seq):
    nr = seq // VEC_ROWS
    ng = GLA_HEADS // VEC_HPS
    kw = VEC_HPS * GLA_DK
    vw = VEC_HPS * GLA_DV
    row = lambda off: (lambda b, h, r: (b * nr + r, off + h))
    return pl.pallas_call(
        _gla_chunk_kernel,
        out_shape=(jax.ShapeDtypeStruct((bsz * seq, GLA_VAL), F32),
                   jax.ShapeDtypeStruct((bsz, GLA_HEADS, GLA_DK, GLA_DV), F32)),
        grid=(bsz, ng, nr),
        in_specs=[pl.BlockSpec((VEC_ROWS, kw), row(AB_Q // kw)),
                  pl.BlockSpec((VEC_ROWS, kw), row(AB_K // kw)),
                  pl.BlockSpec((VEC_ROWS, vw), row(AB_V // vw)),
                  pl.BlockSpec((VEC_ROWS, vw), row(AB_GOUT // vw)),
                  pl.BlockSpec((VEC_ROWS, LANES), lambda b, h, r: (b * nr + r, AB_SMALL // LANES)),
                  pl.BlockSpec((VEC_HPS, LANES, GLA_DK), lambda b, h, r: (h, 0, 0)),
                  pl.BlockSpec((VEC_HPS, 1, GLA_DK), lambda b, h, r: (h, 0, 0)),
                  pl.BlockSpec((1, GLA_DV), lambda b, h, r: (0, 0)),
                  pl.BlockSpec(sel.shape, lambda b, h, r: (0, 0))],
        out_specs=(pl.BlockSpec((VEC_ROWS, vw), lambda b, h, r: (b * nr + r, h)),
                   pl.BlockSpec((1, VEC_HPS, GLA_DK, GLA_DV), lambda b, h, r: (b, h, 0, 0))),
        scratch_shapes=[pltpu.VMEM((VEC_HPS, GLA_DV, GLA_DK), F32)],
        compiler_params=_params("parallel", "parallel", "arbitrary"),
        name="gla_chunk",
    )(proj, proj, proj, proj, proj, w2p, b2, norm_w, sel)


def _hgrn_lower_bound(lbraw, layer):
    m = jnp.max(lbraw, axis=0, keepdims=True)
    ex = jnp.exp(lbraw - m)
    sm = ex / jnp.sum(ex, axis=0, keepdims=True)
    acc = sm[0:1]
    for i in range(1, layer + 1):
        acc = acc + sm[i:i + 1]
    return acc - sm[0:1]


def _hgrn_gates(q_raw, f_raw, lb):
    forget = lb + (1.0 - lb) * _sigmoid(f_raw)
    return _silu(q_raw), 1.0 - forget, jnp.log(forget)


def _hgrn_chunk_kernel(layer, q_ref, f_ref, i_ref, go_ref, lb_ref, nw_ref, sel_ref,
                       o_ref, s_ref, st_ref):
    r = pl.program_id(2)

    @pl.when(r == 0)
    def _():
        st_ref[...] = jnp.zeros_like(st_ref)

    lb_all = _hgrn_lower_bound(lb_ref[...], layer)
    heads = []
    for hh in range(VEC_HPS):
        kc = slice(hh * HG_EXPAND, (hh + 1) * HG_EXPAND)
        vc = slice(hh * HG_DI, (hh + 1) * HG_DI)
        q, k, g = _hgrn_gates(q_ref[:, kc], f_ref[:, kc], lb_all[:, kc])
        heads.append((q, k, i_ref[:, vc], g, st_ref[hh]))
    finals = []
    for hh, (o, st) in enumerate(_vec_heads(heads, sel_ref[...], _OnePass)):
        vc = slice(hh * HG_DI, (hh + 1) * HG_DI)
        o_ref[:, vc] = _rms(o, nw_ref[...]) * _silu(go_ref[:, vc])
        st_ref[hh] = st
        finals.append(st)

    @pl.when(r == pl.num_programs(2) - 1)
    def _():
        for hh in range(VEC_HPS):
            s_ref[0, hh] = finals[hh].T


def _hgrn_prompt(proj, lower_bounds, norm_w, sel, layer, bsz, seq):
    nr = seq // VEC_ROWS
    ng = HG_HEADS // VEC_HPS
    kw = VEC_HPS * HG_EXPAND
    vw = VEC_HPS * HG_DI
    row = lambda off: (lambda b, h, r: (b * nr + r, off + h))
    return pl.pallas_call(
        functools.partial(_hgrn_chunk_kernel, layer),
        out_shape=(jax.ShapeDtypeStruct((bsz * seq, HG_I), F32),
                   jax.ShapeDtypeStruct((bsz, HG_HEADS, HG_EXPAND, HG_DI), F32)),
        grid=(bsz, ng, nr),
        in_specs=[pl.BlockSpec((VEC_ROWS, kw), row(0)),
                  pl.BlockSpec((VEC_ROWS, kw), row(ng)),
                  pl.BlockSpec((VEC_ROWS, vw), row(2 * ng)),
                  pl.BlockSpec((VEC_ROWS, vw), row(3 * ng)),
                  pl.BlockSpec((DEPTH, kw), lambda b, h, r: (0, h)),
                  pl.BlockSpec((1, HG_DI), lambda b, h, r: (0, 0)),
                  pl.BlockSpec(sel.shape, lambda b, h, r: (0, 0))],
        out_specs=(pl.BlockSpec((VEC_ROWS, vw), lambda b, h, r: (b * nr + r, h)),
                   pl.BlockSpec((1, VEC_HPS, HG_EXPAND, HG_DI), lambda b, h, r: (b, h, 0, 0))),
        scratch_shapes=[pltpu.VMEM((VEC_HPS, HG_DI, HG_EXPAND), F32)],
        compiler_params=_params("parallel", "parallel", "arbitrary"),
        name="hgrn_chunk",
    )(proj, proj, proj, proj, lower_bounds, norm_w, sel)


def _conv_silu(xp, cw, cb, n, lead):
    acc = cb + cw[SSD_CONV - 1:SSD_CONV] * xp[lead:lead + n]
    for m in range(1, SSD_CONV):
        acc = acc + cw[SSD_CONV - 1 - m:SSD_CONV - m] * xp[lead - m:lead - m + n]
    return _silu(acc)


def _ssd_gate_norm(y, z, nw):
    yz = y * _silu(z)
    parts = []
    for g in range(SSD_GROUPS):
        cols = slice(g * SSD_GROUP_W, (g + 1) * SSD_GROUP_W)
        parts.append(_rms(yz[:, cols], nw[:, cols]))
    return jnp.concatenate(parts, axis=1)


def _ssd_chunk_kernel(z_ref, xbc_ref, sm_ref, cw_ref, cb_ref, dtb_ref, alog_ref, dsk_ref,
                      nw_ref, ex_ref, o_ref, s_ref, conv_ref, st_ref, prev_ref):
    r = pl.program_id(1)
    c = SSD_CHUNK
    mm = _ThreePass

    @pl.when(r == 0)
    def _():
        st_ref[...] = jnp.zeros_like(st_ref)
        prev_ref[...] = jnp.zeros_like(prev_ref)

    x_raw = xbc_ref[...]
    xp = jnp.concatenate([prev_ref[...], x_raw], axis=0)
    prev_ref[...] = x_raw[c - 8:c]
    xc = _conv_silu(xp, cw_ref[...], cb_ref[...], c, 8)
    xs = xc[:, :SSD_INNER]
    bm = xc[:, SSD_INNER:SSD_INNER + SSD_BC]
    cm = xc[:, SSD_INNER + SSD_BC:]

    dt = _softplus(sm_ref[...] + dtb_ref[...])
    a_neg = -jnp.exp(alog_ref[...])
    big_g = _dot_exact_rhs(_tril(c).astype(BF16), dt * a_neg)
    g_t = big_g.T
    g_last = big_g[c - 1:c, :]
    ex = ex_ref[...]
    dt_x = _dot_exact_lhs(dt, ex)
    eg_x = _dot_exact_lhs(jnp.exp(big_g), ex)
    w_x = _dot_exact_lhs(dt * jnp.exp(g_last - big_g), ex)
    xdt = xs * dt_x
    xw = xs * w_x
    causal = _tril(c)
    lane = lax.broadcasted_iota(jnp.int32, (c, LANES), 1)
    st = st_ref[...]
    y_parts = []
    u_parts = []
    for g in range(SSD_GROUPS):
        gcols = slice(g * SSD_GROUP_W, (g + 1) * SSD_GROUP_W)
        bg = bm[:, g * SSD_STATE:(g + 1) * SSD_STATE]
        cg = cm[:, g * SSD_STATE:(g + 1) * SSD_STATE]
        sc = mm.nt(cg, bg)
        inter = mm.nn(cg, st[:, gcols])
        u_parts.append(mm.tn(bg, xw[:, gcols]))
        pair_cols = []
        heads_per_group = SSD_HEADS // SSD_GROUPS
        for p in range(heads_per_group // 2):
            h0 = g * heads_per_group + 2 * p
            xpair = xdt[:, h0 * SSD_HEADDIM:(h0 + 2) * SSD_HEADDIM]
            ws = []
            for h in (h0, h0 + 1):
                diff = big_g[:, h:h + 1] - g_t[h:h + 1, :]
                ws.append(sc * jnp.exp(jnp.where(causal, diff, -jnp.inf)))
            x_diag = jnp.concatenate([jnp.where(lane < SSD_HEADDIM, xpair, 0.0),
                                      jnp.where(lane < SSD_HEADDIM, 0.0, xpair)], axis=0)
            pair_cols.append(mm.nn(jnp.concatenate(ws, axis=1), x_diag))
        y_intra = jnp.concatenate(pair_cols, axis=1)
        y_parts.append(y_intra + inter * eg_x[:, gcols])
    y = jnp.concatenate(y_parts, axis=1) + dsk_ref[...] * xs
    o_ref[...] = _ssd_gate_norm(y, z_ref[...], nw_ref[...])
    st = st * eg_x[c - 1:c, :] + jnp.concatenate(u_parts, axis=1)
    st_ref[...] = st

    @pl.when(r == pl.num_programs(1) - 1)
    def _():
        s_ref[0] = st
        conv_ref[0] = x_raw[c - (SSD_CONV - 1):c]


def _ssd_prompt(proj, conv_w, conv_b, dtb_p, alog_p, dskip_x, norm_w, expand, bsz, seq):
    nr = seq // SSD_CHUNK
    fixed = lambda b, r: (0, 0)
    return pl.pallas_call(
        _ssd_chunk_kernel,
        out_shape=(jax.ShapeDtypeStruct((bsz * seq, SSD_INNER), F32),
                   jax.ShapeDtypeStruct((bsz, SSD_STATE, SSD_INNER), F32),
                   jax.ShapeDtypeStruct((bsz, SSD_CONV - 1, SSD_CONV_DIM), F32)),
        grid=(bsz, nr),
        in_specs=[pl.BlockSpec((SSD_CHUNK, SSD_INNER), lambda b, r: (b * nr + r, AB_Z // SSD_INNER)),
                  pl.BlockSpec((SSD_CHUNK, SSD_CONV_DIM), lambda b, r: (b * nr + r, AB_XBC // SSD_CONV_DIM)),
                  pl.BlockSpec((SSD_CHUNK, LANES), lambda b, r: (b * nr + r, AB_SMALL // LANES)),
                  pl.BlockSpec((SSD_CONV, SSD_CONV_DIM), fixed),
                  pl.BlockSpec((1, SSD_CONV_DIM), fixed),
                  pl.BlockSpec((1, LANES), fixed),
                  pl.BlockSpec((1, LANES), fixed),
                  pl.BlockSpec((1, SSD_INNER), fixed),
                  pl.BlockSpec((1, SSD_INNER), fixed),
                  pl.BlockSpec((LANES, SSD_INNER), fixed)],
        out_specs=(pl.BlockSpec((SSD_CHUNK, SSD_INNER), lambda b, r: (b * nr + r, 0)),
                   pl.BlockSpec((1, SSD_STATE, SSD_INNER), lambda b, r: (b, 0, 0)),
                   pl.BlockSpec((1, SSD_CONV - 1, SSD_CONV_DIM), lambda b, r: (b, 0, 0))),
        scratch_shapes=[pltpu.VMEM((SSD_STATE, SSD_INNER), F32),
                        pltpu.VMEM((8, SSD_CONV_DIM), F32)],
        compiler_params=_params("parallel", "arbitrary"),
        name="ssd_chunk",
    )(proj, proj, proj, conv_w, conv_b, dtb_p, alog_p, dskip_x, norm_w, expand)


def _ab_prep_kernel(q_ref, sm_ref, xbc_ref, cs_ref, w2_ref, b2_ref, cw_ref, cb_ref, dtb_ref,
                    alog_ref, ex_ref, qs_ref, dec_ref, xc_ref, xdt_ref, dax_ref, cs_out_ref):
    sm = sm_ref[...]
    gk = _log_sigmoid(_ThreePass.nn(sm, w2_ref[...]) + b2_ref[...]) / GLA_NORMALIZER
    qs_ref[...] = q_ref[...] * (GLA_DK ** -0.5)
    dec_ref[...] = jnp.exp(gk)
    cw = cw_ref[...]
    x_raw = xbc_ref[...]
    acc = cb_ref[...] + cw[SSD_CONV - 1:SSD_CONV] * x_raw
    for j in range(SSD_CONV - 1):
        acc = acc + cw[j:j + 1] * cs_ref[j]
    xc = _silu(acc)
    xc_ref[...] = xc
    for j in range(SSD_CONV - 2):
        cs_out_ref[j] = cs_ref[j + 1]
    cs_out_ref[SSD_CONV - 2] = x_raw
    dt = _softplus(sm + dtb_ref[...])
    ex = ex_ref[...]
    xdt_ref[...] = xc[:, :SSD_INNER] * _dot_exact_lhs(dt, ex)
    dax_ref[...] = _dot_exact_lhs(jnp.exp(dt * -jnp.exp(alog_ref[...])), ex)


def _ab_prep(proj, conv_state, w2_wide, b2_wide, conv_w, conv_b, dtb_p, alog_p, expand):
    bsz = proj.shape[0]
    fixed = lambda i: (0, 0)
    sds = jax.ShapeDtypeStruct
    return pl.pallas_call(
        _ab_prep_kernel,
        out_shape=(sds((bsz, GLA_KEY), F32), sds((bsz, GLA_KEY), F32),
                   sds((bsz, SSD_CONV_DIM), F32), sds((bsz, SSD_INNER), F32),
                   sds((bsz, SSD_INNER), F32),
                   sds((SSD_CONV - 1, bsz, SSD_CONV_DIM), F32)),
        grid=(1,),
        in_specs=[pl.BlockSpec((bsz, GLA_KEY), lambda i: (0, AB_Q // GLA_KEY)),
                  pl.BlockSpec((bsz, LANES), lambda i: (0, AB_SMALL // LANES)),
                  pl.BlockSpec((bsz, SSD_CONV_DIM), lambda i: (0, AB_XBC // SSD_CONV_DIM)),
                  pl.BlockSpec((SSD_CONV - 1, bsz, SSD_CONV_DIM), lambda i: (0, 0, 0)),
                  pl.BlockSpec((LANES, GLA_KEY), fixed),
                  pl.BlockSpec((1, GLA_KEY), fixed),
                  pl.BlockSpec((SSD_CONV, SSD_CONV_DIM), fixed),
                  pl.BlockSpec((1, SSD_CONV_DIM), fixed),
                  pl.BlockSpec((1, LANES), fixed),
                  pl.BlockSpec((1, LANES), fixed),
                  pl.BlockSpec((LANES, SSD_INNER), fixed)],
        out_specs=(pl.BlockSpec((bsz, GLA_KEY), fixed), pl.BlockSpec((bsz, GLA_KEY), fixed),
                   pl.BlockSpec((bsz, SSD_CONV_DIM), fixed), pl.BlockSpec((bsz, SSD_INNER), fixed),
                   pl.BlockSpec((bsz, SSD_INNER), fixed),
                   pl.BlockSpec((SSD_CONV - 1, bsz, SSD_CONV_DIM), lambda i: (0, 0, 0))),
        compiler_params=_params("arbitrary"),
        name="ab_prep",
    )(proj, proj, proj, conv_state, w2_wide, b2_wide, conv_w, conv_b, dtb_p, alog_p, expand)


def _hgrn_prep_kernel(layer, q_ref, f_ref, lb_ref, qs_ref, k_ref, dec_ref):
    lb = _hgrn_lower_bound(lb_ref[...], layer)
    forget = lb + (1.0 - lb) * _sigmoid(f_ref[...])
    qs_ref[...] = _silu(q_ref[...])
    k_ref[...] = 1.0 - forget
    dec_ref[...] = jnp.exp(jnp.log(forget))


def _hgrn_prep(proj, lower_bounds, layer):
    bsz = proj.shape[0]
    blk = lambda j: pl.BlockSpec((bsz, HG_F), lambda i: (0, j))
    return pl.pallas_call(
        functools.partial(_hgrn_prep_kernel, layer),
        out_shape=tuple(jax.ShapeDtypeStruct((bsz, HG_F), F32) for _ in range(3)),
        grid=(1,),
        in_specs=[blk(0), blk(1), pl.BlockSpec((DEPTH, HG_F), lambda i: (0, 0))],
        out_specs=tuple(blk(0) for _ in range(3)),
        compiler_params=_params("arbitrary"),
        name="hgrn_prep",
    )(proj, proj, lower_bounds)


def _vec_step_kernel(mm, s_ref, q_ref, k_ref, d_ref, v_ref, go_ref, nw_ref, so_ref, o_ref):
    q = q_ref[...]
    kt = k_ref[0, 0]
    dt = d_ref[0, 0]
    v = v_ref[...]
    sb = v.shape[0]
    row = lax.broadcasted_iota(jnp.int32, v.shape, 0)
    new = []
    for b in range(sb):
        only_b = row == b
        decay = _dot_exact_lhs(dt, jnp.where(only_b, 1.0, 0.0).astype(BF16))
        new.append(s_ref[b, 0] * decay + mm.nn(kt, jnp.where(only_b, v, 0.0)))
    for b in range(sb):
        so_ref[b, 0] = new[b]
    o = jnp.concatenate([mm.nn(q, new[b])[b:b + 1] for b in range(sb)], axis=0)
    o_ref[...] = _rms(o, nw_ref[...]) * _silu(go_ref[...])


def _vec_step(state, q_rows, k_cols, d_cols, vsrc, v_off, gsrc, g_off, norm_w, mm):
    bsz, nh, kdim, vdim = state.shape
    sb = k_cols.shape[3]
    col = lambda j, h: (h, j, 0, 0)
    return pl.pallas_call(
        functools.partial(_vec_step_kernel, mm),
        out_shape=(jax.ShapeDtypeStruct(state.shape, F32),
                   jax.ShapeDtypeStruct((bsz, nh * vdim), F32)),
        grid=(bsz // sb, nh),
        in_specs=[pl.BlockSpec((sb, 1, kdim, vdim), lambda j, h: (j, h, 0, 0)),
                  pl.BlockSpec((sb, kdim), lambda j, h: (j, h)),
                  pl.BlockSpec((1, 1, kdim, sb), col),
                  pl.BlockSpec((1, 1, kdim, sb), col),
                  pl.BlockSpec((sb, vdim), lambda j, h: (j, v_off + h)),
                  pl.BlockSpec((sb, vdim), lambda j, h: (j, g_off + h)),
                  pl.BlockSpec((1, vdim), lambda j, h: (0, 0))],
        out_specs=(pl.BlockSpec((sb, 1, kdim, vdim), lambda j, h: (j, h, 0, 0)),
                   pl.BlockSpec((sb, vdim), lambda j, h: (j, h))),
        compiler_params=_params("parallel", "parallel"),
        name="vec_step",
    )(state, q_rows, k_cols, d_cols, vsrc, gsrc, norm_w)


def _ssd_step_kernel(s_ref, xt_ref, b_ref, c_ref, x_ref, dax_ref, dsk_ref, so_ref, y_ref):
    mm = _ThreePass
    xt = xt_ref[0, 0]
    bm = b_ref[...]
    c = c_ref[...]
    dax = dax_ref[...]
    hpg = SSD_HEADS // SSD_GROUPS
    row = lax.broadcasted_iota(jnp.int32, bm.shape, 0)
    new = []
    for b in range(STEP_B):
        outer = mm.nn(xt, jnp.where(row == b, bm, 0.0))
        per_head = []
        for hh in range(hpg):
            sn = (s_ref[b, hh] * dax[b:b + 1, hh * LANES:(hh + 1) * LANES]
                  + outer[hh * SSD_HEADDIM:(hh + 1) * SSD_HEADDIM])
            so_ref[b, hh] = sn
            per_head.append(sn)
        new.append(per_head)
    rows = [jnp.concatenate([mm.nt(c, new[b][hh])[b:b + 1] for hh in range(hpg)], axis=1)
            for b in range(STEP_B)]
    y_ref[...] = jnp.concatenate(rows, axis=0) + dsk_ref[...] * x_ref[...]


def _ssd_step(state_t, xdt_cols, xc, dax, dskip_x):
    bsz = state_t.shape[0]
    hpg = SSD_HEADS // SSD_GROUPS
    grp = lambda j, g: (j, g)
    b_off = SSD_INNER // SSD_STATE
    c_off = (SSD_INNER + SSD_BC) // SSD_STATE
    tile = pl.BlockSpec((STEP_B, hpg, SSD_HEADDIM, SSD_STATE), lambda j, g: (j, g, 0, 0))
    return pl.pallas_call(
        _ssd_step_kernel,
        out_shape=(jax.ShapeDtypeStruct(state_t.shape, F32),
                   jax.ShapeDtypeStruct((bsz, SSD_INNER), F32)),
        grid=(bsz // STEP_B, SSD_GROUPS),
        in_specs=[tile,
                  pl.BlockSpec((1, 1, SSD_GROUP_W, STEP_B), lambda j, g: (g, j, 0, 0)),
                  pl.BlockSpec((STEP_B, SSD_STATE), lambda j, g: (j, b_off + g)),
                  pl.BlockSpec((STEP_B, SSD_STATE), lambda j, g: (j, c_off + g)),
                  pl.BlockSpec((STEP_B, SSD_GROUP_W), grp),
                  pl.BlockSpec((STEP_B, hpg * LANES), grp),
                  pl.BlockSpec((1, SSD_GROUP_W), lambda j, g: (0, g))],
        out_specs=(tile, pl.BlockSpec((STEP_B, SSD_GROUP_W), grp)),
        compiler_params=_params("parallel", "parallel"),
        name="ssd_step",
    )(state_t, xdt_cols, xc, xc, xc, dax, dskip_x)


def _ssd_post_kernel(y_ref, z_ref, nw_ref, o_ref):
    o_ref[...] = _ssd_gate_norm(y_ref[...], z_ref[...], nw_ref[...])


def _ssd_post(y, proj, norm_w):
    bsz = y.shape[0]
    return pl.pallas_call(
        _ssd_post_kernel,
        out_shape=jax.ShapeDtypeStruct((bsz, SSD_INNER), F32),
        grid=(1,),
        in_specs=[pl.BlockSpec((bsz, SSD_INNER), lambda i: (0, 0)),
                  pl.BlockSpec((bsz, SSD_INNER), lambda i: (0, AB_Z // SSD_INNER)),
                  pl.BlockSpec((1, SSD_INNER), lambda i: (0, 0))],
        out_specs=pl.BlockSpec((bsz, SSD_INNER), lambda i: (0, 0)),
        compiler_params=_params("arbitrary"),
        name="ssd_post",
    )(y, proj, norm_w)


def _to_cols(a, nh, sb=STEP_B):
    bsz = a.shape[0]
    return a.reshape(bsz // sb, sb, nh, -1).transpose(2, 0, 3, 1)


def _prep_weights(w_in_ab, w_gk2, b_gk2, gla_norm_w, conv_w, conv_b, dt_bias, a_log, d_skip,
                  ssd_norm_w, w_out_ab, w_in_c, hg_norm_w, w_out_c, router_w, router_bias,
                  w_gate, w_up, w_down, ln1_w, ln1_b, ln2_w, ln2_b):
    offs = np.cumsum([0, GLA_KEY, GLA_KEY, GLA_VAL, GLA_VAL, GLA_RANK, SSD_INNER, SSD_CONV_DIM,
                      SSD_HEADS])
    sec = lambda w, i: w[:, offs[i]:offs[i + 1]]
    w = w_in_ab[0]
    pad = jnp.zeros((D_MODEL, LANES - SSD_HEADS - GLA_RANK), w.dtype)
    w_ab = jnp.concatenate([sec(w, 5), sec(w, 2), sec(w, 3), sec(w, 6), sec(w, 0), sec(w, 1),
                            sec(w, 7), sec(w, 4), pad], axis=1)
    hi_lo = lambda m: (m.astype(BF16), (m - m.astype(BF16).astype(F32)).astype(BF16))
    pad_e = lambda m: jnp.pad(m, ((0, LANES - N_EXPERTS), (0, 0)))
    router_pieces = jnp.concatenate([pad_e(piece) for piece in _split3(router_w.T)], axis=0)
    w2_wide = jnp.zeros((LANES, GLA_KEY), F32).at[SSD_HEADS:SSD_HEADS + GLA_RANK].set(w_gk2[0])
    lane_pad = lambda v: jnp.zeros((1, LANES), F32).at[0, :SSD_HEADS].set(v)
    expand = np.zeros((LANES, SSD_INNER), np.float32)
    for h in range(SSD_HEADS):
        expand[h, h * SSD_HEADDIM:(h + 1) * SSD_HEADDIM] = 1.0
    return dict(
        w_ab=hi_lo(w_ab),
        w2_wide=w2_wide,
        w2_heads=w2_wide.reshape(LANES, GLA_HEADS, GLA_DK).transpose(1, 0, 2),
        b2_wide=b_gk2[0].reshape(1, GLA_KEY),
        b2_heads=b_gk2[0].reshape(GLA_HEADS, 1, GLA_DK),
        gla_norm_w=gla_norm_w[0].reshape(1, GLA_DV),
        conv_w=conv_w[0], conv_b=conv_b[0].reshape(1, SSD_CONV_DIM),
        dtb_p=lane_pad(dt_bias[0]), alog_p=lane_pad(a_log[0]),
        dskip_x=jnp.repeat(d_skip[0], SSD_HEADDIM).reshape(1, SSD_INNER),
        ssd_norm_w=ssd_norm_w[0].reshape(1, SSD_INNER),
        expand=jnp.asarray(expand, BF16),
        prefix_sel=jnp.asarray(_prefix_selector(), BF16),
        w_out_gla=hi_lo(w_out_ab[0, :GLA_VAL]),
        w_out_ssd=hi_lo(w_out_ab[0, GLA_VAL:]),
        w_c=w_in_c[0].astype(BF16),
        hg_norm_w=hg_norm_w[0].reshape(1, HG_DI),
        w_out_c=w_out_c[0].astype(BF16),
        rwt=router_pieces,
        rbias=router_bias.reshape(N_EXPERTS, 1),
        w_gate=w_gate.reshape(DEPTH * N_EXPERTS, D_MODEL, D_FF_EXPERT),
        w_up=w_up.reshape(DEPTH * N_EXPERTS, D_MODEL, D_FF_EXPERT),
        w_down=w_down.reshape(DEPTH * N_EXPERTS, D_FF_EXPERT, D_MODEL),
        ln1_w=ln1_w.reshape(DEPTH, 1, D_MODEL), ln1_b=ln1_b.reshape(DEPTH, 1, D_MODEL),
        ln2_w=ln2_w.reshape(DEPTH, 1, D_MODEL), ln2_b=ln2_b.reshape(DEPTH, 1, D_MODEL),
    )


def _ffn(x, p, layer, tm, tm_moe):
    gates = _router(x, p['rwt'], p['rbias'], tm)
    return _moe_ln(x, gates, p['w_gate'], p['w_up'], p['w_down'], layer,
                   p['ln2_w'][layer], p['ln2_b'][layer], tm_moe)


def _ssd_state_from_wide(s_wide):
    bsz = s_wide.shape[0]
    return s_wide.reshape(bsz, SSD_STATE, SSD_HEADS, SSD_HEADDIM).transpose(0, 2, 1, 3)


def _trunk_prompt(x3, p, lower_bounds, tm, tn_ab, tn_c):
    bsz, seq, _ = x3.shape
    x = x3.reshape(bsz * seq, D_MODEL)
    tm_big = 2 * tm
    proj = _proj(x, p['w_ab'], tm_big, tn_ab)
    o_gla, s_gla = _gla_prompt(proj, p['w2_heads'], p['b2_heads'], p['gla_norm_w'],
                               p['prefix_sel'], bsz, seq)
    yz, s_ssd, s_conv = _ssd_prompt(proj, p['conv_w'], p['conv_b'], p['dtb_p'], p['alog_p'],
                                    p['dskip_x'], p['ssd_norm_w'], p['expand'], bsz, seq)
    x = _outproj_ln([o_gla, yz], [p['w_out_gla'], p['w_out_ssd']], x, p['ln1_w'][0], p['ln1_b'][0], tm)
    x = _ffn_sorted(x, p, 0, tm, tm_big)
    proj_c = _proj(x, p['w_c'], tm_big, tn_c)
    o_hg, s_hg = _hgrn_prompt(proj_c, lower_bounds, p['hg_norm_w'], p['prefix_sel'], 1, bsz, seq)
    x = _outproj_ln([o_hg], [p['w_out_c']], x, p['ln1_w'][1], p['ln1_b'][1], tm)
    x = _ffn_sorted(x, p, 1, tm, tm_big)
    return (x.reshape(bsz, seq, D_MODEL), s_gla[None], _ssd_state_from_wide(s_ssd)[None],
            s_conv[None], s_hg[None])


def _trunk_sample(x3, st_gla, st_ssd, st_conv, st_hg, p, lower_bounds, tn_ab, tn_c):
    bsz = x3.shape[0]
    tm = bsz
    x = x3.reshape(bsz, D_MODEL)
    proj = _proj(x, p['w_ab'], tm, tn_ab)
    qs, dec, xc, xdt, dax, conv_new = _ab_prep(proj, st_conv[0].transpose(1, 0, 2), p['w2_wide'],
                                               p['b2_wide'], p['conv_w'], p['conv_b'], p['dtb_p'],
                                               p['alog_p'], p['expand'])
    conv_new = conv_new.transpose(1, 0, 2)
    k_gla = proj[:, AB_K:AB_K + GLA_KEY]
    s_gla, o_gla = _vec_step(st_gla[0], qs, _to_cols(k_gla, GLA_HEADS, VEC_STEP_B),
                             _to_cols(dec, GLA_HEADS, VEC_STEP_B), proj, AB_V // GLA_DV, proj,
                             AB_GOUT // GLA_DV, p['gla_norm_w'], _ThreePass)
    dax_wide = jnp.repeat(dax[:, ::SSD_HEADDIM], LANES, axis=1)
    s_ssd_t, y = _ssd_step(st_ssd[0].transpose(0, 1, 3, 2), _to_cols(xdt, SSD_GROUPS),
                           xc, dax_wide, p['dskip_x'])
    s_ssd = s_ssd_t.transpose(0, 1, 3, 2)
    yz = _ssd_post(y, proj, p['ssd_norm_w'])
    x = _outproj_ln([o_gla, yz], [p['w_out_gla'], p['w_out_ssd']], x, p['ln1_w'][0], p['ln1_b'][0], tm)
    x = _ffn(x, p, 0, tm, tm)
    proj_c = _proj(x, p['w_c'], tm, tn_c)
    qh, kh, dh = _hgrn_prep(proj_c, lower_bounds, 1)
    s_hg, o_hg = _vec_step(st_hg[0], qh, _to_cols(kh, HG_HEADS, VEC_STEP_B),
                           _to_cols(dh, HG_HEADS, VEC_STEP_B), proj_c, 2 * HG_HEADS, proj_c, 3 * HG_HEADS,
                           p['hg_norm_w'], _OnePass)
    x = _outproj_ln([o_hg], [p['w_out_c']], x, p['ln1_w'][1], p['ln1_b'][1], tm)
    x = _ffn(x, p, 1, tm, tm)
    return x.reshape(bsz, 1, D_MODEL), s_gla[None], s_ssd[None], conv_new[None], s_hg[None]


def kernel(x_prompt, x_sample, state_gla, state_ssd, state_conv, state_hgrn, w_in_ab, w_gk2, b_gk2, gla_norm_w, conv_w, conv_b, dt_bias, a_log, d_skip, ssd_norm_w, w_out_ab, w_in_c, lower_bounds, hg_norm_w, w_out_c, router_w, router_bias, w_gate, w_up, w_down, ln1_w, ln1_b, ln2_w, ln2_b):
    p = _prep_weights(w_in_ab, w_gk2, b_gk2, gla_norm_w, conv_w, conv_b, dt_bias, a_log, d_skip,
                      ssd_norm_w, w_out_ab, w_in_c, hg_norm_w, w_out_c, router_w, router_bias,
                      w_gate, w_up, w_down, ln1_w, ln1_b, ln2_w, ln2_b)
    y_p, gla_p, ssd_p, conv_p, hg_p = _trunk_prompt(x_prompt, p, lower_bounds, 512, 1152, 1024)
    y_s, gla_s, ssd_s, conv_s, hg_s = _trunk_sample(x_sample, state_gla, state_ssd, state_conv,
                                                    state_hgrn, p, lower_bounds, 1152, 1024)
    return (y_p, y_s, gla_p, ssd_p, conv_p, hg_p, gla_s, ssd_s, conv_s, hg_s)
```

```python
import functools

import numpy as np
import jax
import jax.numpy as jnp
from jax import lax
from jax.experimental import pallas as pl
from jax.experimental.pallas import tpu as pltpu

F32 = jnp.float32
BF16 = jnp.bfloat16

D_MODEL = 1024
DEPTH = 2
GLA_HEADS = 4
GLA_DK = 128
GLA_DV = 256
GLA_KEY = GLA_HEADS * GLA_DK
GLA_VAL = GLA_HEADS * GLA_DV
GLA_RANK = 16
GLA_NORMALIZER = 16.0
SSD_INNER = 1024
SSD_HEADDIM = 64
SSD_HEADS = 16
SSD_STATE = 128
SSD_GROUPS = 2
SSD_CONV = 4
SSD_GROUP_W = SSD_INNER // SSD_GROUPS
SSD_BC = SSD_GROUPS * SSD_STATE
SSD_CONV_DIM = SSD_INNER + 2 * SSD_BC
HG_EXPAND = 128
HG_HEADS = 8
HG_F = HG_HEADS * HG_EXPAND
HG_I = D_MODEL
HG_DI = HG_I // HG_HEADS
N_EXPERTS = 16
N_GROUPS = 4
EXPERTS_PER_GROUP = 4
D_FF_EXPERT = 512
ALPHA = (2 * DEPTH) ** 0.25
EPS = 1e-5

LANES = 128
VMEM_LIMIT = 48 * 1024 * 1024

AB_Z = 0
AB_V = 1024
AB_GOUT = 2048
AB_XBC = 3072
AB_Q = 4608
AB_K = 5120
AB_SMALL = 5632
AB_COLS = 5760
C_COLS = 4096

VEC_CHUNK = 64
VEC_SUB = 16
VEC_TILE = 256
VEC_ROWS = 512
VEC_HPS = 4
SSD_CHUNK = 128
STEP_B = 8
VEC_STEP_B = 16


def _params(*sem):
    return pltpu.CompilerParams(dimension_semantics=sem, vmem_limit_bytes=VMEM_LIMIT)


_NN = (((1,), (0,)), ((), ()))
_NT = (((1,), (1,)), ((), ()))
_TN = (((0,), (0,)), ((), ()))


def _dot1(dims, a, b):
    return lax.dot_general(a.astype(BF16), b.astype(BF16), dims, preferred_element_type=F32)


def _split2(a):
    hi = a.astype(BF16)
    return hi, (a - hi.astype(F32)).astype(BF16)


def _dot3(dims, a, b):
    ah, al = _split2(a)
    bh, bl = _split2(b)
    d = lambda x, y: lax.dot_general(x, y, dims, preferred_element_type=F32)
    return (d(al, bh) + d(ah, bl)) + d(ah, bh)


class _OnePass:
    nn = staticmethod(lambda a, b: _dot1(_NN, a, b))
    nt = staticmethod(lambda a, b: _dot1(_NT, a, b))
    tn = staticmethod(lambda a, b: _dot1(_TN, a, b))


class _ThreePass:
    nn = staticmethod(lambda a, b: _dot3(_NN, a, b))
    nt = staticmethod(lambda a, b: _dot3(_NT, a, b))
    tn = staticmethod(lambda a, b: _dot3(_TN, a, b))


def _dot(a, b):
    return _dot1(_NN, a, b)


def _dot_nt(a, b):
    return _dot1(_NT, a, b)


def _dot_tn(a, b):
    return _dot1(_TN, a, b)


def _split3(a):
    hi = a.astype(BF16)
    r1 = a - hi.astype(F32)
    mid = r1.astype(BF16)
    lo = (r1 - mid.astype(F32)).astype(BF16)
    return hi, mid, lo


def _dot_exact_rhs(sel, a):
    hi, mid, lo = _split3(a)
    d = lambda p: jnp.dot(sel, p, preferred_element_type=F32)
    return (d(lo) + d(mid)) + d(hi)


def _dot_exact_lhs(a, sel):
    hi, mid, lo = _split3(a)
    d = lambda p: jnp.dot(p, sel, preferred_element_type=F32)
    return (d(lo) + d(mid)) + d(hi)


def _tril(n):
    r = lax.broadcasted_iota(jnp.int32, (n, n), 0)
    c = lax.broadcasted_iota(jnp.int32, (n, n), 1)
    return r >= c


def _sigmoid(x):
    return 1.0 / (1.0 + jnp.exp(-x))


def _silu(x):
    return x * _sigmoid(x)


def _softplus(x):
    return jnp.maximum(x, 0.0) + jnp.log(1.0 + jnp.exp(-jnp.abs(x)))


def _log_sigmoid(x):
    return -_softplus(-x)


def _rms(x, w):
    return x * lax.rsqrt(jnp.mean(x * x, axis=-1, keepdims=True) + EPS) * w


def _layer_norm(x, w, b):
    mu = jnp.mean(x, axis=-1, keepdims=True)
    xc = x - mu
    var = jnp.mean(xc * xc, axis=-1, keepdims=True)
    return xc * lax.rsqrt(var + EPS) * w + b


def _proj_kernel(x_ref, w_ref, o_ref):
    o_ref[...] = jnp.dot(x_ref[...].astype(BF16), w_ref[...], preferred_element_type=F32)


def _proj3_kernel(x_ref, wh_ref, wl_ref, o_ref, xh_ref, xl_ref):
    @pl.when(pl.program_id(1) == 0)
    def _():
        hi, lo = _split2(x_ref[...])
        xh_ref[...] = hi
        xl_ref[...] = lo

    d = lambda a, b: jnp.dot(a, b, preferred_element_type=F32)
    xh = xh_ref[...]
    wh = wh_ref[...]
    o_ref[...] = (d(xl_ref[...], wh) + d(xh, wl_ref[...])) + d(xh, wh)


def _proj(x, w, tm, tn):
    t, k = x.shape
    three = isinstance(w, tuple)
    ws = w if three else (w,)
    n = ws[0].shape[1]
    return pl.pallas_call(
        _proj3_kernel if three else _proj_kernel,
        out_shape=jax.ShapeDtypeStruct((t, n), F32),
        grid=(t // tm, n // tn),
        in_specs=[pl.BlockSpec((tm, k), lambda i, j: (i, 0))]
                 + [pl.BlockSpec((k, tn), lambda i, j: (0, j)) for _ in ws],
        out_specs=pl.BlockSpec((tm, tn), lambda i, j: (i, j)),
        scratch_shapes=[pltpu.VMEM((tm, k), BF16), pltpu.VMEM((tm, k), BF16)] if three else [],
        compiler_params=_params("parallel", "arbitrary"),
        name="in_proj",
    )(x, *ws)


def _outproj_ln_kernel(n_in, three, *refs):
    a_refs = refs[:n_in]
    nw = 2 if three else 1
    w_refs = refs[n_in:n_in + nw * n_in]
    x_ref, lw_ref, lb_ref, o_ref = refs[n_in + nw * n_in:]
    d = lambda a, b: jnp.dot(a, b, preferred_element_type=F32)
    mix = None
    for i, a_ref in enumerate(a_refs):
        if three:
            ah, al = _split2(a_ref[...])
            wh = w_refs[2 * i][...]
            part = (d(al, wh) + d(ah, w_refs[2 * i + 1][...])) + d(ah, wh)
        else:
            part = d(a_ref[...].astype(BF16), w_refs[i][...])
        mix = part if mix is None else mix + part
    o_ref[...] = _layer_norm(ALPHA * x_ref[...] + mix, lw_ref[...], lb_ref[...])


def _outproj_ln(acts, ws, x, ln_w, ln_b, tm):
    t = x.shape[0]
    n_in = len(acts)
    three = isinstance(ws[0], tuple)
    flat_ws = [w for pair in ws for w in pair] if three else list(ws)
    row = lambda i: (i, 0)
    fixed = lambda i: (0, 0)
    in_specs = ([pl.BlockSpec((tm, a.shape[1]), row) for a in acts]
                + [pl.BlockSpec(w.shape, fixed) for w in flat_ws]
                + [pl.BlockSpec((tm, D_MODEL), row),
                   pl.BlockSpec((1, D_MODEL), fixed), pl.BlockSpec((1, D_MODEL), fixed)])
    return pl.pallas_call(
        functools.partial(_outproj_ln_kernel, n_in, three),
        out_shape=jax.ShapeDtypeStruct((t, D_MODEL), F32),
        grid=(t // tm,),
        in_specs=in_specs,
        out_specs=pl.BlockSpec((tm, D_MODEL), row),
        compiler_params=_params("parallel"),
        name="out_proj_ln",
    )(*acts, *flat_ws, x, ln_w, ln_b)


def _router_scores(x, rwt, bias):
    xh, xl = _split2(x)
    nt = lambda a, b: lax.dot_general(a, b, _NT, preferred_element_type=F32)
    a = nt(xh, rwt)
    b = nt(xl, rwt[:2 * LANES])
    by_token = ((a[:, 2 * LANES:] + b[:, LANES:]) + (a[:, LANES:2 * LANES] + b[:, :LANES])) + a[:, :LANES]
    scores = _sigmoid(by_token.T[:N_EXPERTS])
    return scores, scores + bias


def _best_group(sel):
    tm = sel.shape[1]
    s = [sel[e:e + 1, :] for e in range(N_EXPERTS)]
    grp = []
    for g in range(N_GROUPS):
        m = s[g * EXPERTS_PER_GROUP:(g + 1) * EXPERTS_PER_GROUP]
        best = None
        for i in range(EXPERTS_PER_GROUP):
            for j in range(i + 1, EXPERTS_PER_GROUP):
                p = m[i] + m[j]
                best = p if best is None else jnp.maximum(best, p)
        grp.append(best)
    best_g = jnp.zeros((1, tm), jnp.int32)
    best_v = grp[0]
    for g in range(1, N_GROUPS):
        upd = grp[g] > best_v
        best_g = jnp.where(upd, g, best_g)
        best_v = jnp.where(upd, grp[g], best_v)
    return best_g


def _top2(vals, weights):
    tm = vals[0].shape[1]
    neg = jnp.full((1, tm), -jnp.inf, F32)

    def first_argmax(rows):
        idx = jnp.zeros((1, tm), jnp.int32)
        top = rows[0]
        for e in range(1, len(rows)):
            upd = rows[e] > top
            idx = jnp.where(upd, e, idx)
            top = jnp.where(upd, rows[e], top)
        return idx

    idx1 = first_argmax(vals)
    idx2 = first_argmax([jnp.where(idx1 == e, neg, v) for e, v in enumerate(vals)])
    zero = jnp.zeros((1, tm), F32)
    w1 = zero
    w2 = zero
    for e, w in enumerate(weights):
        w1 = w1 + jnp.where(idx1 == e, w, zero)
        w2 = w2 + jnp.where(idx2 == e, w, zero)
    tot = w1 + w2
    g1 = w1 / tot
    g2 = w2 / tot
    return [jnp.where(idx1 == e, g1, zero) + jnp.where(idx2 == e, g2, zero)
            for e in range(len(vals))]


def _pad_rows(rows, tm):
    return jnp.concatenate(rows + [jnp.zeros((LANES - len(rows), tm), F32)], axis=0)


def _route_in_group(x, rwt, bias, group):
    tm = x.shape[0]
    scores, sel = _router_scores(x, rwt, bias)
    zero = jnp.zeros((1, tm), F32)
    vals, weights = [], []
    for m in range(EXPERTS_PER_GROUP):
        v = zero
        w = zero
        for g in range(N_GROUPS):
            e = g * EXPERTS_PER_GROUP + m
            v = jnp.where(group == g, sel[e:e + 1, :], v)
            w = jnp.where(group == g, scores[e:e + 1, :], w)
        vals.append(v)
        weights.append(w)
    return _pad_rows(_top2(vals, weights), tm)


def _route(x, rwt, bias):
    tm = x.shape[0]
    scores, sel = _router_scores(x, rwt, bias)
    s = [sel[e:e + 1, :] for e in range(N_EXPERTS)]
    sc = [scores[e:e + 1, :] for e in range(N_EXPERTS)]
    best_g = _best_group(sel)
    neg = jnp.full((1, tm), -jnp.inf, F32)
    ms = [jnp.where(best_g == e // EXPERTS_PER_GROUP, s[e], neg) for e in range(N_EXPERTS)]
    return _pad_rows(_top2(ms, sc), tm), best_g


def _router_kernel(x_ref, rwt_ref, bias_ref, g_ref):
    g_ref[...] = _route(x_ref[...], rwt_ref[...], bias_ref[...])[0].T


def _router(x, rwt, bias, tm):
    t = x.shape[0]
    return pl.pallas_call(
        _router_kernel,
        out_shape=jax.ShapeDtypeStruct((t, LANES), F32),
        grid=(t // tm,),
        in_specs=[pl.BlockSpec((tm, D_MODEL), lambda i: (i, 0)),
                  pl.BlockSpec((3 * LANES, D_MODEL), lambda i: (0, 0)),
                  pl.BlockSpec((N_EXPERTS, 1), lambda i: (0, 0))],
        out_specs=pl.BlockSpec((tm, LANES), lambda i: (i, 0)),
        compiler_params=_params("parallel"),
        name="router",
    )(x, rwt, bias)


def _group_router_kernel(x_ref, rwt_ref, bias_ref, g_ref, tot_ref, carry_ref):
    tm = x_ref.shape[0]

    @pl.when(pl.program_id(0) == 0)
    def _():
        carry_ref[...] = jnp.zeros_like(carry_ref)

    best_g = _best_group(_router_scores(x_ref[...], rwt_ref[...], bias_ref[...])[1])
    grow = lax.broadcasted_iota(jnp.int32, (8, tm), 0)
    onehot = jnp.where(grow == best_g, 1.0, 0.0)
    rr = lax.broadcasted_iota(jnp.int32, (tm, tm), 0)
    cc = lax.broadcasted_iota(jnp.int32, (tm, tm), 1)
    before = jnp.where(rr < cc, 1.0, 0.0).astype(BF16)
    earlier = jnp.dot(onehot.astype(BF16), before, preferred_element_type=F32)
    carry = carry_ref[...]
    earlier = earlier + jnp.concatenate([carry] * (tm // LANES), axis=1)
    rank = jnp.sum(onehot * earlier, axis=0, keepdims=True)
    row = lax.broadcasted_iota(jnp.int32, (LANES, tm), 0)
    packed = jnp.where(row == 0, best_g.astype(F32), jnp.where(row == 1, rank, 0.0))
    g_ref[...] = packed.T
    carry = carry + jnp.sum(onehot, axis=1, keepdims=True)
    carry_ref[...] = carry
    tot_ref[...] = carry


def _group_router(x, rwt, bias, tm):
    t = x.shape[0]
    return pl.pallas_call(
        _group_router_kernel,
        out_shape=(jax.ShapeDtypeStruct((t, LANES), F32), jax.ShapeDtypeStruct((8, LANES), F32)),
        grid=(t // tm,),
        in_specs=[pl.BlockSpec((tm, D_MODEL), lambda i: (i, 0)),
                  pl.BlockSpec((3 * LANES, D_MODEL), lambda i: (0, 0)),
                  pl.BlockSpec((N_EXPERTS, 1), lambda i: (0, 0))],
        out_specs=(pl.BlockSpec((tm, LANES), lambda i: (i, 0)),
                   pl.BlockSpec((8, LANES), lambda i: (0, 0))),
        scratch_shapes=[pltpu.VMEM((8, LANES), F32)],
        compiler_params=_params("arbitrary"),
        name="group_router",
    )(x, rwt, bias)


def _moe_kernel(x_ref, g_ref, wg_ref, wu_ref, wd_ref, lw_ref, lb_ref, o_ref, acc_ref, xb_ref):
    e = pl.program_id(1)

    @pl.when(e == 0)
    def _():
        xb_ref[...] = x_ref[...].astype(BF16)
        acc_ref[...] = jnp.zeros_like(acc_ref)

    xb = xb_ref[...]
    hg = jnp.dot(xb, wg_ref[0].astype(BF16), preferred_element_type=F32)
    hu = jnp.dot(xb, wu_ref[0].astype(BF16), preferred_element_type=F32)
    he = _silu(hg) * hu
    gates = g_ref[...]
    lane = lax.broadcasted_iota(jnp.int32, gates.shape, 1)
    ge = jnp.sum(jnp.where(lane == e, gates, 0.0), axis=1, keepdims=True)
    acc_ref[...] += ge * jnp.dot(he.astype(BF16), wd_ref[0].astype(BF16),
                                  preferred_element_type=F32)

    @pl.when(e == N_EXPERTS - 1)
    def _():
        o_ref[...] = _layer_norm(ALPHA * x_ref[...] + acc_ref[...], lw_ref[...], lb_ref[...])


def _moe_ln(x, gates, wg, wu, wd, layer, ln_w, ln_b, tm):
    t = x.shape[0]
    return pl.pallas_call(
        _moe_kernel,
        out_shape=jax.ShapeDtypeStruct((t, D_MODEL), F32),
        grid=(t // tm, N_EXPERTS),
        in_specs=[pl.BlockSpec((tm, D_MODEL), lambda i, e: (i, 0)),
                  pl.BlockSpec((tm, LANES), lambda i, e: (i, 0)),
                  pl.BlockSpec((1, D_MODEL, D_FF_EXPERT), lambda i, e: (layer * N_EXPERTS + e, 0, 0)),
                  pl.BlockSpec((1, D_MODEL, D_FF_EXPERT), lambda i, e: (layer * N_EXPERTS + e, 0, 0)),
                  pl.BlockSpec((1, D_FF_EXPERT, D_MODEL), lambda i, e: (layer * N_EXPERTS + e, 0, 0)),
                  pl.BlockSpec((1, D_MODEL), lambda i, e: (0, 0)),
                  pl.BlockSpec((1, D_MODEL), lambda i, e: (0, 0))],
        out_specs=pl.BlockSpec((tm, D_MODEL), lambda i, e: (i, 0)),
        scratch_shapes=[pltpu.VMEM((tm, D_MODEL), F32), pltpu.VMEM((tm, D_MODEL), BF16)],
        compiler_params=_params("parallel", "arbitrary"),
        name="moe_ln",
    )(x, gates, wg, wu, wd, ln_w, ln_b)


def _row_copy(src, dst, sem):
    return pltpu.make_async_copy(src, dst, sem)


def _scatter_rows_kernel(dest_ref, x_ref, init_ref, o_hbm, buf_ref, sem):
    del init_ref
    n = x_ref.shape[0]
    base = pl.program_id(0) * n
    buf_ref[...] = x_ref[...].reshape(buf_ref.shape)

    def start(i, carry):
        _row_copy(buf_ref.at[i], o_hbm.at[dest_ref[base + i]], sem).start()
        return carry

    lax.fori_loop(0, n, start, 0)
    _row_copy(buf_ref, o_hbm.at[pl.ds(0, n)], sem).wait()


def _scatter_rows(x, dest, init, tr):
    t = x.shape[0]
    slabs = D_MODEL // LANES
    return pl.pallas_call(
        _scatter_rows_kernel,
        out_shape=jax.ShapeDtypeStruct(init.shape, F32),
        grid_spec=pltpu.PrefetchScalarGridSpec(
            num_scalar_prefetch=1,
            grid=(t // tr,),
            in_specs=[pl.BlockSpec((tr, D_MODEL), lambda i, d: (i, 0)),
                      pl.BlockSpec(memory_space=pl.ANY)],
            out_specs=pl.BlockSpec(memory_space=pl.ANY),
            scratch_shapes=[pltpu.VMEM((tr, slabs, LANES), F32), pltpu.SemaphoreType.DMA(())]),
        input_output_aliases={2: 0},
        compiler_params=_params("arbitrary"),
        name="scatter_rows",
    )(dest, x, init)


def _gather_rows_kernel(src_ref, y_hbm, o_ref, buf_ref, sems):
    n = o_ref.shape[0]
    i = pl.program_id(0)
    slot = i % 2

    def issue(step, to_slot):
        def start(r, carry):
            _row_copy(y_hbm.at[src_ref[step * n + r]], buf_ref.at[to_slot, r],
                      sems.at[to_slot]).start()
            return carry
        lax.fori_loop(0, n, start, 0)

    @pl.when(i == 0)
    def _():
        issue(0, 0)

    @pl.when(i + 1 < pl.num_programs(0))
    def _():
        issue(i + 1, 1 - slot)

    _row_copy(y_hbm.at[pl.ds(0, n)], buf_ref.at[slot], sems.at[slot]).wait()
    o_ref[...] = buf_ref[slot].reshape(o_ref.shape)


def _gather_rows(y3, src, tr):
    t = src.shape[0]
    slabs = D_MODEL // LANES
    return pl.pallas_call(
        _gather_rows_kernel,
        out_shape=jax.ShapeDtypeStruct((t, D_MODEL), F32),
        grid_spec=pltpu.PrefetchScalarGridSpec(
            num_scalar_prefetch=1,
            grid=(t // tr,),
            in_specs=[pl.BlockSpec(memory_space=pl.ANY)],
            out_specs=pl.BlockSpec((tr, D_MODEL), lambda i, d: (i, 0)),
            scratch_shapes=[pltpu.VMEM((2, tr, slabs, LANES), F32),
                            pltpu.SemaphoreType.DMA((2,))]),
        compiler_params=_params("arbitrary"),
        name="gather_rows",
    )(src, y3)


def _moe_group_kernel(tg_ref, x3_ref, rwt_ref, rb_ref, wg_ref, wu_ref, wd_ref, lw_ref, lb_ref,
                      o3_ref, acc_ref, x_ref, xb_ref, gate_ref):
    i = pl.program_id(0)
    j = pl.program_id(1)
    group = tg_ref[i]
    slabs = D_MODEL // LANES

    @pl.when(group < 0)
    def _():
        o3_ref[...] = jnp.zeros_like(o3_ref)

    @pl.when(group >= 0)
    def _():
        @pl.when(j == 0)
        def _():
            x = x3_ref[...].reshape(x_ref.shape)
            x_ref[...] = x
            xb_ref[...] = x.astype(BF16)
            acc_ref[...] = jnp.zeros_like(acc_ref)
            gate_ref[...] = _route_in_group(x, rwt_ref[...], rb_ref[...], group).T

        xb = xb_ref[...]
        hg = jnp.dot(xb, wg_ref[0].astype(BF16), preferred_element_type=F32)
        hu = jnp.dot(xb, wu_ref[0].astype(BF16), preferred_element_type=F32)
        he = _silu(hg) * hu
        gates = gate_ref[...]
        lane = lax.broadcasted_iota(jnp.int32, gates.shape, 1)
        ge = jnp.sum(jnp.where(lane == j, gates, 0.0), axis=1, keepdims=True)
        acc_ref[...] += ge * jnp.dot(he.astype(BF16), wd_ref[0].astype(BF16),
                                     preferred_element_type=F32)

        @pl.when(j == EXPERTS_PER_GROUP - 1)
        def _():
            y = _layer_norm(ALPHA * x_ref[...] + acc_ref[...], lw_ref[...], lb_ref[...])
            o3_ref[...] = y.reshape(y.shape[0], slabs, LANES)


def _moe_group_ln(xs3, tile_group, rwt, rbias, wg, wu, wd, layer, ln_w, ln_b, tm):
    n = xs3.shape[0]
    slabs = D_MODEL // LANES
    expert = lambda i, j, tg: (layer * N_EXPERTS + jnp.maximum(tg[i], 0) * EXPERTS_PER_GROUP + j, 0, 0)
    fixed = lambda i, j, tg: (0, 0)
    return pl.pallas_call(
        _moe_group_kernel,
        out_shape=jax.ShapeDtypeStruct((n, slabs, LANES), F32),
        grid_spec=pltpu.PrefetchScalarGridSpec(
            num_scalar_prefetch=1,
            grid=(n // tm, EXPERTS_PER_GROUP),
            in_specs=[pl.BlockSpec((tm, slabs, LANES), lambda i, j, tg: (i, 0, 0)),
                      pl.BlockSpec((3 * LANES, D_MODEL), fixed),
                      pl.BlockSpec((N_EXPERTS, 1), fixed),
                      pl.BlockSpec((1, D_MODEL, D_FF_EXPERT), expert),
                      pl.BlockSpec((1, D_MODEL, D_FF_EXPERT), expert),
                      pl.BlockSpec((1, D_FF_EXPERT, D_MODEL), expert),
                      pl.BlockSpec((1, D_MODEL), fixed),
                      pl.BlockSpec((1, D_MODEL), fixed)],
            out_specs=pl.BlockSpec((tm, slabs, LANES), lambda i, j, tg: (i, 0, 0)),
            scratch_shapes=[pltpu.VMEM((tm, D_MODEL), F32), pltpu.VMEM((tm, D_MODEL), F32),
                            pltpu.VMEM((tm, D_MODEL), BF16), pltpu.VMEM((tm, LANES), F32)]),
        compiler_params=_params("arbitrary", "arbitrary"),
        name="moe_group_ln",
    )(tile_group, xs3, rwt, rbias, wg, wu, wd, ln_w, ln_b)


def _ffn_sorted(x, p, layer, tm, tm_moe, spare=None):
    t = x.shape[0]
    n_tiles = t // tm_moe + N_GROUPS
    gmat, totals = _group_router(x, p['rwt'], p['rbias'], tm)
    group = gmat[:, 0].astype(jnp.int32)
    counts = totals[:N_GROUPS, 0].astype(jnp.int32)
    seg_tiles = (counts + tm_moe - 1) // tm_moe
    seg_end = jnp.cumsum(seg_tiles)
    seg_start = seg_end - seg_tiles
    is_group = group[:, None] == jnp.arange(N_GROUPS, dtype=jnp.int32)[None, :]
    dest = (jnp.sum(jnp.where(is_group, seg_start[None, :], 0), axis=1) * tm_moe
            + gmat[:, 1].astype(jnp.int32))
    tile_id = jnp.arange(n_tiles, dtype=jnp.int32)
    tile_group = jnp.sum((tile_id[:, None] >= seg_end[None, :]).astype(jnp.int32), axis=1)
    tile_group = jnp.where(tile_id < seg_end[N_GROUPS - 1], tile_group, -1)
    if spare is None:
        spare = jnp.zeros((n_tiles * tm_moe, D_MODEL // LANES, LANES), F32)
    xs3 = _scatter_rows(x, dest, spare, tm)
    ys3 = _moe_group_ln(xs3, tile_group, p['rwt'], p['rbias'], p['w_gate'], p['w_up'], p['w_down'],
                        layer, p['ln2_w'][layer], p['ln2_b'][layer], tm_moe)
    return _gather_rows(ys3, dest, tm), ys3


def _prefix_selector():
    n = VEC_TILE
    nsub = VEC_CHUNK // VEC_SUB
    t = np.arange(n)[:, None]
    s = np.arange(n)[None, :]
    incl = ((t // VEC_CHUNK) == (s // VEC_CHUNK)) & ((s % VEC_CHUNK) <= (t % VEC_CHUNK))
    r = np.arange((n // VEC_CHUNK) * nsub)[:, None]
    starts = ((r // nsub) == (s // VEC_CHUNK)) & ((s % VEC_CHUNK) < VEC_SUB * (r % nsub))
    return np.concatenate([incl, starts], axis=0).astype(np.float32)


def _vec_heads(heads, sel, mm):
    n = VEC_TILE
    nsub = VEC_CHUNK // VEC_SUB
    nchunk = n // VEC_CHUNK
    nrows = heads[0][0].shape[0]
    kdim = heads[0][0].shape[1]
    streams = [(h, i) for h in range(len(heads)) for i in range(0, nrows, n)]
    tile = lambda h, i, which: heads[h][which][i:i + n]

    prefs = [_dot_exact_rhs(sel, tile(h, i, 3)) for h, i in streams]
    rows_of = lambda fn, m: jnp.concatenate(
        [jnp.broadcast_to(fn(j), (m, kdim)) for j in range(n // m)], axis=0)
    sub = (lax.broadcasted_iota(jnp.int32, (n, kdim), 0) // VEC_SUB) % nsub
    q_cat, k_cat, q_dec0, updates = [], [], [], []
    for (h, i), pref in zip(streams, prefs):
        q, k, v = tile(h, i, 0), tile(h, i, 1), tile(h, i, 2)
        big_g = pref[0:n]
        start = lambda c, j, pref=pref: pref[n + c * nsub + j:n + c * nsub + j + 1]
        q_dec = [q * jnp.exp(big_g)]
        for j in range(1, nsub):
            base_j = rows_of(lambda c: start(c, j), VEC_CHUNK)
            q_dec.append(q * jnp.exp(jnp.minimum(big_g - base_j, 0.0)))
        base_own = rows_of(lambda m: start(m // nsub, m % nsub), VEC_SUB)
        k_rel = k * jnp.exp(base_own - big_g)
        k_cat.append(jnp.concatenate([jnp.where(sub == j, k_rel, 0.0) for j in range(nsub)],
                                     axis=1))
        q_cat.append(jnp.concatenate(q_dec, axis=1))
        q_dec0.append(q_dec[0])
        per_chunk = []
        for c in range(nchunk):
            rows = slice(c * VEC_CHUNK, (c + 1) * VEC_CHUNK)
            g_last = big_g[(c + 1) * VEC_CHUNK - 1:(c + 1) * VEC_CHUNK, :]
            kd = k[rows] * jnp.exp(g_last - big_g[rows])
            per_chunk.append((jnp.exp(g_last), mm.tn(v[rows], kd)))
        updates.append(per_chunk)
    scores = [mm.nt(qc, kc) for qc, kc in zip(q_cat, k_cat)]
    rr = lax.broadcasted_iota(jnp.int32, (n, n), 0)
    cc = lax.broadcasted_iota(jnp.int32, (n, n), 1)
    keep = (rr >= cc) & ((rr // VEC_CHUNK) == (cc // VEC_CHUNK))
    intra = [mm.nn(jnp.where(keep, sc, 0.0), tile(h, i, 2)) for (h, i), sc in zip(streams, scores)]

    states = [hd[4] for hd in heads]
    o_rows = [[] for _ in heads]
    for si, (h, i) in enumerate(streams):
        for c, (decay_last, update) in enumerate(updates[si]):
            rows = slice(c * VEC_CHUNK, (c + 1) * VEC_CHUNK)
            o_rows[h].append(intra[si][rows] + mm.nt(q_dec0[si][rows], states[h]))
            states[h] = states[h] * decay_last + update
    return [(jnp.concatenate(o_rows[h], axis=0), states[h]) for h in range(len(heads))]


def _gla_chunk_kernel(q_ref, k_ref, v_ref, go_ref, sm_ref, w2_ref, b2_ref, nw_ref, sel_ref,
                      o_ref, s_ref, st_ref):
    r = pl.program_id(2)

    @pl.when(r == 0)
    def _():
        st_ref[...] = jnp.zeros_like(st_ref)

    sm = sm_ref[...]
    heads = []
    for hh in range(VEC_HPS):
        kc = slice(hh * GLA_DK, (hh + 1) * GLA_DK)
        vc = slice(hh * GLA_DV, (hh + 1) * GLA_DV)
        gk = _log_sigmoid(_ThreePass.nn(sm, w2_ref[hh]) + b2_ref[hh]) / GLA_NORMALIZER
        heads.append((q_ref[:, kc] * (GLA_DK ** -0.5), k_ref[:, kc], v_ref[:, vc], gk, st_ref[hh]))
    finals = []
    for hh, (o, st) in enumerate(_vec_heads(heads, sel_ref[...], _ThreePass)):
        vc = slice(hh * GLA_DV, (hh + 1) * GLA_DV)
        o_ref[:, vc] = _rms(o, nw_ref[...]) * _silu(go_ref[:, vc])
        st_ref[hh] = st
        finals.append(st)

    @pl.when(r == pl.num_programs(2) - 1)
    def _():
        for hh in range(VEC_HPS):
            s_ref[0, hh] = finals[hh].T


def _gla_prompt(proj, w2p, b2, norm_w, sel, bsz, seq):
    nr = seq // VEC_ROWS
    ng = GLA_HEADS // VEC_HPS
    kw = VEC_HPS * GLA_DK
    vw = VEC_HPS * GLA_DV
    row = lambda off: (lambda b, h, r: (b * nr + r, off + h))
    return pl.pallas_call(
        _gla_chunk_kernel,
        out_shape=(jax.ShapeDtypeStruct((bsz * seq, GLA_VAL), F32),
                   jax.ShapeDtypeStruct((bsz, GLA_HEADS, GLA_DK, GLA_DV), F32)),
        grid=(bsz, ng, nr),
        in_specs=[pl.BlockSpec((VEC_ROWS, kw), row(AB_Q // kw)),
                  pl.BlockSpec((VEC_ROWS, kw), row(AB_K // kw)),
                  pl.BlockSpec((VEC_ROWS, vw), row(AB_V // vw)),
                  pl.BlockSpec((VEC_ROWS, vw), row(AB_GOUT // vw)),
                  pl.BlockSpec((VEC_ROWS, LANES), lambda b, h, r: (b * nr + r, AB_SMALL // LANES)),
                  pl.BlockSpec((VEC_HPS, LANES, GLA_DK), lambda b, h, r: (h, 0, 0)),
                  pl.BlockSpec((VEC_HPS, 1, GLA_DK), lambda b, h, r: (h, 0, 0)),
                  pl.BlockSpec((1, GLA_DV), lambda b, h, r: (0, 0)),
                  pl.BlockSpec(sel.shape, lambda b, h, r: (0, 0))],
        out_specs=(pl.BlockSpec((VEC_ROWS, vw), lambda b, h, r: (b * nr + r, h)),
                   pl.BlockSpec((1, VEC_HPS, GLA_DK, GLA_DV), lambda b, h, r: (b, h, 0, 0))),
        scratch_shapes=[pltpu.VMEM((VEC_HPS, GLA_DV, GLA_DK), F32)],
        compiler_params=_params("parallel", "parallel", "arbitrary"),
        name="gla_chunk",
    )(proj, proj, proj, proj, proj, w2p, b2, norm_w, sel)


def _hgrn_lower_bound(lbraw, layer):
    m = jnp.max(lbraw, axis=0, keepdims=True)
    ex = jnp.exp(lbraw - m)
    sm = ex / jnp.sum(ex, axis=0, keepdims=True)
    acc = sm[0:1]
    for i in range(1, layer + 1):
        acc = acc + sm[i:i + 1]
    return acc - sm[0:1]


def _hgrn_gates(q_raw, f_raw, lb):
    forget = lb + (1.0 - lb) * _sigmoid(f_raw)
    return _silu(q_raw), 1.0 - forget, jnp.log(forget)


def _hgrn_chunk_kernel(layer, q_ref, f_ref, i_ref, go_ref, lb_ref, nw_ref, sel_ref,
                       o_ref, s_ref, st_ref):
    r = pl.program_id(2)

    @pl.when(r == 0)
    def _():
        st_ref[...] = jnp.zeros_like(st_ref)

    lb_all = _hgrn_lower_bound(lb_ref[...], layer)
    heads = []
    for hh in range(VEC_HPS):
        kc = slice(hh * HG_EXPAND, (hh + 1) * HG_EXPAND)
        vc = slice(hh * HG_DI, (hh + 1) * HG_DI)
        q, k, g = _hgrn_gates(q_ref[:, kc], f_ref[:, kc], lb_all[:, kc])
        heads.append((q, k, i_ref[:, vc], g, st_ref[hh]))
    finals = []
    for hh, (o, st) in enumerate(_vec_heads(heads, sel_ref[...], _OnePass)):
        vc = slice(hh * HG_DI, (hh + 1) * HG_DI)
        o_ref[:, vc] = _rms(o, nw_ref[...]) * _silu(go_ref[:, vc])
        st_ref[hh] = st
        finals.append(st)

    @pl.when(r == pl.num_programs(2) - 1)
    def _():
        for hh in range(VEC_HPS):
            s_ref[0, hh] = finals[hh].T


def _hgrn_prompt(proj, lower_bounds, norm_w, sel, layer, bsz, seq):
    nr = seq // VEC_ROWS
    ng = HG_HEADS // VEC_HPS
    kw = VEC_HPS * HG_EXPAND
    vw = VEC_HPS * HG_DI
    row = lambda off: (lambda b, h, r: (b * nr + r, off + h))
    return pl.pallas_call(
        functools.partial(_hgrn_chunk_kernel, layer),
        out_shape=(jax.ShapeDtypeStruct((bsz * seq, HG_I), F32),
                   jax.ShapeDtypeStruct((bsz, HG_HEADS, HG_EXPAND, HG_DI), F32)),
        grid=(bsz, ng, nr),
        in_specs=[pl.BlockSpec((VEC_ROWS, kw), row(0)),
                  pl.BlockSpec((VEC_ROWS, kw), row(ng)),
                  pl.BlockSpec((VEC_ROWS, vw), row(2 * ng)),
                  pl.BlockSpec((VEC_ROWS, vw), row(3 * ng)),
                  pl.BlockSpec((DEPTH, kw), lambda b, h, r: (0, h)),
                  pl.BlockSpec((1, HG_DI), lambda b, h, r: (0, 0)),
                  pl.BlockSpec(sel.shape, lambda b, h, r: (0, 0))],
        out_specs=(pl.BlockSpec((VEC_ROWS, vw), lambda b, h, r: (b * nr + r, h)),
                   pl.BlockSpec((1, VEC_HPS, HG_EXPAND, HG_DI), lambda b, h, r: (b, h, 0, 0))),
        scratch_shapes=[pltpu.VMEM((VEC_HPS, HG_DI, HG_EXPAND), F32)],
        compiler_params=_params("parallel", "parallel", "arbitrary"),
        name="hgrn_chunk",
    )(proj, proj, proj, proj, lower_bounds, norm_w, sel)


def _ssd_gate_norm(y, z, nw):
    yz = y * _silu(z)
    parts = []
    for g in range(SSD_GROUPS):
        cols = slice(g * SSD_GROUP_W, (g + 1) * SSD_GROUP_W)
        parts.append(_rms(yz[:, cols], nw[:, cols]))
    return jnp.concatenate(parts, axis=1)


def _ssd_chunk_kernel(z_ref, xbc_ref, sm_ref, cw_ref, cb_ref, dtb_ref, alog_ref, dsk_ref,
                      nw_ref, ex_ref, o_ref, s_ref, conv_ref, st_ref, prev_ref):
    r = pl.program_id(1)
    c = SSD_CHUNK
    mm = _ThreePass

    @pl.when(r == 0)
    def _():
        st_ref[...] = jnp.zeros_like(st_ref)
        prev_ref[...] = jnp.zeros_like(prev_ref)

    x_raw = xbc_ref[...]
    prev_ref[8:8 + c, :] = x_raw
    cw = cw_ref[...]
    acc = cb_ref[...] + cw[SSD_CONV - 1:SSD_CONV] * x_raw
    for m in range(1, SSD_CONV):
        acc = acc + cw[SSD_CONV - 1 - m:SSD_CONV - m] * prev_ref[8 - m:8 - m + c, :]
    xc = _silu(acc)
    prev_ref[0:8, :] = x_raw[c - 8:c]
    xs = xc[:, :SSD_INNER]
    bm = xc[:, SSD_INNER:SSD_INNER + SSD_BC]
    cm = xc[:, SSD_INNER + SSD_BC:]

    dt = _softplus(sm_ref[...] + dtb_ref[...])
    a_neg = -jnp.exp(alog_ref[...])
    big_g = _dot_exact_rhs(_tril(c).astype(BF16), dt * a_neg)
    g_t = big_g.T
    g_last = big_g[c - 1:c, :]
    ex = ex_ref[...]
    dt_x = _dot_exact_lhs(dt, ex)
    eg_x = _dot_exact_lhs(jnp.exp(big_g), ex)
    w_x = _dot_exact_lhs(dt * jnp.exp(g_last - big_g), ex)
    xdt = xs * dt_x
    xw = xs * w_x
    causal = _tril(c)
    lane = lax.broadcasted_iota(jnp.int32, (c, LANES), 1)
    st = st_ref[...]
    y_parts = []
    u_parts = []
    for g in range(SSD_GROUPS):
        gcols = slice(g * SSD_GROUP_W, (g + 1) * SSD_GROUP_W)
        bg = bm[:, g * SSD_STATE:(g + 1) * SSD_STATE]
        cg = cm[:, g * SSD_STATE:(g + 1) * SSD_STATE]
        sc = mm.nt(cg, bg)
        inter = mm.nn(cg, st[:, gcols])
        u_parts.append(mm.tn(bg, xw[:, gcols]))
        pair_cols = []
        heads_per_group = SSD_HEADS // SSD_GROUPS
        for p in range(heads_per_group // 2):
            h0 = g * heads_per_group + 2 * p
            xpair = xdt[:, h0 * SSD_HEADDIM:(h0 + 2) * SSD_HEADDIM]
            ws = []
            for h in (h0, h0 + 1):
                diff = big_g[:, h:h + 1] - g_t[h:h + 1, :]
                ws.append(sc * jnp.exp(jnp.where(causal, diff, -jnp.inf)))
            x_diag = jnp.concatenate([jnp.where(lane < SSD_HEADDIM, xpair, 0.0),
                                      jnp.where(lane < SSD_HEADDIM, 0.0, xpair)], axis=0)
            pair_cols.append(mm.nn(jnp.concatenate(ws, axis=1), x_diag))
        y_intra = jnp.concatenate(pair_cols, axis=1)
        y_parts.append(y_intra + inter * eg_x[:, gcols])
    y = jnp.concatenate(y_parts, axis=1) + dsk_ref[...] * xs
    o_ref[...] = _ssd_gate_norm(y, z_ref[...], nw_ref[...])
    st = st * eg_x[c - 1:c, :] + jnp.concatenate(u_parts, axis=1)
    st_ref[...] = st

    @pl.when(r == pl.num_programs(1) - 1)
    def _():
        s_ref[0] = st
        conv_ref[0] = x_raw[c - (SSD_CONV - 1):c]


def _ssd_prompt(proj, conv_w, conv_b, dtb_p, alog_p, dskip_x, norm_w, expand, bsz, seq):
    nr = seq // SSD_CHUNK
    fixed = lambda b, r: (0, 0)
    return pl.pallas_call(
        _ssd_chunk_kernel,
        out_shape=(jax.ShapeDtypeStruct((bsz * seq, SSD_INNER), F32),
                   jax.ShapeDtypeStruct((bsz, SSD_STATE, SSD_INNER), F32),
                   jax.ShapeDtypeStruct((bsz, SSD_CONV - 1, SSD_CONV_DIM), F32)),
        grid=(bsz, nr),
        in_specs=[pl.BlockSpec((SSD_CHUNK, SSD_INNER), lambda b, r: (b * nr + r, AB_Z // SSD_INNER)),
                  pl.BlockSpec((SSD_CHUNK, SSD_CONV_DIM), lambda b, r: (b * nr + r, AB_XBC // SSD_CONV_DIM)),
                  pl.BlockSpec((SSD_CHUNK, LANES), lambda b, r: (b * nr + r, AB_SMALL // LANES)),
                  pl.BlockSpec((SSD_CONV, SSD_CONV_DIM), fixed),
                  pl.BlockSpec((1, SSD_CONV_DIM), fixed),
                  pl.BlockSpec((1, LANES), fixed),
                  pl.BlockSpec((1, LANES), fixed),
                  pl.BlockSpec((1, SSD_INNER), fixed),
                  pl.BlockSpec((1, SSD_INNER), fixed),
                  pl.BlockSpec((LANES, SSD_INNER), fixed)],
        out_specs=(pl.BlockSpec((SSD_CHUNK, SSD_INNER), lambda b, r: (b * nr + r, 0)),
                   pl.BlockSpec((1, SSD_STATE, SSD_INNER), lambda b, r: (b, 0, 0)),
                   pl.BlockSpec((1, SSD_CONV - 1, SSD_CONV_DIM), lambda b, r: (b, 0, 0))),
        scratch_shapes=[pltpu.VMEM((SSD_STATE, SSD_INNER), F32),
                        pltpu.VMEM((8 + SSD_CHUNK, SSD_CONV_DIM), F32)],
        compiler_params=_params("parallel", "arbitrary"),
        name="ssd_chunk",
    )(proj, proj, proj, conv_w, conv_b, dtb_p, alog_p, dskip_x, norm_w, expand)


def _ab_prep_kernel(q_ref, sm_ref, xbc_ref, cs_ref, w2_ref, b2_ref, cw_ref, cb_ref, dtb_ref,
                    alog_ref, ex_ref, qs_ref, dec_ref, xc_ref, xdt_ref, dax_ref, cs_out_ref):
    sm = sm_ref[...]
    gk = _log_sigmoid(_ThreePass.nn(sm, w2_ref[...]) + b2_ref[...]) / GLA_NORMALIZER
    qs_ref[...] = q_ref[...] * (GLA_DK ** -0.5)
    dec_ref[...] = jnp.exp(gk)
    cw = cw_ref[...]
    x_raw = xbc_ref[...]
    acc = cb_ref[...] + cw[SSD_CONV - 1:SSD_CONV] * x_raw
    for j in range(SSD_CONV - 1):
        acc = acc + cw[j:j + 1] * cs_ref[j]
    xc = _silu(acc)
    xc_ref[...] = xc
    for j in range(SSD_CONV - 2):
        cs_out_ref[j] = cs_ref[j + 1]
    cs_out_ref[SSD_CONV - 2] = x_raw
    dt = _softplus(sm + dtb_ref[...])
    ex = ex_ref[...]
    xdt_ref[...] = xc[:, :SSD_INNER] * _dot_exact_lhs(dt, ex)
    dax_ref[...] = _dot_exact_lhs(jnp.exp(dt * -jnp.exp(alog_ref[...])), ex)


def _ab_prep(proj, conv_state, w2_wide, b2_wide, conv_w, conv_b, dtb_p, alog_p, expand):
    bsz = proj.shape[0]
    fixed = lambda i: (0, 0)
    sds = jax.ShapeDtypeStruct
    return pl.pallas_call(
        _ab_prep_kernel,
        out_shape=(sds((bsz, GLA_KEY), F32), sds((bsz, GLA_KEY), F32),
                   sds((bsz, SSD_CONV_DIM), F32), sds((bsz, SSD_INNER), F32),
                   sds((bsz, SSD_INNER), F32),
                   sds((SSD_CONV - 1, bsz, SSD_CONV_DIM), F32)),
        grid=(1,),
        in_specs=[pl.BlockSpec((bsz, GLA_KEY), lambda i: (0, AB_Q // GLA_KEY)),
                  pl.BlockSpec((bsz, LANES), lambda i: (0, AB_SMALL // LANES)),
                  pl.BlockSpec((bsz, SSD_CONV_DIM), lambda i: (0, AB_XBC // SSD_CONV_DIM)),
                  pl.BlockSpec((SSD_CONV - 1, bsz, SSD_CONV_DIM), lambda i: (0, 0, 0)),
                  pl.BlockSpec((LANES, GLA_KEY), fixed),
                  pl.BlockSpec((1, GLA_KEY), fixed),
                  pl.BlockSpec((SSD_CONV, SSD_CONV_DIM), fixed),
                  pl.BlockSpec((1, SSD_CONV_DIM), fixed),
                  pl.BlockSpec((1, LANES), fixed),
                  pl.BlockSpec((1, LANES), fixed),
                  pl.BlockSpec((LANES, SSD_INNER), fixed)],
        out_specs=(pl.BlockSpec((bsz, GLA_KEY), fixed), pl.BlockSpec((bsz, GLA_KEY), fixed),
                   pl.BlockSpec((bsz, SSD_CONV_DIM), fixed), pl.BlockSpec((bsz, SSD_INNER), fixed),
                   pl.BlockSpec((bsz, SSD_INNER), fixed),
                   pl.BlockSpec((SSD_CONV - 1, bsz, SSD_CONV_DIM), lambda i: (0, 0, 0))),
        compiler_params=_params("arbitrary"),
        name="ab_prep",
    )(proj, proj, proj, conv_state, w2_wide, b2_wide, conv_w, conv_b, dtb_p, alog_p, expand)


def _hgrn_prep_kernel(layer, q_ref, f_ref, lb_ref, qs_ref, k_ref, dec_ref):
    lb = _hgrn_lower_bound(lb_ref[...], layer)
    forget = lb + (1.0 - lb) * _sigmoid(f_ref[...])
    qs_ref[...] = _silu(q_ref[...])
    k_ref[...] = 1.0 - forget
    dec_ref[...] = jnp.exp(jnp.log(forget))


def _hgrn_prep(proj, lower_bounds, layer):
    bsz = proj.shape[0]
    blk = lambda j: pl.BlockSpec((bsz, HG_F), lambda i: (0, j))
    return pl.pallas_call(
        functools.partial(_hgrn_prep_kernel, layer),
        out_shape=tuple(jax.ShapeDtypeStruct((bsz, HG_F), F32) for _ in range(3)),
        grid=(1,),
        in_specs=[blk(0), blk(1), pl.BlockSpec((DEPTH, HG_F), lambda i: (0, 0))],
        out_specs=tuple(blk(0) for _ in range(3)),
        compiler_params=_params("arbitrary"),
        name="hgrn_prep",
    )(proj, proj, lower_bounds)


def _vec_step_kernel(mm, s_ref, q_ref, k_ref, d_ref, v_ref, go_ref, nw_ref, so_ref, o_ref):
    q = q_ref[...]
    kt = k_ref[0, 0]
    dt = d_ref[0, 0]
    v = v_ref[...]
    sb = v.shape[0]
    row = lax.broadcasted_iota(jnp.int32, v.shape, 0)
    new = []
    for b in range(sb):
        only_b = row == b
        decay = _dot_exact_lhs(dt, jnp.where(only_b, 1.0, 0.0).astype(BF16))
        new.append(s_ref[b, 0] * decay + mm.nn(kt, jnp.where(only_b, v, 0.0)))
    for b in range(sb):
        so_ref[b, 0] = new[b]
    o = jnp.concatenate([mm.nn(q, new[b])[b:b + 1] for b in range(sb)], axis=0)
    o_ref[...] = _rms(o, nw_ref[...]) * _silu(go_ref[...])


def _vec_step(state, q_rows, k_cols, d_cols, vsrc, v_off, gsrc, g_off, norm_w, mm):
    bsz, nh, kdim, vdim = state.shape
    sb = k_cols.shape[3]
    col = lambda j, h: (h, j, 0, 0)
    return pl.pallas_call(
        functools.partial(_vec_step_kernel, mm),
        out_shape=(jax.ShapeDtypeStruct(state.shape, F32),
                   jax.ShapeDtypeStruct((bsz, nh * vdim), F32)),
        grid=(bsz // sb, nh),
        in_specs=[pl.BlockSpec((sb, 1, kdim, vdim), lambda j, h: (j, h, 0, 0)),
                  pl.BlockSpec((sb, kdim), lambda j, h: (j, h)),
                  pl.BlockSpec((1, 1, kdim, sb), col),
                  pl.BlockSpec((1, 1, kdim, sb), col),
                  pl.BlockSpec((sb, vdim), lambda j, h: (j, v_off + h)),
                  pl.BlockSpec((sb, vdim), lambda j, h: (j, g_off + h)),
                  pl.BlockSpec((1, vdim), lambda j, h: (0, 0))],
        out_specs=(pl.BlockSpec((sb, 1, kdim, vdim), lambda j, h: (j, h, 0, 0)),
                   pl.BlockSpec((sb, vdim), lambda j, h: (j, h))),
        compiler_params=_params("parallel", "parallel"),
        name="vec_step",
    )(state, q_rows, k_cols, d_cols, vsrc, gsrc, norm_w)


def _ssd_step_kernel(s_ref, xt_ref, b_ref, c_ref, x_ref, dax_ref, dsk_ref, so_ref, y_ref):
    mm = _ThreePass
    xt = xt_ref[0, 0]
    bm = b_ref[...]
    c = c_ref[...]
    dax = dax_ref[...]
    hpg = SSD_HEADS // SSD_GROUPS
    row = lax.broadcasted_iota(jnp.int32, bm.shape, 0)
    new = []
    for b in range(STEP_B):
        outer = mm.nn(xt, jnp.where(row == b, bm, 0.0))
        per_head = []
        for hh in range(hpg):
            sn = (s_ref[b, hh] * dax[b:b + 1, hh * LANES:(hh + 1) * LANES]
                  + outer[hh * SSD_HEADDIM:(hh + 1) * SSD_HEADDIM])
            so_ref[b, hh] = sn
            per_head.append(sn)
        new.append(per_head)
    rows = [jnp.concatenate([mm.nt(c, new[b][hh])[b:b + 1] for hh in range(hpg)], axis=1)
            for b in range(STEP_B)]
    y_ref[...] = jnp.concatenate(rows, axis=0) + dsk_ref[...] * x_ref[...]


def _ssd_step(state_t, xdt_cols, xc, dax, dskip_x):
    bsz = state_t.shape[0]
    hpg = SSD_HEADS // SSD_GROUPS
    grp = lambda j, g: (j, g)
    b_off = SSD_INNER // SSD_STATE
    c_off = (SSD_INNER + SSD_BC) // SSD_STATE
    tile = pl.BlockSpec((STEP_B, hpg, SSD_HEADDIM, SSD_STATE), lambda j, g: (j, g, 0, 0))
    return pl.pallas_call(
        _ssd_step_kernel,
        out_shape=(jax.ShapeDtypeStruct(state_t.shape, F32),
                   jax.ShapeDtypeStruct((bsz, SSD_INNER), F32)),
        grid=(bsz // STEP_B, SSD_GROUPS),
        in_specs=[tile,
                  pl.BlockSpec((1, 1, SSD_GROUP_W, STEP_B), lambda j, g: (g, j, 0, 0)),
                  pl.BlockSpec((STEP_B, SSD_STATE), lambda j, g: (j, b_off + g)),
                  pl.BlockSpec((STEP_B, SSD_STATE), lambda j, g: (j, c_off + g)),
                  pl.BlockSpec((STEP_B, SSD_GROUP_W), grp),
                  pl.BlockSpec((STEP_B, hpg * LANES), grp),
                  pl.BlockSpec((1, SSD_GROUP_W), lambda j, g: (0, g))],
        out_specs=(tile, pl.BlockSpec((STEP_B, SSD_GROUP_W), grp)),
        compiler_params=_params("parallel", "parallel"),
        name="ssd_step",
    )(state_t, xdt_cols, xc, xc, xc, dax, dskip_x)


def _ssd_post_kernel(y_ref, z_ref, nw_ref, o_ref):
    o_ref[...] = _ssd_gate_norm(y_ref[...], z_ref[...], nw_ref[...])


def _ssd_post(y, proj, norm_w):
    bsz = y.shape[0]
    return pl.pallas_call(
        _ssd_post_kernel,
        out_shape=jax.ShapeDtypeStruct((bsz, SSD_INNER), F32),
        grid=(1,),
        in_specs=[pl.BlockSpec((bsz, SSD_INNER), lambda i: (0, 0)),
                  pl.BlockSpec((bsz, SSD_INNER), lambda i: (0, AB_Z // SSD_INNER)),
                  pl.BlockSpec((1, SSD_INNER), lambda i: (0, 0))],
        out_specs=pl.BlockSpec((bsz, SSD_INNER), lambda i: (0, 0)),
        compiler_params=_params("arbitrary"),
        name="ssd_post",
    )(y, proj, norm_w)


def _to_cols(a, nh, sb=STEP_B):
    bsz = a.shape[0]
    return a.reshape(bsz // sb, sb, nh, -1).transpose(2, 0, 3, 1)


def _prep_weights(w_in_ab, w_gk2, b_gk2, gla_norm_w, conv_w, conv_b, dt_bias, a_log, d_skip,
                  ssd_norm_w, w_out_ab, w_in_c, hg_norm_w, w_out_c, router_w, router_bias,
                  w_gate, w_up, w_down, ln1_w, ln1_b, ln2_w, ln2_b):
    offs = np.cumsum([0, GLA_KEY, GLA_KEY, GLA_VAL, GLA_VAL, GLA_RANK, SSD_INNER, SSD_CONV_DIM,
                      SSD_HEADS])
    sec = lambda w, i: w[:, offs[i]:offs[i + 1]]
    w = w_in_ab[0]
    pad = jnp.zeros((D_MODEL, LANES - SSD_HEADS - GLA_RANK), w.dtype)
    w_ab = jnp.concatenate([sec(w, 5), sec(w, 2), sec(w, 3), sec(w, 6), sec(w, 0), sec(w, 1),
                            sec(w, 7), sec(w, 4), pad], axis=1)
    hi_lo = lambda m: (m.astype(BF16), (m - m.astype(BF16).astype(F32)).astype(BF16))
    pad_e = lambda m: jnp.pad(m, ((0, LANES - N_EXPERTS), (0, 0)))
    router_pieces = jnp.concatenate([pad_e(piece) for piece in _split3(router_w.T)], axis=0)
    w2_wide = jnp.zeros((LANES, GLA_KEY), F32).at[SSD_HEADS:SSD_HEADS + GLA_RANK].set(w_gk2[0])
    lane_pad = lambda v: jnp.zeros((1, LANES), F32).at[0, :SSD_HEADS].set(v)
    expand = np.zeros((LANES, SSD_INNER), np.float32)
    for h in range(SSD_HEADS):
        expand[h, h * SSD_HEADDIM:(h + 1) * SSD_HEADDIM] = 1.0
    return dict(
        w_ab=hi_lo(w_ab),
        w2_wide=w2_wide,
        w2_heads=w2_wide.reshape(LANES, GLA_HEADS, GLA_DK).transpose(1, 0, 2),
        b2_wide=b_gk2[0].reshape(1, GLA_KEY),
        b2_heads=b_gk2[0].reshape(GLA_HEADS, 1, GLA_DK),
        gla_norm_w=gla_norm_w[0].reshape(1, GLA_DV),
        conv_w=conv_w[0], conv_b=conv_b[0].reshape(1, SSD_CONV_DIM),
        dtb_p=lane_pad(dt_bias[0]), alog_p=lane_pad(a_log[0]),
        dskip_x=jnp.repeat(d_skip[0], SSD_HEADDIM).reshape(1, SSD_INNER),
        ssd_norm_w=ssd_norm_w[0].reshape(1, SSD_INNER),
        expand=jnp.asarray(expand, BF16),
        prefix_sel=jnp.asarray(_prefix_selector(), BF16),
        w_out_gla=hi_lo(w_out_ab[0, :GLA_VAL]),
        w_out_ssd=hi_lo(w_out_ab[0, GLA_VAL:]),
        w_c=w_in_c[0].astype(BF16),
        hg_norm_w=hg_norm_w[0].reshape(1, HG_DI),
        w_out_c=w_out_c[0].astype(BF16),
        rwt=router_pieces,
        rbias=router_bias.reshape(N_EXPERTS, 1),
        w_gate=w_gate.reshape(DEPTH * N_EXPERTS, D_MODEL, D_FF_EXPERT),
        w_up=w_up.reshape(DEPTH * N_EXPERTS, D_MODEL, D_FF_EXPERT),
        w_down=w_down.reshape(DEPTH * N_EXPERTS, D_FF_EXPERT, D_MODEL),
        ln1_w=ln1_w.reshape(DEPTH, 1, D_MODEL), ln1_b=ln1_b.reshape(DEPTH, 1, D_MODEL),
        ln2_w=ln2_w.reshape(DEPTH, 1, D_MODEL), ln2_b=ln2_b.reshape(DEPTH, 1, D_MODEL),
    )


def _ffn(x, p, layer, tm, tm_moe):
    gates = _router(x, p['rwt'], p['rbias'], tm)
    return _moe_ln(x, gates, p['w_gate'], p['w_up'], p['w_down'], layer,
                   p['ln2_w'][layer], p['ln2_b'][layer], tm_moe)


def _ssd_state_from_wide(s_wide):
    bsz = s_wide.shape[0]
    return s_wide.reshape(bsz, SSD_STATE, SSD_HEADS, SSD_HEADDIM).transpose(0, 2, 1, 3)


def _trunk_prompt(x3, p, lower_bounds, tm, tn_ab, tn_c):
    bsz, seq, _ = x3.shape
    x = x3.reshape(bsz * seq, D_MODEL)
    tm_big = 2 * tm
    proj = _proj(x, p['w_ab'], tm_big, tn_ab)
    o_gla, s_gla = _gla_prompt(proj, p['w2_heads'], p['b2_heads'], p['gla_norm_w'],
                               p['prefix_sel'], bsz, seq)
    yz, s_ssd, s_conv = _ssd_prompt(proj, p['conv_w'], p['conv_b'], p['dtb_p'], p['alog_p'],
                                    p['dskip_x'], p['ssd_norm_w'], p['expand'], bsz, seq)
    x = _outproj_ln([o_gla, yz], [p['w_out_gla'], p['w_out_ssd']], x, p['ln1_w'][0], p['ln1_b'][0], tm)
    x, spare = _ffn_sorted(x, p, 0, tm, tm_big)
    proj_c = _proj(x, p['w_c'], tm_big, tn_c)
    o_hg, s_hg = _hgrn_prompt(proj_c, lower_bounds, p['hg_norm_w'], p['prefix_sel'], 1, bsz, seq)
    x = _outproj_ln([o_hg], [p['w_out_c']], x, p['ln1_w'][1], p['ln1_b'][1], tm)
    x, _ = _ffn_sorted(x, p, 1, tm, tm_big, spare)
    return (x.reshape(bsz, seq, D_MODEL), s_gla[None], _ssd_state_from_wide(s_ssd)[None],
            s_conv[None], s_hg[None])


def _trunk_sample(x3, st_gla, st_ssd, st_conv, st_hg, p, lower_bounds, tn_ab, tn_c):
    bsz = x3.shape[0]
    tm = bsz
    x = x3.reshape(bsz, D_MODEL)
    proj = _proj(x, p['w_ab'], tm, tn_ab)
    qs, dec, xc, xdt, dax, conv_new = _ab_prep(proj, st_conv[0].transpose(1, 0, 2), p['w2_wide'],
                                               p['b2_wide'], p['conv_w'], p['conv_b'], p['dtb_p'],
                                               p['alog_p'], p['expand'])
    conv_new = conv_new.transpose(1, 0, 2)
    k_gla = proj[:, AB_K:AB_K + GLA_KEY]
    s_gla, o_gla = _vec_step(st_gla[0], qs, _to_cols(k_gla, GLA_HEADS, VEC_STEP_B),
                             _to_cols(dec, GLA_HEADS, VEC_STEP_B), proj, AB_V // GLA_DV, proj,
                             AB_GOUT // GLA_DV, p['gla_norm_w'], _ThreePass)
    dax_wide = jnp.repeat(dax[:, ::SSD_HEADDIM], LANES, axis=1)
    s_ssd_t, y = _ssd_step(st_ssd[0].transpose(0, 1, 3, 2), _to_cols(xdt, SSD_GROUPS),
                           xc, dax_wide, p['dskip_x'])
    s_ssd = s_ssd_t.transpose(0, 1, 3, 2)
    yz = _ssd_post(y, proj, p['ssd_norm_w'])
    x = _outproj_ln([o_gla, yz], [p['w_out_gla'], p['w_out_ssd']], x, p['ln1_w'][0], p['ln1_b'][0], tm)
    x = _ffn(x, p, 0, tm, tm)
    proj_c = _proj(x, p['w_c'], tm, tn_c)
    qh, kh, dh = _hgrn_prep(proj_c, lower_bounds, 1)
    s_hg, o_hg = _vec_step(st_hg[0], qh, _to_cols(kh, HG_HEADS, VEC_STEP_B),
                           _to_cols(dh, HG_HEADS, VEC_STEP_B), proj_c, 2 * HG_HEADS, proj_c, 3 * HG_HEADS,
                           p['hg_norm_w'], _OnePass)
    x = _outproj_ln([o_hg], [p['w_out_c']], x, p['ln1_w'][1], p['ln1_b'][1], tm)
    x = _ffn(x, p, 1, tm, tm)
    return x.reshape(bsz, 1, D_MODEL), s_gla[None], s_ssd[None], conv_new[None], s_hg[None]


def kernel(x_prompt, x_sample, state_gla, state_ssd, state_conv, state_hgrn, w_in_ab, w_gk2, b_gk2, gla_norm_w, conv_w, conv_b, dt_bias, a_log, d_skip, ssd_norm_w, w_out_ab, w_in_c, lower_bounds, hg_norm_w, w_out_c, router_w, router_bias, w_gate, w_up, w_down, ln1_w, ln1_b, ln2_w, ln2_b):
    p = _prep_weights(w_in_ab, w_gk2, b_gk2, gla_norm_w, conv_w, conv_b, dt_bias, a_log, d_skip,
                      ssd_norm_w, w_out_ab, w_in_c, hg_norm_w, w_out_c, router_w, router_bias,
                      w_gate, w_up, w_down, ln1_w, ln1_b, ln2_w, ln2_b)
    y_p, gla_p, ssd_p, conv_p, hg_p = _trunk_prompt(x_prompt, p, lower_bounds, 512, 1152, 1024)
    y_s, gla_s, ssd_s, conv_s, hg_s = _trunk_sample(x_sample, state_gla, state_ssd, state_conv,
                                                    state_hgrn, p, lower_bounds, 1152, 1024)
    return (y_p, y_s, gla_p, ssd_p, conv_p, hg_p, gla_s, ssd_s, conv_s, hg_s)
```

```python
import functools

import numpy as np
import jax
import jax.numpy as jnp
from jax import lax
from jax.experimental import pallas as pl
from jax.experimental.pallas import tpu as pltpu

F32 = jnp.float32
BF16 = jnp.bfloat16

D_MODEL = 1024
DEPTH = 2
GLA_HEADS = 4
GLA_DK = 128
GLA_DV = 256
GLA_KEY = GLA_HEADS * GLA_DK
GLA_VAL = GLA_HEADS * GLA_DV
GLA_RANK = 16
GLA_NORMALIZER = 16.0
SSD_INNER = 1024
SSD_HEADDIM = 64
SSD_HEADS = 16
SSD_STATE = 128
SSD_GROUPS = 2
SSD_CONV = 4
SSD_GROUP_W = SSD_INNER // SSD_GROUPS
SSD_BC = SSD_GROUPS * SSD_STATE
SSD_CONV_DIM = SSD_INNER + 2 * SSD_BC
HG_EXPAND = 128
HG_HEADS = 8
HG_F = HG_HEADS * HG_EXPAND
HG_I = D_MODEL
HG_DI = HG_I // HG_HEADS
N_EXPERTS = 16
N_GROUPS = 4
EXPERTS_PER_GROUP = 4
D_FF_EXPERT = 512
ALPHA = (2 * DEPTH) ** 0.25
EPS = 1e-5

LANES = 128
VMEM_LIMIT = 48 * 1024 * 1024

AB_Z = 0
AB_V = 1024
AB_GOUT = 2048
AB_XBC = 3072
AB_Q = 4608
AB_K = 5120
AB_SMALL = 5632
AB_COLS = 5760
C_COLS = 4096

VEC_CHUNK = 64
VEC_SUB = 16
VEC_TILE = 256
VEC_ROWS = 512
VEC_HPS = 4
SSD_CHUNK = 128
ROW_TILE = 512
AB_COL_TILE = 1152
C_COL_TILE = 1024
STEP_B = 8
VEC_STEP_B = 16


def _params(*sem):
    return pltpu.CompilerParams(dimension_semantics=sem, vmem_limit_bytes=VMEM_LIMIT)


_NN = (((1,), (0,)), ((), ()))
_NT = (((1,), (1,)), ((), ()))
_TN = (((0,), (0,)), ((), ()))


def _dot1(dims, a, b):
    return lax.dot_general(a.astype(BF16), b.astype(BF16), dims, preferred_element_type=F32)


def _split2(a):
    hi = a.astype(BF16)
    return hi, (a - hi.astype(F32)).astype(BF16)


def _dot3(dims, a, b):
    ah, al = _split2(a)
    bh, bl = _split2(b)
    d = lambda x, y: lax.dot_general(x, y, dims, preferred_element_type=F32)
    return (d(al, bh) + d(ah, bl)) + d(ah, bh)


class _OnePass:
    nn = staticmethod(lambda a, b: _dot1(_NN, a, b))
    nt = staticmethod(lambda a, b: _dot1(_NT, a, b))
    tn = staticmethod(lambda a, b: _dot1(_TN, a, b))


class _ThreePass:
    nn = staticmethod(lambda a, b: _dot3(_NN, a, b))
    nt = staticmethod(lambda a, b: _dot3(_NT, a, b))
    tn = staticmethod(lambda a, b: _dot3(_TN, a, b))


def _split3(a):
    hi = a.astype(BF16)
    r1 = a - hi.astype(F32)
    mid = r1.astype(BF16)
    lo = (r1 - mid.astype(F32)).astype(BF16)
    return hi, mid, lo


def _dot_exact_rhs(sel, a):
    hi, mid, lo = _split3(a)
    d = lambda p: jnp.dot(sel, p, preferred_element_type=F32)
    return (d(lo) + d(mid)) + d(hi)


def _dot_exact_lhs(a, sel):
    hi, mid, lo = _split3(a)
    d = lambda p: jnp.dot(p, sel, preferred_element_type=F32)
    return (d(lo) + d(mid)) + d(hi)


def _tril(n):
    r = lax.broadcasted_iota(jnp.int32, (n, n), 0)
    c = lax.broadcasted_iota(jnp.int32, (n, n), 1)
    return r >= c


def _sigmoid(x):
    return 1.0 / (1.0 + jnp.exp(-x))


def _silu(x):
    return x * _sigmoid(x)


def _softplus(x):
    return jnp.maximum(x, 0.0) + jnp.log(1.0 + jnp.exp(-jnp.abs(x)))


def _log_sigmoid(x):
    return -_softplus(-x)


def _rms(x, w):
    return x * lax.rsqrt(jnp.mean(x * x, axis=-1, keepdims=True) + EPS) * w


def _layer_norm(x, w, b):
    mu = jnp.mean(x, axis=-1, keepdims=True)
    xc = x - mu
    var = jnp.mean(xc * xc, axis=-1, keepdims=True)
    return xc * lax.rsqrt(var + EPS) * w + b


def _proj_kernel(x_ref, w_ref, o_ref):
    o_ref[...] = jnp.dot(x_ref[...].astype(BF16), w_ref[...], preferred_element_type=F32)


def _proj3_kernel(x_ref, wh_ref, wl_ref, o_ref, xh_ref, xl_ref):
    @pl.when(pl.program_id(1) == 0)
    def _():
        hi, lo = _split2(x_ref[...])
        xh_ref[...] = hi
        xl_ref[...] = lo

    d = lambda a, b: jnp.dot(a, b, preferred_element_type=F32)
    xh = xh_ref[...]
    wh = wh_ref[...]
    o_ref[...] = (d(xl_ref[...], wh) + d(xh, wl_ref[...])) + d(xh, wh)


def _proj(x, w, tm, tn):
    t, k = x.shape
    three = isinstance(w, tuple)
    ws = w if three else (w,)
    n = ws[0].shape[1]
    return pl.pallas_call(
        _proj3_kernel if three else _proj_kernel,
        out_shape=jax.ShapeDtypeStruct((t, n), F32),
        grid=(t // tm, n // tn),
        in_specs=[pl.BlockSpec((tm, k), lambda i, j: (i, 0))]
                 + [pl.BlockSpec((k, tn), lambda i, j: (0, j)) for _ in ws],
        out_specs=pl.BlockSpec((tm, tn), lambda i, j: (i, j)),
        scratch_shapes=[pltpu.VMEM((tm, k), BF16), pltpu.VMEM((tm, k), BF16)] if three else [],
        compiler_params=_params("parallel", "arbitrary"),
        name="in_proj",
    )(x, *ws)


def _outproj_ln_kernel(n_in, three, *refs):
    a_refs = refs[:n_in]
    nw = 2 if three else 1
    w_refs = refs[n_in:n_in + nw * n_in]
    x_ref, lw_ref, lb_ref, o_ref = refs[n_in + nw * n_in:]
    d = lambda a, b: jnp.dot(a, b, preferred_element_type=F32)
    mix = None
    for i, a_ref in enumerate(a_refs):
        if three:
            ah, al = _split2(a_ref[...])
            wh = w_refs[2 * i][...]
            part = (d(al, wh) + d(ah, w_refs[2 * i + 1][...])) + d(ah, wh)
        else:
            part = d(a_ref[...].astype(BF16), w_refs[i][...])
        mix = part if mix is None else mix + part
    o_ref[...] = _layer_norm(ALPHA * x_ref[...] + mix, lw_ref[...], lb_ref[...])


def _outproj_ln(acts, ws, x, ln_w, ln_b, tm):
    t = x.shape[0]
    n_in = len(acts)
    three = isinstance(ws[0], tuple)
    flat_ws = [w for pair in ws for w in pair] if three else list(ws)
    row = lambda i: (i, 0)
    fixed = lambda i: (0, 0)
    in_specs = ([pl.BlockSpec((tm, a.shape[1]), row) for a in acts]
                + [pl.BlockSpec(w.shape, fixed) for w in flat_ws]
                + [pl.BlockSpec((tm, D_MODEL), row),
                   pl.BlockSpec((1, D_MODEL), fixed), pl.BlockSpec((1, D_MODEL), fixed)])
    return pl.pallas_call(
        functools.partial(_outproj_ln_kernel, n_in, three),
        out_shape=jax.ShapeDtypeStruct((t, D_MODEL), F32),
        grid=(t // tm,),
        in_specs=in_specs,
        out_specs=pl.BlockSpec((tm, D_MODEL), row),
        compiler_params=_params("parallel"),
        name="out_proj_ln",
    )(*acts, *flat_ws, x, ln_w, ln_b)


def _router_scores(x, rwt, bias):
    xh, xl = _split2(x)
    nt = lambda a, b: lax.dot_general(a, b, _NT, preferred_element_type=F32)
    a = nt(xh, rwt)
    b = nt(xl, rwt[:2 * LANES])
    by_token = ((a[:, 2 * LANES:] + b[:, LANES:]) + (a[:, LANES:2 * LANES] + b[:, :LANES])) + a[:, :LANES]
    scores = _sigmoid(by_token.T[:N_EXPERTS])
    return scores, scores + bias


def _best_group(sel):
    tm = sel.shape[1]
    s = [sel[e:e + 1, :] for e in range(N_EXPERTS)]
    grp = []
    for g in range(N_GROUPS):
        m = s[g * EXPERTS_PER_GROUP:(g + 1) * EXPERTS_PER_GROUP]
        best = None
        for i in range(EXPERTS_PER_GROUP):
            for j in range(i + 1, EXPERTS_PER_GROUP):
                p = m[i] + m[j]
                best = p if best is None else jnp.maximum(best, p)
        grp.append(best)
    best_g = jnp.zeros((1, tm), jnp.int32)
    best_v = grp[0]
    for g in range(1, N_GROUPS):
        upd = grp[g] > best_v
        best_g = jnp.where(upd, g, best_g)
        best_v = jnp.where(upd, grp[g], best_v)
    return best_g


def _top2(vals, weights):
    tm = vals[0].shape[1]
    neg = jnp.full((1, tm), -jnp.inf, F32)

    def first_argmax(rows):
        idx = jnp.zeros((1, tm), jnp.int32)
        top = rows[0]
        for e in range(1, len(rows)):
            upd = rows[e] > top
            idx = jnp.where(upd, e, idx)
            top = jnp.where(upd, rows[e], top)
        return idx

    idx1 = first_argmax(vals)
    idx2 = first_argmax([jnp.where(idx1 == e, neg, v) for e, v in enumerate(vals)])
    zero = jnp.zeros((1, tm), F32)
    w1 = zero
    w2 = zero
    for e, w in enumerate(weights):
        w1 = w1 + jnp.where(idx1 == e, w, zero)
        w2 = w2 + jnp.where(idx2 == e, w, zero)
    tot = w1 + w2
    g1 = w1 / tot
    g2 = w2 / tot
    return [jnp.where(idx1 == e, g1, zero) + jnp.where(idx2 == e, g2, zero)
            for e in range(len(vals))]


def _pad_rows(rows, tm):
    return jnp.concatenate(rows + [jnp.zeros((LANES - len(rows), tm), F32)], axis=0)


def _route_in_group(x, rwt, bias, group):
    tm = x.shape[0]
    scores, sel = _router_scores(x, rwt, bias)
    zero = jnp.zeros((1, tm), F32)
    vals, weights = [], []
    for m in range(EXPERTS_PER_GROUP):
        v = zero
        w = zero
        for g in range(N_GROUPS):
            e = g * EXPERTS_PER_GROUP + m
            v = jnp.where(group == g, sel[e:e + 1, :], v)
            w = jnp.where(group == g, scores[e:e + 1, :], w)
        vals.append(v)
        weights.append(w)
    return _pad_rows(_top2(vals, weights), tm)


def _route(x, rwt, bias):
    tm = x.shape[0]
    scores, sel = _router_scores(x, rwt, bias)
    s = [sel[e:e + 1, :] for e in range(N_EXPERTS)]
    sc = [scores[e:e + 1, :] for e in range(N_EXPERTS)]
    best_g = _best_group(sel)
    neg = jnp.full((1, tm), -jnp.inf, F32)
    ms = [jnp.where(best_g == e // EXPERTS_PER_GROUP, s[e], neg) for e in range(N_EXPERTS)]
    return _pad_rows(_top2(ms, sc), tm), best_g


def _router_kernel(x_ref, rwt_ref, bias_ref, g_ref):
    g_ref[...] = _route(x_ref[...], rwt_ref[...], bias_ref[...])[0].T


def _router(x, rwt, bias, tm):
    t = x.shape[0]
    return pl.pallas_call(
        _router_kernel,
        out_shape=jax.ShapeDtypeStruct((t, LANES), F32),
        grid=(t // tm,),
        in_specs=[pl.BlockSpec((tm, D_MODEL), lambda i: (i, 0)),
                  pl.BlockSpec((3 * LANES, D_MODEL), lambda i: (0, 0)),
                  pl.BlockSpec((N_EXPERTS, 1), lambda i: (0, 0))],
        out_specs=pl.BlockSpec((tm, LANES), lambda i: (i, 0)),
        compiler_params=_params("parallel"),
        name="router",
    )(x, rwt, bias)


def _group_router_kernel(x_ref, rwt_ref, bias_ref, g_ref, tot_ref, carry_ref):
    tm = x_ref.shape[0]

    @pl.when(pl.program_id(0) == 0)
    def _():
        carry_ref[...] = jnp.zeros_like(carry_ref)

    best_g = _best_group(_router_scores(x_ref[...], rwt_ref[...], bias_ref[...])[1])
    grow = lax.broadcasted_iota(jnp.int32, (8, tm), 0)
    onehot = jnp.where(grow == best_g, 1.0, 0.0)
    rr = lax.broadcasted_iota(jnp.int32, (tm, tm), 0)
    cc = lax.broadcasted_iota(jnp.int32, (tm, tm), 1)
    before = jnp.where(rr < cc, 1.0, 0.0).astype(BF16)
    earlier = jnp.dot(onehot.astype(BF16), before, preferred_element_type=F32)
    carry = carry_ref[...]
    earlier = earlier + jnp.concatenate([carry] * (tm // LANES), axis=1)
    rank = jnp.sum(onehot * earlier, axis=0, keepdims=True)
    row = lax.broadcasted_iota(jnp.int32, (LANES, tm), 0)
    packed = jnp.where(row == 0, best_g.astype(F32), jnp.where(row == 1, rank, 0.0))
    g_ref[...] = packed.T
    carry = carry + jnp.sum(onehot, axis=1, keepdims=True)
    carry_ref[...] = carry
    tot_ref[...] = carry


def _group_router(x, rwt, bias, tm):
    t = x.shape[0]
    return pl.pallas_call(
        _group_router_kernel,
        out_shape=(jax.ShapeDtypeStruct((t, LANES), F32), jax.ShapeDtypeStruct((8, LANES), F32)),
        grid=(t // tm,),
        in_specs=[pl.BlockSpec((tm, D_MODEL), lambda i: (i, 0)),
                  pl.BlockSpec((3 * LANES, D_MODEL), lambda i: (0, 0)),
                  pl.BlockSpec((N_EXPERTS, 1), lambda i: (0, 0))],
        out_specs=(pl.BlockSpec((tm, LANES), lambda i: (i, 0)),
                   pl.BlockSpec((8, LANES), lambda i: (0, 0))),
        scratch_shapes=[pltpu.VMEM((8, LANES), F32)],
        compiler_params=_params("arbitrary"),
        name="group_router",
    )(x, rwt, bias)


def _moe_kernel(x_ref, g_ref, wg_ref, wu_ref, wd_ref, lw_ref, lb_ref, o_ref, acc_ref, xb_ref):
    e = pl.program_id(1)

    @pl.when(e == 0)
    def _():
        xb_ref[...] = x_ref[...].astype(BF16)
        acc_ref[...] = jnp.zeros_like(acc_ref)

    xb = xb_ref[...]
    hg = jnp.dot(xb, wg_ref[0].astype(BF16), preferred_element_type=F32)
    hu = jnp.dot(xb, wu_ref[0].astype(BF16), preferred_element_type=F32)
    he = _silu(hg) * hu
    gates = g_ref[...]
    lane = lax.broadcasted_iota(jnp.int32, gates.shape, 1)
    ge = jnp.sum(jnp.where(lane == e, gates, 0.0), axis=1, keepdims=True)
    acc_ref[...] += ge * jnp.dot(he.astype(BF16), wd_ref[0].astype(BF16),
                                  preferred_element_type=F32)

    @pl.when(e == N_EXPERTS - 1)
    def _():
        o_ref[...] = _layer_norm(ALPHA * x_ref[...] + acc_ref[...], lw_ref[...], lb_ref[...])


def _moe_ln(x, gates, wg, wu, wd, layer, ln_w, ln_b, tm):
    t = x.shape[0]
    return pl.pallas_call(
        _moe_kernel,
        out_shape=jax.ShapeDtypeStruct((t, D_MODEL), F32),
        grid=(t // tm, N_EXPERTS),
        in_specs=[pl.BlockSpec((tm, D_MODEL), lambda i, e: (i, 0)),
                  pl.BlockSpec((tm, LANES), lambda i, e: (i, 0)),
                  pl.BlockSpec((1, D_MODEL, D_FF_EXPERT), lambda i, e: (layer * N_EXPERTS + e, 0, 0)),
                  pl.BlockSpec((1, D_MODEL, D_FF_EXPERT), lambda i, e: (layer * N_EXPERTS + e, 0, 0)),
                  pl.BlockSpec((1, D_FF_EXPERT, D_MODEL), lambda i, e: (layer * N_EXPERTS + e, 0, 0)),
                  pl.BlockSpec((1, D_MODEL), lambda i, e: (0, 0)),
                  pl.BlockSpec((1, D_MODEL), lambda i, e: (0, 0))],
        out_specs=pl.BlockSpec((tm, D_MODEL), lambda i, e: (i, 0)),
        scratch_shapes=[pltpu.VMEM((tm, D_MODEL), F32), pltpu.VMEM((tm, D_MODEL), BF16)],
        compiler_params=_params("parallel", "arbitrary"),
        name="moe_ln",
    )(x, gates, wg, wu, wd, ln_w, ln_b)


def _row_copy(src, dst, sem):
    return pltpu.make_async_copy(src, dst, sem)


def _scatter_rows_kernel(dest_ref, x_ref, init_ref, o_hbm, buf_ref, sem):
    del init_ref
    n = x_ref.shape[0]
    base = pl.program_id(0) * n
    buf_ref[...] = x_ref[...].reshape(buf_ref.shape)

    def start(i, carry):
        _row_copy(buf_ref.at[i], o_hbm.at[dest_ref[base + i]], sem).start()
        return carry

    lax.fori_loop(0, n, start, 0)
    _row_copy(buf_ref, o_hbm.at[pl.ds(0, n)], sem).wait()


def _scatter_rows(x, dest, init, tr):
    t = x.shape[0]
    slabs = D_MODEL // LANES
    return pl.pallas_call(
        _scatter_rows_kernel,
        out_shape=jax.ShapeDtypeStruct(init.shape, F32),
        grid_spec=pltpu.PrefetchScalarGridSpec(
            num_scalar_prefetch=1,
            grid=(t // tr,),
            in_specs=[pl.BlockSpec((tr, D_MODEL), lambda i, d: (i, 0)),
                      pl.BlockSpec(memory_space=pl.ANY)],
            out_specs=pl.BlockSpec(memory_space=pl.ANY),
            scratch_shapes=[pltpu.VMEM((tr, slabs, LANES), F32), pltpu.SemaphoreType.DMA(())]),
        input_output_aliases={2: 0},
        compiler_params=_params("arbitrary"),
        name="scatter_rows",
    )(dest, x, init)


def _gather_rows_kernel(src_ref, y_hbm, o_ref, buf_ref, sems):
    n = o_ref.shape[0]
    i = pl.program_id(0)
    slot = i % 2

    def issue(step, to_slot):
        def start(r, carry):
            _row_copy(y_hbm.at[src_ref[step * n + r]], buf_ref.at[to_slot, r],
                      sems.at[to_slot]).start()
            return carry
        lax.fori_loop(0, n, start, 0)

    @pl.when(i == 0)
    def _():
        issue(0, 0)

    @pl.when(i + 1 < pl.num_programs(0))
    def _():
        issue(i + 1, 1 - slot)

    _row_copy(y_hbm.at[pl.ds(0, n)], buf_ref.at[slot], sems.at[slot]).wait()
    o_ref[...] = buf_ref[slot].reshape(o_ref.shape)


def _gather_rows(y3, src, tr):
    t = src.shape[0]
    slabs = D_MODEL // LANES
    return pl.pallas_call(
        _gather_rows_kernel,
        out_shape=jax.ShapeDtypeStruct((t, D_MODEL), F32),
        grid_spec=pltpu.PrefetchScalarGridSpec(
            num_scalar_prefetch=1,
            grid=(t // tr,),
            in_specs=[pl.BlockSpec(memory_space=pl.ANY)],
            out_specs=pl.BlockSpec((tr, D_MODEL), lambda i, d: (i, 0)),
            scratch_shapes=[pltpu.VMEM((2, tr, slabs, LANES), F32),
                            pltpu.SemaphoreType.DMA((2,))]),
        compiler_params=_params("arbitrary"),
        name="gather_rows",
    )(src, y3)


def _moe_group_kernel(tg_ref, x3_ref, rwt_ref, rb_ref, wg_ref, wu_ref, wd_ref, lw_ref, lb_ref,
                      o3_ref, acc_ref, x_ref, xb_ref, gate_ref):
    i = pl.program_id(0)
    j = pl.program_id(1)
    group = tg_ref[i]
    slabs = D_MODEL // LANES

    @pl.when(group < 0)
    def _():
        o3_ref[...] = jnp.zeros_like(o3_ref)

    @pl.when(group >= 0)
    def _():
        @pl.when(j == 0)
        def _():
            x = x3_ref[...].reshape(x_ref.shape)
            x_ref[...] = x
            xb_ref[...] = x.astype(BF16)
            acc_ref[...] = jnp.zeros_like(acc_ref)
            gate_ref[...] = _route_in_group(x, rwt_ref[...], rb_ref[...], group).T

        xb = xb_ref[...]
        hg = jnp.dot(xb, wg_ref[0].astype(BF16), preferred_element_type=F32)
        hu = jnp.dot(xb, wu_ref[0].astype(BF16), preferred_element_type=F32)
        he = _silu(hg) * hu
        gates = gate_ref[...]
        lane = lax.broadcasted_iota(jnp.int32, gates.shape, 1)
        ge = jnp.sum(jnp.where(lane == j, gates, 0.0), axis=1, keepdims=True)
        acc_ref[...] += ge * jnp.dot(he.astype(BF16), wd_ref[0].astype(BF16),
                                     preferred_element_type=F32)

        @pl.when(j == EXPERTS_PER_GROUP - 1)
        def _():
            y = _layer_norm(ALPHA * x_ref[...] + acc_ref[...], lw_ref[...], lb_ref[...])
            o3_ref[...] = y.reshape(y.shape[0], slabs, LANES)


def _moe_group_ln(xs3, tile_group, rwt, rbias, wg, wu, wd, layer, ln_w, ln_b, tm):
    n = xs3.shape[0]
    slabs = D_MODEL // LANES
    expert = lambda i, j, tg: (layer * N_EXPERTS + jnp.maximum(tg[i], 0) * EXPERTS_PER_GROUP + j, 0, 0)
    fixed = lambda i, j, tg: (0, 0)
    return pl.pallas_call(
        _moe_group_kernel,
        out_shape=jax.ShapeDtypeStruct((n, slabs, LANES), F32),
        grid_spec=pltpu.PrefetchScalarGridSpec(
            num_scalar_prefetch=1,
            grid=(n // tm, EXPERTS_PER_GROUP),
            in_specs=[pl.BlockSpec((tm, slabs, LANES), lambda i, j, tg: (i, 0, 0)),
                      pl.BlockSpec((3 * LANES, D_MODEL), fixed),
                      pl.BlockSpec((N_EXPERTS, 1), fixed),
                      pl.BlockSpec((1, D_MODEL, D_FF_EXPERT), expert),
                      pl.BlockSpec((1, D_MODEL, D_FF_EXPERT), expert),
                      pl.BlockSpec((1, D_FF_EXPERT, D_MODEL), expert),
                      pl.BlockSpec((1, D_MODEL), fixed),
                      pl.BlockSpec((1, D_MODEL), fixed)],
            out_specs=pl.BlockSpec((tm, slabs, LANES), lambda i, j, tg: (i, 0, 0)),
            scratch_shapes=[pltpu.VMEM((tm, D_MODEL), F32), pltpu.VMEM((tm, D_MODEL), F32),
                            pltpu.VMEM((tm, D_MODEL), BF16), pltpu.VMEM((tm, LANES), F32)]),
        compiler_params=_params("arbitrary", "arbitrary"),
        name="moe_group_ln",
    )(tile_group, xs3, rwt, rbias, wg, wu, wd, ln_w, ln_b)


def _ffn_sorted(x, p, layer, tm, tm_moe, spare=None):
    t = x.shape[0]
    n_tiles = t // tm_moe + N_GROUPS
    gmat, totals = _group_router(x, p['rwt'], p['rbias'], tm)
    group = gmat[:, 0].astype(jnp.int32)
    counts = totals[:N_GROUPS, 0].astype(jnp.int32)
    seg_tiles = (counts + tm_moe - 1) // tm_moe
    seg_end = jnp.cumsum(seg_tiles)
    seg_start = seg_end - seg_tiles
    is_group = group[:, None] == jnp.arange(N_GROUPS, dtype=jnp.int32)[None, :]
    dest = (jnp.sum(jnp.where(is_group, seg_start[None, :], 0), axis=1) * tm_moe
            + gmat[:, 1].astype(jnp.int32))
    tile_id = jnp.arange(n_tiles, dtype=jnp.int32)
    tile_group = jnp.sum((tile_id[:, None] >= seg_end[None, :]).astype(jnp.int32), axis=1)
    tile_group = jnp.where(tile_id < seg_end[N_GROUPS - 1], tile_group, -1)
    if spare is None:
        spare = jnp.zeros((n_tiles * tm_moe, D_MODEL // LANES, LANES), F32)
    xs3 = _scatter_rows(x, dest, spare, tm)
    ys3 = _moe_group_ln(xs3, tile_group, p['rwt'], p['rbias'], p['w_gate'], p['w_up'], p['w_down'],
                        layer, p['ln2_w'][layer], p['ln2_b'][layer], tm_moe)
    return _gather_rows(ys3, dest, tm), ys3


def _prefix_selector():
    n = VEC_TILE
    nsub = VEC_CHUNK // VEC_SUB
    t = np.arange(n)[:, None]
    s = np.arange(n)[None, :]
    incl = ((t // VEC_CHUNK) == (s // VEC_CHUNK)) & ((s % VEC_CHUNK) <= (t % VEC_CHUNK))
    r = np.arange((n // VEC_CHUNK) * nsub)[:, None]
    starts = ((r // nsub) == (s // VEC_CHUNK)) & ((s % VEC_CHUNK) < VEC_SUB * (r % nsub))
    return np.concatenate([incl, starts], axis=0).astype(np.float32)


def _vec_heads(heads, sel, mm):
    n = VEC_TILE
    nsub = VEC_CHUNK // VEC_SUB
    nchunk = n // VEC_CHUNK
    nrows = heads[0][0].shape[0]
    kdim = heads[0][0].shape[1]
    streams = [(h, i) for h in range(len(heads)) for i in range(0, nrows, n)]
    tile = lambda h, i, which: heads[h][which][i:i + n]

    prefs = [_dot_exact_rhs(sel, tile(h, i, 3)) for h, i in streams]
    rows_of = lambda fn, m: jnp.concatenate(
        [jnp.broadcast_to(fn(j), (m, kdim)) for j in range(n // m)], axis=0)
    sub = (lax.broadcasted_iota(jnp.int32, (n, kdim), 0) // VEC_SUB) % nsub
    q_cat, k_cat, q_dec0, updates = [], [], [], []
    for (h, i), pref in zip(streams, prefs):
        q, k, v = tile(h, i, 0), tile(h, i, 1), tile(h, i, 2)
        big_g = pref[0:n]
        start = lambda c, j, pref=pref: pref[n + c * nsub + j:n + c * nsub + j + 1]
        q_dec = [q * jnp.exp(big_g)]
        for j in range(1, nsub):
            base_j = rows_of(lambda c: start(c, j), VEC_CHUNK)
            q_dec.append(q * jnp.exp(jnp.minimum(big_g - base_j, 0.0)))
        base_own = rows_of(lambda m: start(m // nsub, m % nsub), VEC_SUB)
        k_rel = k * jnp.exp(base_own - big_g)
        k_cat.append(jnp.concatenate([jnp.where(sub == j, k_rel, 0.0) for j in range(nsub)],
                                     axis=1))
        q_cat.append(jnp.concatenate(q_dec, axis=1))
        q_dec0.append(q_dec[0])
        per_chunk = []
        for c in range(nchunk):
            rows = slice(c * VEC_CHUNK, (c + 1) * VEC_CHUNK)
            g_last = big_g[(c + 1) * VEC_CHUNK - 1:(c + 1) * VEC_CHUNK, :]
            kd = k[rows] * jnp.exp(g_last - big_g[rows])
            per_chunk.append((jnp.exp(g_last), mm.tn(v[rows], kd)))
        updates.append(per_chunk)
    scores = [mm.nt(qc, kc) for qc, kc in zip(q_cat, k_cat)]
    rr = lax.broadcasted_iota(jnp.int32, (n, n), 0)
    cc = lax.broadcasted_iota(jnp.int32, (n, n), 1)
    keep = (rr >= cc) & ((rr // VEC_CHUNK) == (cc // VEC_CHUNK))
    intra = [mm.nn(jnp.where(keep, sc, 0.0), tile(h, i, 2)) for (h, i), sc in zip(streams, scores)]

    states = [hd[4] for hd in heads]
    o_rows = [[] for _ in heads]
    for si, (h, i) in enumerate(streams):
        for c, (decay_last, update) in enumerate(updates[si]):
            rows = slice(c * VEC_CHUNK, (c + 1) * VEC_CHUNK)
            o_rows[h].append(intra[si][rows] + mm.nt(q_dec0[si][rows], states[h]))
            states[h] = states[h] * decay_last + update
    return [(jnp.concatenate(o_rows[h], axis=0), states[h]) for h in range(len(heads))]


def _gla_chunk_kernel(q_ref, k_ref, v_ref, go_ref, sm_ref, w2_ref, b2_ref, nw_ref, sel_ref,
                      o_ref, s_ref, st_ref):
    r = pl.program_id(2)

    @pl.when(r == 0)
    def _():
        st_ref[...] = jnp.zeros_like(st_ref)

    sm = sm_ref[...]
    heads = []
    for hh in range(VEC_HPS):
        kc = slice(hh * GLA_DK, (hh + 1) * GLA_DK)
        vc = slice(hh * GLA_DV, (hh + 1) * GLA_DV)
        gk = _log_sigmoid(_ThreePass.nn(sm, w2_ref[hh]) + b2_ref[hh]) / GLA_NORMALIZER
        heads.append((q_ref[:, kc] * (GLA_DK ** -0.5), k_ref[:, kc], v_ref[:, vc], gk, st_ref[hh]))
    finals = []
    for hh, (o, st) in enumerate(_vec_heads(heads, sel_ref[...], _ThreePass)):
        vc = slice(hh * GLA_DV, (hh + 1) * GLA_DV)
        o_ref[:, vc] = _rms(o, nw_ref[...]) * _silu(go_ref[:, vc])
        st_ref[hh] = st
        finals.append(st)

    @pl.when(r == pl.num_programs(2) - 1)
    def _():
        for hh in range(VEC_HPS):
            s_ref[0, hh] = finals[hh].T


def _gla_prompt(proj, w2p, b2, norm_w, sel, bsz, seq):
    nr = seq // VEC_ROWS
    ng = GLA_HEADS // VEC_HPS
    kw = VEC_HPS * GLA_DK
    vw = VEC_HPS * GLA_DV
    row = lambda off: (lambda b, h, r: (b * nr + r, off + h))
    return pl.pallas_call(
        _gla_chunk_kernel,
        out_shape=(jax.ShapeDtypeStruct((bsz * seq, GLA_VAL), F32),
                   jax.ShapeDtypeStruct((bsz, GLA_HEADS, GLA_DK, GLA_DV), F32)),
        grid=(bsz, ng, nr),
        in_specs=[pl.BlockSpec((VEC_ROWS, kw), row(AB_Q // kw)),
                  pl.BlockSpec((VEC_ROWS, kw), row(AB_K // kw)),
                  pl.BlockSpec((VEC_ROWS, vw), row(AB_V // vw)),
                  pl.BlockSpec((VEC_ROWS, vw), row(AB_GOUT // vw)),
                  pl.BlockSpec((VEC_ROWS, LANES), lambda b, h, r: (b * nr + r, AB_SMALL // LANES)),
                  pl.BlockSpec((VEC_HPS, LANES, GLA_DK), lambda b, h, r: (h, 0, 0)),
                  pl.BlockSpec((VEC_HPS, 1, GLA_DK), lambda b, h, r: (h, 0, 0)),
                  pl.BlockSpec((1, GLA_DV), lambda b, h, r: (0, 0)),
                  pl.BlockSpec(sel.shape, lambda b, h, r: (0, 0))],
        out_specs=(pl.BlockSpec((VEC_ROWS, vw), lambda b, h, r: (b * nr + r, h)),
                   pl.BlockSpec((1, VEC_HPS, GLA_DK, GLA_DV), lambda b, h, r: (b, h, 0, 0))),
        scratch_shapes=[pltpu.VMEM((VEC_HPS, GLA_DV, GLA_DK), F32)],
        compiler_params=_params("parallel", "parallel", "arbitrary"),
        name="gla_chunk",
    )(proj, proj, proj, proj, proj, w2p, b2, norm_w, sel)


def _hgrn_lower_bound(lbraw, layer):
    m = jnp.max(lbraw, axis=0, keepdims=True)
    ex = jnp.exp(lbraw - m)
    sm = ex / jnp.sum(ex, axis=0, keepdims=True)
    acc = sm[0:1]
    for i in range(1, layer + 1):
        acc = acc + sm[i:i + 1]
    return acc - sm[0:1]


def _hgrn_gates(q_raw, f_raw, lb):
    forget = lb + (1.0 - lb) * _sigmoid(f_raw)
    return _silu(q_raw), 1.0 - forget, jnp.log(forget)


def _hgrn_chunk_kernel(layer, q_ref, f_ref, i_ref, go_ref, lb_ref, nw_ref, sel_ref,
                       o_ref, s_ref, st_ref):
    r = pl.program_id(2)

    @pl.when(r == 0)
    def _():
        st_ref[...] = jnp.zeros_like(st_ref)

    lb_all = _hgrn_lower_bound(lb_ref[...], layer)
    heads = []
    for hh in range(VEC_HPS):
        kc = slice(hh * HG_EXPAND, (hh + 1) * HG_EXPAND)
        vc = slice(hh * HG_DI, (hh + 1) * HG_DI)
        q, k, g = _hgrn_gates(q_ref[:, kc], f_ref[:, kc], lb_all[:, kc])
        heads.append((q, k, i_ref[:, vc], g, st_ref[hh]))
    finals = []
    for hh, (o, st) in enumerate(_vec_heads(heads, sel_ref[...], _OnePass)):
        vc = slice(hh * HG_DI, (hh + 1) * HG_DI)
        o_ref[:, vc] = _rms(o, nw_ref[...]) * _silu(go_ref[:, vc])
        st_ref[hh] = st
        finals.append(st)

    @pl.when(r == pl.num_programs(2) - 1)
    def _():
        for hh in range(VEC_HPS):
            s_ref[0, hh] = finals[hh].T


def _hgrn_prompt(proj, lower_bounds, norm_w, sel, layer, bsz, seq):
    nr = seq // VEC_ROWS
    ng = HG_HEADS // VEC_HPS
    kw = VEC_HPS * HG_EXPAND
    vw = VEC_HPS * HG_DI
    row = lambda off: (lambda b, h, r: (b * nr + r, off + h))
    return pl.pallas_call(
        functools.partial(_hgrn_chunk_kernel, layer),
        out_shape=(jax.ShapeDtypeStruct((bsz * seq, HG_I), F32),
                   jax.ShapeDtypeStruct((bsz, HG_HEADS, HG_EXPAND, HG_DI), F32)),
        grid=(bsz, ng, nr),
        in_specs=[pl.BlockSpec((VEC_ROWS, kw), row(0)),
                  pl.BlockSpec((VEC_ROWS, kw), row(ng)),
                  pl.BlockSpec((VEC_ROWS, vw), row(2 * ng)),
                  pl.BlockSpec((VEC_ROWS, vw), row(3 * ng)),
                  pl.BlockSpec((DEPTH, kw), lambda b, h, r: (0, h)),
                  pl.BlockSpec((1, HG_DI), lambda b, h, r: (0, 0)),
                  pl.BlockSpec(sel.shape, lambda b, h, r: (0, 0))],
        out_specs=(pl.BlockSpec((VEC_ROWS, vw), lambda b, h, r: (b * nr + r, h)),
                   pl.BlockSpec((1, VEC_HPS, HG_EXPAND, HG_DI), lambda b, h, r: (b, h, 0, 0))),
        scratch_shapes=[pltpu.VMEM((VEC_HPS, HG_DI, HG_EXPAND), F32)],
        compiler_params=_params("parallel", "parallel", "arbitrary"),
        name="hgrn_chunk",
    )(proj, proj, proj, proj, lower_bounds, norm_w, sel)


def _ssd_gate_norm(y, z, nw):
    yz = y * _silu(z)
    parts = []
    for g in range(SSD_GROUPS):
        cols = slice(g * SSD_GROUP_W, (g + 1) * SSD_GROUP_W)
        parts.append(_rms(yz[:, cols], nw[:, cols]))
    return jnp.concatenate(parts, axis=1)


def _ssd_chunk_kernel(z_ref, xbc_ref, sm_ref, cw_ref, cb_ref, dtb_ref, alog_ref, dsk_ref,
                      nw_ref, ex_ref, o_ref, s_ref, conv_ref, st_ref, prev_ref):
    r = pl.program_id(1)
    c = SSD_CHUNK
    mm = _ThreePass

    @pl.when(r == 0)
    def _():
        st_ref[...] = jnp.zeros_like(st_ref)
        prev_ref[...] = jnp.zeros_like(prev_ref)

    x_raw = xbc_ref[...]
    prev_ref[8:8 + c, :] = x_raw
    cw = cw_ref[...]
    acc = cb_ref[...] + cw[SSD_CONV - 1:SSD_CONV] * x_raw
    for m in range(1, SSD_CONV):
        acc = acc + cw[SSD_CONV - 1 - m:SSD_CONV - m] * prev_ref[8 - m:8 - m + c, :]
    xc = _silu(acc)
    prev_ref[0:8, :] = x_raw[c - 8:c]
    xs = xc[:, :SSD_INNER]
    bm = xc[:, SSD_INNER:SSD_INNER + SSD_BC]
    cm = xc[:, SSD_INNER + SSD_BC:]

    dt = _softplus(sm_ref[...] + dtb_ref[...])
    a_neg = -jnp.exp(alog_ref[...])
    big_g = _dot_exact_rhs(_tril(c).astype(BF16), dt * a_neg)
    g_t = big_g.T
    g_last = big_g[c - 1:c, :]
    ex = ex_ref[...]
    dt_x = _dot_exact_lhs(dt, ex)
    eg_x = _dot_exact_lhs(jnp.exp(big_g), ex)
    w_x = _dot_exact_lhs(dt * jnp.exp(g_last - big_g), ex)
    xdt = xs * dt_x
    xw = xs * w_x
    causal = _tril(c)
    lane = lax.broadcasted_iota(jnp.int32, (c, LANES), 1)
    st = st_ref[...]
    y_parts = []
    u_parts = []
    for g in range(SSD_GROUPS):
        gcols = slice(g * SSD_GROUP_W, (g + 1) * SSD_GROUP_W)
        bg = bm[:, g * SSD_STATE:(g + 1) * SSD_STATE]
        cg = cm[:, g * SSD_STATE:(g + 1) * SSD_STATE]
        sc = mm.nt(cg, bg)
        inter = mm.nn(cg, st[:, gcols])
        u_parts.append(mm.tn(bg, xw[:, gcols]))
        pair_cols = []
        heads_per_group = SSD_HEADS // SSD_GROUPS
        for p in range(heads_per_group // 2):
            h0 = g * heads_per_group + 2 * p
            xpair = xdt[:, h0 * SSD_HEADDIM:(h0 + 2) * SSD_HEADDIM]
            ws = []
            for h in (h0, h0 + 1):
                diff = big_g[:, h:h + 1] - g_t[h:h + 1, :]
                ws.append(sc * jnp.exp(jnp.where(causal, diff, -jnp.inf)))
            x_diag = jnp.concatenate([jnp.where(lane < SSD_HEADDIM, xpair, 0.0),
                                      jnp.where(lane < SSD_HEADDIM, 0.0, xpair)], axis=0)
            pair_cols.append(mm.nn(jnp.concatenate(ws, axis=1), x_diag))
        y_intra = jnp.concatenate(pair_cols, axis=1)
        y_parts.append(y_intra + inter * eg_x[:, gcols])
    y = jnp.concatenate(y_parts, axis=1) + dsk_ref[...] * xs
    o_ref[...] = _ssd_gate_norm(y, z_ref[...], nw_ref[...])
    st = st * eg_x[c - 1:c, :] + jnp.concatenate(u_parts, axis=1)
    st_ref[...] = st

    @pl.when(r == pl.num_programs(1) - 1)
    def _():
        s_ref[0] = st
        conv_ref[0] = x_raw[c - (SSD_CONV - 1):c]


def _ssd_prompt(proj, conv_w, conv_b, dtb_p, alog_p, dskip_x, norm_w, expand, bsz, seq):
    nr = seq // SSD_CHUNK
    fixed = lambda b, r: (0, 0)
    return pl.pallas_call(
        _ssd_chunk_kernel,
        out_shape=(jax.ShapeDtypeStruct((bsz * seq, SSD_INNER), F32),
                   jax.ShapeDtypeStruct((bsz, SSD_STATE, SSD_INNER), F32),
                   jax.ShapeDtypeStruct((bsz, SSD_CONV - 1, SSD_CONV_DIM), F32)),
        grid=(bsz, nr),
        in_specs=[pl.BlockSpec((SSD_CHUNK, SSD_INNER), lambda b, r: (b * nr + r, AB_Z // SSD_INNER)),
                  pl.BlockSpec((SSD_CHUNK, SSD_CONV_DIM), lambda b, r: (b * nr + r, AB_XBC // SSD_CONV_DIM)),
                  pl.BlockSpec((SSD_CHUNK, LANES), lambda b, r: (b * nr + r, AB_SMALL // LANES)),
                  pl.BlockSpec((SSD_CONV, SSD_CONV_DIM), fixed),
                  pl.BlockSpec((1, SSD_CONV_DIM), fixed),
                  pl.BlockSpec((1, LANES), fixed),
                  pl.BlockSpec((1, LANES), fixed),
                  pl.BlockSpec((1, SSD_INNER), fixed),
                  pl.BlockSpec((1, SSD_INNER), fixed),
                  pl.BlockSpec((LANES, SSD_INNER), fixed)],
        out_specs=(pl.BlockSpec((SSD_CHUNK, SSD_INNER), lambda b, r: (b * nr + r, 0)),
                   pl.BlockSpec((1, SSD_STATE, SSD_INNER), lambda b, r: (b, 0, 0)),
                   pl.BlockSpec((1, SSD_CONV - 1, SSD_CONV_DIM), lambda b, r: (b, 0, 0))),
        scratch_shapes=[pltpu.VMEM((SSD_STATE, SSD_INNER), F32),
                        pltpu.VMEM((8 + SSD_CHUNK, SSD_CONV_DIM), F32)],
        compiler_params=_params("parallel", "arbitrary"),
        name="ssd_chunk",
    )(proj, proj, proj, conv_w, conv_b, dtb_p, alog_p, dskip_x, norm_w, expand)


def _ab_prep_kernel(q_ref, sm_ref, xbc_ref, cs_ref, w2_ref, b2_ref, cw_ref, cb_ref, dtb_ref,
                    alog_ref, ex_ref, qs_ref, dec_ref, xc_ref, xdt_ref, dax_ref, cs_out_ref):
    sm = sm_ref[...]
    gk = _log_sigmoid(_ThreePass.nn(sm, w2_ref[...]) + b2_ref[...]) / GLA_NORMALIZER
    qs_ref[...] = q_ref[...] * (GLA_DK ** -0.5)
    dec_ref[...] = jnp.exp(gk)
    cw = cw_ref[...]
    x_raw = xbc_ref[...]
    acc = cb_ref[...] + cw[SSD_CONV - 1:SSD_CONV] * x_raw
    for j in range(SSD_CONV - 1):
        acc = acc + cw[j:j + 1] * cs_ref[j]
    xc = _silu(acc)
    xc_ref[...] = xc
    for j in range(SSD_CONV - 2):
        cs_out_ref[j] = cs_ref[j + 1]
    cs_out_ref[SSD_CONV - 2] = x_raw
    dt = _softplus(sm + dtb_ref[...])
    ex = ex_ref[...]
    xdt_ref[...] = xc[:, :SSD_INNER] * _dot_exact_lhs(dt, ex)
    dax_ref[...] = _dot_exact_lhs(jnp.exp(dt * -jnp.exp(alog_ref[...])), ex)


def _ab_prep(proj, conv_state, w2_wide, b2_wide, conv_w, conv_b, dtb_p, alog_p, expand):
    bsz = proj.shape[0]
    fixed = lambda i: (0, 0)
    sds = jax.ShapeDtypeStruct
    return pl.pallas_call(
        _ab_prep_kernel,
        out_shape=(sds((bsz, GLA_KEY), F32), sds((bsz, GLA_KEY), F32),
                   sds((bsz, SSD_CONV_DIM), F32), sds((bsz, SSD_INNER), F32),
                   sds((bsz, SSD_INNER), F32),
                   sds((SSD_CONV - 1, bsz, SSD_CONV_DIM), F32)),
        grid=(1,),
        in_specs=[pl.BlockSpec((bsz, GLA_KEY), lambda i: (0, AB_Q // GLA_KEY)),
                  pl.BlockSpec((bsz, LANES), lambda i: (0, AB_SMALL // LANES)),
                  pl.BlockSpec((bsz, SSD_CONV_DIM), lambda i: (0, AB_XBC // SSD_CONV_DIM)),
                  pl.BlockSpec((SSD_CONV - 1, bsz, SSD_CONV_DIM), lambda i: (0, 0, 0)),
                  pl.BlockSpec((LANES, GLA_KEY), fixed),
                  pl.BlockSpec((1, GLA_KEY), fixed),
                  pl.BlockSpec((SSD_CONV, SSD_CONV_DIM), fixed),
                  pl.BlockSpec((1, SSD_CONV_DIM), fixed),
                  pl.BlockSpec((1, LANES), fixed),
                  pl.BlockSpec((1, LANES), fixed),
                  pl.BlockSpec((LANES, SSD_INNER), fixed)],
        out_specs=(pl.BlockSpec((bsz, GLA_KEY), fixed), pl.BlockSpec((bsz, GLA_KEY), fixed),
                   pl.BlockSpec((bsz, SSD_CONV_DIM), fixed), pl.BlockSpec((bsz, SSD_INNER), fixed),
                   pl.BlockSpec((bsz, SSD_INNER), fixed),
                   pl.BlockSpec((SSD_CONV - 1, bsz, SSD_CONV_DIM), lambda i: (0, 0, 0))),
        compiler_params=_params("arbitrary"),
        name="ab_prep",
    )(proj, proj, proj, conv_state, w2_wide, b2_wide, conv_w, conv_b, dtb_p, alog_p, expand)


def _hgrn_prep_kernel(layer, q_ref, f_ref, lb_ref, qs_ref, k_ref, dec_ref):
    lb = _hgrn_lower_bound(lb_ref[...], layer)
    forget = lb + (1.0 - lb) * _sigmoid(f_ref[...])
    qs_ref[...] = _silu(q_ref[...])
    k_ref[...] = 1.0 - forget
    dec_ref[...] = jnp.exp(jnp.log(forget))


def _hgrn_prep(proj, lower_bounds, layer):
    bsz = proj.shape[0]
    blk = lambda j: pl.BlockSpec((bsz, HG_F), lambda i: (0, j))
    return pl.pallas_call(
        functools.partial(_hgrn_prep_kernel, layer),
        out_shape=tuple(jax.ShapeDtypeStruct((bsz, HG_F), F32) for _ in range(3)),
        grid=(1,),
        in_specs=[blk(0), blk(1), pl.BlockSpec((DEPTH, HG_F), lambda i: (0, 0))],
        out_specs=tuple(blk(0) for _ in range(3)),
        compiler_params=_params("arbitrary"),
        name="hgrn_prep",
    )(proj, proj, lower_bounds)


def _vec_step_kernel(mm, s_ref, q_ref, k_ref, d_ref, v_ref, go_ref, nw_ref, so_ref, o_ref):
    q = q_ref[...]
    kt = k_ref[0, 0]
    dt = d_ref[0, 0]
    v = v_ref[...]
    sb = v.shape[0]
    row = lax.broadcasted_iota(jnp.int32, v.shape, 0)
    new = []
    for b in range(sb):
        only_b = row == b
        decay = _dot_exact_lhs(dt, jnp.where(only_b, 1.0, 0.0).astype(BF16))
        new.append(s_ref[b, 0] * decay + mm.nn(kt, jnp.where(only_b, v, 0.0)))
    for b in range(sb):
        so_ref[b, 0] = new[b]
    o = jnp.concatenate([mm.nn(q, new[b])[b:b + 1] for b in range(sb)], axis=0)
    o_ref[...] = _rms(o, nw_ref[...]) * _silu(go_ref[...])


def _vec_step(state, q_rows, k_cols, d_cols, vsrc, v_off, gsrc, g_off, norm_w, mm):
    bsz, nh, kdim, vdim = state.shape
    sb = k_cols.shape[3]
    col = lambda j, h: (h, j, 0, 0)
    return pl.pallas_call(
        functools.partial(_vec_step_kernel, mm),
        out_shape=(jax.ShapeDtypeStruct(state.shape, F32),
                   jax.ShapeDtypeStruct((bsz, nh * vdim), F32)),
        grid=(bsz // sb, nh),
        in_specs=[pl.BlockSpec((sb, 1, kdim, vdim), lambda j, h: (j, h, 0, 0)),
                  pl.BlockSpec((sb, kdim), lambda j, h: (j, h)),
                  pl.BlockSpec((1, 1, kdim, sb), col),
                  pl.BlockSpec((1, 1, kdim, sb), col),
                  pl.BlockSpec((sb, vdim), lambda j, h: (j, v_off + h)),
                  pl.BlockSpec((sb, vdim), lambda j, h: (j, g_off + h)),
                  pl.BlockSpec((1, vdim), lambda j, h: (0, 0))],
        out_specs=(pl.BlockSpec((sb, 1, kdim, vdim), lambda j, h: (j, h, 0, 0)),
                   pl.BlockSpec((sb, vdim), lambda j, h: (j, h))),
        compiler_params=_params("parallel", "parallel"),
        name="vec_step",
    )(state, q_rows, k_cols, d_cols, vsrc, gsrc, norm_w)


def _ssd_step_kernel(s_ref, xt_ref, b_ref, c_ref, x_ref, dax_ref, dsk_ref, so_ref, y_ref):
    mm = _ThreePass
    xt = xt_ref[0, 0]
    bm = b_ref[...]
    c = c_ref[...]
    dax = dax_ref[...]
    hpg = SSD_HEADS // SSD_GROUPS
    row = lax.broadcasted_iota(jnp.int32, bm.shape, 0)
    new = []
    for b in range(STEP_B):
        outer = mm.nn(xt, jnp.where(row == b, bm, 0.0))
        per_head = []
        for hh in range(hpg):
            sn = (s_ref[b, hh] * dax[b:b + 1, hh * LANES:(hh + 1) * LANES]
                  + outer[hh * SSD_HEADDIM:(hh + 1) * SSD_HEADDIM])
            so_ref[b, hh] = sn
            per_head.append(sn)
        new.append(per_head)
    rows = [jnp.concatenate([mm.nt(c, new[b][hh])[b:b + 1] for hh in range(hpg)], axis=1)
            for b in range(STEP_B)]
    y_ref[...] = jnp.concatenate(rows, axis=0) + dsk_ref[...] * x_ref[...]


def _ssd_step(state_t, xdt_cols, xc, dax, dskip_x):
    bsz = state_t.shape[0]
    hpg = SSD_HEADS // SSD_GROUPS
    grp = lambda j, g: (j, g)
    b_off = SSD_INNER // SSD_STATE
    c_off = (SSD_INNER + SSD_BC) // SSD_STATE
    tile = pl.BlockSpec((STEP_B, hpg, SSD_HEADDIM, SSD_STATE), lambda j, g: (j, g, 0, 0))
    return pl.pallas_call(
        _ssd_step_kernel,
        out_shape=(jax.ShapeDtypeStruct(state_t.shape, F32),
                   jax.ShapeDtypeStruct((bsz, SSD_INNER), F32)),
        grid=(bsz // STEP_B, SSD_GROUPS),
        in_specs=[tile,
                  pl.BlockSpec((1, 1, SSD_GROUP_W, STEP_B), lambda j, g: (g, j, 0, 0)),
                  pl.BlockSpec((STEP_B, SSD_STATE), lambda j, g: (j, b_off + g)),
                  pl.BlockSpec((STEP_B, SSD_STATE), lambda j, g: (j, c_off + g)),
                  pl.BlockSpec((STEP_B, SSD_GROUP_W), grp),
                  pl.BlockSpec((STEP_B, hpg * LANES), grp),
                  pl.BlockSpec((1, SSD_GROUP_W), lambda j, g: (0, g))],
        out_specs=(tile, pl.BlockSpec((STEP_B, SSD_GROUP_W), grp)),
        compiler_params=_params("parallel", "parallel"),
        name="ssd_step",
    )(state_t, xdt_cols, xc, xc, xc, dax, dskip_x)


def _ssd_post_kernel(y_ref, z_ref, nw_ref, o_ref):
    o_ref[...] = _ssd_gate_norm(y_ref[...], z_ref[...], nw_ref[...])


def _ssd_post(y, proj, norm_w):
    bsz = y.shape[0]
    return pl.pallas_call(
        _ssd_post_kernel,
        out_shape=jax.ShapeDtypeStruct((bsz, SSD_INNER), F32),
        grid=(1,),
        in_specs=[pl.BlockSpec((bsz, SSD_INNER), lambda i: (0, 0)),
                  pl.BlockSpec((bsz, SSD_INNER), lambda i: (0, AB_Z // SSD_INNER)),
                  pl.BlockSpec((1, SSD_INNER), lambda i: (0, 0))],
        out_specs=pl.BlockSpec((bsz, SSD_INNER), lambda i: (0, 0)),
        compiler_params=_params("arbitrary"),
        name="ssd_post",
    )(y, proj, norm_w)


def _to_cols(a, nh, sb=STEP_B):
    bsz = a.shape[0]
    return a.reshape(bsz // sb, sb, nh, -1).transpose(2, 0, 3, 1)


def _prep_weights(w_in_ab, w_gk2, b_gk2, gla_norm_w, conv_w, conv_b, dt_bias, a_log, d_skip,
                  ssd_norm_w, w_out_ab, w_in_c, hg_norm_w, w_out_c, router_w, router_bias,
                  w_gate, w_up, w_down, ln1_w, ln1_b, ln2_w, ln2_b):
    offs = np.cumsum([0, GLA_KEY, GLA_KEY, GLA_VAL, GLA_VAL, GLA_RANK, SSD_INNER, SSD_CONV_DIM,
                      SSD_HEADS])
    sec = lambda w, i: w[:, offs[i]:offs[i + 1]]
    w = w_in_ab[0]
    pad = jnp.zeros((D_MODEL, LANES - SSD_HEADS - GLA_RANK), w.dtype)
    w_ab = jnp.concatenate([sec(w, 5), sec(w, 2), sec(w, 3), sec(w, 6), sec(w, 0), sec(w, 1),
                            sec(w, 7), sec(w, 4), pad], axis=1)
    hi_lo = lambda m: (m.astype(BF16), (m - m.astype(BF16).astype(F32)).astype(BF16))
    pad_e = lambda m: jnp.pad(m, ((0, LANES - N_EXPERTS), (0, 0)))
    router_pieces = jnp.concatenate([pad_e(piece) for piece in _split3(router_w.T)], axis=0)
    w2_wide = jnp.zeros((LANES, GLA_KEY), F32).at[SSD_HEADS:SSD_HEADS + GLA_RANK].set(w_gk2[0])
    lane_pad = lambda v: jnp.zeros((1, LANES), F32).at[0, :SSD_HEADS].set(v)
    expand = np.zeros((LANES, SSD_INNER), np.float32)
    for h in range(SSD_HEADS):
        expand[h, h * SSD_HEADDIM:(h + 1) * SSD_HEADDIM] = 1.0
    return dict(
        w_ab=hi_lo(w_ab),
        w2_wide=w2_wide,
        w2_heads=w2_wide.reshape(LANES, GLA_HEADS, GLA_DK).transpose(1, 0, 2),
        b2_wide=b_gk2[0].reshape(1, GLA_KEY),
        b2_heads=b_gk2[0].reshape(GLA_HEADS, 1, GLA_DK),
        gla_norm_w=gla_norm_w[0].reshape(1, GLA_DV),
        conv_w=conv_w[0], conv_b=conv_b[0].reshape(1, SSD_CONV_DIM),
        dtb_p=lane_pad(dt_bias[0]), alog_p=lane_pad(a_log[0]),
        dskip_x=jnp.repeat(d_skip[0], SSD_HEADDIM).reshape(1, SSD_INNER),
        ssd_norm_w=ssd_norm_w[0].reshape(1, SSD_INNER),
        expand=jnp.asarray(expand, BF16),
        prefix_sel=jnp.asarray(_prefix_selector(), BF16),
        w_out_gla=hi_lo(w_out_ab[0, :GLA_VAL]),
        w_out_ssd=hi_lo(w_out_ab[0, GLA_VAL:]),
        w_c=w_in_c[0].astype(BF16),
        hg_norm_w=hg_norm_w[0].reshape(1, HG_DI),
        w_out_c=w_out_c[0].astype(BF16),
        rwt=router_pieces,
        rbias=router_bias.reshape(N_EXPERTS, 1),
        w_gate=w_gate.reshape(DEPTH * N_EXPERTS, D_MODEL, D_FF_EXPERT),
        w_up=w_up.reshape(DEPTH * N_EXPERTS, D_MODEL, D_FF_EXPERT),
        w_down=w_down.reshape(DEPTH * N_EXPERTS, D_FF_EXPERT, D_MODEL),
        ln1_w=ln1_w.reshape(DEPTH, 1, D_MODEL), ln1_b=ln1_b.reshape(DEPTH, 1, D_MODEL),
        ln2_w=ln2_w.reshape(DEPTH, 1, D_MODEL), ln2_b=ln2_b.reshape(DEPTH, 1, D_MODEL),
    )


def _ffn(x, p, layer, tm, tm_moe):
    gates = _router(x, p['rwt'], p['rbias'], tm)
    return _moe_ln(x, gates, p['w_gate'], p['w_up'], p['w_down'], layer,
                   p['ln2_w'][layer], p['ln2_b'][layer], tm_moe)


def _ssd_state_from_wide(s_wide):
    bsz = s_wide.shape[0]
    return s_wide.reshape(bsz, SSD_STATE, SSD_HEADS, SSD_HEADDIM).transpose(0, 2, 1, 3)


def _trunk_prompt(x3, p, lower_bounds, tm, tn_ab, tn_c):
    bsz, seq, _ = x3.shape
    x = x3.reshape(bsz * seq, D_MODEL)
    tm_big = 2 * tm
    proj = _proj(x, p['w_ab'], tm_big, tn_ab)
    o_gla, s_gla = _gla_prompt(proj, p['w2_heads'], p['b2_heads'], p['gla_norm_w'],
                               p['prefix_sel'], bsz, seq)
    yz, s_ssd, s_conv = _ssd_prompt(proj, p['conv_w'], p['conv_b'], p['dtb_p'], p['alog_p'],
                                    p['dskip_x'], p['ssd_norm_w'], p['expand'], bsz, seq)
    x = _outproj_ln([o_gla, yz], [p['w_out_gla'], p['w_out_ssd']], x, p['ln1_w'][0], p['ln1_b'][0], tm)
    x, spare = _ffn_sorted(x, p, 0, tm, tm_big)
    proj_c = _proj(x, p['w_c'], tm_big, tn_c)
    o_hg, s_hg = _hgrn_prompt(proj_c, lower_bounds, p['hg_norm_w'], p['prefix_sel'], 1, bsz, seq)
    x = _outproj_ln([o_hg], [p['w_out_c']], x, p['ln1_w'][1], p['ln1_b'][1], tm)
    x, _ = _ffn_sorted(x, p, 1, tm, tm_big, spare)
    return (x.reshape(bsz, seq, D_MODEL), s_gla[None], _ssd_state_from_wide(s_ssd)[None],
            s_conv[None], s_hg[None])


def _trunk_sample(x3, st_gla, st_ssd, st_conv, st_hg, p, lower_bounds, tn_ab, tn_c):
    bsz = x3.shape[0]
    tm = bsz
    x = x3.reshape(bsz, D_MODEL)
    proj = _proj(x, p['w_ab'], tm, tn_ab)
    qs, dec, xc, xdt, dax, conv_new = _ab_prep(proj, st_conv[0].transpose(1, 0, 2), p['w2_wide'],
                                               p['b2_wide'], p['conv_w'], p['conv_b'], p['dtb_p'],
                                               p['alog_p'], p['expand'])
    conv_new = conv_new.transpose(1, 0, 2)
    k_gla = proj[:, AB_K:AB_K + GLA_KEY]
    s_gla, o_gla = _vec_step(st_gla[0], qs, _to_cols(k_gla, GLA_HEADS, VEC_STEP_B),
                             _to_cols(dec, GLA_HEADS, VEC_STEP_B), proj, AB_V // GLA_DV, proj,
                             AB_GOUT // GLA_DV, p['gla_norm_w'], _ThreePass)
    dax_wide = jnp.repeat(dax[:, ::SSD_HEADDIM], LANES, axis=1)
    s_ssd_t, y = _ssd_step(st_ssd[0].transpose(0, 1, 3, 2), _to_cols(xdt, SSD_GROUPS),
                           xc, dax_wide, p['dskip_x'])
    s_ssd = s_ssd_t.transpose(0, 1, 3, 2)
    yz = _ssd_post(y, proj, p['ssd_norm_w'])
    x = _outproj_ln([o_gla, yz], [p['w_out_gla'], p['w_out_ssd']], x, p['ln1_w'][0], p['ln1_b'][0], tm)
    x = _ffn(x, p, 0, tm, tm)
    proj_c = _proj(x, p['w_c'], tm, tn_c)
    qh, kh, dh = _hgrn_prep(proj_c, lower_bounds, 1)
    s_hg, o_hg = _vec_step(st_hg[0], qh, _to_cols(kh, HG_HEADS, VEC_STEP_B),
                           _to_cols(dh, HG_HEADS, VEC_STEP_B), proj_c, 2 * HG_HEADS, proj_c, 3 * HG_HEADS,
                           p['hg_norm_w'], _OnePass)
    x = _outproj_ln([o_hg], [p['w_out_c']], x, p['ln1_w'][1], p['ln1_b'][1], tm)
    x = _ffn(x, p, 1, tm, tm)
    return x.reshape(bsz, 1, D_MODEL), s_gla[None], s_ssd[None], conv_new[None], s_hg[None]


def kernel(x_prompt, x_sample, state_gla, state_ssd, state_conv, state_hgrn, w_in_ab, w_gk2, b_gk2, gla_norm_w, conv_w, conv_b, dt_bias, a_log, d_skip, ssd_norm_w, w_out_ab, w_in_c, lower_bounds, hg_norm_w, w_out_c, router_w, router_bias, w_gate, w_up, w_down, ln1_w, ln1_b, ln2_w, ln2_b):
    p = _prep_weights(w_in_ab, w_gk2, b_gk2, gla_norm_w, conv_w, conv_b, dt_bias, a_log, d_skip,
                      ssd_norm_w, w_out_ab, w_in_c, hg_norm_w, w_out_c, router_w, router_bias,
                      w_gate, w_up, w_down, ln1_w, ln1_b, ln2_w, ln2_b)
    y_p, gla_p, ssd_p, conv_p, hg_p = _trunk_prompt(x_prompt, p, lower_bounds, ROW_TILE,
                                                    AB_COL_TILE, C_COL_TILE)
    y_s, gla_s, ssd_s, conv_s, hg_s = _trunk_sample(x_sample, state_gla, state_ssd, state_conv,
                                                    state_hgrn, p, lower_bounds, AB_COL_TILE,
                                                    C_COL_TILE)
    return (y_p, y_s, gla_p, ssd_p, conv_p, hg_p, gla_s, ssd_s, conv_s, hg_s)
```

```python
import functools

import numpy as np
import jax
import jax.numpy as jnp
from jax import lax
from jax.experimental import pallas as pl
from jax.experimental.pallas import tpu as pltpu

F32 = jnp.float32
BF16 = jnp.bfloat16

D_MODEL = 1024
DEPTH = 2
GLA_HEADS = 4
GLA_DK = 128
GLA_DV = 256
GLA_KEY = GLA_HEADS * GLA_DK
GLA_VAL = GLA_HEADS * GLA_DV
GLA_RANK = 16
GLA_NORMALIZER = 16.0
SSD_INNER = 1024
SSD_HEADDIM = 64
SSD_HEADS = 16
SSD_STATE = 128
SSD_GROUPS = 2
SSD_CONV = 4
SSD_GROUP_W = SSD_INNER // SSD_GROUPS
SSD_BC = SSD_GROUPS * SSD_STATE
SSD_CONV_DIM = SSD_INNER + 2 * SSD_BC
HG_EXPAND = 128
HG_HEADS = 8
HG_F = HG_HEADS * HG_EXPAND
HG_I = D_MODEL
HG_DI = HG_I // HG_HEADS
N_EXPERTS = 16
N_GROUPS = 4
EXPERTS_PER_GROUP = 4
D_FF_EXPERT = 512
ALPHA = (2 * DEPTH) ** 0.25
EPS = 1e-5

LANES = 128
VMEM_LIMIT = 48 * 1024 * 1024

AB_Z = 0
AB_V = 1024
AB_GOUT = 2048
AB_XBC = 3072
AB_Q = 4608
AB_K = 5120
AB_SMALL = 5632
AB_COLS = 5760
C_COLS = 4096

VEC_CHUNK = 64
VEC_SUB = 16
VEC_TILE = 256
VEC_ROWS = 512
VEC_HPS = 4
SSD_CHUNK = 128
ROW_TILE = 512
AB_COL_TILE = 1152
C_COL_TILE = 1024
STEP_B = 8
VEC_STEP_B = 16


def _params(*sem):
    return pltpu.CompilerParams(dimension_semantics=sem, vmem_limit_bytes=VMEM_LIMIT)


_NN = (((1,), (0,)), ((), ()))
_NT = (((1,), (1,)), ((), ()))
_TN = (((0,), (0,)), ((), ()))


def _dot1(dims, a, b):
    return lax.dot_general(a.astype(BF16), b.astype(BF16), dims, preferred_element_type=F32)


def _split2(a):
    hi = a.astype(BF16)
    return hi, (a - hi.astype(F32)).astype(BF16)


def _dot3(dims, a, b):
    ah, al = _split2(a)
    bh, bl = _split2(b)
    d = lambda x, y: lax.dot_general(x, y, dims, preferred_element_type=F32)
    return (d(al, bh) + d(ah, bl)) + d(ah, bh)


class _OnePass:
    nn = staticmethod(lambda a, b: _dot1(_NN, a, b))
    nt = staticmethod(lambda a, b: _dot1(_NT, a, b))
    tn = staticmethod(lambda a, b: _dot1(_TN, a, b))


class _ThreePass:
    nn = staticmethod(lambda a, b: _dot3(_NN, a, b))
    nt = staticmethod(lambda a, b: _dot3(_NT, a, b))
    tn = staticmethod(lambda a, b: _dot3(_TN, a, b))


def _split3(a):
    hi = a.astype(BF16)
    r1 = a - hi.astype(F32)
    mid = r1.astype(BF16)
    lo = (r1 - mid.astype(F32)).astype(BF16)
    return hi, mid, lo


def _dot_exact_rhs(sel, a):
    hi, mid, lo = _split3(a)
    d = lambda p: jnp.dot(sel, p, preferred_element_type=F32)
    return (d(lo) + d(mid)) + d(hi)


def _dot_exact_lhs(a, sel):
    hi, mid, lo = _split3(a)
    d = lambda p: jnp.dot(p, sel, preferred_element_type=F32)
    return (d(lo) + d(mid)) + d(hi)


def _tril(n):
    r = lax.broadcasted_iota(jnp.int32, (n, n), 0)
    c = lax.broadcasted_iota(jnp.int32, (n, n), 1)
    return r >= c


def _sigmoid(x):
    return 1.0 / (1.0 + jnp.exp(-x))


def _silu(x):
    return x * _sigmoid(x)


def _softplus(x):
    return jnp.maximum(x, 0.0) + jnp.log(1.0 + jnp.exp(-jnp.abs(x)))


def _log_sigmoid(x):
    return -_softplus(-x)


def _rms(x, w):
    return x * lax.rsqrt(jnp.mean(x * x, axis=-1, keepdims=True) + EPS) * w


def _layer_norm(x, w, b):
    mu = jnp.mean(x, axis=-1, keepdims=True)
    xc = x - mu
    var = jnp.mean(xc * xc, axis=-1, keepdims=True)
    return xc * lax.rsqrt(var + EPS) * w + b


def _proj_kernel(x_ref, w_ref, o_ref):
    o_ref[...] = jnp.dot(x_ref[...].astype(BF16), w_ref[...], preferred_element_type=F32)


def _proj3_kernel(x_ref, wh_ref, wl_ref, o_ref, xh_ref, xl_ref):
    @pl.when(pl.program_id(1) == 0)
    def _():
        hi, lo = _split2(x_ref[...])
        xh_ref[...] = hi
        xl_ref[...] = lo

    d = lambda a, b: jnp.dot(a, b, preferred_element_type=F32)
    xh = xh_ref[...]
    wh = wh_ref[...]
    o_ref[...] = (d(xl_ref[...], wh) + d(xh, wl_ref[...])) + d(xh, wh)


def _proj(x, w, tm, tn):
    t, k = x.shape
    three = isinstance(w, tuple)
    ws = w if three else (w,)
    n = ws[0].shape[1]
    return pl.pallas_call(
        _proj3_kernel if three else _proj_kernel,
        out_shape=jax.ShapeDtypeStruct((t, n), F32),
        grid=(t // tm, n // tn),
        in_specs=[pl.BlockSpec((tm, k), lambda i, j: (i, 0))]
                 + [pl.BlockSpec((k, tn), lambda i, j: (0, j)) for _ in ws],
        out_specs=pl.BlockSpec((tm, tn), lambda i, j: (i, j)),
        scratch_shapes=[pltpu.VMEM((tm, k), BF16), pltpu.VMEM((tm, k), BF16)] if three else [],
        compiler_params=_params("parallel", "arbitrary"),
        name="in_proj",
    )(x, *ws)


def _outproj_ln_kernel(n_in, three, *refs):
    a_refs = refs[:n_in]
    nw = 2 if three else 1
    w_refs = refs[n_in:n_in + nw * n_in]
    x_ref, lw_ref, lb_ref, o_ref = refs[n_in + nw * n_in:]
    d = lambda a, b: jnp.dot(a, b, preferred_element_type=F32)
    mix = None
    for i, a_ref in enumerate(a_refs):
        if three:
            ah, al = _split2(a_ref[...])
            wh = w_refs[2 * i][...]
            part = (d(al, wh) + d(ah, w_refs[2 * i + 1][...])) + d(ah, wh)
        else:
            part = d(a_ref[...].astype(BF16), w_refs[i][...])
        mix = part if mix is None else mix + part
    o_ref[...] = _layer_norm(ALPHA * x_ref[...] + mix, lw_ref[...], lb_ref[...])


def _outproj_ln(acts, ws, x, ln_w, ln_b, tm):
    t = x.shape[0]
    n_in = len(acts)
    three = isinstance(ws[0], tuple)
    flat_ws = [w for pair in ws for w in pair] if three else list(ws)
    row = lambda i: (i, 0)
    fixed = lambda i: (0, 0)
    in_specs = ([pl.BlockSpec((tm, a.shape[1]), row) for a in acts]
                + [pl.BlockSpec(w.shape, fixed) for w in flat_ws]
                + [pl.BlockSpec((tm, D_MODEL), row),
                   pl.BlockSpec((1, D_MODEL), fixed), pl.BlockSpec((1, D_MODEL), fixed)])
    return pl.pallas_call(
        functools.partial(_outproj_ln_kernel, n_in, three),
        out_shape=jax.ShapeDtypeStruct((t, D_MODEL), F32),
        grid=(t // tm,),
        in_specs=in_specs,
        out_specs=pl.BlockSpec((tm, D_MODEL), row),
        compiler_params=_params("parallel"),
        name="out_proj_ln",
    )(*acts, *flat_ws, x, ln_w, ln_b)


def _router_scores(x, rwt, bias):
    xh, xl = _split2(x)
    nt = lambda a, b: lax.dot_general(a, b, _NT, preferred_element_type=F32)
    a = nt(xh, rwt)
    b = nt(xl, rwt[:2 * LANES])
    by_token = ((a[:, 2 * LANES:] + b[:, LANES:]) + (a[:, LANES:2 * LANES] + b[:, :LANES])) + a[:, :LANES]
    scores = _sigmoid(by_token.T[:N_EXPERTS])
    return scores, scores + bias


def _best_group(sel):
    tm = sel.shape[1]
    s = [sel[e:e + 1, :] for e in range(N_EXPERTS)]
    grp = []
    for g in range(N_GROUPS):
        m = s[g * EXPERTS_PER_GROUP:(g + 1) * EXPERTS_PER_GROUP]
        best = None
        for i in range(EXPERTS_PER_GROUP):
            for j in range(i + 1, EXPERTS_PER_GROUP):
                p = m[i] + m[j]
                best = p if best is None else jnp.maximum(best, p)
        grp.append(best)
    best_g = jnp.zeros((1, tm), jnp.int32)
    best_v = grp[0]
    for g in range(1, N_GROUPS):
        upd = grp[g] > best_v
        best_g = jnp.where(upd, g, best_g)
        best_v = jnp.where(upd, grp[g], best_v)
    return best_g


def _top2(vals, weights):
    tm = vals[0].shape[1]
    neg = jnp.full((1, tm), -jnp.inf, F32)

    def first_argmax(rows):
        idx = jnp.zeros((1, tm), jnp.int32)
        top = rows[0]
        for e in range(1, len(rows)):
            upd = rows[e] > top
            idx = jnp.where(upd, e, idx)
            top = jnp.where(upd, rows[e], top)
        return idx

    idx1 = first_argmax(vals)
    idx2 = first_argmax([jnp.where(idx1 == e, neg, v) for e, v in enumerate(vals)])
    zero = jnp.zeros((1, tm), F32)
    w1 = zero
    w2 = zero
    for e, w in enumerate(weights):
        w1 = w1 + jnp.where(idx1 == e, w, zero)
        w2 = w2 + jnp.where(idx2 == e, w, zero)
    tot = w1 + w2
    g1 = w1 / tot
    g2 = w2 / tot
    return [jnp.where(idx1 == e, g1, zero) + jnp.where(idx2 == e, g2, zero)
            for e in range(len(vals))]


def _pad_rows(rows, tm):
    return jnp.concatenate(rows + [jnp.zeros((LANES - len(rows), tm), F32)], axis=0)


def _route_in_group(x, rwt, bias, group):
    tm = x.shape[0]
    scores, sel = _router_scores(x, rwt, bias)
    zero = jnp.zeros((1, tm), F32)
    vals, weights = [], []
    for m in range(EXPERTS_PER_GROUP):
        v = zero
        w = zero
        for g in range(N_GROUPS):
            e = g * EXPERTS_PER_GROUP + m
            v = jnp.where(group == g, sel[e:e + 1, :], v)
            w = jnp.where(group == g, scores[e:e + 1, :], w)
        vals.append(v)
        weights.append(w)
    return _pad_rows(_top2(vals, weights), tm)


def _route(x, rwt, bias):
    tm = x.shape[0]
    scores, sel = _router_scores(x, rwt, bias)
    s = [sel[e:e + 1, :] for e in range(N_EXPERTS)]
    sc = [scores[e:e + 1, :] for e in range(N_EXPERTS)]
    best_g = _best_group(sel)
    neg = jnp.full((1, tm), -jnp.inf, F32)
    ms = [jnp.where(best_g == e // EXPERTS_PER_GROUP, s[e], neg) for e in range(N_EXPERTS)]
    return _pad_rows(_top2(ms, sc), tm), best_g


def _router_kernel(x_ref, rwt_ref, bias_ref, g_ref):
    g_ref[...] = _route(x_ref[...], rwt_ref[...], bias_ref[...])[0].T


def _router(x, rwt, bias, tm):
    t = x.shape[0]
    return pl.pallas_call(
        _router_kernel,
        out_shape=jax.ShapeDtypeStruct((t, LANES), F32),
        grid=(t // tm,),
        in_specs=[pl.BlockSpec((tm, D_MODEL), lambda i: (i, 0)),
                  pl.BlockSpec((3 * LANES, D_MODEL), lambda i: (0, 0)),
                  pl.BlockSpec((N_EXPERTS, 1), lambda i: (0, 0))],
        out_specs=pl.BlockSpec((tm, LANES), lambda i: (i, 0)),
        compiler_params=_params("parallel"),
        name="router",
    )(x, rwt, bias)


def _group_router_kernel(x_ref, rwt_ref, bias_ref, g_ref, tot_ref, carry_ref):
    tm = x_ref.shape[0]

    @pl.when(pl.program_id(0) == 0)
    def _():
        carry_ref[...] = jnp.zeros_like(carry_ref)

    best_g = _best_group(_router_scores(x_ref[...], rwt_ref[...], bias_ref[...])[1])
    grow = lax.broadcasted_iota(jnp.int32, (8, tm), 0)
    onehot = jnp.where(grow == best_g, 1.0, 0.0)
    rr = lax.broadcasted_iota(jnp.int32, (tm, tm), 0)
    cc = lax.broadcasted_iota(jnp.int32, (tm, tm), 1)
    before = jnp.where(rr < cc, 1.0, 0.0).astype(BF16)
    earlier = jnp.dot(onehot.astype(BF16), before, preferred_element_type=F32)
    carry = carry_ref[...]
    earlier = earlier + jnp.concatenate([carry] * (tm // LANES), axis=1)
    rank = jnp.sum(onehot * earlier, axis=0, keepdims=True)
    row = lax.broadcasted_iota(jnp.int32, (LANES, tm), 0)
    packed = jnp.where(row == 0, best_g.astype(F32), jnp.where(row == 1, rank, 0.0))
    g_ref[...] = packed.T
    carry = carry + jnp.sum(onehot, axis=1, keepdims=True)
    carry_ref[...] = carry
    tot_ref[...] = carry


def _group_router(x, rwt, bias, tm):
    t = x.shape[0]
    return pl.pallas_call(
        _group_router_kernel,
        out_shape=(jax.ShapeDtypeStruct((t, LANES), F32), jax.ShapeDtypeStruct((8, LANES), F32)),
        grid=(t // tm,),
        in_specs=[pl.BlockSpec((tm, D_MODEL), lambda i: (i, 0)),
                  pl.BlockSpec((3 * LANES, D_MODEL), lambda i: (0, 0)),
                  pl.BlockSpec((N_EXPERTS, 1), lambda i: (0, 0))],
        out_specs=(pl.BlockSpec((tm, LANES), lambda i: (i, 0)),
                   pl.BlockSpec((8, LANES), lambda i: (0, 0))),
        scratch_shapes=[pltpu.VMEM((8, LANES), F32)],
        compiler_params=_params("arbitrary"),
        name="group_router",
    )(x, rwt, bias)


def _moe_kernel(x_ref, g_ref, wg_ref, wu_ref, wd_ref, lw_ref, lb_ref, o_ref, acc_ref, xb_ref):
    e = pl.program_id(1)

    @pl.when(e == 0)
    def _():
        xb_ref[...] = x_ref[...].astype(BF16)
        acc_ref[...] = jnp.zeros_like(acc_ref)

    xb = xb_ref[...]
    hg = jnp.dot(xb, wg_ref[0].astype(BF16), preferred_element_type=F32)
    hu = jnp.dot(xb, wu_ref[0].astype(BF16), preferred_element_type=F32)
    he = _silu(hg) * hu
    gates = g_ref[...]
    lane = lax.broadcasted_iota(jnp.int32, gates.shape, 1)
    ge = jnp.sum(jnp.where(lane == e, gates, 0.0), axis=1, keepdims=True)
    acc_ref[...] += ge * jnp.dot(he.astype(BF16), wd_ref[0].astype(BF16),
                                  preferred_element_type=F32)

    @pl.when(e == N_EXPERTS - 1)
    def _():
        o_ref[...] = _layer_norm(ALPHA * x_ref[...] + acc_ref[...], lw_ref[...], lb_ref[...])


def _moe_ln(x, gates, wg, wu, wd, layer, ln_w, ln_b, tm):
    t = x.shape[0]
    return pl.pallas_call(
        _moe_kernel,
        out_shape=jax.ShapeDtypeStruct((t, D_MODEL), F32),
        grid=(t // tm, N_EXPERTS),
        in_specs=[pl.BlockSpec((tm, D_MODEL), lambda i, e: (i, 0)),
                  pl.BlockSpec((tm, LANES), lambda i, e: (i, 0)),
                  pl.BlockSpec((1, D_MODEL, D_FF_EXPERT), lambda i, e: (layer * N_EXPERTS + e, 0, 0)),
                  pl.BlockSpec((1, D_MODEL, D_FF_EXPERT), lambda i, e: (layer * N_EXPERTS + e, 0, 0)),
                  pl.BlockSpec((1, D_FF_EXPERT, D_MODEL), lambda i, e: (layer * N_EXPERTS + e, 0, 0)),
                  pl.BlockSpec((1, D_MODEL), lambda i, e: (0, 0)),
                  pl.BlockSpec((1, D_MODEL), lambda i, e: (0, 0))],
        out_specs=pl.BlockSpec((tm, D_MODEL), lambda i, e: (i, 0)),
        scratch_shapes=[pltpu.VMEM((tm, D_MODEL), F32), pltpu.VMEM((tm, D_MODEL), BF16)],
        compiler_params=_params("parallel", "arbitrary"),
        name="moe_ln",
    )(x, gates, wg, wu, wd, ln_w, ln_b)


def _row_copy(src, dst, sem):
    return pltpu.make_async_copy(src, dst, sem)


def _scatter_rows_kernel(dest_ref, x_ref, init_ref, o_hbm, buf_ref, sem):
    del init_ref
    n = x_ref.shape[0]
    base = pl.program_id(0) * n
    buf_ref[...] = x_ref[...].reshape(buf_ref.shape)

    def start(i, carry):
        _row_copy(buf_ref.at[i], o_hbm.at[dest_ref[base + i]], sem).start()
        return carry

    lax.fori_loop(0, n, start, 0)
    _row_copy(buf_ref, o_hbm.at[pl.ds(0, n)], sem).wait()


def _scatter_rows(x, dest, init, tr):
    t = x.shape[0]
    slabs = D_MODEL // LANES
    return pl.pallas_call(
        _scatter_rows_kernel,
        out_shape=jax.ShapeDtypeStruct(init.shape, F32),
        grid_spec=pltpu.PrefetchScalarGridSpec(
            num_scalar_prefetch=1,
            grid=(t // tr,),
            in_specs=[pl.BlockSpec((tr, D_MODEL), lambda i, d: (i, 0)),
                      pl.BlockSpec(memory_space=pl.ANY)],
            out_specs=pl.BlockSpec(memory_space=pl.ANY),
            scratch_shapes=[pltpu.VMEM((tr, slabs, LANES), F32), pltpu.SemaphoreType.DMA(())]),
        input_output_aliases={2: 0},
        compiler_params=_params("arbitrary"),
        name="scatter_rows",
    )(dest, x, init)


def _gather_rows_kernel(src_ref, y_hbm, o_ref, buf_ref, sems):
    n = o_ref.shape[0]
    i = pl.program_id(0)
    slot = i % 2

    def issue(step, to_slot):
        def start(r, carry):
            _row_copy(y_hbm.at[src_ref[step * n + r]], buf_ref.at[to_slot, r],
                      sems.at[to_slot]).start()
            return carry
        lax.fori_loop(0, n, start, 0)

    @pl.when(i == 0)
    def _():
        issue(0, 0)

    @pl.when(i + 1 < pl.num_programs(0))
    def _():
        issue(i + 1, 1 - slot)

    _row_copy(y_hbm.at[pl.ds(0, n)], buf_ref.at[slot], sems.at[slot]).wait()
    o_ref[...] = buf_ref[slot].reshape(o_ref.shape)


def _gather_rows(y3, src, tr):
    t = src.shape[0]
    slabs = D_MODEL // LANES
    return pl.pallas_call(
        _gather_rows_kernel,
        out_shape=jax.ShapeDtypeStruct((t, D_MODEL), F32),
        grid_spec=pltpu.PrefetchScalarGridSpec(
            num_scalar_prefetch=1,
            grid=(t // tr,),
            in_specs=[pl.BlockSpec(memory_space=pl.ANY)],
            out_specs=pl.BlockSpec((tr, D_MODEL), lambda i, d: (i, 0)),
            scratch_shapes=[pltpu.VMEM((2, tr, slabs, LANES), F32),
                            pltpu.SemaphoreType.DMA((2,))]),
        compiler_params=_params("arbitrary"),
        name="gather_rows",
    )(src, y3)


def _moe_group_kernel(tg_ref, x3_ref, rwt_ref, rb_ref, wg_ref, wu_ref, wd_ref, lw_ref, lb_ref,
                      o3_ref, acc_ref, xb_ref, gate_ref):
    i = pl.program_id(0)
    j = pl.program_id(1)
    group = tg_ref[i]
    slabs = D_MODEL // LANES

    @pl.when(group < 0)
    def _():
        o3_ref[...] = jnp.zeros_like(o3_ref)

    @pl.when(group >= 0)
    def _():
        @pl.when(j == 0)
        def _():
            x = x3_ref[...].reshape(acc_ref.shape)
            xb_ref[...] = x.astype(BF16)
            acc_ref[...] = ALPHA * x
            gate_ref[...] = _route_in_group(x, rwt_ref[...], rb_ref[...], group).T

        xb = xb_ref[...]
        hg = jnp.dot(xb, wg_ref[0].astype(BF16), preferred_element_type=F32)
        hu = jnp.dot(xb, wu_ref[0].astype(BF16), preferred_element_type=F32)
        he = _silu(hg) * hu
        gates = gate_ref[...]
        lane = lax.broadcasted_iota(jnp.int32, gates.shape, 1)
        ge = jnp.sum(jnp.where(lane == j, gates, 0.0), axis=1, keepdims=True)
        acc_ref[...] += ge * jnp.dot(he.astype(BF16), wd_ref[0].astype(BF16),
                                     preferred_element_type=F32)

        @pl.when(j == EXPERTS_PER_GROUP - 1)
        def _():
            y = _layer_norm(acc_ref[...], lw_ref[...], lb_ref[...])
            o3_ref[...] = y.reshape(y.shape[0], slabs, LANES)


def _moe_group_ln(xs3, tile_group, rwt, rbias, wg, wu, wd, layer, ln_w, ln_b, tm):
    n = xs3.shape[0]
    slabs = D_MODEL // LANES
    expert = lambda i, j, tg: (
        layer * N_EXPERTS + jnp.where(tg[i] >= 0, tg[i] * EXPERTS_PER_GROUP + j, 0), 0, 0)
    fixed = lambda i, j, tg: (0, 0)
    return pl.pallas_call(
        _moe_group_kernel,
        out_shape=jax.ShapeDtypeStruct((n, slabs, LANES), F32),
        grid_spec=pltpu.PrefetchScalarGridSpec(
            num_scalar_prefetch=1,
            grid=(n // tm, EXPERTS_PER_GROUP),
            in_specs=[pl.BlockSpec((tm, slabs, LANES), lambda i, j, tg: (i, 0, 0)),
                      pl.BlockSpec((3 * LANES, D_MODEL), fixed),
                      pl.BlockSpec((N_EXPERTS, 1), fixed),
                      pl.BlockSpec((1, D_MODEL, D_FF_EXPERT), expert),
                      pl.BlockSpec((1, D_MODEL, D_FF_EXPERT), expert),
                      pl.BlockSpec((1, D_FF_EXPERT, D_MODEL), expert),
                      pl.BlockSpec((1, D_MODEL), fixed),
                      pl.BlockSpec((1, D_MODEL), fixed)],
            out_specs=pl.BlockSpec((tm, slabs, LANES), lambda i, j, tg: (i, 0, 0)),
            scratch_shapes=[pltpu.VMEM((tm, D_MODEL), F32),
                            pltpu.VMEM((tm, D_MODEL), BF16), pltpu.VMEM((tm, LANES), F32)]),
        compiler_params=_params("arbitrary", "arbitrary"),
        name="moe_group_ln",
    )(tile_group, xs3, rwt, rbias, wg, wu, wd, ln_w, ln_b)


def _ffn_sorted(x, p, layer, tm, tm_moe, spare=None):
    t = x.shape[0]
    n_tiles = t // tm_moe + N_GROUPS
    gmat, totals = _group_router(x, p['rwt'], p['rbias'], tm)
    group = gmat[:, 0].astype(jnp.int32)
    counts = totals[:N_GROUPS, 0].astype(jnp.int32)
    seg_tiles = (counts + tm_moe - 1) // tm_moe
    seg_end = jnp.cumsum(seg_tiles)
    seg_start = seg_end - seg_tiles
    is_group = group[:, None] == jnp.arange(N_GROUPS, dtype=jnp.int32)[None, :]
    dest = (jnp.sum(jnp.where(is_group, seg_start[None, :], 0), axis=1) * tm_moe
            + gmat[:, 1].astype(jnp.int32))
    tile_id = jnp.arange(n_tiles, dtype=jnp.int32)
    tile_group = jnp.sum((tile_id[:, None] >= seg_end[None, :]).astype(jnp.int32), axis=1)
    tile_group = jnp.where(tile_id < seg_end[N_GROUPS - 1], tile_group, -1)
    if spare is None:
        spare = jnp.zeros((n_tiles * tm_moe, D_MODEL // LANES, LANES), F32)
    xs3 = _scatter_rows(x, dest, spare, tm)
    ys3 = _moe_group_ln(xs3, tile_group, p['rwt'], p['rbias'], p['w_gate'], p['w_up'], p['w_down'],
                        layer, p['ln2_w'][layer], p['ln2_b'][layer], tm_moe)
    return _gather_rows(ys3, dest, tm), ys3


def _prefix_selector():
    n = VEC_TILE
    nsub = VEC_CHUNK // VEC_SUB
    t = np.arange(n)[:, None]
    s = np.arange(n)[None, :]
    incl = ((t // VEC_CHUNK) == (s // VEC_CHUNK)) & ((s % VEC_CHUNK) <= (t % VEC_CHUNK))
    r = np.arange((n // VEC_CHUNK) * nsub)[:, None]
    starts = ((r // nsub) == (s // VEC_CHUNK)) & ((s % VEC_CHUNK) < VEC_SUB * (r % nsub))
    return np.concatenate([incl, starts], axis=0).astype(np.float32)


def _vec_heads(heads, sel, mm):
    n = VEC_TILE
    nsub = VEC_CHUNK // VEC_SUB
    nchunk = n // VEC_CHUNK
    nrows = heads[0][0].shape[0]
    kdim = heads[0][0].shape[1]
    streams = [(h, i) for h in range(len(heads)) for i in range(0, nrows, n)]
    tile = lambda h, i, which: heads[h][which][i:i + n]

    prefs = [_dot_exact_rhs(sel, tile(h, i, 3)) for h, i in streams]
    rows_of = lambda fn, m: jnp.concatenate(
        [jnp.broadcast_to(fn(j), (m, kdim)) for j in range(n // m)], axis=0)
    sub = (lax.broadcasted_iota(jnp.int32, (n, kdim), 0) // VEC_SUB) % nsub
    q_cat, k_cat, q_dec0, updates = [], [], [], []
    for (h, i), pref in zip(streams, prefs):
        q, k, v = tile(h, i, 0), tile(h, i, 1), tile(h, i, 2)
        big_g = pref[0:n]
        start = lambda c, j, pref=pref: pref[n + c * nsub + j:n + c * nsub + j + 1]
        q_dec = [q * jnp.exp(big_g)]
        for j in range(1, nsub):
            base_j = rows_of(lambda c: start(c, j), VEC_CHUNK)
            q_dec.append(q * jnp.exp(jnp.minimum(big_g - base_j, 0.0)))
        base_own = rows_of(lambda m: start(m // nsub, m % nsub), VEC_SUB)
        k_rel = k * jnp.exp(base_own - big_g)
        k_cat.append(jnp.concatenate([jnp.where(sub == j, k_rel, 0.0) for j in range(nsub)],
                                     axis=1))
        q_cat.append(jnp.concatenate(q_dec, axis=1))
        q_dec0.append(q_dec[0])
        per_chunk = []
        for c in range(nchunk):
            rows = slice(c * VEC_CHUNK, (c + 1) * VEC_CHUNK)
            g_last = big_g[(c + 1) * VEC_CHUNK - 1:(c + 1) * VEC_CHUNK, :]
            kd = k[rows] * jnp.exp(g_last - big_g[rows])
            per_chunk.append((jnp.exp(g_last), mm.tn(v[rows], kd)))
        updates.append(per_chunk)
    scores = [mm.nt(qc, kc) for qc, kc in zip(q_cat, k_cat)]
    rr = lax.broadcasted_iota(jnp.int32, (n, n), 0)
    cc = lax.broadcasted_iota(jnp.int32, (n, n), 1)
    keep = (rr >= cc) & ((rr // VEC_CHUNK) == (cc // VEC_CHUNK))
    intra = [mm.nn(jnp.where(keep, sc, 0.0), tile(h, i, 2)) for (h, i), sc in zip(streams, scores)]

    states = [hd[4] for hd in heads]
    o_rows = [[] for _ in heads]
    for si, (h, i) in enumerate(streams):
        for c, (decay_last, update) in enumerate(updates[si]):
            rows = slice(c * VEC_CHUNK, (c + 1) * VEC_CHUNK)
            o_rows[h].append(intra[si][rows] + mm.nt(q_dec0[si][rows], states[h]))
            states[h] = states[h] * decay_last + update
    return [(jnp.concatenate(o_rows[h], axis=0), states[h]) for h in range(len(heads))]


def _gla_chunk_kernel(q_ref, k_ref, v_ref, go_ref, sm_ref, w2_ref, b2_ref, nw_ref, sel_ref,
                      o_ref, s_ref, st_ref):
    r = pl.program_id(2)

    @pl.when(r == 0)
    def _():
        st_ref[...] = jnp.zeros_like(st_ref)

    sm = sm_ref[...]
    heads = []
    for hh in range(VEC_HPS):
        kc = slice(hh * GLA_DK, (hh + 1) * GLA_DK)
        vc = slice(hh * GLA_DV, (hh + 1) * GLA_DV)
        gk = _log_sigmoid(_ThreePass.nn(sm, w2_ref[hh]) + b2_ref[hh]) / GLA_NORMALIZER
        heads.append((q_ref[:, kc] * (GLA_DK ** -0.5), k_ref[:, kc], v_ref[:, vc], gk, st_ref[hh]))
    finals = []
    for hh, (o, st) in enumerate(_vec_heads(heads, sel_ref[...], _ThreePass)):
        vc = slice(hh * GLA_DV, (hh + 1) * GLA_DV)
        o_ref[:, vc] = _rms(o, nw_ref[...]) * _silu(go_ref[:, vc])
        st_ref[hh] = st
        finals.append(st)

    @pl.when(r == pl.num_programs(2) - 1)
    def _():
        for hh in range(VEC_HPS):
            s_ref[0, hh] = finals[hh].T


def _gla_prompt(proj, w2p, b2, norm_w, sel, bsz, seq):
    nr = seq // VEC_ROWS
    ng = GLA_HEADS // VEC_HPS
    kw = VEC_HPS * GLA_DK
    vw = VEC_HPS * GLA_DV
    row = lambda off: (lambda b, h, r: (b * nr + r, off + h))
    return pl.pallas_call(
        _gla_chunk_kernel,
        out_shape=(jax.ShapeDtypeStruct((bsz * seq, GLA_VAL), F32),
                   jax.ShapeDtypeStruct((bsz, GLA_HEADS, GLA_DK, GLA_DV), F32)),
        grid=(bsz, ng, nr),
        in_specs=[pl.BlockSpec((VEC_ROWS, kw), row(AB_Q // kw)),
                  pl.BlockSpec((VEC_ROWS, kw), row(AB_K // kw)),
                  pl.BlockSpec((VEC_ROWS, vw), row(AB_V // vw)),
                  pl.BlockSpec((VEC_ROWS, vw), row(AB_GOUT // vw)),
                  pl.BlockSpec((VEC_ROWS, LANES), lambda b, h, r: (b * nr + r, AB_SMALL // LANES)),
                  pl.BlockSpec((VEC_HPS, LANES, GLA_DK), lambda b, h, r: (h, 0, 0)),
                  pl.BlockSpec((VEC_HPS, 1, GLA_DK), lambda b, h, r: (h, 0, 0)),
                  pl.BlockSpec((1, GLA_DV), lambda b, h, r: (0, 0)),
                  pl.BlockSpec(sel.shape, lambda b, h, r: (0, 0))],
        out_specs=(pl.BlockSpec((VEC_ROWS, vw), lambda b, h, r: (b * nr + r, h)),
                   pl.BlockSpec((1, VEC_HPS, GLA_DK, GLA_DV), lambda b, h, r: (b, h, 0, 0))),
        scratch_shapes=[pltpu.VMEM((VEC_HPS, GLA_DV, GLA_DK), F32)],
        compiler_params=_params("parallel", "parallel", "arbitrary"),
        name="gla_chunk",
    )(proj, proj, proj, proj, proj, w2p, b2, norm_w, sel)


def _hgrn_lower_bound(lbraw, layer):
    m = jnp.max(lbraw, axis=0, keepdims=True)
    ex = jnp.exp(lbraw - m)
    sm = ex / jnp.sum(ex, axis=0, keepdims=True)
    acc = sm[0:1]
    for i in range(1, layer + 1):
        acc = acc + sm[i:i + 1]
    return acc - sm[0:1]


def _hgrn_gates(q_raw, f_raw, lb):
    forget = lb + (1.0 - lb) * _sigmoid(f_raw)
    return _silu(q_raw), 1.0 - forget, jnp.log(forget)


def _hgrn_chunk_kernel(layer, q_ref, f_ref, i_ref, go_ref, lb_ref, nw_ref, sel_ref,
                       o_ref, s_ref, st_ref):
    r = pl.program_id(2)

    @pl.when(r == 0)
    def _():
        st_ref[...] = jnp.zeros_like(st_ref)

    lb_all = _hgrn_lower_bound(lb_ref[...], layer)
    heads = []
    for hh in range(VEC_HPS):
        kc = slice(hh * HG_EXPAND, (hh + 1) * HG_EXPAND)
        vc = slice(hh * HG_DI, (hh + 1) * HG_DI)
        q, k, g = _hgrn_gates(q_ref[:, kc], f_ref[:, kc], lb_all[:, kc])
        heads.append((q, k, i_ref[:, vc], g, st_ref[hh]))
    finals = []
    for hh, (o, st) in enumerate(_vec_heads(heads, sel_ref[...], _OnePass)):
        vc = slice(hh * HG_DI, (hh + 1) * HG_DI)
        o_ref[:, vc] = _rms(o, nw_ref[...]) * _silu(go_ref[:, vc])
        st_ref[hh] = st
        finals.append(st)

    @pl.when(r == pl.num_programs(2) - 1)
    def _():
        for hh in range(VEC_HPS):
            s_ref[0, hh] = finals[hh].T


def _hgrn_prompt(proj, lower_bounds, norm_w, sel, layer, bsz, seq):
    nr = seq // VEC_ROWS
    ng = HG_HEADS // VEC_HPS
    kw = VEC_HPS * HG_EXPAND
    vw = VEC_HPS * HG_DI
    row = lambda off: (lambda b, h, r: (b * nr + r, off + h))
    return pl.pallas_call(
        functools.partial(_hgrn_chunk_kernel, layer),
        out_shape=(jax.ShapeDtypeStruct((bsz * seq, HG_I), F32),
                   jax.ShapeDtypeStruct((bsz, HG_HEADS, HG_EXPAND, HG_DI), F32)),
        grid=(bsz, ng, nr),
        in_specs=[pl.BlockSpec((VEC_ROWS, kw), row(0)),
                  pl.BlockSpec((VEC_ROWS, kw), row(ng)),
                  pl.BlockSpec((VEC_ROWS, vw), row(2 * ng)),
                  pl.BlockSpec((VEC_ROWS, vw), row(3 * ng)),
                  pl.BlockSpec((DEPTH, kw), lambda b, h, r: (0, h)),
                  pl.BlockSpec((1, HG_DI), lambda b, h, r: (0, 0)),
                  pl.BlockSpec(sel.shape, lambda b, h, r: (0, 0))],
        out_specs=(pl.BlockSpec((VEC_ROWS, vw), lambda b, h, r: (b * nr + r, h)),
                   pl.BlockSpec((1, VEC_HPS, HG_EXPAND, HG_DI), lambda b, h, r: (b, h, 0, 0))),
        scratch_shapes=[pltpu.VMEM((VEC_HPS, HG_DI, HG_EXPAND), F32)],
        compiler_params=_params("parallel", "parallel", "arbitrary"),
        name="hgrn_chunk",
    )(proj, proj, proj, proj, lower_bounds, norm_w, sel)


def _ssd_gate_norm(y, z, nw):
    yz = y * _silu(z)
    parts = []
    for g in range(SSD_GROUPS):
        cols = slice(g * SSD_GROUP_W, (g + 1) * SSD_GROUP_W)
        parts.append(_rms(yz[:, cols], nw[:, cols]))
    return jnp.concatenate(parts, axis=1)


def _ssd_chunk_kernel(z_ref, xbc_ref, sm_ref, cw_ref, cb_ref, dtb_ref, alog_ref, dsk_ref,
                      nw_ref, ex_ref, o_ref, s_ref, conv_ref, st_ref, prev_ref):
    r = pl.program_id(1)
    c = SSD_CHUNK
    mm = _ThreePass

    @pl.when(r == 0)
    def _():
        st_ref[...] = jnp.zeros_like(st_ref)
        prev_ref[...] = jnp.zeros_like(prev_ref)

    x_raw = xbc_ref[...]
    prev_ref[8:8 + c, :] = x_raw
    cw = cw_ref[...]
    acc = cb_ref[...] + cw[SSD_CONV - 1:SSD_CONV] * x_raw
    for m in range(1, SSD_CONV):
        acc = acc + cw[SSD_CONV - 1 - m:SSD_CONV - m] * prev_ref[8 - m:8 - m + c, :]
    xc = _silu(acc)
    prev_ref[0:8, :] = x_raw[c - 8:c]
    xs = xc[:, :SSD_INNER]
    bm = xc[:, SSD_INNER:SSD_INNER + SSD_BC]
    cm = xc[:, SSD_INNER + SSD_BC:]

    dt = _softplus(sm_ref[...] + dtb_ref[...])
    a_neg = -jnp.exp(alog_ref[...])
    big_g = _dot_exact_rhs(_tril(c).astype(BF16), dt * a_neg)
    g_t = big_g.T
    g_last = big_g[c - 1:c, :]
    ex = ex_ref[...]
    dt_x = _dot_exact_lhs(dt, ex)
    eg_x = _dot_exact_lhs(jnp.exp(big_g), ex)
    w_x = _dot_exact_lhs(dt * jnp.exp(g_last - big_g), ex)
    xdt = xs * dt_x
    xw = xs * w_x
    causal = _tril(c)
    lane = lax.broadcasted_iota(jnp.int32, (c, LANES), 1)
    st = st_ref[...]
    y_parts = []
    u_parts = []
    for g in range(SSD_GROUPS):
        gcols = slice(g * SSD_GROUP_W, (g + 1) * SSD_GROUP_W)
        bg = bm[:, g * SSD_STATE:(g + 1) * SSD_STATE]
        cg = cm[:, g * SSD_STATE:(g + 1) * SSD_STATE]
        sc = mm.nt(cg, bg)
        inter = mm.nn(cg, st[:, gcols])
        u_parts.append(mm.tn(bg, xw[:, gcols]))
        pair_cols = []
        heads_per_group = SSD_HEADS // SSD_GROUPS
        for p in range(heads_per_group // 2):
            h0 = g * heads_per_group + 2 * p
            xpair = xdt[:, h0 * SSD_HEADDIM:(h0 + 2) * SSD_HEADDIM]
            ws = []
            for h in (h0, h0 + 1):
                diff = big_g[:, h:h + 1] - g_t[h:h + 1, :]
                ws.append(sc * jnp.exp(jnp.where(causal, diff, -jnp.inf)))
            x_diag = jnp.concatenate([jnp.where(lane < SSD_HEADDIM, xpair, 0.0),
                                      jnp.where(lane < SSD_HEADDIM, 0.0, xpair)], axis=0)
            pair_cols.append(mm.nn(jnp.concatenate(ws, axis=1), x_diag))
        y_intra = jnp.concatenate(pair_cols, axis=1)
        y_parts.append(y_intra + inter * eg_x[:, gcols])
    y = jnp.concatenate(y_parts, axis=1) + dsk_ref[...] * xs
    o_ref[...] = _ssd_gate_norm(y, z_ref[...], nw_ref[...])
    st = st * eg_x[c - 1:c, :] + jnp.concatenate(u_parts, axis=1)
    st_ref[...] = st

    @pl.when(r == pl.num_programs(1) - 1)
    def _():
        s_ref[0] = st
        conv_ref[0] = x_raw[c - (SSD_CONV - 1):c]


def _ssd_prompt(proj, conv_w, conv_b, dtb_p, alog_p, dskip_x, norm_w, expand, bsz, seq):
    nr = seq // SSD_CHUNK
    fixed = lambda b, r: (0, 0)
    return pl.pallas_call(
        _ssd_chunk_kernel,
        out_shape=(jax.ShapeDtypeStruct((bsz * seq, SSD_INNER), F32),
                   jax.ShapeDtypeStruct((bsz, SSD_STATE, SSD_INNER), F32),
                   jax.ShapeDtypeStruct((bsz, SSD_CONV - 1, SSD_CONV_DIM), F32)),
        grid=(bsz, nr),
        in_specs=[pl.BlockSpec((SSD_CHUNK, SSD_INNER), lambda b, r: (b * nr + r, AB_Z // SSD_INNER)),
                  pl.BlockSpec((SSD_CHUNK, SSD_CONV_DIM), lambda b, r: (b * nr + r, AB_XBC // SSD_CONV_DIM)),
                  pl.BlockSpec((SSD_CHUNK, LANES), lambda b, r: (b * nr + r, AB_SMALL // LANES)),
                  pl.BlockSpec((SSD_CONV, SSD_CONV_DIM), fixed),
                  pl.BlockSpec((1, SSD_CONV_DIM), fixed),
                  pl.BlockSpec((1, LANES), fixed),
                  pl.BlockSpec((1, LANES), fixed),
                  pl.BlockSpec((1, SSD_INNER), fixed),
                  pl.BlockSpec((1, SSD_INNER), fixed),
                  pl.BlockSpec((LANES, SSD_INNER), fixed)],
        out_specs=(pl.BlockSpec((SSD_CHUNK, SSD_INNER), lambda b, r: (b * nr + r, 0)),
                   pl.BlockSpec((1, SSD_STATE, SSD_INNER), lambda b, r: (b, 0, 0)),
                   pl.BlockSpec((1, SSD_CONV - 1, SSD_CONV_DIM), lambda b, r: (b, 0, 0))),
        scratch_shapes=[pltpu.VMEM((SSD_STATE, SSD_INNER), F32),
                        pltpu.VMEM((8 + SSD_CHUNK, SSD_CONV_DIM), F32)],
        compiler_params=_params("parallel", "arbitrary"),
        name="ssd_chunk",
    )(proj, proj, proj, conv_w, conv_b, dtb_p, alog_p, dskip_x, norm_w, expand)


def _ab_prep_kernel(q_ref, sm_ref, xbc_ref, cs_ref, w2_ref, b2_ref, cw_ref, cb_ref, dtb_ref,
                    alog_ref, ex_ref, qs_ref, dec_ref, xc_ref, xdt_ref, dax_ref, cs_out_ref):
    sm = sm_ref[...]
    gk = _log_sigmoid(_ThreePass.nn(sm, w2_ref[...]) + b2_ref[...]) / GLA_NORMALIZER
    qs_ref[...] = q_ref[...] * (GLA_DK ** -0.5)
    dec_ref[...] = jnp.exp(gk)
    cw = cw_ref[...]
    x_raw = xbc_ref[...]
    acc = cb_ref[...] + cw[SSD_CONV - 1:SSD_CONV] * x_raw
    for j in range(SSD_CONV - 1):
        acc = acc + cw[j:j + 1] * cs_ref[j]
    xc = _silu(acc)
    xc_ref[...] = xc
    for j in range(SSD_CONV - 2):
        cs_out_ref[j] = cs_ref[j + 1]
    cs_out_ref[SSD_CONV - 2] = x_raw
    dt = _softplus(sm + dtb_ref[...])
    ex = ex_ref[...]
    xdt_ref[...] = xc[:, :SSD_INNER] * _dot_exact_lhs(dt, ex)
    dax_ref[...] = _dot_exact_lhs(jnp.exp(dt * -jnp.exp(alog_ref[...])), ex)


def _ab_prep(proj, conv_state, w2_wide, b2_wide, conv_w, conv_b, dtb_p, alog_p, expand):
    bsz = proj.shape[0]
    fixed = lambda i: (0, 0)
    sds = jax.ShapeDtypeStruct
    return pl.pallas_call(
        _ab_prep_kernel,
        out_shape=(sds((bsz, GLA_KEY), F32), sds((bsz, GLA_KEY), F32),
                   sds((bsz, SSD_CONV_DIM), F32), sds((bsz, SSD_INNER), F32),
                   sds((bsz, SSD_INNER), F32),
                   sds((SSD_CONV - 1, bsz, SSD_CONV_DIM), F32)),
        grid=(1,),
        in_specs=[pl.BlockSpec((bsz, GLA_KEY), lambda i: (0, AB_Q // GLA_KEY)),
                  pl.BlockSpec((bsz, LANES), lambda i: (0, AB_SMALL // LANES)),
                  pl.BlockSpec((bsz, SSD_CONV_DIM), lambda i: (0, AB_XBC // SSD_CONV_DIM)),
                  pl.BlockSpec((SSD_CONV - 1, bsz, SSD_CONV_DIM), lambda i: (0, 0, 0)),
                  pl.BlockSpec((LANES, GLA_KEY), fixed),
                  pl.BlockSpec((1, GLA_KEY), fixed),
                  pl.BlockSpec((SSD_CONV, SSD_CONV_DIM), fixed),
                  pl.BlockSpec((1, SSD_CONV_DIM), fixed),
                  pl.BlockSpec((1, LANES), fixed),
                  pl.BlockSpec((1, LANES), fixed),
                  pl.BlockSpec((LANES, SSD_INNER), fixed)],
        out_specs=(pl.BlockSpec((bsz, GLA_KEY), fixed), pl.BlockSpec((bsz, GLA_KEY), fixed),
                   pl.BlockSpec((bsz, SSD_CONV_DIM), fixed), pl.BlockSpec((bsz, SSD_INNER), fixed),
                   pl.BlockSpec((bsz, SSD_INNER), fixed),
                   pl.BlockSpec((SSD_CONV - 1, bsz, SSD_CONV_DIM), lambda i: (0, 0, 0))),
        compiler_params=_params("arbitrary"),
        name="ab_prep",
    )(proj, proj, proj, conv_state, w2_wide, b2_wide, conv_w, conv_b, dtb_p, alog_p, expand)


def _hgrn_prep_kernel(layer, q_ref, f_ref, lb_ref, qs_ref, k_ref, dec_ref):
    lb = _hgrn_lower_bound(lb_ref[...], layer)
    forget = lb + (1.0 - lb) * _sigmoid(f_ref[...])
    qs_ref[...] = _silu(q_ref[...])
    k_ref[...] = 1.0 - forget
    dec_ref[...] = jnp.exp(jnp.log(forget))


def _hgrn_prep(proj, lower_bounds, layer):
    bsz = proj.shape[0]
    blk = lambda j: pl.BlockSpec((bsz, HG_F), lambda i: (0, j))
    return pl.pallas_call(
        functools.partial(_hgrn_prep_kernel, layer),
        out_shape=tuple(jax.ShapeDtypeStruct((bsz, HG_F), F32) for _ in range(3)),
        grid=(1,),
        in_specs=[blk(0), blk(1), pl.BlockSpec((DEPTH, HG_F), lambda i: (0, 0))],
        out_specs=tuple(blk(0) for _ in range(3)),
        compiler_params=_params("arbitrary"),
        name="hgrn_prep",
    )(proj, proj, lower_bounds)


def _vec_step_kernel(mm, s_ref, q_ref, k_ref, d_ref, v_ref, go_ref, nw_ref, so_ref, o_ref):
    q = q_ref[...]
    kt = k_ref[0, 0]
    dt = d_ref[0, 0]
    v = v_ref[...]
    sb = v.shape[0]
    row = lax.broadcasted_iota(jnp.int32, v.shape, 0)
    new = []
    for b in range(sb):
        only_b = row == b
        decay = _dot_exact_lhs(dt, jnp.where(only_b, 1.0, 0.0).astype(BF16))
        new.append(s_ref[b, 0] * decay + mm.nn(kt, jnp.where(only_b, v, 0.0)))
    for b in range(sb):
        so_ref[b, 0] = new[b]
    o = jnp.concatenate([mm.nn(q, new[b])[b:b + 1] for b in range(sb)], axis=0)
    o_ref[...] = _rms(o, nw_ref[...]) * _silu(go_ref[...])


def _vec_step(state, q_rows, k_cols, d_cols, vsrc, v_off, gsrc, g_off, norm_w, mm):
    bsz, nh, kdim, vdim = state.shape
    sb = k_cols.shape[3]
    col = lambda j, h: (h, j, 0, 0)
    return pl.pallas_call(
        functools.partial(_vec_step_kernel, mm),
        out_shape=(jax.ShapeDtypeStruct(state.shape, F32),
                   jax.ShapeDtypeStruct((bsz, nh * vdim), F32)),
        grid=(bsz // sb, nh),
        in_specs=[pl.BlockSpec((sb, 1, kdim, vdim), lambda j, h: (j, h, 0, 0)),
                  pl.BlockSpec((sb, kdim), lambda j, h: (j, h)),
                  pl.BlockSpec((1, 1, kdim, sb), col),
                  pl.BlockSpec((1, 1, kdim, sb), col),
                  pl.BlockSpec((sb, vdim), lambda j, h: (j, v_off + h)),
                  pl.BlockSpec((sb, vdim), lambda j, h: (j, g_off + h)),
                  pl.BlockSpec((1, vdim), lambda j, h: (0, 0))],
        out_specs=(pl.BlockSpec((sb, 1, kdim, vdim), lambda j, h: (j, h, 0, 0)),
                   pl.BlockSpec((sb, vdim), lambda j, h: (j, h))),
        compiler_params=_params("parallel", "parallel"),
        name="vec_step",
    )(state, q_rows, k_cols, d_cols, vsrc, gsrc, norm_w)


def _ssd_step_kernel(s_ref, xt_ref, b_ref, c_ref, x_ref, dax_ref, dsk_ref, so_ref, y_ref):
    mm = _ThreePass
    xt = xt_ref[0, 0]
    bm = b_ref[...]
    c = c_ref[...]
    dax = dax_ref[...]
    hpg = SSD_HEADS // SSD_GROUPS
    row = lax.broadcasted_iota(jnp.int32, bm.shape, 0)
    new = []
    for b in range(STEP_B):
        outer = mm.nn(xt, jnp.where(row == b, bm, 0.0))
        per_head = []
        for hh in range(hpg):
            sn = (s_ref[b, hh] * dax[b:b + 1, hh * LANES:(hh + 1) * LANES]
                  + outer[hh * SSD_HEADDIM:(hh + 1) * SSD_HEADDIM])
            so_ref[b, hh] = sn
            per_head.append(sn)
        new.append(per_head)
    rows = [jnp.concatenate([mm.nt(c, new[b][hh])[b:b + 1] for hh in range(hpg)], axis=1)
            for b in range(STEP_B)]
    y_ref[...] = jnp.concatenate(rows, axis=0) + dsk_ref[...] * x_ref[...]


def _ssd_step(state_t, xdt_cols, xc, dax, dskip_x):
    bsz = state_t.shape[0]
    hpg = SSD_HEADS // SSD_GROUPS
    grp = lambda j, g: (j, g)
    b_off = SSD_INNER // SSD_STATE
    c_off = (SSD_INNER + SSD_BC) // SSD_STATE
    tile = pl.BlockSpec((STEP_B, hpg, SSD_HEADDIM, SSD_STATE), lambda j, g: (j, g, 0, 0))
    return pl.pallas_call(
        _ssd_step_kernel,
        out_shape=(jax.ShapeDtypeStruct(state_t.shape, F32),
                   jax.ShapeDtypeStruct((bsz, SSD_INNER), F32)),
        grid=(bsz // STEP_B, SSD_GROUPS),
        in_specs=[tile,
                  pl.BlockSpec((1, 1, SSD_GROUP_W, STEP_B), lambda j, g: (g, j, 0, 0)),
                  pl.BlockSpec((STEP_B, SSD_STATE), lambda j, g: (j, b_off + g)),
                  pl.BlockSpec((STEP_B, SSD_STATE), lambda j, g: (j, c_off + g)),
                  pl.BlockSpec((STEP_B, SSD_GROUP_W), grp),
                  pl.BlockSpec((STEP_B, hpg * LANES), grp),
                  pl.BlockSpec((1, SSD_GROUP_W), lambda j, g: (0, g))],
        out_specs=(tile, pl.BlockSpec((STEP_B, SSD_GROUP_W), grp)),
        compiler_params=_params("parallel", "parallel"),
        name="ssd_step",
    )(state_t, xdt_cols, xc, xc, xc, dax, dskip_x)


def _ssd_post_kernel(y_ref, z_ref, nw_ref, o_ref):
    o_ref[...] = _ssd_gate_norm(y_ref[...], z_ref[...], nw_ref[...])


def _ssd_post(y, proj, norm_w):
    bsz = y.shape[0]
    return pl.pallas_call(
        _ssd_post_kernel,
        out_shape=jax.ShapeDtypeStruct((bsz, SSD_INNER), F32),
        grid=(1,),
        in_specs=[pl.BlockSpec((bsz, SSD_INNER), lambda i: (0, 0)),
                  pl.BlockSpec((bsz, SSD_INNER), lambda i: (0, AB_Z // SSD_INNER)),
                  pl.BlockSpec((1, SSD_INNER), lambda i: (0, 0))],
        out_specs=pl.BlockSpec((bsz, SSD_INNER), lambda i: (0, 0)),
        compiler_params=_params("arbitrary"),
        name="ssd_post",
    )(y, proj, norm_w)


def _to_cols(a, nh, sb=STEP_B):
    bsz = a.shape[0]
    return a.reshape(bsz // sb, sb, nh, -1).transpose(2, 0, 3, 1)


def _prep_weights(w_in_ab, w_gk2, b_gk2, gla_norm_w, conv_w, conv_b, dt_bias, a_log, d_skip,
                  ssd_norm_w, w_out_ab, w_in_c, hg_norm_w, w_out_c, router_w, router_bias,
                  w_gate, w_up, w_down, ln1_w, ln1_b, ln2_w, ln2_b):
    offs = np.cumsum([0, GLA_KEY, GLA_KEY, GLA_VAL, GLA_VAL, GLA_RANK, SSD_INNER, SSD_CONV_DIM,
                      SSD_HEADS])
    sec = lambda w, i: w[:, offs[i]:offs[i + 1]]
    w = w_in_ab[0]
    pad = jnp.zeros((D_MODEL, LANES - SSD_HEADS - GLA_RANK), w.dtype)
    w_ab = jnp.concatenate([sec(w, 5), sec(w, 2), sec(w, 3), sec(w, 6), sec(w, 0), sec(w, 1),
                            sec(w, 7), sec(w, 4), pad], axis=1)
    hi_lo = lambda m: (m.astype(BF16), (m - m.astype(BF16).astype(F32)).astype(BF16))
    pad_e = lambda m: jnp.pad(m, ((0, LANES - N_EXPERTS), (0, 0)))
    router_pieces = jnp.concatenate([pad_e(piece) for piece in _split3(router_w.T)], axis=0)
    w2_wide = jnp.zeros((LANES, GLA_KEY), F32).at[SSD_HEADS:SSD_HEADS + GLA_RANK].set(w_gk2[0])
    lane_pad = lambda v: jnp.zeros((1, LANES), F32).at[0, :SSD_HEADS].set(v)
    expand = np.zeros((LANES, SSD_INNER), np.float32)
    for h in range(SSD_HEADS):
        expand[h, h * SSD_HEADDIM:(h + 1) * SSD_HEADDIM] = 1.0
    return dict(
        w_ab=hi_lo(w_ab),
        w2_wide=w2_wide,
        w2_heads=w2_wide.reshape(LANES, GLA_HEADS, GLA_DK).transpose(1, 0, 2),
        b2_wide=b_gk2[0].reshape(1, GLA_KEY),
        b2_heads=b_gk2[0].reshape(GLA_HEADS, 1, GLA_DK),
        gla_norm_w=gla_norm_w[0].reshape(1, GLA_DV),
        conv_w=conv_w[0], conv_b=conv_b[0].reshape(1, SSD_CONV_DIM),
        dtb_p=lane_pad(dt_bias[0]), alog_p=lane_pad(a_log[0]),
        dskip_x=jnp.repeat(d_skip[0], SSD_HEADDIM).reshape(1, SSD_INNER),
        ssd_norm_w=ssd_norm_w[0].reshape(1, SSD_INNER),
        expand=jnp.asarray(expand, BF16),
        prefix_sel=jnp.asarray(_prefix_selector(), BF16),
        w_out_gla=hi_lo(w_out_ab[0, :GLA_VAL]),
        w_out_ssd=hi_lo(w_out_ab[0, GLA_VAL:]),
        w_c=w_in_c[0].astype(BF16),
        hg_norm_w=hg_norm_w[0].reshape(1, HG_DI),
        w_out_c=w_out_c[0].astype(BF16),
        rwt=router_pieces,
        rbias=router_bias.reshape(N_EXPERTS, 1),
        w_gate=w_gate.reshape(DEPTH * N_EXPERTS, D_MODEL, D_FF_EXPERT),
        w_up=w_up.reshape(DEPTH * N_EXPERTS, D_MODEL, D_FF_EXPERT),
        w_down=w_down.reshape(DEPTH * N_EXPERTS, D_FF_EXPERT, D_MODEL),
        ln1_w=ln1_w.reshape(DEPTH, 1, D_MODEL), ln1_b=ln1_b.reshape(DEPTH, 1, D_MODEL),
        ln2_w=ln2_w.reshape(DEPTH, 1, D_MODEL), ln2_b=ln2_b.reshape(DEPTH, 1, D_MODEL),
    )


def _ffn(x, p, layer, tm, tm_moe):
    gates = _router(x, p['rwt'], p['rbias'], tm)
    return _moe_ln(x, gates, p['w_gate'], p['w_up'], p['w_down'], layer,
                   p['ln2_w'][layer], p['ln2_b'][layer], tm_moe)


def _ssd_state_from_wide(s_wide):
    bsz = s_wide.shape[0]
    return s_wide.reshape(bsz, SSD_STATE, SSD_HEADS, SSD_HEADDIM).transpose(0, 2, 1, 3)


def _trunk_prompt(x3, p, lower_bounds, tm, tn_ab, tn_c):
    bsz, seq, _ = x3.shape
    x = x3.reshape(bsz * seq, D_MODEL)
    tm_big = 2 * tm
    proj = _proj(x, p['w_ab'], tm_big, tn_ab)
    o_gla, s_gla = _gla_prompt(proj, p['w2_heads'], p['b2_heads'], p['gla_norm_w'],
                               p['prefix_sel'], bsz, seq)
    yz, s_ssd, s_conv = _ssd_prompt(proj, p['conv_w'], p['conv_b'], p['dtb_p'], p['alog_p'],
                                    p['dskip_x'], p['ssd_norm_w'], p['expand'], bsz, seq)
    x = _outproj_ln([o_gla, yz], [p['w_out_gla'], p['w_out_ssd']], x, p['ln1_w'][0], p['ln1_b'][0], tm)
    x, spare = _ffn_sorted(x, p, 0, tm, tm_big)
    proj_c = _proj(x, p['w_c'], tm_big, tn_c)
    o_hg, s_hg = _hgrn_prompt(proj_c, lower_bounds, p['hg_norm_w'], p['prefix_sel'], 1, bsz, seq)
    x = _outproj_ln([o_hg], [p['w_out_c']], x, p['ln1_w'][1], p['ln1_b'][1], tm)
    x, _ = _ffn_sorted(x, p, 1, tm, tm_big, spare)
    return (x.reshape(bsz, seq, D_MODEL), s_gla[None], _ssd_state_from_wide(s_ssd)[None],
            s_conv[None], s_hg[None])


def _trunk_sample(x3, st_gla, st_ssd, st_conv, st_hg, p, lower_bounds, tn_ab, tn_c):
    bsz = x3.shape[0]
    tm = bsz
    x = x3.reshape(bsz, D_MODEL)
    proj = _proj(x, p['w_ab'], tm, tn_ab)
    qs, dec, xc, xdt, dax, conv_new = _ab_prep(proj, st_conv[0].transpose(1, 0, 2), p['w2_wide'],
                                               p['b2_wide'], p['conv_w'], p['conv_b'], p['dtb_p'],
                                               p['alog_p'], p['expand'])
    conv_new = conv_new.transpose(1, 0, 2)
    k_gla = proj[:, AB_K:AB_K + GLA_KEY]
    s_gla, o_gla = _vec_step(st_gla[0], qs, _to_cols(k_gla, GLA_HEADS, VEC_STEP_B),
                             _to_cols(dec, GLA_HEADS, VEC_STEP_B), proj, AB_V // GLA_DV, proj,
                             AB_GOUT // GLA_DV, p['gla_norm_w'], _ThreePass)
    dax_wide = jnp.repeat(dax[:, ::SSD_HEADDIM], LANES, axis=1)
    s_ssd_t, y = _ssd_step(st_ssd[0].transpose(0, 1, 3, 2), _to_cols(xdt, SSD_GROUPS),
                           xc, dax_wide, p['dskip_x'])
    s_ssd = s_ssd_t.transpose(0, 1, 3, 2)
    yz = _ssd_post(y, proj, p['ssd_norm_w'])
    x = _outproj_ln([o_gla, yz], [p['w_out_gla'], p['w_out_ssd']], x, p['ln1_w'][0], p['ln1_b'][0], tm)
    x = _ffn(x, p, 0, tm, tm)
    proj_c = _proj(x, p['w_c'], tm, tn_c)
    qh, kh, dh = _hgrn_prep(proj_c, lower_bounds, 1)
    s_hg, o_hg = _vec_step(st_hg[0], qh, _to_cols(kh, HG_HEADS, VEC_STEP_B),
                           _to_cols(dh, HG_HEADS, VEC_STEP_B), proj_c, 2 * HG_HEADS, proj_c, 3 * HG_HEADS,
                           p['hg_norm_w'], _OnePass)
    x = _outproj_ln([o_hg], [p['w_out_c']], x, p['ln1_w'][1], p['ln1_b'][1], tm)
    x = _ffn(x, p, 1, tm, tm)
    return x.reshape(bsz, 1, D_MODEL), s_gla[None], s_ssd[None], conv_new[None], s_hg[None]


def kernel(x_prompt, x_sample, state_gla, state_ssd, state_conv, state_hgrn, w_in_ab, w_gk2, b_gk2, gla_norm_w, conv_w, conv_b, dt_bias, a_log, d_skip, ssd_norm_w, w_out_ab, w_in_c, lower_bounds, hg_norm_w, w_out_c, router_w, router_bias, w_gate, w_up, w_down, ln1_w, ln1_b, ln2_w, ln2_b):
    p = _prep_weights(w_in_ab, w_gk2, b_gk2, gla_norm_w, conv_w, conv_b, dt_bias, a_log, d_skip,
                      ssd_norm_w, w_out_ab, w_in_c, hg_norm_w, w_out_c, router_w, router_bias,
                      w_gate, w_up, w_down, ln1_w, ln1_b, ln2_w, ln2_b)
    y_p, gla_p, ssd_p, conv_p, hg_p = _trunk_prompt(x_prompt, p, lower_bounds, ROW_TILE,
                                                    AB_COL_TILE, C_COL_TILE)
    y_s, gla_s, ssd_s, conv_s, hg_s = _trunk_sample(x_sample, state_gla, state_ssd, state_conv,
                                                    state_hgrn, p, lower_bounds, AB_COL_TILE,
                                                    C_COL_TILE)
    return (y_p, y_s, gla_p, ssd_p, conv_p, hg_p, gla_s, ssd_s, conv_s, hg_s)
```

```python
import functools

import numpy as np
import jax
import jax.numpy as jnp
from jax import lax
from jax.experimental import pallas as pl
from jax.experimental.pallas import tpu as pltpu

F32 = jnp.float32
BF16 = jnp.bfloat16

D_MODEL = 1024
DEPTH = 2
GLA_HEADS = 4
GLA_DK = 128
GLA_DV = 256
GLA_KEY = GLA_HEADS * GLA_DK
GLA_VAL = GLA_HEADS * GLA_DV
GLA_RANK = 16
GLA_NORMALIZER = 16.0
SSD_INNER = 1024
SSD_HEADDIM = 64
SSD_HEADS = 16
SSD_STATE = 128
SSD_GROUPS = 2
SSD_CONV = 4
SSD_GROUP_W = SSD_INNER // SSD_GROUPS
SSD_BC = SSD_GROUPS * SSD_STATE
SSD_CONV_DIM = SSD_INNER + 2 * SSD_BC
HG_EXPAND = 128
HG_HEADS = 8
HG_F = HG_HEADS * HG_EXPAND
HG_I = D_MODEL
HG_DI = HG_I // HG_HEADS
N_EXPERTS = 16
N_GROUPS = 4
EXPERTS_PER_GROUP = 4
D_FF_EXPERT = 512
ALPHA = (2 * DEPTH) ** 0.25
EPS = 1e-5

LANES = 128
VMEM_LIMIT = 48 * 1024 * 1024

AB_Z = 0
AB_V = 1024
AB_GOUT = 2048
AB_XBC = 3072
AB_Q = 4608
AB_K = 5120
AB_SMALL = 5632
AB_COLS = 5760
C_COLS = 4096

VEC_CHUNK = 64
VEC_SUB = 16
VEC_TILE = 256
VEC_ROWS = 512
VEC_HPS = 4
SSD_CHUNK = 128
ROW_TILE = 512
AB_COL_TILE = 1152
C_COL_TILE = 1024
STEP_B = 8
VEC_STEP_B = 16


def _params(*sem):
    return pltpu.CompilerParams(dimension_semantics=sem, vmem_limit_bytes=VMEM_LIMIT)


_NN = (((1,), (0,)), ((), ()))
_NT = (((1,), (1,)), ((), ()))
_TN = (((0,), (0,)), ((), ()))


def _dot1(dims, a, b):
    return lax.dot_general(a.astype(BF16), b.astype(BF16), dims, preferred_element_type=F32)


def _split2(a):
    hi = a.astype(BF16)
    return hi, (a - hi.astype(F32)).astype(BF16)


def _dot3(dims, a, b):
    ah, al = _split2(a)
    bh, bl = _split2(b)
    d = lambda x, y: lax.dot_general(x, y, dims, preferred_element_type=F32)
    return (d(al, bh) + d(ah, bl)) + d(ah, bh)


class _OnePass:
    nn = staticmethod(lambda a, b: _dot1(_NN, a, b))
    nt = staticmethod(lambda a, b: _dot1(_NT, a, b))
    tn = staticmethod(lambda a, b: _dot1(_TN, a, b))


class _ThreePass:
    nn = staticmethod(lambda a, b: _dot3(_NN, a, b))
    nt = staticmethod(lambda a, b: _dot3(_NT, a, b))
    tn = staticmethod(lambda a, b: _dot3(_TN, a, b))


def _split3(a):
    hi = a.astype(BF16)
    r1 = a - hi.astype(F32)
    mid = r1.astype(BF16)
    lo = (r1 - mid.astype(F32)).astype(BF16)
    return hi, mid, lo


def _dot_exact_rhs(sel, a):
    hi, mid, lo = _split3(a)
    d = lambda p: jnp.dot(sel, p, preferred_element_type=F32)
    return (d(lo) + d(mid)) + d(hi)


def _dot_exact_lhs(a, sel):
    hi, mid, lo = _split3(a)
    d = lambda p: jnp.dot(p, sel, preferred_element_type=F32)
    return (d(lo) + d(mid)) + d(hi)


def _tril(n):
    r = lax.broadcasted_iota(jnp.int32, (n, n), 0)
    c = lax.broadcasted_iota(jnp.int32, (n, n), 1)
    return r >= c


def _sigmoid(x):
    return 1.0 / (1.0 + jnp.exp(-x))


def _silu(x):
    return x * _sigmoid(x)


def _softplus(x):
    return jnp.maximum(x, 0.0) + jnp.log(1.0 + jnp.exp(-jnp.abs(x)))


def _log_sigmoid(x):
    return -_softplus(-x)


def _rms(x, w):
    return x * lax.rsqrt(jnp.mean(x * x, axis=-1, keepdims=True) + EPS) * w


def _layer_norm(x, w, b):
    mu = jnp.mean(x, axis=-1, keepdims=True)
    xc = x - mu
    var = jnp.mean(xc * xc, axis=-1, keepdims=True)
    return xc * lax.rsqrt(var + EPS) * w + b


def _proj_kernel(x_ref, w_ref, o_ref):
    o_ref[...] = jnp.dot(x_ref[...].astype(BF16), w_ref[...], preferred_element_type=F32)


def _proj3_kernel(x_ref, wh_ref, wl_ref, o_ref, xh_ref, xl_ref):
    @pl.when(pl.program_id(1) == 0)
    def _():
        hi, lo = _split2(x_ref[...])
        xh_ref[...] = hi
        xl_ref[...] = lo

    d = lambda a, b: jnp.dot(a, b, preferred_element_type=F32)
    xh = xh_ref[...]
    wh = wh_ref[...]
    o_ref[...] = (d(xl_ref[...], wh) + d(xh, wl_ref[...])) + d(xh, wh)


def _proj(x, w, tm, tn):
    t, k = x.shape
    three = isinstance(w, tuple)
    ws = w if three else (w,)
    n = ws[0].shape[1]
    return pl.pallas_call(
        _proj3_kernel if three else _proj_kernel,
        out_shape=jax.ShapeDtypeStruct((t, n), F32),
        grid=(t // tm, n // tn),
        in_specs=[pl.BlockSpec((tm, k), lambda i, j: (i, 0))]
                 + [pl.BlockSpec((k, tn), lambda i, j: (0, j)) for _ in ws],
        out_specs=pl.BlockSpec((tm, tn), lambda i, j: (i, j)),
        scratch_shapes=[pltpu.VMEM((tm, k), BF16), pltpu.VMEM((tm, k), BF16)] if three else [],
        compiler_params=_params("parallel", "arbitrary"),
        name="in_proj",
    )(x, *ws)


def _outproj_ln_kernel(n_in, three, *refs):
    a_refs = refs[:n_in]
    nw = 2 if three else 1
    w_refs = refs[n_in:n_in + nw * n_in]
    x_ref, lw_ref, lb_ref, o_ref = refs[n_in + nw * n_in:]
    d = lambda a, b: jnp.dot(a, b, preferred_element_type=F32)
    mix = None
    for i, a_ref in enumerate(a_refs):
        if three:
            ah, al = _split2(a_ref[...])
            wh = w_refs[2 * i][...]
            part = (d(al, wh) + d(ah, w_refs[2 * i + 1][...])) + d(ah, wh)
        else:
            part = d(a_ref[...].astype(BF16), w_refs[i][...])
        mix = part if mix is None else mix + part
    o_ref[...] = _layer_norm(ALPHA * x_ref[...] + mix, lw_ref[...], lb_ref[...])


def _outproj_ln(acts, ws, x, ln_w, ln_b, tm):
    t = x.shape[0]
    n_in = len(acts)
    three = isinstance(ws[0], tuple)
    flat_ws = [w for pair in ws for w in pair] if three else list(ws)
    row = lambda i: (i, 0)
    fixed = lambda i: (0, 0)
    in_specs = ([pl.BlockSpec((tm, a.shape[1]), row) for a in acts]
                + [pl.BlockSpec(w.shape, fixed) for w in flat_ws]
                + [pl.BlockSpec((tm, D_MODEL), row),
                   pl.BlockSpec((1, D_MODEL), fixed), pl.BlockSpec((1, D_MODEL), fixed)])
    return pl.pallas_call(
        functools.partial(_outproj_ln_kernel, n_in, three),
        out_shape=jax.ShapeDtypeStruct((t, D_MODEL), F32),
        grid=(t // tm,),
        in_specs=in_specs,
        out_specs=pl.BlockSpec((tm, D_MODEL), row),
        compiler_params=_params("parallel"),
        name="out_proj_ln",
    )(*acts, *flat_ws, x, ln_w, ln_b)


def _router_scores(x, rwt, bias):
    xh, xl = _split2(x)
    nt = lambda a, b: lax.dot_general(a, b, _NT, preferred_element_type=F32)
    a = nt(xh, rwt)
    b = nt(xl, rwt[:2 * LANES])
    by_token = ((a[:, 2 * LANES:] + b[:, LANES:]) + (a[:, LANES:2 * LANES] + b[:, :LANES])) + a[:, :LANES]
    scores = _sigmoid(by_token.T[:N_EXPERTS])
    return scores, scores + bias


def _best_group(sel):
    tm = sel.shape[1]
    s = [sel[e:e + 1, :] for e in range(N_EXPERTS)]
    grp = []
    for g in range(N_GROUPS):
        m = s[g * EXPERTS_PER_GROUP:(g + 1) * EXPERTS_PER_GROUP]
        best = None
        for i in range(EXPERTS_PER_GROUP):
            for j in range(i + 1, EXPERTS_PER_GROUP):
                p = m[i] + m[j]
                best = p if best is None else jnp.maximum(best, p)
        grp.append(best)
    best_g = jnp.zeros((1, tm), jnp.int32)
    best_v = grp[0]
    for g in range(1, N_GROUPS):
        upd = grp[g] > best_v
        best_g = jnp.where(upd, g, best_g)
        best_v = jnp.where(upd, grp[g], best_v)
    return best_g


def _top2(vals, weights):
    tm = vals[0].shape[1]
    neg = jnp.full((1, tm), -jnp.inf, F32)

    def first_argmax(rows):
        idx = jnp.zeros((1, tm), jnp.int32)
        top = rows[0]
        for e in range(1, len(rows)):
            upd = rows[e] > top
            idx = jnp.where(upd, e, idx)
            top = jnp.where(upd, rows[e], top)
        return idx

    idx1 = first_argmax(vals)
    idx2 = first_argmax([jnp.where(idx1 == e, neg, v) for e, v in enumerate(vals)])
    zero = jnp.zeros((1, tm), F32)
    w1 = zero
    w2 = zero
    for e, w in enumerate(weights):
        w1 = w1 + jnp.where(idx1 == e, w, zero)
        w2 = w2 + jnp.where(idx2 == e, w, zero)
    tot = w1 + w2
    g1 = w1 / tot
    g2 = w2 / tot
    return [jnp.where(idx1 == e, g1, zero) + jnp.where(idx2 == e, g2, zero)
            for e in range(len(vals))]


def _pad_rows(rows, tm):
    return jnp.concatenate(rows + [jnp.zeros((LANES - len(rows), tm), F32)], axis=0)


def _route_in_group(x, rwt, bias, group):
    tm = x.shape[0]
    scores, sel = _router_scores(x, rwt, bias)
    zero = jnp.zeros((1, tm), F32)
    vals, weights = [], []
    for m in range(EXPERTS_PER_GROUP):
        v = zero
        w = zero
        for g in range(N_GROUPS):
            e = g * EXPERTS_PER_GROUP + m
            v = jnp.where(group == g, sel[e:e + 1, :], v)
            w = jnp.where(group == g, scores[e:e + 1, :], w)
        vals.append(v)
        weights.append(w)
    return _pad_rows(_top2(vals, weights), tm)


def _route(x, rwt, bias):
    tm = x.shape[0]
    scores, sel = _router_scores(x, rwt, bias)
    s = [sel[e:e + 1, :] for e in range(N_EXPERTS)]
    sc = [scores[e:e + 1, :] for e in range(N_EXPERTS)]
    best_g = _best_group(sel)
    neg = jnp.full((1, tm), -jnp.inf, F32)
    ms = [jnp.where(best_g == e // EXPERTS_PER_GROUP, s[e], neg) for e in range(N_EXPERTS)]
    return _pad_rows(_top2(ms, sc), tm), best_g


def _router_kernel(x_ref, rwt_ref, bias_ref, g_ref):
    g_ref[...] = _route(x_ref[...], rwt_ref[...], bias_ref[...])[0].T


def _router(x, rwt, bias, tm):
    t = x.shape[0]
    return pl.pallas_call(
        _router_kernel,
        out_shape=jax.ShapeDtypeStruct((t, LANES), F32),
        grid=(t // tm,),
        in_specs=[pl.BlockSpec((tm, D_MODEL), lambda i: (i, 0)),
                  pl.BlockSpec((3 * LANES, D_MODEL), lambda i: (0, 0)),
                  pl.BlockSpec((N_EXPERTS, 1), lambda i: (0, 0))],
        out_specs=pl.BlockSpec((tm, LANES), lambda i: (i, 0)),
        compiler_params=_params("parallel"),
        name="router",
    )(x, rwt, bias)


def _group_router_kernel(x_ref, rwt_ref, bias_ref, g_ref, tot_ref, carry_ref):
    tm = x_ref.shape[0]

    @pl.when(pl.program_id(0) == 0)
    def _():
        carry_ref[...] = jnp.zeros_like(carry_ref)

    best_g = _best_group(_router_scores(x_ref[...], rwt_ref[...], bias_ref[...])[1])
    grow = lax.broadcasted_iota(jnp.int32, (8, tm), 0)
    onehot = jnp.where(grow == best_g, 1.0, 0.0)
    rr = lax.broadcasted_iota(jnp.int32, (tm, tm), 0)
    cc = lax.broadcasted_iota(jnp.int32, (tm, tm), 1)
    before = jnp.where(rr < cc, 1.0, 0.0).astype(BF16)
    earlier = jnp.dot(onehot.astype(BF16), before, preferred_element_type=F32)
    carry = carry_ref[...]
    earlier = earlier + jnp.concatenate([carry] * (tm // LANES), axis=1)
    rank = jnp.sum(onehot * earlier, axis=0, keepdims=True)
    row = lax.broadcasted_iota(jnp.int32, (LANES, tm), 0)
    packed = jnp.where(row == 0, best_g.astype(F32), jnp.where(row == 1, rank, 0.0))
    g_ref[...] = packed.T
    carry = carry + jnp.sum(onehot, axis=1, keepdims=True)
    carry_ref[...] = carry
    tot_ref[...] = carry


def _group_router(x, rwt, bias, tm):
    t = x.shape[0]
    return pl.pallas_call(
        _group_router_kernel,
        out_shape=(jax.ShapeDtypeStruct((t, LANES), F32), jax.ShapeDtypeStruct((8, LANES), F32)),
        grid=(t // tm,),
        in_specs=[pl.BlockSpec((tm, D_MODEL), lambda i: (i, 0)),
                  pl.BlockSpec((3 * LANES, D_MODEL), lambda i: (0, 0)),
                  pl.BlockSpec((N_EXPERTS, 1), lambda i: (0, 0))],
        out_specs=(pl.BlockSpec((tm, LANES), lambda i: (i, 0)),
                   pl.BlockSpec((8, LANES), lambda i: (0, 0))),
        scratch_shapes=[pltpu.VMEM((8, LANES), F32)],
        compiler_params=_params("arbitrary"),
        name="group_router",
    )(x, rwt, bias)


def _moe_kernel(x_ref, g_ref, wg_ref, wu_ref, wd_ref, lw_ref, lb_ref, o_ref, acc_ref, xb_ref):
    e = pl.program_id(1)

    @pl.when(e == 0)
    def _():
        xb_ref[...] = x_ref[...].astype(BF16)
        acc_ref[...] = jnp.zeros_like(acc_ref)

    xb = xb_ref[...]
    hg = jnp.dot(xb, wg_ref[0].astype(BF16), preferred_element_type=F32)
    hu = jnp.dot(xb, wu_ref[0].astype(BF16), preferred_element_type=F32)
    he = _silu(hg) * hu
    gates = g_ref[...]
    lane = lax.broadcasted_iota(jnp.int32, gates.shape, 1)
    ge = jnp.sum(jnp.where(lane == e, gates, 0.0), axis=1, keepdims=True)
    acc_ref[...] += ge * jnp.dot(he.astype(BF16), wd_ref[0].astype(BF16),
                                  preferred_element_type=F32)

    @pl.when(e == N_EXPERTS - 1)
    def _():
        o_ref[...] = _layer_norm(ALPHA * x_ref[...] + acc_ref[...], lw_ref[...], lb_ref[...])


def _moe_ln(x, gates, wg, wu, wd, layer, ln_w, ln_b, tm):
    t = x.shape[0]
    return pl.pallas_call(
        _moe_kernel,
        out_shape=jax.ShapeDtypeStruct((t, D_MODEL), F32),
        grid=(t // tm, N_EXPERTS),
        in_specs=[pl.BlockSpec((tm, D_MODEL), lambda i, e: (i, 0)),
                  pl.BlockSpec((tm, LANES), lambda i, e: (i, 0)),
                  pl.BlockSpec((1, D_MODEL, D_FF_EXPERT), lambda i, e: (layer * N_EXPERTS + e, 0, 0)),
                  pl.BlockSpec((1, D_MODEL, D_FF_EXPERT), lambda i, e: (layer * N_EXPERTS + e, 0, 0)),
                  pl.BlockSpec((1, D_FF_EXPERT, D_MODEL), lambda i, e: (layer * N_EXPERTS + e, 0, 0)),
                  pl.BlockSpec((1, D_MODEL), lambda i, e: (0, 0)),
                  pl.BlockSpec((1, D_MODEL), lambda i, e: (0, 0))],
        out_specs=pl.BlockSpec((tm, D_MODEL), lambda i, e: (i, 0)),
        scratch_shapes=[pltpu.VMEM((tm, D_MODEL), F32), pltpu.VMEM((tm, D_MODEL), BF16)],
        compiler_params=_params("parallel", "arbitrary"),
        name="moe_ln",
    )(x, gates, wg, wu, wd, ln_w, ln_b)


def _row_copy(src, dst, sem):
    return pltpu.make_async_copy(src, dst, sem)


def _scatter_rows_kernel(dest_ref, x_ref, init_ref, o_hbm, buf_ref, sem):
    del init_ref
    n = x_ref.shape[0]
    base = pl.program_id(0) * n
    buf_ref[...] = x_ref[...].reshape(buf_ref.shape)

    def start(i, carry):
        _row_copy(buf_ref.at[i], o_hbm.at[dest_ref[base + i]], sem).start()
        return carry

    lax.fori_loop(0, n, start, 0)
    _row_copy(buf_ref, o_hbm.at[pl.ds(0, n)], sem).wait()


def _scatter_rows(x, dest, init, tr):
    t = x.shape[0]
    slabs = D_MODEL // LANES
    return pl.pallas_call(
        _scatter_rows_kernel,
        out_shape=jax.ShapeDtypeStruct(init.shape, F32),
        grid_spec=pltpu.PrefetchScalarGridSpec(
            num_scalar_prefetch=1,
            grid=(t // tr,),
            in_specs=[pl.BlockSpec((tr, D_MODEL), lambda i, d: (i, 0)),
                      pl.BlockSpec(memory_space=pl.ANY)],
            out_specs=pl.BlockSpec(memory_space=pl.ANY),
            scratch_shapes=[pltpu.VMEM((tr, slabs, LANES), F32), pltpu.SemaphoreType.DMA(())]),
        input_output_aliases={2: 0},
        compiler_params=_params("arbitrary"),
        name="scatter_rows",
    )(dest, x, init)


def _gather_rows_kernel(src_ref, y_hbm, o_ref, buf_ref, sems):
    n = o_ref.shape[0]
    i = pl.program_id(0)
    slot = i % 2

    def issue(step, to_slot):
        def start(r, carry):
            _row_copy(y_hbm.at[src_ref[step * n + r]], buf_ref.at[to_slot, r],
                      sems.at[to_slot]).start()
            return carry
        lax.fori_loop(0, n, start, 0)

    @pl.when(i == 0)
    def _():
        issue(0, 0)

    @pl.when(i + 1 < pl.num_programs(0))
    def _():
        issue(i + 1, 1 - slot)

    _row_copy(y_hbm.at[pl.ds(0, n)], buf_ref.at[slot], sems.at[slot]).wait()
    o_ref[...] = buf_ref[slot].reshape(o_ref.shape)


def _gather_rows(y3, src, tr):
    t = src.shape[0]
    slabs = D_MODEL // LANES
    return pl.pallas_call(
        _gather_rows_kernel,
        out_shape=jax.ShapeDtypeStruct((t, D_MODEL), F32),
        grid_spec=pltpu.PrefetchScalarGridSpec(
            num_scalar_prefetch=1,
            grid=(t // tr,),
            in_specs=[pl.BlockSpec(memory_space=pl.ANY)],
            out_specs=pl.BlockSpec((tr, D_MODEL), lambda i, d: (i, 0)),
            scratch_shapes=[pltpu.VMEM((2, tr, slabs, LANES), F32),
                            pltpu.SemaphoreType.DMA((2,))]),
        compiler_params=_params("arbitrary"),
        name="gather_rows",
    )(src, y3)


def _moe_group_kernel(tg_ref, x3_ref, rwt_ref, rb_ref, wg_ref, wu_ref, wd_ref, lw_ref, lb_ref,
                      o3_ref, acc_ref, xb_ref, gate_ref):
    i = pl.program_id(0)
    j = pl.program_id(1)
    group = tg_ref[i]
    slabs = D_MODEL // LANES

    @pl.when(group < 0)
    def _():
        o3_ref[...] = jnp.zeros_like(o3_ref)

    @pl.when(group >= 0)
    def _():
        @pl.when(j == 0)
        def _():
            x = x3_ref[...].reshape(acc_ref.shape)
            xb_ref[...] = x.astype(BF16)
            acc_ref[...] = ALPHA * x
            gate_ref[...] = _route_in_group(x, rwt_ref[...], rb_ref[...], group).T

        xb = xb_ref[...]
        hg = jnp.dot(xb, wg_ref[0].astype(BF16), preferred_element_type=F32)
        hu = jnp.dot(xb, wu_ref[0].astype(BF16), preferred_element_type=F32)
        he = _silu(hg) * hu
        gates = gate_ref[...]
        lane = lax.broadcasted_iota(jnp.int32, gates.shape, 1)
        ge = jnp.sum(jnp.where(lane == j, gates, 0.0), axis=1, keepdims=True)
        acc_ref[...] += ge * jnp.dot(he.astype(BF16), wd_ref[0].astype(BF16),
                                     preferred_element_type=F32)

        @pl.when(j == EXPERTS_PER_GROUP - 1)
        def _():
            y = _layer_norm(acc_ref[...], lw_ref[...], lb_ref[...])
            o3_ref[...] = y.reshape(y.shape[0], slabs, LANES)


def _moe_group_ln(xs3, tile_group, rwt, rbias, wg, wu, wd, layer, ln_w, ln_b, tm):
    n = xs3.shape[0]
    slabs = D_MODEL // LANES
    expert = lambda i, j, tg: (
        layer * N_EXPERTS + jnp.where(tg[i] >= 0, tg[i] * EXPERTS_PER_GROUP + j, 0), 0, 0)
    fixed = lambda i, j, tg: (0, 0)
    return pl.pallas_call(
        _moe_group_kernel,
        out_shape=jax.ShapeDtypeStruct((n, slabs, LANES), F32),
        grid_spec=pltpu.PrefetchScalarGridSpec(
            num_scalar_prefetch=1,
            grid=(n // tm, EXPERTS_PER_GROUP),
            in_specs=[pl.BlockSpec((tm, slabs, LANES), lambda i, j, tg: (i, 0, 0)),
                      pl.BlockSpec((3 * LANES, D_MODEL), fixed),
                      pl.BlockSpec((N_EXPERTS, 1), fixed),
                      pl.BlockSpec((1, D_MODEL, D_FF_EXPERT), expert),
                      pl.BlockSpec((1, D_MODEL, D_FF_EXPERT), expert),
                      pl.BlockSpec((1, D_FF_EXPERT, D_MODEL), expert),
                      pl.BlockSpec((1, D_MODEL), fixed),
                      pl.BlockSpec((1, D_MODEL), fixed)],
            out_specs=pl.BlockSpec((tm, slabs, LANES), lambda i, j, tg: (i, 0, 0)),
            scratch_shapes=[pltpu.VMEM((tm, D_MODEL), F32),
                            pltpu.VMEM((tm, D_MODEL), BF16), pltpu.VMEM((tm, LANES), F32)]),
        compiler_params=_params("arbitrary", "arbitrary"),
        name="moe_group_ln",
    )(tile_group, xs3, rwt, rbias, wg, wu, wd, ln_w, ln_b)


def _ffn_sorted(x, p, layer, tm, tm_moe, spare=None):
    t = x.shape[0]
    n_tiles = t // tm_moe + N_GROUPS
    gmat, totals = _group_router(x, p['rwt'], p['rbias'], tm)
    group = gmat[:, 0].astype(jnp.int32)
    counts = totals[:N_GROUPS, 0].astype(jnp.int32)
    seg_tiles = (counts + tm_moe - 1) // tm_moe
    seg_end = jnp.cumsum(seg_tiles)
    seg_start = seg_end - seg_tiles
    is_group = group[:, None] == jnp.arange(N_GROUPS, dtype=jnp.int32)[None, :]
    dest = (jnp.sum(jnp.where(is_group, seg_start[None, :], 0), axis=1) * tm_moe
            + gmat[:, 1].astype(jnp.int32))
    tile_id = jnp.arange(n_tiles, dtype=jnp.int32)
    tile_group = jnp.sum((tile_id[:, None] >= seg_end[None, :]).astype(jnp.int32), axis=1)
    tile_group = jnp.where(tile_id < seg_end[N_GROUPS - 1], tile_group, -1)
    if spare is None:
        spare = jnp.zeros((n_tiles * tm_moe, D_MODEL // LANES, LANES), F32)
    xs3 = _scatter_rows(x, dest, spare, tm_moe)
    ys3 = _moe_group_ln(xs3, tile_group, p['rwt'], p['rbias'], p['w_gate'], p['w_up'], p['w_down'],
                        layer, p['ln2_w'][layer], p['ln2_b'][layer], tm_moe)
    return _gather_rows(ys3, dest, tm_moe), ys3


def _prefix_selector():
    n = VEC_TILE
    nsub = VEC_CHUNK // VEC_SUB
    t = np.arange(n)[:, None]
    s = np.arange(n)[None, :]
    incl = ((t // VEC_CHUNK) == (s // VEC_CHUNK)) & ((s % VEC_CHUNK) <= (t % VEC_CHUNK))
    r = np.arange((n // VEC_CHUNK) * nsub)[:, None]
    starts = ((r // nsub) == (s // VEC_CHUNK)) & ((s % VEC_CHUNK) < VEC_SUB * (r % nsub))
    return np.concatenate([incl, starts], axis=0).astype(np.float32)


def _vec_heads(heads, sel, mm):
    n = VEC_TILE
    nsub = VEC_CHUNK // VEC_SUB
    nchunk = n // VEC_CHUNK
    nrows = heads[0][0].shape[0]
    kdim = heads[0][0].shape[1]
    streams = [(h, i) for h in range(len(heads)) for i in range(0, nrows, n)]
    tile = lambda h, i, which: heads[h][which][i:i + n]

    prefs = [_dot_exact_rhs(sel, tile(h, i, 3)) for h, i in streams]
    rows_of = lambda fn, m: jnp.concatenate(
        [jnp.broadcast_to(fn(j), (m, kdim)) for j in range(n // m)], axis=0)
    sub = (lax.broadcasted_iota(jnp.int32, (n, kdim), 0) // VEC_SUB) % nsub
    q_cat, k_cat, q_dec0, updates = [], [], [], []
    for (h, i), pref in zip(streams, prefs):
        q, k, v = tile(h, i, 0), tile(h, i, 1), tile(h, i, 2)
        big_g = pref[0:n]
        start = lambda c, j, pref=pref: pref[n + c * nsub + j:n + c * nsub + j + 1]
        q_dec = [q * jnp.exp(big_g)]
        for j in range(1, nsub):
            base_j = rows_of(lambda c: start(c, j), VEC_CHUNK)
            q_dec.append(q * jnp.exp(jnp.minimum(big_g - base_j, 0.0)))
        base_own = rows_of(lambda m: start(m // nsub, m % nsub), VEC_SUB)
        k_rel = k * jnp.exp(base_own - big_g)
        k_cat.append(jnp.concatenate([jnp.where(sub == j, k_rel, 0.0) for j in range(nsub)],
                                     axis=1))
        q_cat.append(jnp.concatenate(q_dec, axis=1))
        q_dec0.append(q_dec[0])
        per_chunk = []
        for c in range(nchunk):
            rows = slice(c * VEC_CHUNK, (c + 1) * VEC_CHUNK)
            g_last = big_g[(c + 1) * VEC_CHUNK - 1:(c + 1) * VEC_CHUNK, :]
            kd = k[rows] * jnp.exp(g_last - big_g[rows])
            per_chunk.append((jnp.exp(g_last), mm.tn(v[rows], kd)))
        updates.append(per_chunk)
    scores = [mm.nt(qc, kc) for qc, kc in zip(q_cat, k_cat)]
    rr = lax.broadcasted_iota(jnp.int32, (n, n), 0)
    cc = lax.broadcasted_iota(jnp.int32, (n, n), 1)
    keep = (rr >= cc) & ((rr // VEC_CHUNK) == (cc // VEC_CHUNK))
    intra = [mm.nn(jnp.where(keep, sc, 0.0), tile(h, i, 2)) for (h, i), sc in zip(streams, scores)]

    states = [hd[4] for hd in heads]
    o_rows = [[] for _ in heads]
    for si, (h, i) in enumerate(streams):
        for c, (decay_last, update) in enumerate(updates[si]):
            rows = slice(c * VEC_CHUNK, (c + 1) * VEC_CHUNK)
            o_rows[h].append(intra[si][rows] + mm.nt(q_dec0[si][rows], states[h]))
            states[h] = states[h] * decay_last + update
    return [(jnp.concatenate(o_rows[h], axis=0), states[h]) for h in range(len(heads))]


def _gla_chunk_kernel(q_ref, k_ref, v_ref, go_ref, sm_ref, w2_ref, b2_ref, nw_ref, sel_ref,
                      o_ref, s_ref, st_ref):
    r = pl.program_id(2)

    @pl.when(r == 0)
    def _():
        st_ref[...] = jnp.zeros_like(st_ref)

    sm = sm_ref[...]
    heads = []
    for hh in range(VEC_HPS):
        kc = slice(hh * GLA_DK, (hh + 1) * GLA_DK)
        vc = slice(hh * GLA_DV, (hh + 1) * GLA_DV)
        gk = _log_sigmoid(_ThreePass.nn(sm, w2_ref[hh]) + b2_ref[hh]) / GLA_NORMALIZER
        heads.append((q_ref[:, kc] * (GLA_DK ** -0.5), k_ref[:, kc], v_ref[:, vc], gk, st_ref[hh]))
    finals = []
    for hh, (o, st) in enumerate(_vec_heads(heads, sel_ref[...], _ThreePass)):
        vc = slice(hh * GLA_DV, (hh + 1) * GLA_DV)
        o_ref[:, vc] = _rms(o, nw_ref[...]) * _silu(go_ref[:, vc])
        st_ref[hh] = st
        finals.append(st)

    @pl.when(r == pl.num_programs(2) - 1)
    def _():
        for hh in range(VEC_HPS):
            s_ref[0, hh] = finals[hh].T


def _gla_prompt(proj, w2p, b2, norm_w, sel, bsz, seq):
    nr = seq // VEC_ROWS
    ng = GLA_HEADS // VEC_HPS
    kw = VEC_HPS * GLA_DK
    vw = VEC_HPS * GLA_DV
    row = lambda off: (lambda b, h, r: (b * nr + r, off + h))
    return pl.pallas_call(
        _gla_chunk_kernel,
        out_shape=(jax.ShapeDtypeStruct((bsz * seq, GLA_VAL), F32),
                   jax.ShapeDtypeStruct((bsz, GLA_HEADS, GLA_DK, GLA_DV), F32)),
        grid=(bsz, ng, nr),
        in_specs=[pl.BlockSpec((VEC_ROWS, kw), row(AB_Q // kw)),
                  pl.BlockSpec((VEC_ROWS, kw), row(AB_K // kw)),
                  pl.BlockSpec((VEC_ROWS, vw), row(AB_V // vw)),
                  pl.BlockSpec((VEC_ROWS, vw), row(AB_GOUT // vw)),
                  pl.BlockSpec((VEC_ROWS, LANES), lambda b, h, r: (b * nr + r, AB_SMALL // LANES)),
                  pl.BlockSpec((VEC_HPS, LANES, GLA_DK), lambda b, h, r: (h, 0, 0)),
                  pl.BlockSpec((VEC_HPS, 1, GLA_DK), lambda b, h, r: (h, 0, 0)),
                  pl.BlockSpec((1, GLA_DV), lambda b, h, r: (0, 0)),
                  pl.BlockSpec(sel.shape, lambda b, h, r: (0, 0))],
        out_specs=(pl.BlockSpec((VEC_ROWS, vw), lambda b, h, r: (b * nr + r, h)),
                   pl.BlockSpec((1, VEC_HPS, GLA_DK, GLA_DV), lambda b, h, r: (b, h, 0, 0))),
        scratch_shapes=[pltpu.VMEM((VEC_HPS, GLA_DV, GLA_DK), F32)],
        compiler_params=_params("parallel", "parallel", "arbitrary"),
        name="gla_chunk",
    )(proj, proj, proj, proj, proj, w2p, b2, norm_w, sel)


def _hgrn_lower_bound(lbraw, layer):
    m = jnp.max(lbraw, axis=0, keepdims=True)
    ex = jnp.exp(lbraw - m)
    sm = ex / jnp.sum(ex, axis=0, keepdims=True)
    acc = sm[0:1]
    for i in range(1, layer + 1):
        acc = acc + sm[i:i + 1]
    return acc - sm[0:1]


def _hgrn_gates(q_raw, f_raw, lb):
    forget = lb + (1.0 - lb) * _sigmoid(f_raw)
    return _silu(q_raw), 1.0 - forget, jnp.log(forget)


def _hgrn_chunk_kernel(layer, q_ref, f_ref, i_ref, go_ref, lb_ref, nw_ref, sel_ref,
                       o_ref, s_ref, st_ref):
    r = pl.program_id(2)

    @pl.when(r == 0)
    def _():
        st_ref[...] = jnp.zeros_like(st_ref)

    lb_all = _hgrn_lower_bound(lb_ref[...], layer)
    heads = []
    for hh in range(VEC_HPS):
        kc = slice(hh * HG_EXPAND, (hh + 1) * HG_EXPAND)
        vc = slice(hh * HG_DI, (hh + 1) * HG_DI)
        q, k, g = _hgrn_gates(q_ref[:, kc], f_ref[:, kc], lb_all[:, kc])
        heads.append((q, k, i_ref[:, vc], g, st_ref[hh]))
    finals = []
    for hh, (o, st) in enumerate(_vec_heads(heads, sel_ref[...], _OnePass)):
        vc = slice(hh * HG_DI, (hh + 1) * HG_DI)
        o_ref[:, vc] = _rms(o, nw_ref[...]) * _silu(go_ref[:, vc])
        st_ref[hh] = st
        finals.append(st)

    @pl.when(r == pl.num_programs(2) - 1)
    def _():
        for hh in range(VEC_HPS):
            s_ref[0, hh] = finals[hh].T


def _hgrn_prompt(proj, lower_bounds, norm_w, sel, layer, bsz, seq):
    nr = seq // VEC_ROWS
    ng = HG_HEADS // VEC_HPS
    kw = VEC_HPS * HG_EXPAND
    vw = VEC_HPS * HG_DI
    row = lambda off: (lambda b, h, r: (b * nr + r, off + h))
    return pl.pallas_call(
        functools.partial(_hgrn_chunk_kernel, layer),
        out_shape=(jax.ShapeDtypeStruct((bsz * seq, HG_I), F32),
                   jax.ShapeDtypeStruct((bsz, HG_HEADS, HG_EXPAND, HG_DI), F32)),
        grid=(bsz, ng, nr),
        in_specs=[pl.BlockSpec((VEC_ROWS, kw), row(0)),
                  pl.BlockSpec((VEC_ROWS, kw), row(ng)),
                  pl.BlockSpec((VEC_ROWS, vw), row(2 * ng)),
                  pl.BlockSpec((VEC_ROWS, vw), row(3 * ng)),
                  pl.BlockSpec((DEPTH, kw), lambda b, h, r: (0, h)),
                  pl.BlockSpec((1, HG_DI), lambda b, h, r: (0, 0)),
                  pl.BlockSpec(sel.shape, lambda b, h, r: (0, 0))],
        out_specs=(pl.BlockSpec((VEC_ROWS, vw), lambda b, h, r: (b * nr + r, h)),
                   pl.BlockSpec((1, VEC_HPS, HG_EXPAND, HG_DI), lambda b, h, r: (b, h, 0, 0))),
        scratch_shapes=[pltpu.VMEM((VEC_HPS, HG_DI, HG_EXPAND), F32)],
        compiler_params=_params("parallel", "parallel", "arbitrary"),
        name="hgrn_chunk",
    )(proj, proj, proj, proj, lower_bounds, norm_w, sel)


def _ssd_gate_norm(y, z, nw):
    yz = y * _silu(z)
    parts = []
    for g in range(SSD_GROUPS):
        cols = slice(g * SSD_GROUP_W, (g + 1) * SSD_GROUP_W)
        parts.append(_rms(yz[:, cols], nw[:, cols]))
    return jnp.concatenate(parts, axis=1)


def _ssd_chunk_kernel(z_ref, xbc_ref, sm_ref, cw_ref, cb_ref, dtb_ref, alog_ref, dsk_ref,
                      nw_ref, ex_ref, o_ref, s_ref, conv_ref, st_ref, prev_ref):
    r = pl.program_id(1)
    c = SSD_CHUNK
    mm = _ThreePass

    @pl.when(r == 0)
    def _():
        st_ref[...] = jnp.zeros_like(st_ref)
        prev_ref[...] = jnp.zeros_like(prev_ref)

    x_raw = xbc_ref[...]
    prev_ref[8:8 + c, :] = x_raw
    cw = cw_ref[...]
    acc = cb_ref[...] + cw[SSD_CONV - 1:SSD_CONV] * x_raw
    for m in range(1, SSD_CONV):
        acc = acc + cw[SSD_CONV - 1 - m:SSD_CONV - m] * prev_ref[8 - m:8 - m + c, :]
    xc = _silu(acc)
    prev_ref[0:8, :] = x_raw[c - 8:c]
    xs = xc[:, :SSD_INNER]
    bm = xc[:, SSD_INNER:SSD_INNER + SSD_BC]
    cm = xc[:, SSD_INNER + SSD_BC:]

    dt = _softplus(sm_ref[...] + dtb_ref[...])
    a_neg = -jnp.exp(alog_ref[...])
    big_g = _dot_exact_rhs(_tril(c).astype(BF16), dt * a_neg)
    g_t = big_g.T
    g_last = big_g[c - 1:c, :]
    ex = ex_ref[...]
    dt_x = _dot_exact_lhs(dt, ex)
    eg_x = _dot_exact_lhs(jnp.exp(big_g), ex)
    w_x = _dot_exact_lhs(dt * jnp.exp(g_last - big_g), ex)
    xdt = xs * dt_x
    xw = xs * w_x
    causal = _tril(c)
    lane = lax.broadcasted_iota(jnp.int32, (c, LANES), 1)
    st = st_ref[...]
    y_parts = []
    u_parts = []
    for g in range(SSD_GROUPS):
        gcols = slice(g * SSD_GROUP_W, (g + 1) * SSD_GROUP_W)
        bg = bm[:, g * SSD_STATE:(g + 1) * SSD_STATE]
        cg = cm[:, g * SSD_STATE:(g + 1) * SSD_STATE]
        sc = mm.nt(cg, bg)
        inter = mm.nn(cg, st[:, gcols])
        u_parts.append(mm.tn(bg, xw[:, gcols]))
        pair_cols = []
        heads_per_group = SSD_HEADS // SSD_GROUPS
        for p in range(heads_per_group // 2):
            h0 = g * heads_per_group + 2 * p
            xpair = xdt[:, h0 * SSD_HEADDIM:(h0 + 2) * SSD_HEADDIM]
            ws = []
            for h in (h0, h0 + 1):
                diff = big_g[:, h:h + 1] - g_t[h:h + 1, :]
                ws.append(sc * jnp.exp(jnp.where(causal, diff, -jnp.inf)))
            x_diag = jnp.concatenate([jnp.where(lane < SSD_HEADDIM, xpair, 0.0),
                                      jnp.where(lane < SSD_HEADDIM, 0.0, xpair)], axis=0)
            pair_cols.append(mm.nn(jnp.concatenate(ws, axis=1), x_diag))
        y_intra = jnp.concatenate(pair_cols, axis=1)
        y_parts.append(y_intra + inter * eg_x[:, gcols])
    y = jnp.concatenate(y_parts, axis=1) + dsk_ref[...] * xs
    o_ref[...] = _ssd_gate_norm(y, z_ref[...], nw_ref[...])
    st = st * eg_x[c - 1:c, :] + jnp.concatenate(u_parts, axis=1)
    st_ref[...] = st

    @pl.when(r == pl.num_programs(1) - 1)
    def _():
        s_ref[0] = st
        conv_ref[0] = x_raw[c - (SSD_CONV - 1):c]


def _ssd_prompt(proj, conv_w, conv_b, dtb_p, alog_p, dskip_x, norm_w, expand, bsz, seq):
    nr = seq // SSD_CHUNK
    fixed = lambda b, r: (0, 0)
    return pl.pallas_call(
        _ssd_chunk_kernel,
        out_shape=(jax.ShapeDtypeStruct((bsz * seq, SSD_INNER), F32),
                   jax.ShapeDtypeStruct((bsz, SSD_STATE, SSD_INNER), F32),
                   jax.ShapeDtypeStruct((bsz, SSD_CONV - 1, SSD_CONV_DIM), F32)),
        grid=(bsz, nr),
        in_specs=[pl.BlockSpec((SSD_CHUNK, SSD_INNER), lambda b, r: (b * nr + r, AB_Z // SSD_INNER)),
                  pl.BlockSpec((SSD_CHUNK, SSD_CONV_DIM), lambda b, r: (b * nr + r, AB_XBC // SSD_CONV_DIM)),
                  pl.BlockSpec((SSD_CHUNK, LANES), lambda b, r: (b * nr + r, AB_SMALL // LANES)),
                  pl.BlockSpec((SSD_CONV, SSD_CONV_DIM), fixed),
                  pl.BlockSpec((1, SSD_CONV_DIM), fixed),
                  pl.BlockSpec((1, LANES), fixed),
                  pl.BlockSpec((1, LANES), fixed),
                  pl.BlockSpec((1, SSD_INNER), fixed),
                  pl.BlockSpec((1, SSD_INNER), fixed),
                  pl.BlockSpec((LANES, SSD_INNER), fixed)],
        out_specs=(pl.BlockSpec((SSD_CHUNK, SSD_INNER), lambda b, r: (b * nr + r, 0)),
                   pl.BlockSpec((1, SSD_STATE, SSD_INNER), lambda b, r: (b, 0, 0)),
                   pl.BlockSpec((1, SSD_CONV - 1, SSD_CONV_DIM), lambda b, r: (b, 0, 0))),
        scratch_shapes=[pltpu.VMEM((SSD_STATE, SSD_INNER), F32),
                        pltpu.VMEM((8 + SSD_CHUNK, SSD_CONV_DIM), F32)],
        compiler_params=_params("parallel", "arbitrary"),
        name="ssd_chunk",
    )(proj, proj, proj, conv_w, conv_b, dtb_p, alog_p, dskip_x, norm_w, expand)


def _ab_prep_kernel(q_ref, sm_ref, xbc_ref, cs_ref, w2_ref, b2_ref, cw_ref, cb_ref, dtb_ref,
                    alog_ref, ex_ref, qs_ref, dec_ref, xc_ref, xdt_ref, dax_ref, cs_out_ref):
    sm = sm_ref[...]
    gk = _log_sigmoid(_ThreePass.nn(sm, w2_ref[...]) + b2_ref[...]) / GLA_NORMALIZER
    qs_ref[...] = q_ref[...] * (GLA_DK ** -0.5)
    dec_ref[...] = jnp.exp(gk)
    cw = cw_ref[...]
    x_raw = xbc_ref[...]
    acc = cb_ref[...] + cw[SSD_CONV - 1:SSD_CONV] * x_raw
    for j in range(SSD_CONV - 1):
        acc = acc + cw[j:j + 1] * cs_ref[j]
    xc = _silu(acc)
    xc_ref[...] = xc
    for j in range(SSD_CONV - 2):
        cs_out_ref[j] = cs_ref[j + 1]
    cs_out_ref[SSD_CONV - 2] = x_raw
    dt = _softplus(sm + dtb_ref[...])
    ex = ex_ref[...]
    xdt_ref[...] = xc[:, :SSD_INNER] * _dot_exact_lhs(dt, ex)
    dax_ref[...] = _dot_exact_lhs(jnp.exp(dt * -jnp.exp(alog_ref[...])), ex)


def _ab_prep(proj, conv_state, w2_wide, b2_wide, conv_w, conv_b, dtb_p, alog_p, expand):
    bsz = proj.shape[0]
    fixed = lambda i: (0, 0)
    sds = jax.ShapeDtypeStruct
    return pl.pallas_call(
        _ab_prep_kernel,
        out_shape=(sds((bsz, GLA_KEY), F32), sds((bsz, GLA_KEY), F32),
                   sds((bsz, SSD_CONV_DIM), F32), sds((bsz, SSD_INNER), F32),
                   sds((bsz, SSD_INNER), F32),
                   sds((SSD_CONV - 1, bsz, SSD_CONV_DIM), F32)),
        grid=(1,),
        in_specs=[pl.BlockSpec((bsz, GLA_KEY), lambda i: (0, AB_Q // GLA_KEY)),
                  pl.BlockSpec((bsz, LANES), lambda i: (0, AB_SMALL // LANES)),
                  pl.BlockSpec((bsz, SSD_CONV_DIM), lambda i: (0, AB_XBC // SSD_CONV_DIM)),
                  pl.BlockSpec((SSD_CONV - 1, bsz, SSD_CONV_DIM), lambda i: (0, 0, 0)),
                  pl.BlockSpec((LANES, GLA_KEY), fixed),
                  pl.BlockSpec((1, GLA_KEY), fixed),
                  pl.BlockSpec((SSD_CONV, SSD_CONV_DIM), fixed),
                  pl.BlockSpec((1, SSD_CONV_DIM), fixed),
                  pl.BlockSpec((1, LANES), fixed),
                  pl.BlockSpec((1, LANES), fixed),
                  pl.BlockSpec((LANES, SSD_INNER), fixed)],
        out_specs=(pl.BlockSpec((bsz, GLA_KEY), fixed), pl.BlockSpec((bsz, GLA_KEY), fixed),
                   pl.BlockSpec((bsz, SSD_CONV_DIM), fixed), pl.BlockSpec((bsz, SSD_INNER), fixed),
                   pl.BlockSpec((bsz, SSD_INNER), fixed),
                   pl.BlockSpec((SSD_CONV - 1, bsz, SSD_CONV_DIM), lambda i: (0, 0, 0))),
        compiler_params=_params("arbitrary"),
        name="ab_prep",
    )(proj, proj, proj, conv_state, w2_wide, b2_wide, conv_w, conv_b, dtb_p, alog_p, expand)


def _hgrn_prep_kernel(layer, q_ref, f_ref, lb_ref, qs_ref, k_ref, dec_ref):
    lb = _hgrn_lower_bound(lb_ref[...], layer)
    forget = lb + (1.0 - lb) * _sigmoid(f_ref[...])
    qs_ref[...] = _silu(q_ref[...])
    k_ref[...] = 1.0 - forget
    dec_ref[...] = jnp.exp(jnp.log(forget))


def _hgrn_prep(proj, lower_bounds, layer):
    bsz = proj.shape[0]
    blk = lambda j: pl.BlockSpec((bsz, HG_F), lambda i: (0, j))
    return pl.pallas_call(
        functools.partial(_hgrn_prep_kernel, layer),
        out_shape=tuple(jax.ShapeDtypeStruct((bsz, HG_F), F32) for _ in range(3)),
        grid=(1,),
        in_specs=[blk(0), blk(1), pl.BlockSpec((DEPTH, HG_F), lambda i: (0, 0))],
        out_specs=tuple(blk(0) for _ in range(3)),
        compiler_params=_params("arbitrary"),
        name="hgrn_prep",
    )(proj, proj, lower_bounds)


def _vec_step_kernel(mm, s_ref, q_ref, k_ref, d_ref, v_ref, go_ref, nw_ref, so_ref, o_ref):
    q = q_ref[...]
    kt = k_ref[0, 0]
    dt = d_ref[0, 0]
    v = v_ref[...]
    sb = v.shape[0]
    row = lax.broadcasted_iota(jnp.int32, v.shape, 0)
    new = []
    for b in range(sb):
        only_b = row == b
        decay = _dot_exact_lhs(dt, jnp.where(only_b, 1.0, 0.0).astype(BF16))
        new.append(s_ref[b, 0] * decay + mm.nn(kt, jnp.where(only_b, v, 0.0)))
    for b in range(sb):
        so_ref[b, 0] = new[b]
    o = jnp.concatenate([mm.nn(q, new[b])[b:b + 1] for b in range(sb)], axis=0)
    o_ref[...] = _rms(o, nw_ref[...]) * _silu(go_ref[...])


def _vec_step(state, q_rows, k_cols, d_cols, vsrc, v_off, gsrc, g_off, norm_w, mm):
    bsz, nh, kdim, vdim = state.shape
    sb = k_cols.shape[3]
    col = lambda j, h: (h, j, 0, 0)
    return pl.pallas_call(
        functools.partial(_vec_step_kernel, mm),
        out_shape=(jax.ShapeDtypeStruct(state.shape, F32),
                   jax.ShapeDtypeStruct((bsz, nh * vdim), F32)),
        grid=(bsz // sb, nh),
        in_specs=[pl.BlockSpec((sb, 1, kdim, vdim), lambda j, h: (j, h, 0, 0)),
                  pl.BlockSpec((sb, kdim), lambda j, h: (j, h)),
                  pl.BlockSpec((1, 1, kdim, sb), col),
                  pl.BlockSpec((1, 1, kdim, sb), col),
                  pl.BlockSpec((sb, vdim), lambda j, h: (j, v_off + h)),
                  pl.BlockSpec((sb, vdim), lambda j, h: (j, g_off + h)),
                  pl.BlockSpec((1, vdim), lambda j, h: (0, 0))],
        out_specs=(pl.BlockSpec((sb, 1, kdim, vdim), lambda j, h: (j, h, 0, 0)),
                   pl.BlockSpec((sb, vdim), lambda j, h: (j, h))),
        compiler_params=_params("parallel", "parallel"),
        name="vec_step",
    )(state, q_rows, k_cols, d_cols, vsrc, gsrc, norm_w)


def _ssd_step_kernel(s_ref, xt_ref, b_ref, c_ref, x_ref, dax_ref, dsk_ref, so_ref, y_ref):
    mm = _ThreePass
    xt = xt_ref[0, 0]
    bm = b_ref[...]
    c = c_ref[...]
    dax = dax_ref[...]
    hpg = SSD_HEADS // SSD_GROUPS
    row = lax.broadcasted_iota(jnp.int32, bm.shape, 0)
    new = []
    for b in range(STEP_B):
        outer = mm.nn(xt, jnp.where(row == b, bm, 0.0))
        per_head = []
        for hh in range(hpg):
            sn = (s_ref[b, hh] * dax[b:b + 1, hh * LANES:(hh + 1) * LANES]
                  + outer[hh * SSD_HEADDIM:(hh + 1) * SSD_HEADDIM])
            so_ref[b, hh] = sn
            per_head.append(sn)
        new.append(per_head)
    rows = [jnp.concatenate([mm.nt(c, new[b][hh])[b:b + 1] for hh in range(hpg)], axis=1)
            for b in range(STEP_B)]
    y_ref[...] = jnp.concatenate(rows, axis=0) + dsk_ref[...] * x_ref[...]


def _ssd_step(state_t, xdt_cols, xc, dax, dskip_x):
    bsz = state_t.shape[0]
    hpg = SSD_HEADS // SSD_GROUPS
    grp = lambda j, g: (j, g)
    b_off = SSD_INNER // SSD_STATE
    c_off = (SSD_INNER + SSD_BC) // SSD_STATE
    tile = pl.BlockSpec((STEP_B, hpg, SSD_HEADDIM, SSD_STATE), lambda j, g: (j, g, 0, 0))
    return pl.pallas_call(
        _ssd_step_kernel,
        out_shape=(jax.ShapeDtypeStruct(state_t.shape, F32),
                   jax.ShapeDtypeStruct((bsz, SSD_INNER), F32)),
        grid=(bsz // STEP_B, SSD_GROUPS),
        in_specs=[tile,
                  pl.BlockSpec((1, 1, SSD_GROUP_W, STEP_B), lambda j, g: (g, j, 0, 0)),
                  pl.BlockSpec((STEP_B, SSD_STATE), lambda j, g: (j, b_off + g)),
                  pl.BlockSpec((STEP_B, SSD_STATE), lambda j, g: (j, c_off + g)),
                  pl.BlockSpec((STEP_B, SSD_GROUP_W), grp),
                  pl.BlockSpec((STEP_B, hpg * LANES), grp),
                  pl.BlockSpec((1, SSD_GROUP_W), lambda j, g: (0, g))],
        out_specs=(tile, pl.BlockSpec((STEP_B, SSD_GROUP_W), grp)),
        compiler_params=_params("parallel", "parallel"),
        name="ssd_step",
    )(state_t, xdt_cols, xc, xc, xc, dax, dskip_x)


def _ssd_post_kernel(y_ref, z_ref, nw_ref, o_ref):
    o_ref[...] = _ssd_gate_norm(y_ref[...], z_ref[...], nw_ref[...])


def _ssd_post(y, proj, norm_w):
    bsz = y.shape[0]
    return pl.pallas_call(
        _ssd_post_kernel,
        out_shape=jax.ShapeDtypeStruct((bsz, SSD_INNER), F32),
        grid=(1,),
        in_specs=[pl.BlockSpec((bsz, SSD_INNER), lambda i: (0, 0)),
                  pl.BlockSpec((bsz, SSD_INNER), lambda i: (0, AB_Z // SSD_INNER)),
                  pl.BlockSpec((1, SSD_INNER), lambda i: (0, 0))],
        out_specs=pl.BlockSpec((bsz, SSD_INNER), lambda i: (0, 0)),
        compiler_params=_params("arbitrary"),
        name="ssd_post",
    )(y, proj, norm_w)


def _to_cols(a, nh, sb=STEP_B):
    bsz = a.shape[0]
    return a.reshape(bsz // sb, sb, nh, -1).transpose(2, 0, 3, 1)


def _prep_weights(w_in_ab, w_gk2, b_gk2, gla_norm_w, conv_w, conv_b, dt_bias, a_log, d_skip,
                  ssd_norm_w, w_out_ab, w_in_c, hg_norm_w, w_out_c, router_w, router_bias,
                  w_gate, w_up, w_down, ln1_w, ln1_b, ln2_w, ln2_b):
    offs = np.cumsum([0, GLA_KEY, GLA_KEY, GLA_VAL, GLA_VAL, GLA_RANK, SSD_INNER, SSD_CONV_DIM,
                      SSD_HEADS])
    sec = lambda w, i: w[:, offs[i]:offs[i + 1]]
    w = w_in_ab[0]
    pad = jnp.zeros((D_MODEL, LANES - SSD_HEADS - GLA_RANK), w.dtype)
    w_ab = jnp.concatenate([sec(w, 5), sec(w, 2), sec(w, 3), sec(w, 6), sec(w, 0), sec(w, 1),
                            sec(w, 7), sec(w, 4), pad], axis=1)
    hi_lo = lambda m: (m.astype(BF16), (m - m.astype(BF16).astype(F32)).astype(BF16))
    pad_e = lambda m: jnp.pad(m, ((0, LANES - N_EXPERTS), (0, 0)))
    router_pieces = jnp.concatenate([pad_e(piece) for piece in _split3(router_w.T)], axis=0)
    w2_wide = jnp.zeros((LANES, GLA_KEY), F32).at[SSD_HEADS:SSD_HEADS + GLA_RANK].set(w_gk2[0])
    lane_pad = lambda v: jnp.zeros((1, LANES), F32).at[0, :SSD_HEADS].set(v)
    expand = np.zeros((LANES, SSD_INNER), np.float32)
    for h in range(SSD_HEADS):
        expand[h, h * SSD_HEADDIM:(h + 1) * SSD_HEADDIM] = 1.0
    return dict(
        w_ab=hi_lo(w_ab),
        w2_wide=w2_wide,
        w2_heads=w2_wide.reshape(LANES, GLA_HEADS, GLA_DK).transpose(1, 0, 2),
        b2_wide=b_gk2[0].reshape(1, GLA_KEY),
        b2_heads=b_gk2[0].reshape(GLA_HEADS, 1, GLA_DK),
        gla_norm_w=gla_norm_w[0].reshape(1, GLA_DV),
        conv_w=conv_w[0], conv_b=conv_b[0].reshape(1, SSD_CONV_DIM),
        dtb_p=lane_pad(dt_bias[0]), alog_p=lane_pad(a_log[0]),
        dskip_x=jnp.repeat(d_skip[0], SSD_HEADDIM).reshape(1, SSD_INNER),
        ssd_norm_w=ssd_norm_w[0].reshape(1, SSD_INNER),
        expand=jnp.asarray(expand, BF16),
        prefix_sel=jnp.asarray(_prefix_selector(), BF16),
        w_out_gla=hi_lo(w_out_ab[0, :GLA_VAL]),
        w_out_ssd=hi_lo(w_out_ab[0, GLA_VAL:]),
        w_c=w_in_c[0].astype(BF16),
        hg_norm_w=hg_norm_w[0].reshape(1, HG_DI),
        w_out_c=w_out_c[0].astype(BF16),
        rwt=router_pieces,
        rbias=router_bias.reshape(N_EXPERTS, 1),
        w_gate=w_gate.reshape(DEPTH * N_EXPERTS, D_MODEL, D_FF_EXPERT),
        w_up=w_up.reshape(DEPTH * N_EXPERTS, D_MODEL, D_FF_EXPERT),
        w_down=w_down.reshape(DEPTH * N_EXPERTS, D_FF_EXPERT, D_MODEL),
        ln1_w=ln1_w.reshape(DEPTH, 1, D_MODEL), ln1_b=ln1_b.reshape(DEPTH, 1, D_MODEL),
        ln2_w=ln2_w.reshape(DEPTH, 1, D_MODEL), ln2_b=ln2_b.reshape(DEPTH, 1, D_MODEL),
    )


def _ffn(x, p, layer, tm, tm_moe):
    gates = _router(x, p['rwt'], p['rbias'], tm)
    return _moe_ln(x, gates, p['w_gate'], p['w_up'], p['w_down'], layer,
                   p['ln2_w'][layer], p['ln2_b'][layer], tm_moe)


def _ssd_state_from_wide(s_wide):
    bsz = s_wide.shape[0]
    return s_wide.reshape(bsz, SSD_STATE, SSD_HEADS, SSD_HEADDIM).transpose(0, 2, 1, 3)


def _trunk_prompt(x3, p, lower_bounds, tm, tn_ab, tn_c):
    bsz, seq, _ = x3.shape
    x = x3.reshape(bsz * seq, D_MODEL)
    tm_big = 2 * tm
    proj = _proj(x, p['w_ab'], tm_big, tn_ab)
    o_gla, s_gla = _gla_prompt(proj, p['w2_heads'], p['b2_heads'], p['gla_norm_w'],
                               p['prefix_sel'], bsz, seq)
    yz, s_ssd, s_conv = _ssd_prompt(proj, p['conv_w'], p['conv_b'], p['dtb_p'], p['alog_p'],
                                    p['dskip_x'], p['ssd_norm_w'], p['expand'], bsz, seq)
    x = _outproj_ln([o_gla, yz], [p['w_out_gla'], p['w_out_ssd']], x, p['ln1_w'][0], p['ln1_b'][0], tm)
    x, spare = _ffn_sorted(x, p, 0, tm, tm_big)
    proj_c = _proj(x, p['w_c'], tm_big, tn_c)
    o_hg, s_hg = _hgrn_prompt(proj_c, lower_bounds, p['hg_norm_w'], p['prefix_sel'], 1, bsz, seq)
    x = _outproj_ln([o_hg], [p['w_out_c']], x, p['ln1_w'][1], p['ln1_b'][1], tm)
    x, _ = _ffn_sorted(x, p, 1, tm, tm_big, spare)
    return (x.reshape(bsz, seq, D_MODEL), s_gla[None], _ssd_state_from_wide(s_ssd)[None],
            s_conv[None], s_hg[None])


def _trunk_sample(x3, st_gla, st_ssd, st_conv, st_hg, p, lower_bounds, tn_ab, tn_c):
    bsz = x3.shape[0]
    tm = bsz
    x = x3.reshape(bsz, D_MODEL)
    proj = _proj(x, p['w_ab'], tm, tn_ab)
    qs, dec, xc, xdt, dax, conv_new = _ab_prep(proj, st_conv[0].transpose(1, 0, 2), p['w2_wide'],
                                               p['b2_wide'], p['conv_w'], p['conv_b'], p['dtb_p'],
                                               p['alog_p'], p['expand'])
    conv_new = conv_new.transpose(1, 0, 2)
    k_gla = proj[:, AB_K:AB_K + GLA_KEY]
    s_gla, o_gla = _vec_step(st_gla[0], qs, _to_cols(k_gla, GLA_HEADS, VEC_STEP_B),
                             _to_cols(dec, GLA_HEADS, VEC_STEP_B), proj, AB_V // GLA_DV, proj,
                             AB_GOUT // GLA_DV, p['gla_norm_w'], _ThreePass)
    dax_wide = jnp.repeat(dax[:, ::SSD_HEADDIM], LANES, axis=1)
    s_ssd_t, y = _ssd_step(st_ssd[0].transpose(0, 1, 3, 2), _to_cols(xdt, SSD_GROUPS),
                           xc, dax_wide, p['dskip_x'])
    s_ssd = s_ssd_t.transpose(0, 1, 3, 2)
    yz = _ssd_post(y, proj, p['ssd_norm_w'])
    x = _outproj_ln([o_gla, yz], [p['w_out_gla'], p['w_out_ssd']], x, p['ln1_w'][0], p['ln1_b'][0], tm)
    x = _ffn(x, p, 0, tm, tm)
    proj_c = _proj(x, p['w_c'], tm, tn_c)
    qh, kh, dh = _hgrn_prep(proj_c, lower_bounds, 1)
    s_hg, o_hg = _vec_step(st_hg[0], qh, _to_cols(kh, HG_HEADS, VEC_STEP_B),
                           _to_cols(dh, HG_HEADS, VEC_STEP_B), proj_c, 2 * HG_HEADS, proj_c, 3 * HG_HEADS,
                           p['hg_norm_w'], _OnePass)
    x = _outproj_ln([o_hg], [p['w_out_c']], x, p['ln1_w'][1], p['ln1_b'][1], tm)
    x = _ffn(x, p, 1, tm, tm)
    return x.reshape(bsz, 1, D_MODEL), s_gla[None], s_ssd[None], conv_new[None], s_hg[None]


def kernel(x_prompt, x_sample, state_gla, state_ssd, state_conv, state_hgrn, w_in_ab, w_gk2, b_gk2, gla_norm_w, conv_w, conv_b, dt_bias, a_log, d_skip, ssd_norm_w, w_out_ab, w_in_c, lower_bounds, hg_norm_w, w_out_c, router_w, router_bias, w_gate, w_up, w_down, ln1_w, ln1_b, ln2_w, ln2_b):
    p = _prep_weights(w_in_ab, w_gk2, b_gk2, gla_norm_w, conv_w, conv_b, dt_bias, a_log, d_skip,
                      ssd_norm_w, w_out_ab, w_in_c, hg_norm_w, w_out_c, router_w, router_bias,
                      w_gate, w_up, w_down, ln1_w, ln1_b, ln2_w, ln2_b)
    y_p, gla_p, ssd_p, conv_p, hg_p = _trunk_prompt(x_prompt, p, lower_bounds, ROW_TILE,
                                                    AB_COL_TILE, C_COL_TILE)
    y_s, gla_s, ssd_s, conv_s, hg_s = _trunk_sample(x_sample, state_gla, state_ssd, state_conv,
                                                    state_hgrn, p, lower_bounds, AB_COL_TILE,
                                                    C_COL_TILE)
    return (y_p, y_s, gla_p, ssd_p, conv_p, hg_p, gla_s, ssd_s, conv_s, hg_s)
```
